```python
import math
import jax, jax.numpy as jnp
from jax import lax
import numpy as np

D_MODEL = 1024
BATCH = 1
SEQ = 16384
DEPTH = 1
DEC_BATCH = 32
DEC_SEQ = 4
PAST_LEN = 16384
PAGE_SIZE = 128

NSA_HEADS = 8
NSA_KV_HEADS = 2
NSA_GROUP = NSA_HEADS // NSA_KV_HEADS
NSA_HEAD_DIM = 64
CMP_BLOCK = 32
CMP_STRIDE = 16
CMP_HIDDEN = 2 * NSA_HEAD_DIM
SEL_BLOCK = 64
N_SEL = 16
WINDOW = 512
Q_BLOCK = 128
FORCE_BONUS = 1e4
N_BUCKETS = 32
MAX_DISTANCE = 128
HG_HEADS = 4
HG_KEY_DIM = 128
HG_VAL_DIM = 128
HG_CHUNK = 64
D_FF = ((8 * D_MODEL // 3 + 255) // 256) * 256
EPS = 1e-6

NSA_Q_DIM = NSA_HEADS * NSA_HEAD_DIM
NSA_KV_DIM = NSA_KV_HEADS * NSA_HEAD_DIM
HG_F_DIM = HG_HEADS * HG_KEY_DIM
HG_V_DIM = HG_HEADS * HG_VAL_DIM
IN_SIZES = (NSA_Q_DIM, NSA_KV_DIM, NSA_KV_DIM, NSA_KV_DIM, NSA_KV_DIM, NSA_KV_DIM, NSA_KV_DIM,
            3 * NSA_HEADS, HG_F_DIM, HG_F_DIM, HG_V_DIM, HG_V_DIM, D_MODEL, D_MODEL)
D_IN = sum(IN_SIZES)

kernel_name = 'nsa_hgrn2_gated_hybrid_step'


def rmsnorm(x, g):
    xf = x.astype(jnp.float32)
    y = xf * lax.rsqrt(jnp.mean(xf * xf, axis=-1, keepdims=True) + EPS) * g.astype(jnp.float32)
    return y.astype(x.dtype)


def masked_softmax(logits, mask):
    lf = jnp.where(mask, logits.astype(jnp.float32), -1e30)
    p = jax.nn.softmax(lf, axis=-1)
    return jnp.where(mask, p, 0.0)


def rel_bucket(rel):
    n = jnp.maximum(rel, 0)
    max_exact = N_BUCKETS // 2
    nf = jnp.maximum(n, 1).astype(jnp.float32)
    large = max_exact + (jnp.log(nf / max_exact) / math.log(MAX_DISTANCE / max_exact)
                         * (N_BUCKETS - max_exact)).astype(jnp.int32)
    large = jnp.minimum(large, N_BUCKETS - 1)
    return jnp.where(n < max_exact, n, large)


def project(xn, w_in):
    u = jnp.einsum('btd,de->bte', xn, w_in)
    parts, off = [], 0
    for n in IN_SIZES:
        parts.append(u[..., off:off + n])
        off += n
    return parts


def compress(rows, pe, w1, w2):
    B, L, KV, HD = rows.shape
    c = CMP_BLOCK // CMP_STRIDE
    nch = L // CMP_STRIDE
    nc = nch - c + 1
    ch = rows[:, :nch * CMP_STRIDE].reshape(B, nch, CMP_STRIDE, KV, HD)
    pe_r = pe.reshape(c, CMP_STRIDE, HD)
    w1_r = w1.reshape(c, CMP_STRIDE, HD, CMP_HIDDEN)
    h = jnp.einsum('bnskd,sde->bnke', ch[:, 0:nc] + pe_r[0][:, None, :], w1_r[0])
    for j in range(1, c):
        h = h + jnp.einsum('bnskd,sde->bnke', ch[:, j:j + nc] + pe_r[j][:, None, :], w1_r[j])
    blocks = jnp.einsum('bnke,ed->bnkd', jax.nn.gelu(h), w2)
    end_pos = jnp.arange(nc, dtype=jnp.int32) * CMP_STRIDE + (CMP_BLOCK - 1)
    return blocks, end_pos


def sel_blocks(rows):
    B, L, KV, HD = rows.shape
    nb = -(-L // SEL_BLOCK)
    rows = jnp.pad(rows, ((0, 0), (0, nb * SEL_BLOCK - L), (0, 0), (0, 0)))
    return rows.reshape(B, nb, SEL_BLOCK, KV, HD).transpose(0, 3, 1, 2, 4)


def nsa_attend(q, gates, q_pos, kc, vc, c_end, ks_t, vs_t, kw, vw, kw_pos, rel_bias):
    B, QB = q.shape[:2]
    KV, G, HD = NSA_KV_HEADS, NSA_GROUP, NSA_HEAD_DIM
    qg = (q * HD ** -0.5).reshape(B, QB, KV, G, HD).transpose(0, 2, 3, 1, 4)
    tbl = rel_bias.T.reshape(KV, G, N_BUCKETS)
    rel_c = q_pos[:, None] - c_end[None, :]
    logit_c = jnp.einsum('bkgqd,bnkd->bkgqn', qg, kc) + tbl[:, :, rel_bucket(rel_c)]
    p_c = masked_softmax(logit_c, rel_c >= 0)
    o_c = jnp.einsum('bkgqn,bnkd->bkgqd', p_c.astype(vc.dtype), vc)
    nc = kc.shape[1]
    nb = ks_t.shape[2]
    c = CMP_BLOCK // CMP_STRIDE
    ratio = SEL_BLOCK // CMP_STRIDE
    n_ov = ratio + c - 1
    imp = p_c.sum(axis=2)
    right = max(0, ratio * (nb - 1) + n_ov - (c - 1) - nc)
    imp = jnp.pad(imp, ((0, 0), (0, 0), (0, 0), (c - 1, right)))
    score = jnp.zeros(imp.shape[:3] + (nb,), jnp.float32)
    for u in range(n_ov):
        start = CMP_STRIDE * (u - (c - 1))
        w_u = (min(start + CMP_BLOCK, SEL_BLOCK) - max(start, 0)) / CMP_STRIDE
        score = score + w_u * imp[..., u:u + ratio * (nb - 1) + 1:ratio]
    blk = jnp.arange(nb, dtype=jnp.int32)[None, :]
    cur = (q_pos // SEL_BLOCK)[:, None]
    forced = (blk == 0) | (blk == cur) | (blk == cur - 1)
    score = jnp.where(blk <= cur, score + jnp.where(forced, FORCE_BONUS, 0.0), -FORCE_BONUS)
    n_sel = min(N_SEL, nb)
    _, idx = lax.top_k(score, n_sel)
    bi = jnp.arange(B)[:, None, None, None]
    ki = jnp.arange(KV)[None, :, None, None]
    nk = n_sel * SEL_BLOCK
    ksel = ks_t[bi, ki, idx].reshape(B, KV, QB, nk, HD)
    vsel = vs_t[bi, ki, idx].reshape(B, KV, QB, nk, HD)
    spos = (idx[..., None] * SEL_BLOCK + jnp.arange(SEL_BLOCK, dtype=jnp.int32)).reshape(B, KV, QB, nk)
    rel_s = q_pos[None, None, :, None] - spos
    kk = jnp.arange(KV)[None, :, None, None, None]
    gg = jnp.arange(G)[None, None, :, None, None]
    logit_s = jnp.einsum('bkgqd,bkqnd->bkgqn', qg, ksel) + tbl[kk, gg, rel_bucket(rel_s)[:, :, None]]
    p_s = masked_softmax(logit_s, (rel_s >= 0)[:, :, None])
    o_s = jnp.einsum('bkgqn,bkqnd->bkgqd', p_s.astype(vsel.dtype), vsel)
    rel_w = q_pos[:, None] - kw_pos[None, :]
    mask_w = (rel_w >= 0) & (rel_w < WINDOW) & (kw_pos[None, :] >= 0)
    logit_w = jnp.einsum('bkgqd,bnkd->bkgqn', qg, kw) + tbl[:, :, rel_bucket(rel_w)]
    p_w = masked_softmax(logit_w, mask_w)
    o_w = jnp.einsum('bkgqn,bnkd->bkgqd', p_w.astype(vw.dtype), vw)
    gt = gates.reshape(B, QB, KV, G, 3).transpose(0, 2, 3, 1, 4)
    o = gt[..., 0:1] * o_c + gt[..., 1:2] * o_s + gt[..., 2:3] * o_w
    return o.transpose(0, 3, 1, 2, 4).reshape(B, QB, KV * G * HD)


def nsa_prompt(q, gates, kc, vc, c_end, ks_t, vs_t, kw, vw, rel_bias):
    B, T = q.shape[:2]
    KV, HD = NSA_KV_HEADS, NSA_HEAD_DIM
    qb_len = min(Q_BLOCK, T)
    nqb = T // qb_len
    nwb = -(-WINDOW // qb_len)
    padw = nwb * qb_len
    qb = q.reshape(B, nqb, qb_len, NSA_HEADS, HD).swapaxes(0, 1)
    gb = gates.reshape(B, nqb, qb_len, NSA_HEADS, 3).swapaxes(0, 1)
    posb = jnp.arange(T, dtype=jnp.int32).reshape(nqb, qb_len)

    def band(a):
        ap = jnp.pad(a, ((0, 0), (padw, 0), (0, 0), (0, 0))).reshape(B, nqb + nwb, qb_len, KV, HD)
        return jnp.concatenate([ap[:, i:i + nqb] for i in range(nwb + 1)], axis=2).swapaxes(0, 1)

    band_pos = (jnp.arange(nqb, dtype=jnp.int32)[:, None] * qb_len - padw
                + jnp.arange((nwb + 1) * qb_len, dtype=jnp.int32)[None, :])

    def block(args):
        qq, ga, pp, kb, vb, kp = args
        return nsa_attend(qq, ga, pp, kc, vc, c_end, ks_t, vs_t, kb, vb, kp, rel_bias)

    o = lax.map(block, (qb, gb, posb, band(kw), band(vw), band_pos))
    return o.swapaxes(0, 1).reshape(B, T, NSA_Q_DIM)


def hgrn_scan(q, k, v, log_f, S0):
    B, T, H, DK = q.shape
    DV = v.shape[-1]
    C = min(HG_CHUNK, T)
    nc = -(-T // C)
    pad = nc * C - T

    def prep(a):
        a = jnp.pad(a.astype(jnp.float32), ((0, 0), (0, pad), (0, 0), (0, 0)))
        return a.reshape(B, nc, C, H, a.shape[-1]).transpose(1, 0, 3, 2, 4)

    tri = (jnp.arange(C)[:, None] >= jnp.arange(C)[None, :])[:, :, None]

    def step(S, inp):
        qc, kc, vc, lfc = inp
        b = jnp.cumsum(lfc, axis=2)
        inter = jnp.einsum('bhtk,bhkv->bhtv', qc * jnp.exp(b), S)
        diff = b[:, :, :, None, :] - b[:, :, None, :, :]
        decay = jnp.exp(jnp.where(tri, diff, -jnp.inf))
        A = jnp.einsum('bhtk,bhtsk,bhsk->bhts', qc, decay, kc)
        intra = jnp.einsum('bhts,bhsv->bhtv', A, vc)
        bl = b[:, :, -1:, :]
        S_new = jnp.exp(bl[:, :, 0])[..., None] * S + jnp.einsum('bhsk,bhsv->bhkv', kc * jnp.exp(bl - b), vc)
        return S_new, inter + intra

    S, o = lax.scan(step, S0.astype(jnp.float32), (prep(q), prep(k), prep(v), prep(log_f)))
    o = o.transpose(1, 0, 3, 2, 4).reshape(B, nc * C, H, DV)[:, :T]
    return o, S


def hgrn_mix(q_raw, f_raw, i_raw, g_raw, lb, g_norm, S0):
    B, T, _ = q_raw.shape
    sh = lambda a, d: a.reshape(B, T, HG_HEADS, d).astype(jnp.float32)
    lbh = lb.reshape(HG_HEADS, HG_KEY_DIM)
    z = sh(f_raw, HG_KEY_DIM)
    q = jax.nn.silu(sh(q_raw, HG_KEY_DIM))
    log_f = jnp.logaddexp(jnp.log(lbh), jnp.log1p(-lbh) + jax.nn.log_sigmoid(z))
    k = (1.0 - lbh) * jax.nn.sigmoid(-z)
    v = sh(i_raw, HG_VAL_DIM)
    o, S = hgrn_scan(q, k, v, log_f, S0)
    o = rmsnorm(o, g_norm) * jax.nn.silu(sh(g_raw, HG_VAL_DIM))
    return o.reshape(B, T, HG_V_DIM).astype(q_raw.dtype), S


def layer_forward(x, pos0, past, norm_mix, w_in, cmp_pe_k, cmp_w1_k, cmp_w2_k, cmp_pe_v, cmp_w1_v, cmp_w2_v,
                  rel_bias, lb, hg_norm, w_proj_a, w_proj_b, w_out, norm_ffn, w_gate, w_up, w_down):
    B, T, _ = x.shape
    q_pos = pos0 + jnp.arange(T, dtype=jnp.int32)
    xn = rmsnorm(x, norm_mix)
    (q_a, kc_r, vc_r, ks_r, vs_r, kw_r, vw_r, g_a, q_b, f_b, i_b, g_b, gate_a, gate_b) = project(xn, w_in)
    kvh = lambda a: a.reshape(B, T, NSA_KV_HEADS, NSA_HEAD_DIM)
    kc_r, vc_r, ks_r, vs_r, kw_r, vw_r = kvh(kc_r), kvh(vc_r), kvh(ks_r), kvh(vs_r), kvh(kw_r), kvh(vw_r)
    q_a = q_a.reshape(B, T, NSA_HEADS, NSA_HEAD_DIM)
    g_a = jax.nn.sigmoid(g_a.reshape(B, T, NSA_HEADS, 3))
    if past is None:
        kc_all, vc_all, ks_all, vs_all = kc_r, vc_r, ks_r, vs_r
        S0 = jnp.zeros((B, HG_HEADS, HG_KEY_DIM, HG_VAL_DIM), jnp.float32)
    else:
        pk_c, pv_c, pk_s, pv_s, buf_k, buf_v, S0 = past
        kc_all = jnp.concatenate([pk_c, kc_r], axis=1)
        vc_all = jnp.concatenate([pv_c, vc_r], axis=1)
        ks_all = jnp.concatenate([pk_s, ks_r], axis=1)
        vs_all = jnp.concatenate([pv_s, vs_r], axis=1)
    kc, c_end = compress(kc_all, cmp_pe_k, cmp_w1_k, cmp_w2_k)
    vc, _ = compress(vc_all, cmp_pe_v, cmp_w1_v, cmp_w2_v)
    ks_t, vs_t = sel_blocks(ks_all), sel_blocks(vs_all)
    if past is None:
        o_a = nsa_prompt(q_a, g_a, kc, vc, c_end, ks_t, vs_t, kw_r, vw_r, rel_bias)
        wl = min(WINDOW, T)
        win_k, win_v = kw_r[:, -wl:], vw_r[:, -wl:]
    else:
        W = buf_k.shape[1]
        kw = jnp.concatenate([buf_k, kw_r], axis=1)
        vw = jnp.concatenate([buf_v, vw_r], axis=1)
        kw_pos = pos0 - W + jnp.arange(W + T, dtype=jnp.int32)
        o_a = nsa_attend(q_a, g_a, q_pos, kc, vc, c_end, ks_t, vs_t, kw, vw, kw_pos, rel_bias)
        win_k, win_v = kw[:, -W:], vw[:, -W:]
    o_b, S = hgrn_mix(q_b, f_b, i_b, g_b, lb, hg_norm, S0)
    merged = (jax.nn.sigmoid(gate_a) * jnp.einsum('bte,ed->btd', o_a, w_proj_a)
              + jax.nn.sigmoid(gate_b) * jnp.einsum('bte,ed->btd', o_b, w_proj_b))
    x = x + jnp.einsum('btd,de->bte', merged, w_out)
    hn = rmsnorm(x, norm_ffn)
    ff = jax.nn.silu(jnp.einsum('btd,df->btf', hn, w_gate)) * jnp.einsum('btd,df->btf', hn, w_up)
    x = x + jnp.einsum('btf,fd->btd', ff, w_down)
    return x, (kc_r, vc_r, ks_r, vs_r, win_k, win_v, S)


def setup_inputs(seed: int = 0) -> dict:
    key = jax.random.key(seed)
    ks = jax.random.split(key, 32)
    f32 = jnp.float32
    nrm = lambda k, shape, s=1.0: s * jax.random.normal(k, shape, f32)
    n_pages = PAST_LEN // PAGE_SIZE
    n_pool = (5 * DEC_BATCH * n_pages + 3) // 4
    w_buf = min(WINDOW, PAST_LEN)
    pool_shape = (DEPTH, n_pool, PAGE_SIZE, NSA_KV_HEADS, NSA_HEAD_DIM)
    win_shape = (DEPTH, DEC_BATCH, w_buf, NSA_KV_HEADS, NSA_HEAD_DIM)
    page_table = jax.random.permutation(ks[9], n_pool)[:DEC_BATCH * n_pages].reshape(DEC_BATCH, n_pages).astype(jnp.int32)
    hd = NSA_HEAD_DIM
    return {
        'x_prompt': nrm(ks[0], (BATCH, SEQ, D_MODEL)),
        'x_sample': nrm(ks[1], (DEC_BATCH, DEC_SEQ, D_MODEL)),
        'cache_k_cmp': nrm(ks[2], pool_shape),
        'cache_v_cmp': nrm(ks[3], pool_shape),
        'cache_k_slc': nrm(ks[4], pool_shape),
        'cache_v_slc': nrm(ks[5], pool_shape),
        'state_k_win': nrm(ks[6], win_shape),
        'state_v_win': nrm(ks[7], win_shape),
        'state_hgrn': nrm(ks[8], (DEPTH, DEC_BATCH, HG_HEADS, HG_KEY_DIM, HG_VAL_DIM), 0.5),
        'page_table': page_table,
        'norm_mix': 1.0 + nrm(ks[10], (DEPTH, D_MODEL), 0.1),
        'w_in': nrm(ks[11], (DEPTH, D_MODEL, D_IN), D_MODEL ** -0.5),
        'cmp_pe_k': nrm(ks[12], (DEPTH, CMP_BLOCK, hd), 0.1),
        'cmp_w1_k': nrm(ks[13], (DEPTH, CMP_BLOCK, hd, CMP_HIDDEN), (CMP_BLOCK * hd) ** -0.5),
        'cmp_w2_k': nrm(ks[14], (DEPTH, CMP_HIDDEN, hd), CMP_HIDDEN ** -0.5),
        'cmp_pe_v': nrm(ks[15], (DEPTH, CMP_BLOCK, hd), 0.1),
        'cmp_w1_v': nrm(ks[16], (DEPTH, CMP_BLOCK, hd, CMP_HIDDEN), (CMP_BLOCK * hd) ** -0.5),
        'cmp_w2_v': nrm(ks[17], (DEPTH, CMP_HIDDEN, hd), CMP_HIDDEN ** -0.5),
        'rel_bias': nrm(ks[18], (N_BUCKETS, NSA_HEADS), 0.5),
        'hg_lb_logits': nrm(ks[19], (DEPTH + 1, HG_F_DIM)),
        'hg_norm': 1.0 + nrm(ks[20], (DEPTH, HG_VAL_DIM), 0.1),
        'w_proj_a': nrm(ks[21], (DEPTH, NSA_Q_DIM, D_MODEL), NSA_Q_DIM ** -0.5),
        'w_proj_b': nrm(ks[22], (DEPTH, HG_V_DIM, D_MODEL), HG_V_DIM ** -0.5),
        'w_out': nrm(ks[23], (DEPTH, D_MODEL, D_MODEL), D_MODEL ** -0.5),
        'norm_ffn': 1.0 + nrm(ks[24], (DEPTH, D_MODEL), 0.1),
        'w_gate': nrm(ks[25], (DEPTH, D_MODEL, D_FF), D_MODEL ** -0.5),
        'w_up': nrm(ks[26], (DEPTH, D_MODEL, D_FF), D_MODEL ** -0.5),
        'w_down': nrm(ks[27], (DEPTH, D_FF, D_MODEL), D_FF ** -0.5),
        'norm_final': 1.0 + nrm(ks[28], (D_MODEL,), 0.1),
    }


def reference(x_prompt, x_sample, cache_k_cmp, cache_v_cmp, cache_k_slc, cache_v_slc, state_k_win, state_v_win,
              state_hgrn, page_table, norm_mix, w_in, cmp_pe_k, cmp_w1_k, cmp_w2_k, cmp_pe_v, cmp_w1_v, cmp_w2_v,
              rel_bias, hg_lb_logits, hg_norm, w_proj_a, w_proj_b, w_out, norm_ffn, w_gate, w_up, w_down, norm_final):
    lbs = jnp.cumsum(jax.nn.softmax(hg_lb_logits.astype(jnp.float32), axis=0), axis=0)
    dec_b, n_pages = page_table.shape
    past_len = n_pages * cache_k_cmp.shape[2]

    def gather(pool):
        return pool[page_table].reshape(dec_b, past_len, NSA_KV_HEADS, NSA_HEAD_DIM)

    xp, xs = x_prompt, x_sample
    new_p, new_s = [], []
    for l in range(DEPTH):
        lw = (norm_mix[l], w_in[l], cmp_pe_k[l], cmp_w1_k[l], cmp_w2_k[l], cmp_pe_v[l], cmp_w1_v[l], cmp_w2_v[l],
              rel_bias, lbs[l], hg_norm[l], w_proj_a[l], w_proj_b[l], w_out[l], norm_ffn[l], w_gate[l], w_up[l], w_down[l])
        xp, sp = layer_forward(xp, 0, None, *lw)
        past = (gather(cache_k_cmp[l]), gather(cache_v_cmp[l]), gather(cache_k_slc[l]), gather(cache_v_slc[l]),
                state_k_win[l], state_v_win[l], state_hgrn[l])
        xs, ss = layer_forward(xs, past_len, past, *lw)
        new_p.append(sp)
        new_s.append(ss)
    y_prompt = rmsnorm(xp, norm_final)
    y_sample = rmsnorm(xs, norm_final)
    p_k_cmp, p_v_cmp, p_k_slc, p_v_slc, p_k_win, p_v_win, p_hgrn = [jnp.stack(z) for z in zip(*new_p)]
    s_k_cmp, s_v_cmp, s_k_slc, s_v_slc, s_k_win, s_v_win, s_hgrn = [jnp.stack(z) for z in zip(*new_s)]
    return (y_prompt, y_sample, p_k_cmp, p_v_cmp, p_k_slc, p_v_slc, p_k_win, p_v_win, p_hgrn,
            s_k_cmp, s_v_cmp, s_k_slc, s_v_slc, s_k_win, s_v_win, s_hgrn)
```

```python
import functools
import math

import numpy as np
import jax
import jax.numpy as jnp
from jax import lax
from jax.experimental import pallas as pl
from jax.experimental.pallas import tpu as pltpu

F32 = jnp.float32
BF16 = jnp.bfloat16
HIGHEST = lax.Precision.HIGHEST

D_MODEL = 1024
N_HEADS = 8
N_KV = 2
GROUP = N_HEADS // N_KV
HEAD_DIM = 64
KV_DIM = N_KV * HEAD_DIM
Q_DIM = N_HEADS * HEAD_DIM
CMP_BLOCK = 32
CMP_STRIDE = 16
CMP_HIDDEN = 2 * HEAD_DIM
SEL_BLOCK = 64
N_SEL = 16
WINDOW = 512
Q_BLOCK = 128
FORCE_BONUS = 1e4
N_BUCKETS = 32
MAX_DISTANCE = 128
HG_HEADS = 4
HG_DIM = 128
HG_CHUNK = 64
HG_SUB = 16
HG_W = HG_HEADS * HG_DIM
D_FF = ((8 * D_MODEL // 3 + 255) // 256) * 256
EPS = 1e-6
PAGE = 128
NEG = -1e30

LANES = 128
SUBLANES = 8
VMEM_LIMIT = 56 * 1024 * 1024

_OFF_Q = 0
_OFF_KV = _OFF_Q + Q_DIM
_OFF_G = _OFF_KV + 6 * KV_DIM
_OFF_HG = _OFF_G + LANES
_OFF_GATE = _OFF_HG + 4 * HG_W
_PROJ_N = _OFF_GATE + 2 * D_MODEL

_HEAD_PERM = np.array([h for j in range(GROUP) for h in (j, GROUP + j)])

SEL_TILE = 512
BLK_PER_TILE = SEL_TILE // SEL_BLOCK
CMP_PAD = 16
CMP_BAND = 24


def _cparams(sem, vmem=VMEM_LIMIT):
    return pltpu.CompilerParams(dimension_semantics=sem, vmem_limit_bytes=vmem)


def _const_spec(shape):
    nd = len(shape)
    return pl.BlockSpec(shape, lambda *_: (0,) * nd, pipeline_mode=pl.Buffered(1))


def _bucket_table():
    n = np.arange(256)
    max_exact = N_BUCKETS // 2
    nf = np.maximum(n, 1).astype(np.float64)
    large = max_exact + (np.log(nf / max_exact) / math.log(MAX_DISTANCE / max_exact)
                         * (N_BUCKETS - max_exact)).astype(np.int64)
    large = np.minimum(large, N_BUCKETS - 1)
    return np.where(n < max_exact, n, large)


_BUCKET = _bucket_table()


def _proj_kernel(x_ref, g_ref, w_ref, lb_ref,
                 qa_ref, kc_ref, vc_ref, ks_ref, vs_ref, kw_ref, vw_ref,
                 ksb_ref, vsb_ref, kwb_ref, vwb_ref, ga_ref,
                 qh_ref, lf_ref, kh_ref, vh_ref, gs_ref, sa_ref, sb_ref):
    x = x_ref[...]
    xn = x * lax.rsqrt(jnp.mean(x * x, axis=-1, keepdims=True) + EPS) * g_ref[...]
    xb = xn.astype(BF16)

    def seg(a, n):
        return jnp.dot(xb, w_ref[:, a:a + n], preferred_element_type=F32)

    qa_ref[...] = (seg(_OFF_Q, Q_DIM) * (HEAD_DIM ** -0.5)).astype(BF16)
    f32_refs = (kc_ref, vc_ref, ks_ref, vs_ref, kw_ref, vw_ref)
    b16_refs = (None, None, ksb_ref, vsb_ref, kwb_ref, vwb_ref)
    for i in range(6):
        u = seg(_OFF_KV + i * KV_DIM, KV_DIM)
        f32_refs[i][...] = u
        if b16_refs[i] is not None:
            b16_refs[i][...] = u.astype(BF16)
    ga_ref[...] = jax.nn.sigmoid(seg(_OFF_G, LANES))

    log_lb = lb_ref[0:1, :]
    log_1m = lb_ref[1:2, :]
    one_m = lb_ref[2:3, :]
    qh_ref[...] = jax.nn.silu(seg(_OFF_HG, HG_W))
    z = seg(_OFF_HG + HG_W, HG_W)
    b = log_1m + (jnp.minimum(z, 0.0) - jnp.log1p(jnp.exp(-jnp.abs(z))))
    hi = jnp.maximum(log_lb, b)
    lf_ref[...] = hi + jnp.log1p(jnp.exp(-jnp.abs(log_lb - b)))
    kh_ref[...] = one_m * jax.nn.sigmoid(-z)
    vh_ref[...] = seg(_OFF_HG + 2 * HG_W, HG_W)
    gs_ref[...] = jax.nn.silu(seg(_OFF_HG + 3 * HG_W, HG_W))
    sa_ref[...] = jax.nn.sigmoid(seg(_OFF_GATE, D_MODEL))
    sb_ref[...] = jax.nn.sigmoid(seg(_OFF_GATE + D_MODEL, D_MODEL))


def _proj(x2d, g, w, lb3, tm):
    rows = x2d.shape[0]
    widths = ([(Q_DIM, BF16)] + [(KV_DIM, F32)] * 6 + [(KV_DIM, BF16)] * 4 + [(LANES, F32)]
              + [(HG_W, F32)] * 5 + [(D_MODEL, F32)] * 2)
    return pl.pallas_call(
        _proj_kernel,
        grid=(rows // tm,),
        in_specs=[pl.BlockSpec((tm, D_MODEL), lambda i: (i, 0)),
                  _const_spec((1, D_MODEL)),
                  _const_spec((D_MODEL, _PROJ_N)),
                  _const_spec((SUBLANES, HG_W))],
        out_specs=[pl.BlockSpec((tm, n), lambda i: (i, 0)) for n, _ in widths],
        out_shape=[jax.ShapeDtypeStruct((rows, n), dt) for n, dt in widths],
        compiler_params=_cparams(("arbitrary",)),
        name="proj",
    )(x2d, g, w, lb3)


_CH_W = CMP_STRIDE * KV_DIM
_CH_PER_PAGE = PAGE // CMP_STRIDE


def _compress_kernel(pt_ref, pool_ref, pe_ref, w1_ref, w2_ref, out_ref, outt_ref, buf, hbuf, sem,
                     *, n_pages):
    b = pl.program_id(0)
    nb = pl.num_programs(0)
    slot = b % 2
    n_ch = n_pages * _CH_PER_PAGE

    def page_copy(bb, p, s):
        return pltpu.make_async_copy(pool_ref.at[pt_ref[bb, p]],
                                     buf.at[s, pl.ds(p * _CH_PER_PAGE, _CH_PER_PAGE)],
                                     sem.at[s])

    def start_all(bb, s):
        def body(p, c):
            page_copy(bb, p, s).start()
            return c
        lax.fori_loop(0, n_pages, body, 0)

    @pl.when(b == 0)
    def _():
        start_all(b, slot)

    @pl.when(b + 1 < nb)
    def _():
        start_all(b + 1, 1 - slot)

    def wait_body(p, c):
        page_copy(b, p, slot).wait()
        return c
    lax.fori_loop(0, n_pages, wait_body, 0)

    rows = math.gcd(n_ch, 256)
    for j in range(2):
        pe = pe_ref[j:j + 1, :]
        for r in range(n_ch // rows):
            xa = (buf[slot, r * rows:(r + 1) * rows, :] + pe).astype(BF16)
            hbuf[j, r * rows:(r + 1) * rows, :] = jnp.dot(xa, w1_ref[j], preferred_element_type=F32)
    h = hbuf[0] + pltpu.roll(hbuf[1], n_ch - 1, 0)
    blocks = jnp.dot(jax.nn.gelu(h).astype(BF16), w2_ref[...], preferred_element_type=F32)
    row = lax.broadcasted_iota(jnp.int32, blocks.shape, 0)
    blocks = jnp.where(row < n_ch - 1, blocks, 0.0)
    out_ref[0] = blocks.astype(BF16)
    outt_ref[0] = blocks.T.astype(BF16)


def _compress(pool, page_table, pe, w1, w2):
    nbatch, n_pages = page_table.shape
    n_ch = n_pages * _CH_PER_PAGE
    grid_spec = pltpu.PrefetchScalarGridSpec(
        num_scalar_prefetch=1,
        grid=(nbatch,),
        in_specs=[pl.BlockSpec(memory_space=pl.ANY),
                  _const_spec((SUBLANES, _CH_W)),
                  _const_spec((2, _CH_W, 2 * CMP_HIDDEN)),
                  _const_spec((2 * CMP_HIDDEN, KV_DIM))],
        out_specs=[pl.BlockSpec((1, n_ch, KV_DIM), lambda b, pt: (b, 0, 0)),
                   pl.BlockSpec((1, KV_DIM, n_ch), lambda b, pt: (b, 0, 0))],
        scratch_shapes=[pltpu.VMEM((2, n_ch, _CH_W), F32),
                        pltpu.VMEM((2, n_ch, 2 * CMP_HIDDEN), F32),
                        pltpu.SemaphoreType.DMA((2,))],
    )
    return pl.pallas_call(
        functools.partial(_compress_kernel, n_pages=n_pages),
        grid_spec=grid_spec,
        out_shape=[jax.ShapeDtypeStruct((nbatch, n_ch, KV_DIM), BF16),
                   jax.ShapeDtypeStruct((nbatch, KV_DIM, n_ch), BF16)],
        compiler_params=_cparams(("arbitrary",)),
        name="compress",
    )(page_table, pool, pe, w1, w2)


def _compress_weights(pe, w1, w2):
    c = CMP_BLOCK // CMP_STRIDE
    pe_r = pe.reshape(c, CMP_STRIDE, 1, HEAD_DIM)
    pe_x = jnp.broadcast_to(pe_r, (c, CMP_STRIDE, N_KV, HEAD_DIM)).reshape(c, _CH_W)
    pe_x = jnp.pad(pe_x, ((0, SUBLANES - c), (0, 0)))
    w1_r = w1.reshape(c, CMP_STRIDE, HEAD_DIM, CMP_HIDDEN)
    eye = jnp.eye(N_KV, dtype=w1.dtype)
    w1_x = jnp.einsum('jsde,kq->jskdqe', w1_r, eye).reshape(c, _CH_W, N_KV * CMP_HIDDEN)
    w2_x = jnp.einsum('ed,kq->keqd', w2, eye).reshape(N_KV * CMP_HIDDEN, KV_DIM)
    return pe_x, w1_x.astype(BF16), w2_x.astype(BF16)


def _hgrn_chunk(q, k, v, lf, st_ref, chunk, sub):
    if chunk > SUBLANES:
        r = lax.broadcasted_iota(jnp.int32, (chunk, chunk), 0)
        c = lax.broadcasted_iota(jnp.int32, (chunk, chunk), 1)
        tri = (r >= c).astype(F32)
        b = jnp.dot(tri, lf, preferred_element_type=F32, precision=HIGHEST)
    else:
        rows = [lf[0:1, :]]
        for t in range(1, chunk):
            rows.append(rows[-1] + lf[t:t + 1, :])
        b = jnp.concatenate(rows, axis=0)
    bl = b[chunk - 1:chunk, :]
    qe = q * jnp.exp(b)
    kd = k * jnp.exp(bl - b)
    ebl = jnp.exp(bl)
    n_sub = chunk // sub
    trow = lax.broadcasted_iota(jnp.int32, (sub, HG_W), 0)
    crow = lax.broadcasted_iota(jnp.int32, (chunk, HG_W), 0)

    diag = []
    for i in range(n_sub):
        qi = q[i * sub:(i + 1) * sub, :]
        bi = b[i * sub:(i + 1) * sub, :]
        acc = [jnp.zeros((sub, HG_DIM), F32) for _ in range(HG_HEADS)]
        for s in range(sub):
            row = i * sub + s
            dec = jnp.exp(jnp.where(trow >= s, bi - b[row:row + 1, :], -jnp.inf))
            prod = qi * k[row:row + 1, :] * dec
            for h in range(HG_HEADS):
                a = jnp.sum(prod[:, h * HG_DIM:(h + 1) * HG_DIM], axis=1, keepdims=True)
                acc[h] = acc[h] + a * v[row:row + 1, h * HG_DIM:(h + 1) * HG_DIM]
        diag.append(acc)

    off = []
    for i in range(n_sub):
        if i == 0:
            off.append(None)
            continue
        b0 = b[i * sub - 1:i * sub, :]
        qs = (q[i * sub:(i + 1) * sub, :] * jnp.exp(b[i * sub:(i + 1) * sub, :] - b0)).astype(BF16)
        ks = (k * jnp.exp(jnp.where(crow < i * sub, b0 - b, -jnp.inf))).astype(BF16)
        off.append((qs, ks))

    vb = v.astype(BF16)
    outs = []
    for h in range(HG_HEADS):
        sl = slice(h * HG_DIM, (h + 1) * HG_DIM)
        st = st_ref[h]
        o_h = lax.dot_general(qe[:, sl].astype(BF16), st.astype(BF16), (((1,), (1,)), ((), ())),
                              preferred_element_type=F32)
        parts = []
        for i in range(n_sub):
            d = diag[i][h]
            if off[i] is not None:
                qs, ks = off[i]
                a = lax.dot_general(qs[:, sl], ks[:, sl], (((1,), (1,)), ((), ())),
                                    preferred_element_type=F32)
                d = d + jnp.dot(a.astype(BF16), vb[:, sl], preferred_element_type=F32)
            parts.append(d)
        intra = parts[0] if n_sub == 1 else jnp.concatenate(parts, axis=0)
        outs.append(o_h + intra)
        st_ref[h] = st * ebl[:, sl] + lax.dot_general(
            vb[:, sl], kd[:, sl].astype(BF16), (((0,), (0,)), ((), ())), preferred_element_type=F32)
    return jnp.concatenate(outs, axis=1)


def _hgrn_finish(o, gs, gn):
    outs = []
    for h in range(HG_HEADS):
        oh = o[:, h * HG_DIM:(h + 1) * HG_DIM]
        y = oh * lax.rsqrt(jnp.mean(oh * oh, axis=-1, keepdims=True) + EPS) * gn
        outs.append(y)
    return (jnp.concatenate(outs, axis=1) * gs).astype(BF16)


def _hgrn_prompt_kernel(q_ref, k_ref, v_ref, lf_ref, gs_ref, gn_ref, o_ref, s_ref, st_ref, *, n_chunks):
    i = pl.program_id(0)

    @pl.when(i == 0)
    def _():
        st_ref[...] = jnp.zeros_like(st_ref)

    def body(c, carry):
        r = pl.ds(pl.multiple_of(c * HG_CHUNK, HG_CHUNK), HG_CHUNK)
        o = _hgrn_chunk(q_ref[r, :], k_ref[r, :], v_ref[r, :], lf_ref[r, :], st_ref, HG_CHUNK, HG_SUB)
        o_ref[r, :] = _hgrn_finish(o, gs_ref[r, :], gn_ref[...])
        return carry
    lax.fori_loop(0, n_chunks, body, 0)

    @pl.when(i == pl.num_programs(0) - 1)
    def _():
        for h in range(HG_HEADS):
            s_ref[h] = st_ref[h].T


def _hgrn_prompt(qh, kh, vh, lf, gs, gn, rows_per_step=512):
    t = qh.shape[0]
    spec = pl.BlockSpec((rows_per_step, HG_W), lambda i: (i, 0))
    return pl.pallas_call(
        functools.partial(_hgrn_prompt_kernel, n_chunks=rows_per_step // HG_CHUNK),
        grid=(t // rows_per_step,),
        in_specs=[spec] * 5 + [_const_spec((1, HG_DIM))],
        out_specs=[spec, pl.BlockSpec((HG_HEADS, HG_DIM, HG_DIM), lambda i: (0, 0, 0))],
        out_shape=[jax.ShapeDtypeStruct((t, HG_W), BF16),
                   jax.ShapeDtypeStruct((HG_HEADS, HG_DIM, HG_DIM), F32)],
        scratch_shapes=[pltpu.VMEM((HG_HEADS, HG_DIM, HG_DIM), F32)],
        compiler_params=_cparams(("arbitrary",)),
        name="hgrn_prompt",
    )(qh, kh, vh, lf, gs, gn)


def _hgrn_sample_kernel(q_ref, k_ref, v_ref, lf_ref, gs_ref, gn_ref, s0_ref, o_ref, s_ref, st_ref, *, t):
    for h in range(HG_HEADS):
        st_ref[h] = s0_ref[0, h].T
    o = _hgrn_chunk(q_ref[0], k_ref[0], v_ref[0], lf_ref[0], st_ref, t, t)
    o_ref[0] = _hgrn_finish(o, gs_ref[0], gn_ref[...])
    for h in range(HG_HEADS):
        s_ref[0, h] = st_ref[h].T


def _hgrn_sample(qh, kh, vh, lf, gs, gn, s0):
    nb, t, _ = qh.shape
    spec = pl.BlockSpec((1, t, HG_W), lambda b: (b, 0, 0))
    sspec = pl.BlockSpec((1, HG_HEADS, HG_DIM, HG_DIM), lambda b: (b, 0, 0, 0))
    return pl.pallas_call(
        functools.partial(_hgrn_sample_kernel, t=t),
        grid=(nb,),
        in_specs=[spec] * 5 + [_const_spec((1, HG_DIM)), sspec],
        out_specs=[spec, sspec],
        out_shape=[jax.ShapeDtypeStruct((nb, t, HG_W), BF16),
                   jax.ShapeDtypeStruct((nb, HG_HEADS, HG_DIM, HG_DIM), F32)],
        scratch_shapes=[pltpu.VMEM((HG_HEADS, HG_DIM, HG_DIM), F32)],
        compiler_params=_cparams(("arbitrary",)),
        name="hgrn_sample",
    )(qh, kh, vh, lf, gs, gn, s0)


def _select_topk(x, blk, n):
    nblk = x.shape[0]
    sel = jnp.zeros_like(x)
    for _ in range(n):
        m = jnp.max(x, axis=0, keepdims=True)
        idx = jnp.min(jnp.where(x == m, blk, float(nblk)), axis=0, keepdims=True)
        pick = blk == idx
        sel = jnp.where(pick, 1.0, sel)
        x = jnp.where(pick, -3e38, x)
    return sel


def _softmax_cols(s, valid):
    m = jnp.max(jnp.where(valid, s, NEG), axis=0, keepdims=True)
    p = jnp.where(valid, jnp.exp(s - m), 0.0)
    l = jnp.sum(p, axis=0, keepdims=True)
    return p * jnp.where(l > 0.0, 1.0 / l, 0.0)


def _split_dot(a_bf16, x):
    hi = x.astype(BF16)
    lo = (x - hi.astype(F32)).astype(BF16)
    return (jnp.dot(a_bf16, hi, preferred_element_type=F32)
            + jnp.dot(a_bf16, lo, preferred_element_type=F32))


def _nt(a, b):
    return lax.dot_general(a, b, (((1,), (1,)), ((), ())), preferred_element_type=F32)


def _online_update(s, v, m_ref, l_ref, acc_ref):
    m_old = m_ref[...]
    m_new = jnp.maximum(m_old, jnp.max(s, axis=1, keepdims=True))
    p = jnp.exp(s - m_new)
    alpha = jnp.exp(m_old - m_new)
    l_ref[...] = alpha * l_ref[...] + jnp.sum(p, axis=1, keepdims=True)
    acc_ref[...] = alpha * acc_ref[...] + jnp.dot(p.astype(BF16), v, preferred_element_type=F32)
    m_ref[...] = m_new


def _score_matrix(n_rows, row_offset, n_blocks):
    c = CMP_BLOCK // CMP_STRIDE
    ratio = SEL_BLOCK // CMP_STRIDE
    n_ov = ratio + c - 1
    m = np.zeros((n_blocks, n_rows), np.float32)
    for j in range(n_blocks):
        for u in range(n_ov):
            start = CMP_STRIDE * (u - (c - 1))
            w_u = (min(start + CMP_BLOCK, SEL_BLOCK) - max(start, 0)) / CMP_STRIDE
            n = ratio * j + u - (c - 1)
            if 0 <= n and n + row_offset < n_rows:
                m[j, n + row_offset] = w_u
    return m


def _nsa_prompt_kernel(qa_ref, ga_ref, kc_ref, vct_ref, ks_ref, vs_ref, kw_ref, vw_ref,
                       mt_ref, cc_ref, cs_ref, bw_ref, o_ref,
                       sc_ref, mask_ref, m_ref, l_ref, acc_ref, okv_ref, *, n_blocks):
    qb = pl.program_id(0)
    nrow = kc_ref.shape[0]
    lane = lax.broadcasted_iota(jnp.int32, (Q_BLOCK, LANES), 1)
    gq = GROUP * Q_BLOCK
    ga = ga_ref[...]

    for kv in range(N_KV):
        half = (lane >= HEAD_DIM) if kv else (lane < HEAD_DIM)
        qx = jnp.concatenate(
            [jnp.where(half, qa_ref[:, j * LANES:(j + 1) * LANES], jnp.zeros((), BF16))
             for j in range(GROUP)], axis=0)

        sc_ref[...] = _nt(kc_ref[...], qx)
        band = pl.ds(pl.multiple_of(qb * SUBLANES, SUBLANES), CMP_BAND)
        sc_ref[band, :] = sc_ref[band, :] + cc_ref[kv]
        r = lax.broadcasted_iota(jnp.int32, (nrow, Q_BLOCK), 0)
        qpos_c = qb * Q_BLOCK + lax.broadcasted_iota(jnp.int32, (nrow, Q_BLOCK), 1)
        end_pos = (r - CMP_PAD) * CMP_STRIDE + (CMP_BLOCK - 1)
        vis = (r >= CMP_PAD) & (r < CMP_PAD + n_blocks * (SEL_BLOCK // CMP_STRIDE) - 1) & (end_pos <= qpos_c)
        imp = jnp.zeros((nrow, Q_BLOCK), F32)
        pcs = []
        for j in range(GROUP):
            pj = _softmax_cols(sc_ref[:, j * Q_BLOCK:(j + 1) * Q_BLOCK], vis)
            imp = imp + pj
            pcs.append(pj.astype(BF16))
        o_ct = jnp.dot(vct_ref[...], jnp.concatenate(pcs, axis=1), preferred_element_type=F32)
        o_c = o_ct.T

        score = _split_dot(mt_ref[...], imp)
        blk_i = lax.broadcasted_iota(jnp.int32, (n_blocks, Q_BLOCK), 0)
        cur = (qb * Q_BLOCK + lax.broadcasted_iota(jnp.int32, (n_blocks, Q_BLOCK), 1)) // SEL_BLOCK
        forced = (blk_i == 0) | (blk_i == cur) | (blk_i == cur - 1)
        score = jnp.where(blk_i <= cur, score + jnp.where(forced, FORCE_BONUS, 0.0), -FORCE_BONUS)
        sel = _select_topk(score, blk_i.astype(F32), N_SEL).T
        selm1 = sel - 1.0
        e_r = lax.broadcasted_iota(jnp.int32, (BLK_PER_TILE, SEL_TILE), 0)
        e_c = lax.broadcasted_iota(jnp.int32, (BLK_PER_TILE, SEL_TILE), 1)
        expand = jnp.where(e_c // SEL_BLOCK == e_r, -NEG, 0.0)
        last_tile = qb // (SEL_TILE // Q_BLOCK)
        for kt in range(n_blocks // BLK_PER_TILE):
            @pl.when(kt <= last_tile)
            def _(kt=kt):
                mask_ref[kt] = jnp.dot(selm1[:, kt * BLK_PER_TILE:(kt + 1) * BLK_PER_TILE], expand,
                                       preferred_element_type=F32)

        m_ref[...] = jnp.full_like(m_ref, NEG)
        l_ref[...] = jnp.zeros_like(l_ref)
        acc_ref[...] = jnp.zeros_like(acc_ref)

        def tile_scores(kt):
            rows = pl.ds(pl.multiple_of(kt * SEL_TILE, SEL_TILE), SEL_TILE)
            s = _nt(qx, ks_ref[rows, :])
            s = (s.reshape(GROUP, Q_BLOCK, SEL_TILE) + mask_ref[kt][None]).reshape(gq, SEL_TILE)
            return s, vs_ref[rows, :]

        n_far = jnp.maximum(qb - 1, 0) // (SEL_TILE // Q_BLOCK)

        def far_body(kt, c):
            s, v = tile_scores(kt)
            _online_update(s, v, m_ref, l_ref, acc_ref)
            return c
        lax.fori_loop(0, n_far, far_body, 0)

        def near_body(kt, c):
            s, v = tile_scores(kt)
            d0 = qb - kt * (SEL_TILE // Q_BLOCK)
            strip = jnp.concatenate(
                [cs_ref[kv, jnp.clip(d0 - i, -1, 2) + 1] for i in range(SEL_TILE // Q_BLOCK)], axis=1)
            _online_update(s + strip, v, m_ref, l_ref, acc_ref)
            return c
        lax.fori_loop(n_far, last_tile + 1, near_body, 0)
        o_s = acc_ref[...] / l_ref[...]

        nband = WINDOW + Q_BLOCK
        wrows = pl.ds(pl.multiple_of(qb * Q_BLOCK, Q_BLOCK), nband)
        s = _nt(qx, kw_ref[wrows, :]) + bw_ref[kv]
        col = lax.broadcasted_iota(jnp.int32, (gq, nband), 1)
        s = jnp.where(col >= WINDOW - qb * Q_BLOCK, s, NEG)
        m = jnp.max(s, axis=1, keepdims=True)
        p = jnp.exp(s - m)
        o_w = jnp.dot(p.astype(BF16), vw_ref[wrows, :], preferred_element_type=F32)
        o_w = o_w / jnp.sum(p, axis=1, keepdims=True)

        for j in range(GROUP):
            h = kv * GROUP + j
            rs = slice(j * Q_BLOCK, (j + 1) * Q_BLOCK)
            okv_ref[kv, rs, :] = (ga[:, 3 * h:3 * h + 1] * o_c[rs, :]
                                  + ga[:, 3 * h + 1:3 * h + 2] * o_s[rs, :]
                                  + ga[:, 3 * h + 2:3 * h + 3] * o_w[rs, :])

    for j in range(GROUP):
        rs = slice(j * Q_BLOCK, (j + 1) * Q_BLOCK)
        o_ref[:, j * LANES:(j + 1) * LANES] = jnp.where(
            lane < HEAD_DIM, okv_ref[0, rs, :], okv_ref[1, rs, :]).astype(BF16)


def _nsa_prompt(qa, ga, kc, vct, ksb, vsb, kwb, vwb, mt, cc, cs, bw):
    t = qa.shape[0]
    n_blocks = t // SEL_BLOCK
    nrow = kc.shape[0]
    gq = GROUP * Q_BLOCK
    return pl.pallas_call(
        functools.partial(_nsa_prompt_kernel, n_blocks=n_blocks),
        grid=(t // Q_BLOCK,),
        in_specs=[pl.BlockSpec((Q_BLOCK, Q_DIM), lambda i: (i, 0)),
                  pl.BlockSpec((Q_BLOCK, LANES), lambda i: (i, 0)),
                  _const_spec(kc.shape), _const_spec(vct.shape),
                  _const_spec(ksb.shape), _const_spec(vsb.shape),
                  _const_spec(kwb.shape), _const_spec(vwb.shape),
                  _const_spec(mt.shape), _const_spec(cc.shape),
                  _const_spec(cs.shape), _const_spec(bw.shape)],
        out_specs=pl.BlockSpec((Q_BLOCK, Q_DIM), lambda i: (i, 0)),
        out_shape=jax.ShapeDtypeStruct((t, Q_DIM), BF16),
        scratch_shapes=[pltpu.VMEM((nrow, gq), F32),
                        pltpu.VMEM((n_blocks // BLK_PER_TILE, Q_BLOCK, SEL_TILE), F32),
                        pltpu.VMEM((gq, 1), F32), pltpu.VMEM((gq, 1), F32),
                        pltpu.VMEM((gq, KV_DIM), F32),
                        pltpu.VMEM((N_KV, gq, KV_DIM), F32)],
        compiler_params=_cparams(("arbitrary",)),
        name="nsa_prompt",
    )(qa, ga, kc, vct, ksb, vsb, kwb, vwb, mt, cc, cs, bw)


S_ROWS = 128
S_CHUNK_PAGES = 32
S_CHUNK = S_CHUNK_PAGES * PAGE


def _nsa_sample_kernel(pt_ref, qx_ref, gm_ref, kc_ref, vct_ref, kpool_ref, vpool_ref,
                       knew_ref, vnew_ref, kwin_ref, vwin_ref, kwnew_ref, vwnew_ref,
                       mt_ref, gsum_ref, ccs_ref, css_ref, cns_ref, cws_ref, o_ref,
                       kbuf, vbuf, sem, mask_ref, m_ref, l_ref, acc_ref, oc_ref, ow_ref,
                       *, n_chunks, n_blocks):
    b = pl.program_id(0)
    c = pl.program_id(1)
    step = b * n_chunks + c
    total = pl.num_programs(0) * n_chunks
    slot = step % 2

    def copies(bb, cc, s, p):
        pg = cc * S_CHUNK_PAGES + p
        dst = pl.ds(p * PAGE, PAGE)
        return (pltpu.make_async_copy(kpool_ref.at[pt_ref[bb, pg]], kbuf.at[s, dst], sem.at[0, s]),
                pltpu.make_async_copy(vpool_ref.at[pt_ref[bb, pg]], vbuf.at[s, dst], sem.at[1, s]))

    def start_all(st, s):
        bb = st // n_chunks
        cc = st % n_chunks

        def body(p, carry):
            ck, cv = copies(bb, cc, s, p)
            ck.start()
            cv.start()
            return carry
        lax.fori_loop(0, S_CHUNK_PAGES, body, 0)

    @pl.when(step == 0)
    def _():
        start_all(step, slot)

    @pl.when(step + 1 < total)
    def _():
        start_all(step + 1, 1 - slot)

    qx = qx_ref[0]

    @pl.when(c == 0)
    def _():
        nrow = kc_ref.shape[1]
        s = _nt(kc_ref[0], qx)
        r = lax.broadcasted_iota(jnp.int32, (nrow, S_ROWS), 0)
        band0 = nrow - CMP_BAND
        s = s + jnp.concatenate([jnp.zeros((band0, S_ROWS), F32), ccs_ref[...]], axis=0)
        pn = _softmax_cols(s, r < nrow - 1)
        oc_ref[...] = jnp.dot(vct_ref[0], pn.astype(BF16), preferred_element_type=F32).T
        imp = jnp.dot(pn, gsum_ref[...], preferred_element_type=F32, precision=HIGHEST)
        score = _split_dot(mt_ref[...], imp)
        blk_i = lax.broadcasted_iota(jnp.int32, (n_blocks, S_ROWS), 0)
        forced = (blk_i == 0) | (blk_i == n_blocks - 1)
        score = score + jnp.where(forced, FORCE_BONUS, 0.0)
        selt = _select_topk(score, blk_i.astype(F32), N_SEL - 1)
        selt = lax.dot_general(selt.astype(BF16), gsum_ref[...].astype(BF16), (((1,), (1,)), ((), ())),
                               preferred_element_type=F32)
        selm1 = selt.T - 1.0
        e_r = lax.broadcasted_iota(jnp.int32, (BLK_PER_TILE, SEL_TILE), 0)
        e_c = lax.broadcasted_iota(jnp.int32, (BLK_PER_TILE, SEL_TILE), 1)
        expand = jnp.where(e_c // SEL_BLOCK == e_r, -NEG, 0.0)
        for kt in range(n_blocks // BLK_PER_TILE):
            mask_ref[:, kt * SEL_TILE:(kt + 1) * SEL_TILE] = jnp.dot(
                selm1[:, kt * BLK_PER_TILE:(kt + 1) * BLK_PER_TILE], expand, preferred_element_type=F32)

        sw = _nt(qx, kwin_ref[0].astype(BF16)) + cws_ref[...]
        sn = _nt(qx, kwnew_ref[0].astype(BF16)) + cns_ref[...]
        m = jnp.maximum(jnp.max(sw, axis=1, keepdims=True), jnp.max(sn, axis=1, keepdims=True))
        pw = jnp.exp(sw - m)
        pn2 = jnp.exp(sn - m)
        l = jnp.sum(pw, axis=1, keepdims=True) + jnp.sum(pn2, axis=1, keepdims=True)
        ow = (jnp.dot(pw.astype(BF16), vwin_ref[0].astype(BF16), preferred_element_type=F32)
              + jnp.dot(pn2.astype(BF16), vwnew_ref[0].astype(BF16), preferred_element_type=F32))
        ow_ref[...] = ow / l

        m_ref[...] = jnp.full_like(m_ref, NEG)
        l_ref[...] = jnp.zeros_like(l_ref)
        acc_ref[...] = jnp.zeros_like(acc_ref)

    def wait_body(p, carry):
        ck, cv = copies(b, c, slot, p)
        ck.wait()
        cv.wait()
        return carry
    lax.fori_loop(0, S_CHUNK_PAGES, wait_body, 0)

    col0 = pl.multiple_of(c * S_CHUNK, S_CHUNK)
    s = _nt(qx, kbuf[slot].astype(BF16)) + mask_ref[:, pl.ds(col0, S_CHUNK)]

    @pl.when(c < n_chunks - 1)
    def _():
        _online_update(s, vbuf[slot].astype(BF16), m_ref, l_ref, acc_ref)

    @pl.when(c == n_chunks - 1)
    def _():
        near = jnp.concatenate([jnp.zeros((S_ROWS, S_CHUNK - LANES), F32), css_ref[...]], axis=1)
        _online_update(s + near, vbuf[slot].astype(BF16), m_ref, l_ref, acc_ref)
        sn = _nt(qx, knew_ref[0].astype(BF16)) + cns_ref[...]
        _online_update(sn, vnew_ref[0].astype(BF16), m_ref, l_ref, acc_ref)
        o_s = acc_ref[...] / l_ref[...]
        o_ref[0] = gm_ref[0, 0] * oc_ref[...] + gm_ref[0, 1] * o_s + gm_ref[0, 2] * ow_ref[...]


def _nsa_sample(page_table, qx, gm, kc, vct, kpool, vpool, knew, vnew, kwin, vwin, kwnew, vwnew,
                mt, gsum, ccs, css, cns, cws):
    nb = qx.shape[0]
    n_pages = page_table.shape[1]
    n_chunks = n_pages // S_CHUNK_PAGES
    past = n_pages * PAGE
    n_blocks = past // SEL_BLOCK
    nrow = kc.shape[1]

    def bspec(shape):
        nd = len(shape)
        return pl.BlockSpec((1,) + tuple(shape[1:]), lambda b, c, pt: (b,) + (0,) * (nd - 1))

    def cspec(shape):
        nd = len(shape)
        return pl.BlockSpec(tuple(shape), lambda b, c, pt: (0,) * nd, pipeline_mode=pl.Buffered(1))

    grid_spec = pltpu.PrefetchScalarGridSpec(
        num_scalar_prefetch=1,
        grid=(nb, n_chunks),
        in_specs=[bspec(qx.shape), bspec(gm.shape), bspec(kc.shape), bspec(vct.shape),
                  pl.BlockSpec(memory_space=pl.ANY), pl.BlockSpec(memory_space=pl.ANY),
                  bspec(knew.shape), bspec(vnew.shape), bspec(kwin.shape), bspec(vwin.shape),
                  bspec(kwnew.shape), bspec(vwnew.shape),
                  cspec(mt.shape), cspec(gsum.shape), cspec(ccs.shape), cspec(css.shape),
                  cspec(cns.shape), cspec(cws.shape)],
        out_specs=pl.BlockSpec((1, S_ROWS, KV_DIM), lambda b, c, pt: (b, 0, 0)),
        scratch_shapes=[pltpu.VMEM((2, S_CHUNK, KV_DIM), F32),
                        pltpu.VMEM((2, S_CHUNK, KV_DIM), F32),
                        pltpu.SemaphoreType.DMA((2, 2)),
                        pltpu.VMEM((S_ROWS, past), F32),
                        pltpu.VMEM((S_ROWS, 1), F32), pltpu.VMEM((S_ROWS, 1), F32),
                        pltpu.VMEM((S_ROWS, KV_DIM), F32),
                        pltpu.VMEM((S_ROWS, KV_DIM), F32), pltpu.VMEM((S_ROWS, KV_DIM), F32)],
    )
    return pl.pallas_call(
        functools.partial(_nsa_sample_kernel, n_chunks=n_chunks, n_blocks=n_blocks),
        grid_spec=grid_spec,
        out_shape=jax.ShapeDtypeStruct((nb, S_ROWS, KV_DIM), F32),
        compiler_params=_cparams(("arbitrary", "arbitrary")),
        name="nsa_sample",
    )(page_table, qx, gm, kc, vct, kpool, vpool, knew, vnew, kwin, vwin, kwnew, vwnew,
      mt, gsum, ccs, css, cns, cws)


def _ffn_kernel(x_ref, oa_ref, ob_ref, sa_ref, sb_ref, wpa_ref, wpb_ref, wo_ref, nf_ref,
                wg_ref, wu_ref, wd_ref, nl_ref, y_ref):
    pa = jnp.dot(oa_ref[...], wpa_ref[...], preferred_element_type=F32)
    pb = jnp.dot(ob_ref[...], wpb_ref[...], preferred_element_type=F32)
    merged = sa_ref[...] * pa + sb_ref[...] * pb
    x = x_ref[...] + jnp.dot(merged.astype(BF16), wo_ref[...], preferred_element_type=F32)
    hn = (x * lax.rsqrt(jnp.mean(x * x, axis=-1, keepdims=True) + EPS) * nf_ref[...]).astype(BF16)
    gate = jnp.dot(hn, wg_ref[...], preferred_element_type=F32)
    up = jnp.dot(hn, wu_ref[...], preferred_element_type=F32)
    ff = (jax.nn.silu(gate) * up).astype(BF16)
    x = x + jnp.dot(ff, wd_ref[...], preferred_element_type=F32)
    y_ref[...] = x * lax.rsqrt(jnp.mean(x * x, axis=-1, keepdims=True) + EPS) * nl_ref[...]


def _ffn(x2d, oa, ob, sa, sb, wpa, wpb, wo, nf, wg, wu, wd, nl, tm):
    rows = x2d.shape[0]

    def rspec(n):
        return pl.BlockSpec((tm, n), lambda i: (i, 0))

    return pl.pallas_call(
        _ffn_kernel,
        grid=(rows // tm,),
        in_specs=[rspec(D_MODEL), rspec(Q_DIM), rspec(HG_W), rspec(D_MODEL), rspec(D_MODEL),
                  _const_spec(wpa.shape), _const_spec(wpb.shape), _const_spec(wo.shape),
                  _const_spec(nf.shape), _const_spec(wg.shape), _const_spec(wu.shape),
                  _const_spec(wd.shape), _const_spec(nl.shape)],
        out_specs=rspec(D_MODEL),
        out_shape=jax.ShapeDtypeStruct((rows, D_MODEL), F32),
        compiler_params=_cparams(("arbitrary",)),
        name="ffn",
    )(x2d, oa, ob, sa, sb, wpa, wpb, wo, nf, wg, wu, wd, nl)


def _pack_w_in(w_in):
    sizes = (Q_DIM,) + (KV_DIM,) * 6 + (3 * N_HEADS,) + (HG_W,) * 4 + (D_MODEL,) * 2
    offs = np.concatenate([[0], np.cumsum(sizes)])
    q = w_in[:, offs[0]:offs[1]].reshape(D_MODEL, N_HEADS, HEAD_DIM)[:, _HEAD_PERM, :].reshape(D_MODEL, Q_DIM)
    g = jnp.pad(w_in[:, offs[7]:offs[8]], ((0, 0), (0, LANES - 3 * N_HEADS)))
    return jnp.concatenate([q, w_in[:, offs[1]:offs[7]], g, w_in[:, offs[8]:]], axis=1).astype(BF16)


def _strip(bvc, rel, lo=0, hi=None, masked=NEG):
    val = bvc[:, np.clip(rel, 0, 255)]
    ok = rel >= lo
    if hi is not None:
        ok = ok & (rel < hi)
    return jnp.where(jnp.asarray(ok)[None], val, masked)


def _bias_strips_prompt(bvc):
    jq = np.arange(Q_BLOCK)[:, None]
    gq = GROUP * Q_BLOCK
    i = np.arange(Q_BLOCK)[None, :]
    rel_d = np.stack([Q_BLOCK * d + jq - i for d in (-1, 0, 1, 2)])
    cs = _strip(bvc, rel_d).reshape(N_KV, GROUP, 4, Q_BLOCK, Q_BLOCK)
    cs = cs.transpose(0, 2, 1, 3, 4).reshape(N_KV, 4, gq, Q_BLOCK)
    iw = np.arange(WINDOW + Q_BLOCK)[None, :]
    bw = _strip(bvc, WINDOW + jq - iw, 0, WINDOW).reshape(N_KV, gq, WINDOW + Q_BLOCK)
    rr = np.arange(CMP_BAND)[:, None]
    rel_c = np.arange(Q_BLOCK)[None, :] - CMP_STRIDE * (rr - CMP_PAD) - (CMP_BLOCK - 1)
    cc = _strip(bvc, rel_c, masked=0.0).reshape(N_KV, GROUP, CMP_BAND, Q_BLOCK)
    cc = cc.transpose(0, 2, 1, 3).reshape(N_KV, CMP_BAND, gq)
    return cc, cs, bw


def _bias_strips_sample(bvc, past, t):
    def rows(a):
        a = a.reshape(N_HEADS * t, a.shape[-1])
        return jnp.pad(a, ((0, S_ROWS - N_HEADS * t), (0, 0)))
    tt = np.arange(t)[:, None]
    nrow = past // CMP_STRIDE
    n = (nrow - CMP_BAND + np.arange(CMP_BAND))[None, :]
    ccs = rows(_strip(bvc, past + tt - CMP_STRIDE * n - (CMP_BLOCK - 1), masked=0.0)).T
    i = np.arange(LANES)[None, :]
    css = rows(_strip(bvc, LANES + tt - i))
    cns = rows(_strip(bvc, np.where(i < t, tt - i, -1)))
    iw = np.arange(WINDOW)[None, :]
    cws = rows(_strip(bvc, WINDOW + tt - iw, 0, WINDOW))
    return ccs, css, cns, cws


def kernel(x_prompt, x_sample, cache_k_cmp, cache_v_cmp, cache_k_slc, cache_v_slc, state_k_win, state_v_win,
           state_hgrn, page_table, norm_mix, w_in, cmp_pe_k, cmp_w1_k, cmp_w2_k, cmp_pe_v, cmp_w1_v, cmp_w2_v,
           rel_bias, hg_lb_logits, hg_norm, w_proj_a, w_proj_b, w_out, norm_ffn, w_gate, w_up, w_down, norm_final):
    nbp, t_p, _ = x_prompt.shape
    nbs, t_s, _ = x_sample.shape
    assert nbp == 1 and norm_mix.shape[0] == 1
    n_pages = page_table.shape[1]
    past = n_pages * PAGE
    assert state_k_win.shape[2] == WINDOW and past % S_CHUNK == 0 and t_s <= SUBLANES

    lb = jnp.cumsum(jax.nn.softmax(hg_lb_logits.astype(F32), axis=0), axis=0)[0]
    lb3 = jnp.pad(jnp.stack([jnp.log(lb), jnp.log1p(-lb), 1.0 - lb]), ((0, SUBLANES - 3), (0, 0)))
    w_pack = _pack_w_in(w_in[0])
    g_mix = norm_mix[0][None, :]
    wpa = w_proj_a[0].reshape(N_HEADS, HEAD_DIM, D_MODEL)[_HEAD_PERM].reshape(Q_DIM, D_MODEL).astype(BF16)
    wpb = w_proj_b[0].astype(BF16)
    wo = w_out[0].astype(BF16)
    wg, wu, wd = w_gate[0].astype(BF16), w_up[0].astype(BF16), w_down[0].astype(BF16)
    nf, nl = norm_ffn[0][None, :], norm_final[None, :]
    gn = hg_norm[0][None, :]
    pe_k, w1_k, w2_k = _compress_weights(cmp_pe_k[0], cmp_w1_k[0], cmp_w2_k[0])
    pe_v, w1_v, w2_v = _compress_weights(cmp_pe_v[0], cmp_w1_v[0], cmp_w2_v[0])
    bvc = (rel_bias[_BUCKET] - rel_bias[N_BUCKETS - 1][None, :]).T
    cc, cs, bw = _bias_strips_prompt(bvc)
    ccs, css, cns, cws = _bias_strips_sample(bvc, past, t_s)

    xp2 = x_prompt.reshape(t_p, D_MODEL)
    xs2 = x_sample.reshape(nbs * t_s, D_MODEL)
    pp = _proj(xp2, g_mix, w_pack, lb3, 256)
    ps = _proj(xs2, g_mix, w_pack, lb3, nbs * t_s)
    (qa_p, kc_p, vc_p, ks_p, vs_p, kw_p, vw_p, ksb_p, vsb_p, kwb_p, vwb_p, ga_p,
     qh_p, lf_p, kh_p, vh_p, gs_p, sa_p, sb_p) = pp
    (qa_s, kc_s, vc_s, ks_s, vs_s, kw_s, vw_s, _, _, _, _, ga_s,
     qh_s, lf_s, kh_s, vh_s, gs_s, sa_s, sb_s) = ps

    ob_p, s_p = _hgrn_prompt(qh_p, kh_p, vh_p, lf_p, gs_p, gn)
    r3 = lambda a: a.reshape(nbs, t_s, a.shape[-1])
    ob_s, s_s = _hgrn_sample(r3(qh_s), r3(kh_s), r3(vh_s), r3(lf_s), r3(gs_s), gn, state_hgrn[0])

    ident = jnp.arange(t_p // PAGE, dtype=jnp.int32)[None, :]
    pool_rows = lambda a: a.reshape(-1, _CH_PER_PAGE, _CH_W)
    kcb_p, _ = _compress(pool_rows(kc_p), ident, pe_k, w1_k, w2_k)
    _, vct_p = _compress(pool_rows(vc_p), ident, pe_v, w1_v, w2_v)
    kcb_p = jnp.pad(kcb_p[0], ((CMP_PAD, 0), (0, 0)))
    vct_p = jnp.pad(vct_p[0], ((0, 0), (CMP_PAD, 0)))
    nrow_p = kcb_p.shape[0]
    mt_p = jnp.asarray(_score_matrix(nrow_p, CMP_PAD, t_p // SEL_BLOCK), BF16)
    kwb_pad = jnp.pad(kwb_p, ((WINDOW, 0), (0, 0)))
    vwb_pad = jnp.pad(vwb_p, ((WINDOW, 0), (0, 0)))
    oa_p = _nsa_prompt(qa_p, ga_p, kcb_p, vct_p, ksb_p, vsb_p, kwb_pad, vwb_pad, mt_p, cc, cs, bw)

    kcb_s, _ = _compress(pool_rows(cache_k_cmp[0]), page_table, pe_k, w1_k, w2_k)
    _, vct_s = _compress(pool_rows(cache_v_cmp[0]), page_table, pe_v, w1_v, w2_v)
    nq = N_HEADS * t_s
    qs4 = qa_s.reshape(nbs, t_s, GROUP, N_KV, HEAD_DIM).astype(F32)
    qx = jnp.einsum('btjkd,kq->bkjtqd', qs4, jnp.eye(N_KV, dtype=F32)).reshape(nbs, nq, KV_DIM)
    qx = jnp.pad(qx, ((0, 0), (0, S_ROWS - nq), (0, 0))).astype(BF16)
    g4 = ga_s[:, :3 * N_HEADS].reshape(nbs, t_s, N_KV, GROUP, 3)
    gm = jnp.transpose(g4, (0, 4, 2, 3, 1)).reshape(nbs, 3, nq, 1)
    gm = jnp.broadcast_to(jnp.pad(gm, ((0, 0), (0, 0), (0, S_ROWS - nq), (0, 0))), (nbs, 3, S_ROWS, KV_DIM))
    new_tile = lambda a: jnp.pad(a.reshape(nbs, t_s, KV_DIM), ((0, 0), (0, LANES - t_s), (0, 0)))
    mt_s = jnp.asarray(_score_matrix(past // CMP_STRIDE, 0, past // SEL_BLOCK), BF16)
    rr = np.arange(S_ROWS)
    gsum = np.zeros((S_ROWS, S_ROWS), np.float32)
    for kv in range(N_KV):
        for j in range(GROUP):
            for t in range(t_s):
                gsum[(kv * GROUP + j) * t_s + t, kv * t_s + t] = 1.0
    o_kv = _nsa_sample(page_table, qx, gm, kcb_s, vct_s,
                       cache_k_slc[0].reshape(-1, PAGE, KV_DIM), cache_v_slc[0].reshape(-1, PAGE, KV_DIM),
                       new_tile(ks_s), new_tile(vs_s), state_k_win[0].reshape(nbs, WINDOW, KV_DIM),
                       state_v_win[0].reshape(nbs, WINDOW, KV_DIM), new_tile(kw_s), new_tile(vw_s),
                       mt_s, jnp.asarray(gsum), ccs, css, cns, cws)
    o5 = o_kv[:, :nq].reshape(nbs, N_KV, GROUP, t_s, N_KV, HEAD_DIM)
    oa_s = jnp.einsum('bkjtqd,kq->btjkd', o5, jnp.eye(N_KV, dtype=F32)).reshape(nbs * t_s, Q_DIM).astype(BF16)

    y_p = _ffn(xp2, oa_p, ob_p, sa_p, sb_p, wpa, wpb, wo, nf, wg, wu, wd, nl, 256)
    y_s = _ffn(xs2, oa_s, ob_s.reshape(nbs * t_s, HG_W), sa_s, sb_s, wpa, wpb, wo, nf, wg, wu, wd, nl, nbs * t_s)

    kv5 = lambda a, nb_, tt: a.reshape(1, nb_, tt, N_KV, HEAD_DIM)
    wl = min(WINDOW, t_p)
    win = lambda st, new: jnp.concatenate(
        [st[0], new.reshape(nbs, t_s, N_KV, HEAD_DIM)], axis=1)[:, -WINDOW:][None]
    return (y_p.reshape(1, t_p, D_MODEL), y_s.reshape(nbs, t_s, D_MODEL),
            kv5(kc_p, 1, t_p), kv5(vc_p, 1, t_p), kv5(ks_p, 1, t_p), kv5(vs_p, 1, t_p),
            kv5(kw_p[-wl:], 1, wl), kv5(vw_p[-wl:], 1, wl), s_p[None, None],
            kv5(kc_s, nbs, t_s), kv5(vc_s, nbs, t_s), kv5(ks_s, nbs, t_s), kv5(vs_s, nbs, t_s),
            win(state_k_win, kw_s), win(state_v_win, vw_s), s_s[None])
```

```python
import functools
import math

import numpy as np
import jax
import jax.numpy as jnp
from jax import lax
from jax.experimental import pallas as pl
from jax.experimental.pallas import tpu as pltpu

F32 = jnp.float32
BF16 = jnp.bfloat16
HIGHEST = lax.Precision.HIGHEST

D_MODEL = 1024
N_HEADS = 8
N_KV = 2
GROUP = N_HEADS // N_KV
HEAD_DIM = 64
KV_DIM = N_KV * HEAD_DIM
Q_DIM = N_HEADS * HEAD_DIM
CMP_BLOCK = 32
CMP_STRIDE = 16
CMP_HIDDEN = 2 * HEAD_DIM
SEL_BLOCK = 64
N_SEL = 16
WINDOW = 512
Q_BLOCK = 128
FORCE_BONUS = 1e4
N_BUCKETS = 32
MAX_DISTANCE = 128
HG_HEADS = 4
HG_DIM = 128
HG_CHUNK = 64
HG_SUB = 16
HG_W = HG_HEADS * HG_DIM
D_FF = ((8 * D_MODEL // 3 + 255) // 256) * 256
EPS = 1e-6
PAGE = 128
NEG = -1e30

LANES = 128
SUBLANES = 8
VMEM_LIMIT = 56 * 1024 * 1024

_OFF_Q = 0
_OFF_KV = _OFF_Q + Q_DIM
_OFF_G = _OFF_KV + 6 * KV_DIM
_OFF_HG = _OFF_G + LANES
_OFF_GATE = _OFF_HG + 4 * HG_W
_PROJ_N = _OFF_GATE + 2 * D_MODEL

_HEAD_PERM = np.array([h for j in range(GROUP) for h in (j, GROUP + j)])

SEL_TILE = 512
BLK_PER_TILE = SEL_TILE // SEL_BLOCK
CMP_PAD = 16
CMP_BAND = 24


def _cparams(sem, vmem=VMEM_LIMIT):
    return pltpu.CompilerParams(dimension_semantics=sem, vmem_limit_bytes=vmem)


def _const_spec(shape):
    nd = len(shape)
    return pl.BlockSpec(shape, lambda *_: (0,) * nd, pipeline_mode=pl.Buffered(1))


def _bucket_table():
    n = np.arange(256)
    max_exact = N_BUCKETS // 2
    nf = np.maximum(n, 1).astype(np.float64)
    large = max_exact + (np.log(nf / max_exact) / math.log(MAX_DISTANCE / max_exact)
                         * (N_BUCKETS - max_exact)).astype(np.int64)
    large = np.minimum(large, N_BUCKETS - 1)
    return np.where(n < max_exact, n, large)


_BUCKET = _bucket_table()


def _proj_kernel(x_ref, g_ref, w_ref, wt_ref, lb_ref,
                 qa_ref, kc_ref, vc_ref, ks_ref, vs_ref, kw_ref, vw_ref,
                 ksb_ref, kwb_ref, ga_ref,
                 qh_ref, lf_ref, kh_ref, vh_ref, gs_ref, sa_ref, sb_ref,
                 qat_ref, gat_ref, vst_ref, vwt_ref):
    x = x_ref[...]
    xn = x * lax.rsqrt(jnp.mean(x * x, axis=-1, keepdims=True) + EPS) * g_ref[...]
    xb = xn.astype(BF16)

    def seg(a, n):
        return jnp.dot(xb, w_ref[:, a:a + n], preferred_element_type=F32)

    qa_ref[...] = (seg(_OFF_Q, Q_DIM) * (HEAD_DIM ** -0.5)).astype(BF16)
    f32_refs = (kc_ref, vc_ref, ks_ref, vs_ref, kw_ref, vw_ref)
    b16_refs = (None, None, ksb_ref, None, kwb_ref, None)
    for i in range(6):
        u = seg(_OFF_KV + i * KV_DIM, KV_DIM)
        f32_refs[i][...] = u
        if b16_refs[i] is not None:
            b16_refs[i][...] = u.astype(BF16)
    ga_ref[...] = jax.nn.sigmoid(seg(_OFF_G, LANES))

    def seg_t(a, n):
        return lax.dot_general(wt_ref[a:a + n, :], xb, (((1,), (1,)), ((), ())), preferred_element_type=F32)

    qat_ref[...] = (seg_t(0, Q_DIM) * (HEAD_DIM ** -0.5)).astype(BF16)
    gat_ref[...] = jax.nn.sigmoid(seg_t(Q_DIM, LANES))
    vst_ref[...] = seg_t(Q_DIM + LANES, KV_DIM).astype(BF16)
    vwt_ref[...] = seg_t(Q_DIM + LANES + KV_DIM, KV_DIM).astype(BF16)

    log_lb = lb_ref[0:1, :]
    log_1m = lb_ref[1:2, :]
    one_m = lb_ref[2:3, :]
    qh_ref[...] = jax.nn.silu(seg(_OFF_HG, HG_W))
    z = seg(_OFF_HG + HG_W, HG_W)
    b = log_1m + (jnp.minimum(z, 0.0) - jnp.log1p(jnp.exp(-jnp.abs(z))))
    hi = jnp.maximum(log_lb, b)
    lf_ref[...] = hi + jnp.log1p(jnp.exp(-jnp.abs(log_lb - b)))
    kh_ref[...] = one_m * jax.nn.sigmoid(-z)
    vh_ref[...] = seg(_OFF_HG + 2 * HG_W, HG_W)
    gs_ref[...] = jax.nn.silu(seg(_OFF_HG + 3 * HG_W, HG_W))
    sa_ref[...] = jax.nn.sigmoid(seg(_OFF_GATE, D_MODEL))
    sb_ref[...] = jax.nn.sigmoid(seg(_OFF_GATE + D_MODEL, D_MODEL))


_PROJ_T = Q_DIM + LANES + 2 * KV_DIM


def _proj(x2d, g, w, wt, lb3, tm):
    rows = x2d.shape[0]
    widths = ([(Q_DIM, BF16)] + [(KV_DIM, F32)] * 6 + [(KV_DIM, BF16)] * 2 + [(LANES, F32)]
              + [(HG_W, F32)] * 5 + [(D_MODEL, F32)] * 2)
    heights = [(Q_DIM, BF16), (LANES, F32), (KV_DIM, BF16), (KV_DIM, BF16)]
    return pl.pallas_call(
        _proj_kernel,
        grid=(rows // tm,),
        in_specs=[pl.BlockSpec((tm, D_MODEL), lambda i: (i, 0)),
                  _const_spec((1, D_MODEL)),
                  _const_spec((D_MODEL, _PROJ_N)),
                  _const_spec((_PROJ_T, D_MODEL)),
                  _const_spec((SUBLANES, HG_W))],
        out_specs=([pl.BlockSpec((tm, n), lambda i: (i, 0)) for n, _ in widths]
                   + [pl.BlockSpec((n, tm), lambda i: (0, i)) for n, _ in heights]),
        out_shape=([jax.ShapeDtypeStruct((rows, n), dt) for n, dt in widths]
                   + [jax.ShapeDtypeStruct((n, rows), dt) for n, dt in heights]),
        compiler_params=_cparams(("arbitrary",)),
        name="proj",
    )(x2d, g, w, wt, lb3)


_CH_W = CMP_STRIDE * KV_DIM
_CH_PER_PAGE = PAGE // CMP_STRIDE


def _compress_kernel(pt_ref, pool_ref, pe_ref, w1_ref, w2_ref, out_ref, outt_ref, buf, hbuf, sem,
                     *, n_pages):
    b = pl.program_id(0)
    nb = pl.num_programs(0)
    slot = b % 2
    n_ch = n_pages * _CH_PER_PAGE

    def page_copy(bb, p, s):
        return pltpu.make_async_copy(pool_ref.at[pt_ref[bb, p]],
                                     buf.at[s, pl.ds(p * _CH_PER_PAGE, _CH_PER_PAGE)],
                                     sem.at[s])

    def start_all(bb, s):
        def body(p, c):
            page_copy(bb, p, s).start()
            return c
        lax.fori_loop(0, n_pages, body, 0)

    @pl.when(b == 0)
    def _():
        start_all(b, slot)

    @pl.when(b + 1 < nb)
    def _():
        start_all(b + 1, 1 - slot)

    def wait_body(p, c):
        page_copy(b, p, slot).wait()
        return c
    lax.fori_loop(0, n_pages, wait_body, 0)

    rows = math.gcd(n_ch, 256)
    for j in range(2):
        pe = pe_ref[j:j + 1, :]
        for r in range(n_ch // rows):
            xa = (buf[slot, r * rows:(r + 1) * rows, :] + pe).astype(BF16)
            hbuf[j, r * rows:(r + 1) * rows, :] = jnp.dot(xa, w1_ref[j], preferred_element_type=F32)
    h = hbuf[0] + pltpu.roll(hbuf[1], n_ch - 1, 0)
    blocks = jnp.dot(jax.nn.gelu(h).astype(BF16), w2_ref[...], preferred_element_type=F32)
    row = lax.broadcasted_iota(jnp.int32, blocks.shape, 0)
    blocks = jnp.where(row < n_ch - 1, blocks, 0.0)
    out_ref[0] = blocks.astype(BF16)
    outt_ref[0] = blocks.T.astype(BF16)


def _compress(pool, page_table, pe, w1, w2):
    nbatch, n_pages = page_table.shape
    n_ch = n_pages * _CH_PER_PAGE
    grid_spec = pltpu.PrefetchScalarGridSpec(
        num_scalar_prefetch=1,
        grid=(nbatch,),
        in_specs=[pl.BlockSpec(memory_space=pl.ANY),
                  _const_spec((SUBLANES, _CH_W)),
                  _const_spec((2, _CH_W, 2 * CMP_HIDDEN)),
                  _const_spec((2 * CMP_HIDDEN, KV_DIM))],
        out_specs=[pl.BlockSpec((1, n_ch, KV_DIM), lambda b, pt: (b, 0, 0)),
                   pl.BlockSpec((1, KV_DIM, n_ch), lambda b, pt: (b, 0, 0))],
        scratch_shapes=[pltpu.VMEM((2, n_ch, _CH_W), F32),
                        pltpu.VMEM((2, n_ch, 2 * CMP_HIDDEN), F32),
                        pltpu.SemaphoreType.DMA((2,))],
    )
    return pl.pallas_call(
        functools.partial(_compress_kernel, n_pages=n_pages),
        grid_spec=grid_spec,
        out_shape=[jax.ShapeDtypeStruct((nbatch, n_ch, KV_DIM), BF16),
                   jax.ShapeDtypeStruct((nbatch, KV_DIM, n_ch), BF16)],
        compiler_params=_cparams(("arbitrary",)),
        name="compress",
    )(page_table, pool, pe, w1, w2)


def _compress_weights(pe, w1, w2):
    c = CMP_BLOCK // CMP_STRIDE
    pe_r = pe.reshape(c, CMP_STRIDE, 1, HEAD_DIM)
    pe_x = jnp.broadcast_to(pe_r, (c, CMP_STRIDE, N_KV, HEAD_DIM)).reshape(c, _CH_W)
    pe_x = jnp.pad(pe_x, ((0, SUBLANES - c), (0, 0)))
    w1_r = w1.reshape(c, CMP_STRIDE, HEAD_DIM, CMP_HIDDEN)
    eye = jnp.eye(N_KV, dtype=w1.dtype)
    w1_x = jnp.einsum('jsde,kq->jskdqe', w1_r, eye).reshape(c, _CH_W, N_KV * CMP_HIDDEN)
    w2_x = jnp.einsum('ed,kq->keqd', w2, eye).reshape(N_KV * CMP_HIDDEN, KV_DIM)
    return pe_x, w1_x.astype(BF16), w2_x.astype(BF16)


def _hgrn_chunk(q, k, v, lf, st_ref, chunk, sub):
    if chunk > SUBLANES:
        r = lax.broadcasted_iota(jnp.int32, (chunk, chunk), 0)
        c = lax.broadcasted_iota(jnp.int32, (chunk, chunk), 1)
        tri = (r >= c).astype(F32)
        b = jnp.dot(tri, lf, preferred_element_type=F32, precision=HIGHEST)
    else:
        rows = [lf[0:1, :]]
        for t in range(1, chunk):
            rows.append(rows[-1] + lf[t:t + 1, :])
        b = jnp.concatenate(rows, axis=0)
    bl = b[chunk - 1:chunk, :]
    qe = q * jnp.exp(b)
    kd = k * jnp.exp(bl - b)
    ebl = jnp.exp(bl)
    n_sub = chunk // sub
    trow = lax.broadcasted_iota(jnp.int32, (sub, HG_W), 0)
    crow = lax.broadcasted_iota(jnp.int32, (chunk, HG_W), 0)

    diag = []
    for i in range(n_sub):
        qi = q[i * sub:(i + 1) * sub, :]
        bi = b[i * sub:(i + 1) * sub, :]
        acc = [jnp.zeros((sub, HG_DIM), F32) for _ in range(HG_HEADS)]
        for s in range(sub):
            row = i * sub + s
            dec = jnp.exp(jnp.where(trow >= s, bi - b[row:row + 1, :], -jnp.inf))
            prod = qi * k[row:row + 1, :] * dec
            for h in range(HG_HEADS):
                a = jnp.sum(prod[:, h * HG_DIM:(h + 1) * HG_DIM], axis=1, keepdims=True)
                acc[h] = acc[h] + a * v[row:row + 1, h * HG_DIM:(h + 1) * HG_DIM]
        diag.append(acc)

    off = []
    for i in range(n_sub):
        if i == 0:
            off.append(None)
            continue
        b0 = b[i * sub - 1:i * sub, :]
        qs = (q[i * sub:(i + 1) * sub, :] * jnp.exp(b[i * sub:(i + 1) * sub, :] - b0)).astype(BF16)
        ks = (k * jnp.exp(jnp.where(crow < i * sub, b0 - b, -jnp.inf))).astype(BF16)
        off.append((qs, ks))

    vb = v.astype(BF16)
    outs = []
    for h in range(HG_HEADS):
        sl = slice(h * HG_DIM, (h + 1) * HG_DIM)
        st = st_ref[h]
        o_h = lax.dot_general(qe[:, sl].astype(BF16), st.astype(BF16), (((1,), (1,)), ((), ())),
                              preferred_element_type=F32)
        parts = []
        for i in range(n_sub):
            d = diag[i][h]
            if off[i] is not None:
                qs, ks = off[i]
                a = lax.dot_general(qs[:, sl], ks[:, sl], (((1,), (1,)), ((), ())),
                                    preferred_element_type=F32)
                d = d + jnp.dot(a.astype(BF16), vb[:, sl], preferred_element_type=F32)
            parts.append(d)
        intra = parts[0] if n_sub == 1 else jnp.concatenate(parts, axis=0)
        outs.append(o_h + intra)
        st_ref[h] = st * ebl[:, sl] + lax.dot_general(
            vb[:, sl], kd[:, sl].astype(BF16), (((0,), (0,)), ((), ())), preferred_element_type=F32)
    return jnp.concatenate(outs, axis=1)


def _hgrn_finish(o, gs, gn):
    outs = []
    for h in range(HG_HEADS):
        oh = o[:, h * HG_DIM:(h + 1) * HG_DIM]
        y = oh * lax.rsqrt(jnp.mean(oh * oh, axis=-1, keepdims=True) + EPS) * gn
        outs.append(y)
    return (jnp.concatenate(outs, axis=1) * gs).astype(BF16)


def _hgrn_prompt_kernel(q_ref, k_ref, v_ref, lf_ref, gs_ref, gn_ref, o_ref, s_ref, st_ref, *, n_chunks):
    i = pl.program_id(0)

    @pl.when(i == 0)
    def _():
        st_ref[...] = jnp.zeros_like(st_ref)

    def body(c, carry):
        r = pl.ds(pl.multiple_of(c * HG_CHUNK, HG_CHUNK), HG_CHUNK)
        o = _hgrn_chunk(q_ref[r, :], k_ref[r, :], v_ref[r, :], lf_ref[r, :], st_ref, HG_CHUNK, HG_SUB)
        o_ref[r, :] = _hgrn_finish(o, gs_ref[r, :], gn_ref[...])
        return carry
    lax.fori_loop(0, n_chunks, body, 0)

    @pl.when(i == pl.num_programs(0) - 1)
    def _():
        for h in range(HG_HEADS):
            s_ref[h] = st_ref[h].T


def _hgrn_prompt(qh, kh, vh, lf, gs, gn, rows_per_step=512):
    t = qh.shape[0]
    spec = pl.BlockSpec((rows_per_step, HG_W), lambda i: (i, 0))
    return pl.pallas_call(
        functools.partial(_hgrn_prompt_kernel, n_chunks=rows_per_step // HG_CHUNK),
        grid=(t // rows_per_step,),
        in_specs=[spec] * 5 + [_const_spec((1, HG_DIM))],
        out_specs=[spec, pl.BlockSpec((HG_HEADS, HG_DIM, HG_DIM), lambda i: (0, 0, 0))],
        out_shape=[jax.ShapeDtypeStruct((t, HG_W), BF16),
                   jax.ShapeDtypeStruct((HG_HEADS, HG_DIM, HG_DIM), F32)],
        scratch_shapes=[pltpu.VMEM((HG_HEADS, HG_DIM, HG_DIM), F32)],
        compiler_params=_cparams(("arbitrary",)),
        name="hgrn_prompt",
    )(qh, kh, vh, lf, gs, gn)


def _hgrn_sample_kernel(q_ref, k_ref, v_ref, lf_ref, gs_ref, gn_ref, s0_ref, o_ref, s_ref, st_ref, *, t):
    for h in range(HG_HEADS):
        st_ref[h] = s0_ref[0, h].T
    o = _hgrn_chunk(q_ref[0], k_ref[0], v_ref[0], lf_ref[0], st_ref, t, t)
    o_ref[0] = _hgrn_finish(o, gs_ref[0], gn_ref[...])
    for h in range(HG_HEADS):
        s_ref[0, h] = st_ref[h].T


def _hgrn_sample(qh, kh, vh, lf, gs, gn, s0):
    nb, t, _ = qh.shape
    spec = pl.BlockSpec((1, t, HG_W), lambda b: (b, 0, 0))
    sspec = pl.BlockSpec((1, HG_HEADS, HG_DIM, HG_DIM), lambda b: (b, 0, 0, 0))
    return pl.pallas_call(
        functools.partial(_hgrn_sample_kernel, t=t),
        grid=(nb,),
        in_specs=[spec] * 5 + [_const_spec((1, HG_DIM)), sspec],
        out_specs=[spec, sspec],
        out_shape=[jax.ShapeDtypeStruct((nb, t, HG_W), BF16),
                   jax.ShapeDtypeStruct((nb, HG_HEADS, HG_DIM, HG_DIM), F32)],
        scratch_shapes=[pltpu.VMEM((HG_HEADS, HG_DIM, HG_DIM), F32)],
        compiler_params=_cparams(("arbitrary",)),
        name="hgrn_sample",
    )(qh, kh, vh, lf, gs, gn, s0)


def _select_topk(x, blk, n):
    nblk = x.shape[0]
    sel = jnp.zeros_like(x)
    for _ in range(n):
        m = jnp.max(x, axis=0, keepdims=True)
        idx = jnp.min(jnp.where(x == m, blk, float(nblk)), axis=0, keepdims=True)
        pick = blk == idx
        sel = jnp.where(pick, 1.0, sel)
        x = jnp.where(pick, -3e38, x)
    return sel


def _softmax_cols(s, valid):
    m = jnp.max(jnp.where(valid, s, NEG), axis=0, keepdims=True)
    p = jnp.where(valid, jnp.exp(s - m), 0.0)
    l = jnp.sum(p, axis=0, keepdims=True)
    return p * jnp.where(l > 0.0, 1.0 / l, 0.0)


def _split_dot(a_bf16, x):
    hi = x.astype(BF16)
    lo = (x - hi.astype(F32)).astype(BF16)
    return (jnp.dot(a_bf16, hi, preferred_element_type=F32)
            + jnp.dot(a_bf16, lo, preferred_element_type=F32))


def _nt(a, b):
    return lax.dot_general(a, b, (((1,), (1,)), ((), ())), preferred_element_type=F32)


def _online_update(s, v, m_ref, l_ref, acc_ref, v_transposed=False):
    m_old = m_ref[...]
    m_new = jnp.maximum(m_old, jnp.max(s, axis=1, keepdims=True))
    p = jnp.exp(s - m_new)
    alpha = jnp.exp(m_old - m_new)
    l_ref[...] = alpha * l_ref[...] + jnp.sum(p, axis=1, keepdims=True)
    pv = _nt(p.astype(BF16), v) if v_transposed else jnp.dot(p.astype(BF16), v, preferred_element_type=F32)
    acc_ref[...] = alpha * acc_ref[...] + pv
    m_ref[...] = m_new


def _score_matrix(n_rows, row_offset, n_blocks):
    c = CMP_BLOCK // CMP_STRIDE
    ratio = SEL_BLOCK // CMP_STRIDE
    n_ov = ratio + c - 1
    m = np.zeros((n_blocks, n_rows), np.float32)
    for j in range(n_blocks):
        for u in range(n_ov):
            start = CMP_STRIDE * (u - (c - 1))
            w_u = (min(start + CMP_BLOCK, SEL_BLOCK) - max(start, 0)) / CMP_STRIDE
            n = ratio * j + u - (c - 1)
            if 0 <= n and n + row_offset < n_rows:
                m[j, n + row_offset] = w_u
    return m


def _nsa_prompt_kernel(qt_ref, gt_ref, kc_ref, vct_ref, ks_ref, vst_ref, kw_ref, vwt_ref,
                       mt_ref, cc_ref, cs_ref, bw_ref, o_ref,
                       sc_ref, sel_ref, m_ref, l_ref, acc_ref, oc_ref, *, n_blocks):
    qb = pl.program_id(0)
    nrow = kc_ref.shape[0]
    gq = GROUP * Q_BLOCK
    tiles_per_q = SEL_TILE // Q_BLOCK
    frow = lax.broadcasted_iota(jnp.int32, (KV_DIM, Q_BLOCK), 0)

    def lanes4(x):
        return jnp.concatenate([x] * GROUP, axis=1)

    qx = []
    for kv in range(N_KV):
        keep = (frow >= HEAD_DIM) if kv else (frow < HEAD_DIM)
        qx.append(jnp.concatenate(
            [jnp.where(keep, qt_ref[j * KV_DIM:(j + 1) * KV_DIM, :], jnp.zeros((), BF16))
             for j in range(GROUP)], axis=1))

    r = lax.broadcasted_iota(jnp.int32, (nrow, Q_BLOCK), 0)
    qpos_c = qb * Q_BLOCK + lax.broadcasted_iota(jnp.int32, (nrow, Q_BLOCK), 1)
    end_pos = (r - CMP_PAD) * CMP_STRIDE + (CMP_BLOCK - 1)
    vis = (r >= CMP_PAD) & (r < CMP_PAD + n_blocks * (SEL_BLOCK // CMP_STRIDE) - 1) & (end_pos <= qpos_c)
    vis_add = lanes4(jnp.where(vis, 0.0, NEG))
    band = pl.ds(pl.multiple_of(qb * SUBLANES, SUBLANES), CMP_BAND)
    blk_i = lax.broadcasted_iota(jnp.int32, (n_blocks, Q_BLOCK), 0)
    cur = (qb * Q_BLOCK + lax.broadcasted_iota(jnp.int32, (n_blocks, Q_BLOCK), 1)) // SEL_BLOCK
    forced = (blk_i == 0) | (blk_i == cur) | (blk_i == cur - 1)
    for kv in range(N_KV):
        sc_ref[...] = jnp.dot(kc_ref[...], qx[kv], preferred_element_type=F32) + vis_add
        sc_ref[band, :] = sc_ref[band, :] + cc_ref[kv]
        s = sc_ref[...]
        m = jnp.max(s, axis=0, keepdims=True)
        p = jnp.exp(s - m)
        l = jnp.sum(p, axis=0, keepdims=True)
        pn = p * jnp.where(m > 0.5 * NEG, 1.0 / l, 0.0)
        oc_ref[kv] = jnp.dot(vct_ref[...], pn.astype(BF16), preferred_element_type=F32)
        imp = pn[:, 0:Q_BLOCK]
        for j in range(1, GROUP):
            imp = imp + pn[:, j * Q_BLOCK:(j + 1) * Q_BLOCK]
        score = _split_dot(mt_ref[...], imp)
        score = jnp.where(blk_i <= cur, score + jnp.where(forced, FORCE_BONUS, 0.0), -FORCE_BONUS)
        sel = _select_topk(score, blk_i.astype(F32), N_SEL)
        sel_ref[kv] = (sel - 1.0) * (-NEG)

    m_ref[...] = jnp.full_like(m_ref, NEG)
    l_ref[...] = jnp.zeros_like(l_ref)
    acc_ref[...] = jnp.zeros_like(acc_ref)

    def tile(kt, near):
        rows = pl.ds(pl.multiple_of(kt * SEL_TILE, SEL_TILE), SEL_TILE)
        k = ks_ref[rows, :]
        vt = vst_ref[kt]
        d0 = qb - kt * tiles_per_q
        for kv in range(N_KV):
            s = jnp.dot(k, qx[kv], preferred_element_type=F32)
            srow = sel_ref[kv, pl.ds(pl.multiple_of(kt * BLK_PER_TILE, BLK_PER_TILE), BLK_PER_TILE), :]
            mask = jnp.concatenate(
                [jnp.broadcast_to(srow[i:i + 1, :], (SEL_BLOCK, Q_BLOCK)) for i in range(BLK_PER_TILE)], axis=0)
            s = s + lanes4(mask)
            if near:
                s = s + jnp.concatenate(
                    [cs_ref[kv, jnp.clip(d0 - i, -1, 2) + 1] for i in range(tiles_per_q)], axis=0)
            m_old = m_ref[kv]
            m_new = jnp.maximum(m_old, jnp.max(s, axis=0, keepdims=True))
            p = jnp.exp(s - m_new)
            alpha = jnp.exp(m_old - m_new)
            l_ref[kv] = alpha * l_ref[kv] + jnp.sum(p, axis=0, keepdims=True)
            acc_ref[kv] = alpha * acc_ref[kv] + jnp.dot(vt, p.astype(BF16), preferred_element_type=F32)
            m_ref[kv] = m_new

    n_far = jnp.maximum(qb - 1, 0) // tiles_per_q

    def far_body(kt, c):
        tile(kt, False)
        return c
    lax.fori_loop(0, n_far, far_body, 0)

    def near_body(kt, c):
        tile(kt, True)
        return c
    lax.fori_loop(n_far, qb // tiles_per_q + 1, near_body, 0)

    nband = WINDOW + Q_BLOCK
    kband = kw_ref[pl.ds(pl.multiple_of(qb * Q_BLOCK, Q_BLOCK), nband), :]
    vband = jnp.concatenate([vwt_ref[qb + i] for i in range(nband // Q_BLOCK)], axis=1)
    krow = lax.broadcasted_iota(jnp.int32, (nband, Q_BLOCK), 0)
    pos_add = lanes4(jnp.where(krow >= WINDOW - qb * Q_BLOCK, 0.0, NEG))
    o_kv = []
    for kv in range(N_KV):
        s = jnp.dot(kband, qx[kv], preferred_element_type=F32) + bw_ref[kv] + pos_add
        m = jnp.max(s, axis=0, keepdims=True)
        p = jnp.exp(s - m)
        o_w = (jnp.dot(vband, p.astype(BF16), preferred_element_type=F32)
               / jnp.sum(p, axis=0, keepdims=True))
        o_s = acc_ref[kv] / l_ref[kv]

        def gate(i):
            return jnp.concatenate(
                [gt_ref[3 * (kv * GROUP + j) + i:3 * (kv * GROUP + j) + i + 1, :] for j in range(GROUP)], axis=1)
        o_kv.append(gate(0) * oc_ref[kv] + gate(1) * o_s + gate(2) * o_w)

    frow4 = lax.broadcasted_iota(jnp.int32, (KV_DIM, gq), 0)
    o_t = jnp.where(frow4 < HEAD_DIM, o_kv[0], o_kv[1])
    for j in range(GROUP):
        o_ref[:, j * LANES:(j + 1) * LANES] = o_t[:, j * Q_BLOCK:(j + 1) * Q_BLOCK].T.astype(BF16)


def _nsa_prompt(qat, gat, kc, vct, ksb, vst3, kwb, vwt3, mt, cc, cs, bw):
    t = qat.shape[1]
    n_blocks = t // SEL_BLOCK
    nrow = kc.shape[0]
    gq = GROUP * Q_BLOCK
    return pl.pallas_call(
        functools.partial(_nsa_prompt_kernel, n_blocks=n_blocks),
        grid=(t // Q_BLOCK,),
        in_specs=[pl.BlockSpec((Q_DIM, Q_BLOCK), lambda i: (0, i)),
                  pl.BlockSpec((LANES, Q_BLOCK), lambda i: (0, i)),
                  _const_spec(kc.shape), _const_spec(vct.shape),
                  _const_spec(ksb.shape), _const_spec(vst3.shape),
                  _const_spec(kwb.shape), _const_spec(vwt3.shape),
                  _const_spec(mt.shape), _const_spec(cc.shape),
                  _const_spec(cs.shape), _const_spec(bw.shape)],
        out_specs=pl.BlockSpec((Q_BLOCK, Q_DIM), lambda i: (i, 0)),
        out_shape=jax.ShapeDtypeStruct((t, Q_DIM), BF16),
        scratch_shapes=[pltpu.VMEM((nrow, gq), F32),
                        pltpu.VMEM((N_KV, n_blocks, Q_BLOCK), F32),
                        pltpu.VMEM((N_KV, 1, gq), F32), pltpu.VMEM((N_KV, 1, gq), F32),
                        pltpu.VMEM((N_KV, KV_DIM, gq), F32),
                        pltpu.VMEM((N_KV, KV_DIM, gq), F32)],
        compiler_params=_cparams(("arbitrary",)),
        name="nsa_prompt",
    )(qat, gat, kc, vct, ksb, vst3, kwb, vwt3, mt, cc, cs, bw)


S_ROWS = 128
S_CHUNK_PAGES = 32
S_CHUNK = S_CHUNK_PAGES * PAGE


def _nsa_sample_kernel(pt_ref, qx_ref, gm_ref, kc_ref, vct_ref, kpool_ref, vpool_ref,
                       knew_ref, vnew_ref, kwin_ref, vwin_ref, kwnew_ref, vwnew_ref,
                       mt_ref, gsum_ref, ccs_ref, css_ref, cns_ref, cws_ref, o_ref,
                       kbuf, vbuf, sem, mask_ref, m_ref, l_ref, acc_ref, oc_ref, ow_ref,
                       *, n_chunks, n_blocks):
    b = pl.program_id(0)
    c = pl.program_id(1)
    step = b * n_chunks + c
    total = pl.num_programs(0) * n_chunks
    slot = step % 2

    def copies(bb, cc, s, p):
        pg = cc * S_CHUNK_PAGES + p
        dst = pl.ds(pl.multiple_of(p * PAGE, PAGE), PAGE)
        return (pltpu.make_async_copy(kpool_ref.at[pt_ref[bb, pg]], kbuf.at[s, :, dst], sem.at[0, s]),
                pltpu.make_async_copy(vpool_ref.at[pt_ref[bb, pg]], vbuf.at[s, :, dst], sem.at[1, s]))

    def start_all(st, s):
        bb = st // n_chunks
        cc = st % n_chunks

        def body(p, carry):
            ck, cv = copies(bb, cc, s, p)
            ck.start()
            cv.start()
            return carry
        lax.fori_loop(0, S_CHUNK_PAGES, body, 0)

    @pl.when(step == 0)
    def _():
        start_all(step, slot)

    @pl.when(step + 1 < total)
    def _():
        start_all(step + 1, 1 - slot)

    qx = qx_ref[0]

    @pl.when(c == 0)
    def _():
        nrow = kc_ref.shape[1]
        s = _nt(kc_ref[0], qx)
        r = lax.broadcasted_iota(jnp.int32, (nrow, S_ROWS), 0)
        band0 = nrow - CMP_BAND
        s = s + jnp.concatenate([jnp.zeros((band0, S_ROWS), F32), ccs_ref[...]], axis=0)
        pn = _softmax_cols(s, r < nrow - 1)
        oc_ref[...] = jnp.dot(vct_ref[0], pn.astype(BF16), preferred_element_type=F32).T
        imp = jnp.dot(pn, gsum_ref[...], preferred_element_type=F32, precision=HIGHEST)
        score = _split_dot(mt_ref[...], imp)
        blk_i = lax.broadcasted_iota(jnp.int32, (n_blocks, S_ROWS), 0)
        forced = (blk_i == 0) | (blk_i == n_blocks - 1)
        score = score + jnp.where(forced, FORCE_BONUS, 0.0)
        selt = _select_topk(score, blk_i.astype(F32), N_SEL - 1)
        selt = lax.dot_general(selt.astype(BF16), gsum_ref[...].astype(BF16), (((1,), (1,)), ((), ())),
                               preferred_element_type=F32)
        selm1 = selt.T - 1.0
        e_r = lax.broadcasted_iota(jnp.int32, (BLK_PER_TILE, SEL_TILE), 0)
        e_c = lax.broadcasted_iota(jnp.int32, (BLK_PER_TILE, SEL_TILE), 1)
        expand = jnp.where(e_c // SEL_BLOCK == e_r, -NEG, 0.0)
        for kt in range(n_blocks // BLK_PER_TILE):
            mask_ref[:, kt * SEL_TILE:(kt + 1) * SEL_TILE] = jnp.dot(
                selm1[:, kt * BLK_PER_TILE:(kt + 1) * BLK_PER_TILE], expand, preferred_element_type=F32)

        sw = _nt(qx, kwin_ref[0].astype(BF16)) + cws_ref[...]
        sn = _nt(qx, kwnew_ref[0].astype(BF16)) + cns_ref[...]
        m = jnp.maximum(jnp.max(sw, axis=1, keepdims=True), jnp.max(sn, axis=1, keepdims=True))
        pw = jnp.exp(sw - m)
        pn2 = jnp.exp(sn - m)
        l = jnp.sum(pw, axis=1, keepdims=True) + jnp.sum(pn2, axis=1, keepdims=True)
        ow = (jnp.dot(pw.astype(BF16), vwin_ref[0].astype(BF16), preferred_element_type=F32)
              + jnp.dot(pn2.astype(BF16), vwnew_ref[0].astype(BF16), preferred_element_type=F32))
        ow_ref[...] = ow / l

        m_ref[...] = jnp.full_like(m_ref, NEG)
        l_ref[...] = jnp.zeros_like(l_ref)
        acc_ref[...] = jnp.zeros_like(acc_ref)

    def wait_body(p, carry):
        ck, cv = copies(b, c, slot, p)
        ck.wait()
        cv.wait()
        return carry
    lax.fori_loop(0, S_CHUNK_PAGES, wait_body, 0)

    col0 = pl.multiple_of(c * S_CHUNK, S_CHUNK)
    s = (jnp.dot(qx, kbuf[slot].astype(BF16), preferred_element_type=F32)
         + mask_ref[:, pl.ds(col0, S_CHUNK)])

    @pl.when(c < n_chunks - 1)
    def _():
        _online_update(s, vbuf[slot].astype(BF16), m_ref, l_ref, acc_ref, v_transposed=True)

    @pl.when(c == n_chunks - 1)
    def _():
        near = jnp.concatenate([jnp.zeros((S_ROWS, S_CHUNK - LANES), F32), css_ref[...]], axis=1)
        _online_update(s + near, vbuf[slot].astype(BF16), m_ref, l_ref, acc_ref, v_transposed=True)
        sn = _nt(qx, knew_ref[0].astype(BF16)) + cns_ref[...]
        _online_update(sn, vnew_ref[0].astype(BF16), m_ref, l_ref, acc_ref)
        o_s = acc_ref[...] / l_ref[...]
        o_ref[0] = gm_ref[0, 0] * oc_ref[...] + gm_ref[0, 1] * o_s + gm_ref[0, 2] * ow_ref[...]


def _nsa_sample(page_table, qx, gm, kc, vct, kpool, vpool, knew, vnew, kwin, vwin, kwnew, vwnew,
                mt, gsum, ccs, css, cns, cws):
    nb = qx.shape[0]
    n_pages = page_table.shape[1]
    n_chunks = n_pages // S_CHUNK_PAGES
    past = n_pages * PAGE
    n_blocks = past // SEL_BLOCK
    nrow = kc.shape[1]

    def bspec(shape):
        nd = len(shape)
        return pl.BlockSpec((1,) + tuple(shape[1:]), lambda b, c, pt: (b,) + (0,) * (nd - 1))

    def cspec(shape):
        nd = len(shape)
        return pl.BlockSpec(tuple(shape), lambda b, c, pt: (0,) * nd, pipeline_mode=pl.Buffered(1))

    grid_spec = pltpu.PrefetchScalarGridSpec(
        num_scalar_prefetch=1,
        grid=(nb, n_chunks),
        in_specs=[bspec(qx.shape), bspec(gm.shape), bspec(kc.shape), bspec(vct.shape),
                  pl.BlockSpec(memory_space=pl.ANY), pl.BlockSpec(memory_space=pl.ANY),
                  bspec(knew.shape), bspec(vnew.shape), bspec(kwin.shape), bspec(vwin.shape),
                  bspec(kwnew.shape), bspec(vwnew.shape),
                  cspec(mt.shape), cspec(gsum.shape), cspec(ccs.shape), cspec(css.shape),
                  cspec(cns.shape), cspec(cws.shape)],
        out_specs=pl.BlockSpec((1, S_ROWS, KV_DIM), lambda b, c, pt: (b, 0, 0)),
        scratch_shapes=[pltpu.VMEM((2, KV_DIM, S_CHUNK), F32),
                        pltpu.VMEM((2, KV_DIM, S_CHUNK), F32),
                        pltpu.SemaphoreType.DMA((2, 2)),
                        pltpu.VMEM((S_ROWS, past), F32),
                        pltpu.VMEM((S_ROWS, 1), F32), pltpu.VMEM((S_ROWS, 1), F32),
                        pltpu.VMEM((S_ROWS, KV_DIM), F32),
                        pltpu.VMEM((S_ROWS, KV_DIM), F32), pltpu.VMEM((S_ROWS, KV_DIM), F32)],
    )
    return pl.pallas_call(
        functools.partial(_nsa_sample_kernel, n_chunks=n_chunks, n_blocks=n_blocks),
        grid_spec=grid_spec,
        out_shape=jax.ShapeDtypeStruct((nb, S_ROWS, KV_DIM), F32),
        compiler_params=_cparams(("arbitrary", "arbitrary")),
        name="nsa_sample",
    )(page_table, qx, gm, kc, vct, kpool, vpool, knew, vnew, kwin, vwin, kwnew, vwnew,
      mt, gsum, ccs, css, cns, cws)


def _ffn_kernel(x_ref, oa_ref, ob_ref, sa_ref, sb_ref, wpa_ref, wpb_ref, wo_ref, nf_ref,
                wg_ref, wu_ref, wd_ref, nl_ref, y_ref):
    pa = jnp.dot(oa_ref[...], wpa_ref[...], preferred_element_type=F32)
    pb = jnp.dot(ob_ref[...], wpb_ref[...], preferred_element_type=F32)
    merged = sa_ref[...] * pa + sb_ref[...] * pb
    x = x_ref[...] + jnp.dot(merged.astype(BF16), wo_ref[...], preferred_element_type=F32)
    hn = (x * lax.rsqrt(jnp.mean(x * x, axis=-1, keepdims=True) + EPS) * nf_ref[...]).astype(BF16)
    gate = jnp.dot(hn, wg_ref[...], preferred_element_type=F32)
    up = jnp.dot(hn, wu_ref[...], preferred_element_type=F32)
    ff = (jax.nn.silu(gate) * up).astype(BF16)
    x = x + jnp.dot(ff, wd_ref[...], preferred_element_type=F32)
    y_ref[...] = x * lax.rsqrt(jnp.mean(x * x, axis=-1, keepdims=True) + EPS) * nl_ref[...]


def _ffn(x2d, oa, ob, sa, sb, wpa, wpb, wo, nf, wg, wu, wd, nl, tm):
    rows = x2d.shape[0]

    def rspec(n):
        return pl.BlockSpec((tm, n), lambda i: (i, 0))

    return pl.pallas_call(
        _ffn_kernel,
        grid=(rows // tm,),
        in_specs=[rspec(D_MODEL), rspec(Q_DIM), rspec(HG_W), rspec(D_MODEL), rspec(D_MODEL),
                  _const_spec(wpa.shape), _const_spec(wpb.shape), _const_spec(wo.shape),
                  _const_spec(nf.shape), _const_spec(wg.shape), _const_spec(wu.shape),
                  _const_spec(wd.shape), _const_spec(nl.shape)],
        out_specs=rspec(D_MODEL),
        out_shape=jax.ShapeDtypeStruct((rows, D_MODEL), F32),
        compiler_params=_cparams(("arbitrary",)),
        name="ffn",
    )(x2d, oa, ob, sa, sb, wpa, wpb, wo, nf, wg, wu, wd, nl)


def _pack_w_in(w_in):
    sizes = (Q_DIM,) + (KV_DIM,) * 6 + (3 * N_HEADS,) + (HG_W,) * 4 + (D_MODEL,) * 2
    offs = np.concatenate([[0], np.cumsum(sizes)])
    q = w_in[:, offs[0]:offs[1]].reshape(D_MODEL, N_HEADS, HEAD_DIM)[:, _HEAD_PERM, :].reshape(D_MODEL, Q_DIM)
    g = jnp.pad(w_in[:, offs[7]:offs[8]], ((0, 0), (0, LANES - 3 * N_HEADS)))
    return jnp.concatenate([q, w_in[:, offs[1]:offs[7]], g, w_in[:, offs[8]:]], axis=1).astype(BF16)


def _strip(bvc, rel, lo=0, hi=None, masked=NEG):
    val = bvc[:, np.clip(rel, 0, 255)]
    ok = rel >= lo
    if hi is not None:
        ok = ok & (rel < hi)
    return jnp.where(jnp.asarray(ok)[None], val, masked)


def _toeplitz(bvc, a, n_rows, n_cols, lo=0, hi=None):
    n = n_rows + n_cols - 1
    u = _strip(bvc, a - (n_rows - 1) + np.arange(n), lo, hi)
    u = jnp.pad(u, ((0, 0), (0, 1)))
    circ = jnp.tile(u, (1, n_rows))[:, :n_rows * n].reshape(N_HEADS, n_rows, n)
    return circ[:, :, n_rows - 1:n_rows - 1 + n_cols]


def _bias_strips_prompt(bvc):
    gq = GROUP * Q_BLOCK

    def lanes(x):
        return x.reshape(N_KV, GROUP, x.shape[1], Q_BLOCK).transpose(0, 2, 1, 3).reshape(N_KV, x.shape[1], gq)

    cs = jnp.stack([lanes(_toeplitz(bvc, Q_BLOCK * d, Q_BLOCK, Q_BLOCK)) for d in (-1, 0, 1, 2)], axis=1)
    bw = lanes(_toeplitz(bvc, WINDOW, WINDOW + Q_BLOCK, Q_BLOCK, 0, WINDOW))
    rr = np.arange(CMP_BAND)[:, None]
    rel_c = np.arange(Q_BLOCK)[None, :] - CMP_STRIDE * (rr - CMP_PAD) - (CMP_BLOCK - 1)
    cc = _strip(bvc, rel_c, masked=0.0).reshape(N_KV, GROUP, CMP_BAND, Q_BLOCK)
    cc = cc.transpose(0, 2, 1, 3).reshape(N_KV, CMP_BAND, gq)
    return cc, cs, bw


def _bias_strips_sample(bvc, past, t):
    def rows(a):
        a = a.reshape(N_HEADS * t, a.shape[-1])
        return jnp.pad(a, ((0, S_ROWS - N_HEADS * t), (0, 0)))
    tt = np.arange(t)[:, None]
    nrow = past // CMP_STRIDE
    n = (nrow - CMP_BAND + np.arange(CMP_BAND))[None, :]
    ccs = rows(_strip(bvc, past + tt - CMP_STRIDE * n - (CMP_BLOCK - 1), masked=0.0)).T
    i = np.arange(LANES)[None, :]
    css = rows(_strip(bvc, LANES + tt - i))
    cns = rows(_strip(bvc, np.where(i < t, tt - i, -1)))
    iw = np.arange(WINDOW)[None, :]
    cws = rows(_strip(bvc, WINDOW + tt - iw, 0, WINDOW))
    return ccs, css, cns, cws


def kernel(x_prompt, x_sample, cache_k_cmp, cache_v_cmp, cache_k_slc, cache_v_slc, state_k_win, state_v_win,
           state_hgrn, page_table, norm_mix, w_in, cmp_pe_k, cmp_w1_k, cmp_w2_k, cmp_pe_v, cmp_w1_v, cmp_w2_v,
           rel_bias, hg_lb_logits, hg_norm, w_proj_a, w_proj_b, w_out, norm_ffn, w_gate, w_up, w_down, norm_final):
    nbp, t_p, _ = x_prompt.shape
    nbs, t_s, _ = x_sample.shape
    assert nbp == 1 and norm_mix.shape[0] == 1
    n_pages = page_table.shape[1]
    past = n_pages * PAGE
    assert state_k_win.shape[2] == WINDOW and past % S_CHUNK == 0 and t_s <= SUBLANES

    lb = jnp.cumsum(jax.nn.softmax(hg_lb_logits.astype(F32), axis=0), axis=0)[0]
    lb3 = jnp.pad(jnp.stack([jnp.log(lb), jnp.log1p(-lb), 1.0 - lb]), ((0, SUBLANES - 3), (0, 0)))
    w_pack = _pack_w_in(w_in[0])
    g_mix = norm_mix[0][None, :]
    wpa = w_proj_a[0].reshape(N_HEADS, HEAD_DIM, D_MODEL)[_HEAD_PERM].reshape(Q_DIM, D_MODEL).astype(BF16)
    wpb = w_proj_b[0].astype(BF16)
    wo = w_out[0].astype(BF16)
    wg, wu, wd = w_gate[0].astype(BF16), w_up[0].astype(BF16), w_down[0].astype(BF16)
    nf, nl = norm_ffn[0][None, :], norm_final[None, :]
    gn = hg_norm[0][None, :]
    pe_k, w1_k, w2_k = _compress_weights(cmp_pe_k[0], cmp_w1_k[0], cmp_w2_k[0])
    pe_v, w1_v, w2_v = _compress_weights(cmp_pe_v[0], cmp_w1_v[0], cmp_w2_v[0])
    bvc = (rel_bias[_BUCKET] - rel_bias[N_BUCKETS - 1][None, :]).T
    cc, cs, bw = _bias_strips_prompt(bvc)
    ccs, css, cns, cws = _bias_strips_sample(bvc, past, t_s)

    xp2 = x_prompt.reshape(t_p, D_MODEL)
    xs2 = x_sample.reshape(nbs * t_s, D_MODEL)
    seg = lambda off, n: w_pack[:, off:off + n]
    w_t = jnp.concatenate([seg(_OFF_Q, Q_DIM), seg(_OFF_G, LANES), seg(_OFF_KV + 3 * KV_DIM, KV_DIM),
                           seg(_OFF_KV + 5 * KV_DIM, KV_DIM)], axis=1).T
    pp = _proj(xp2, g_mix, w_pack, w_t, lb3, 256)
    ps = _proj(xs2, g_mix, w_pack, w_t, lb3, nbs * t_s)
    (_, kc_p, vc_p, ks_p, vs_p, kw_p, vw_p, ksb_p, kwb_p, _,
     qh_p, lf_p, kh_p, vh_p, gs_p, sa_p, sb_p, qat_p, gat_p, vst_p, vwt_p) = pp
    (qa_s, kc_s, vc_s, ks_s, vs_s, kw_s, vw_s, _, _, ga_s,
     qh_s, lf_s, kh_s, vh_s, gs_s, sa_s, sb_s, _, _, _, _) = ps

    ob_p, s_p = _hgrn_prompt(qh_p, kh_p, vh_p, lf_p, gs_p, gn)
    r3 = lambda a: a.reshape(nbs, t_s, a.shape[-1])
    ob_s, s_s = _hgrn_sample(r3(qh_s), r3(kh_s), r3(vh_s), r3(lf_s), r3(gs_s), gn, state_hgrn[0])

    ident = jnp.arange(t_p // PAGE, dtype=jnp.int32)[None, :]
    pool_rows = lambda a: a.reshape(-1, _CH_PER_PAGE, _CH_W)
    kcb_p, _ = _compress(pool_rows(kc_p), ident, pe_k, w1_k, w2_k)
    _, vct_p = _compress(pool_rows(vc_p), ident, pe_v, w1_v, w2_v)
    kcb_p = jnp.pad(kcb_p[0], ((CMP_PAD, 0), (0, 0)))
    vct_p = jnp.pad(vct_p[0], ((0, 0), (CMP_PAD, 0)))
    nrow_p = kcb_p.shape[0]
    mt_p = jnp.asarray(_score_matrix(nrow_p, CMP_PAD, t_p // SEL_BLOCK), BF16)
    kwb_pad = jnp.pad(kwb_p, ((WINDOW, 0), (0, 0)))
    vst3 = vst_p.reshape(KV_DIM, t_p // SEL_TILE, SEL_TILE).transpose(1, 0, 2)
    vwt3 = jnp.pad(vwt_p, ((0, 0), (WINDOW, 0))).reshape(KV_DIM, (t_p + WINDOW) // Q_BLOCK, Q_BLOCK)
    vwt3 = vwt3.transpose(1, 0, 2)
    oa_p = _nsa_prompt(qat_p, gat_p, kcb_p, vct_p, ksb_p, vst3, kwb_pad, vwt3, mt_p, cc, cs, bw)

    kcb_s, _ = _compress(pool_rows(cache_k_cmp[0]), page_table, pe_k, w1_k, w2_k)
    _, vct_s = _compress(pool_rows(cache_v_cmp[0]), page_table, pe_v, w1_v, w2_v)
    nq = N_HEADS * t_s
    qs4 = qa_s.reshape(nbs, t_s, GROUP, N_KV, HEAD_DIM).astype(F32)
    qx = jnp.einsum('btjkd,kq->bkjtqd', qs4, jnp.eye(N_KV, dtype=F32)).reshape(nbs, nq, KV_DIM)
    qx = jnp.pad(qx, ((0, 0), (0, S_ROWS - nq), (0, 0))).astype(BF16)
    g4 = ga_s[:, :3 * N_HEADS].reshape(nbs, t_s, N_KV, GROUP, 3)
    gm = jnp.transpose(g4, (0, 4, 2, 3, 1)).reshape(nbs, 3, nq, 1)
    gm = jnp.broadcast_to(jnp.pad(gm, ((0, 0), (0, 0), (0, S_ROWS - nq), (0, 0))), (nbs, 3, S_ROWS, KV_DIM))
    new_tile = lambda a: jnp.pad(a.reshape(nbs, t_s, KV_DIM), ((0, 0), (0, LANES - t_s), (0, 0)))
    mt_s = jnp.asarray(_score_matrix(past // CMP_STRIDE, 0, past // SEL_BLOCK), BF16)
    pool_t = lambda a: a.transpose(0, 2, 3, 1).reshape(-1, KV_DIM, PAGE)
    gsum = np.zeros((S_ROWS, S_ROWS), np.float32)
    for kv in range(N_KV):
        for j in range(GROUP):
            for t in range(t_s):
                gsum[(kv * GROUP + j) * t_s + t, kv * t_s + t] = 1.0
    o_kv = _nsa_sample(page_table, qx, gm, kcb_s, vct_s,
                       pool_t(cache_k_slc[0]), pool_t(cache_v_slc[0]),
                       new_tile(ks_s), new_tile(vs_s), state_k_win[0].reshape(nbs, WINDOW, KV_DIM),
                       state_v_win[0].reshape(nbs, WINDOW, KV_DIM), new_tile(kw_s), new_tile(vw_s),
                       mt_s, jnp.asarray(gsum), ccs, css, cns, cws)
    o5 = o_kv[:, :nq].reshape(nbs, N_KV, GROUP, t_s, N_KV, HEAD_DIM)
    oa_s = jnp.einsum('bkjtqd,kq->btjkd', o5, jnp.eye(N_KV, dtype=F32)).reshape(nbs * t_s, Q_DIM).astype(BF16)

    y_p = _ffn(xp2, oa_p, ob_p, sa_p, sb_p, wpa, wpb, wo, nf, wg, wu, wd, nl, 256)
    y_s = _ffn(xs2, oa_s, ob_s.reshape(nbs * t_s, HG_W), sa_s, sb_s, wpa, wpb, wo, nf, wg, wu, wd, nl, nbs * t_s)

    kv5 = lambda a, nb_, tt: a.reshape(1, nb_, tt, N_KV, HEAD_DIM)
    wl = min(WINDOW, t_p)
    win = lambda st, new: jnp.concatenate(
        [st[0], new.reshape(nbs, t_s, N_KV, HEAD_DIM)], axis=1)[:, -WINDOW:][None]
    return (y_p.reshape(1, t_p, D_MODEL), y_s.reshape(nbs, t_s, D_MODEL),
            kv5(kc_p, 1, t_p), kv5(vc_p, 1, t_p), kv5(ks_p, 1, t_p), kv5(vs_p, 1, t_p),
            kv5(kw_p[-wl:], 1, wl), kv5(vw_p[-wl:], 1, wl), s_p[None, None],
            kv5(kc_s, nbs, t_s), kv5(vc_s, nbs, t_s), kv5(ks_s, nbs, t_s), kv5(vs_s, nbs, t_s),
            win(state_k_win, kw_s), win(state_v_win, vw_s), s_s[None])
```

```python
import functools
import math

import numpy as np
import jax
import jax.numpy as jnp
from jax import lax
from jax.experimental import pallas as pl
from jax.experimental.pallas import tpu as pltpu

F32 = jnp.float32
BF16 = jnp.bfloat16
HIGHEST = lax.Precision.HIGHEST

D_MODEL = 1024
N_HEADS = 8
N_KV = 2
GROUP = N_HEADS // N_KV
HEAD_DIM = 64
KV_DIM = N_KV * HEAD_DIM
Q_DIM = N_HEADS * HEAD_DIM
CMP_BLOCK = 32
CMP_STRIDE = 16
CMP_HIDDEN = 2 * HEAD_DIM
SEL_BLOCK = 64
N_SEL = 16
WINDOW = 512
Q_BLOCK = 128
FORCE_BONUS = 1e4
N_BUCKETS = 32
MAX_DISTANCE = 128
HG_HEADS = 4
HG_DIM = 128
HG_CHUNK = 64
HG_SUB = 16
HG_W = HG_HEADS * HG_DIM
D_FF = ((8 * D_MODEL // 3 + 255) // 256) * 256
EPS = 1e-6
PAGE = 128
NEG = -1e30
LOG2E = math.log2(math.e)
Q_SCALE = HEAD_DIM ** -0.5 * LOG2E

LANES = 128
SUBLANES = 8
VMEM_LIMIT = 56 * 1024 * 1024

_OFF_Q = 0
_OFF_KV = _OFF_Q + Q_DIM
_OFF_G = _OFF_KV + 6 * KV_DIM
_OFF_HG = _OFF_G + LANES
_OFF_GATE = _OFF_HG + 4 * HG_W
_PROJ_N = _OFF_GATE + 2 * D_MODEL

_HEAD_PERM = np.array([h for j in range(GROUP) for h in (j, GROUP + j)])

SEL_TILE = 512
BLK_PER_TILE = SEL_TILE // SEL_BLOCK
CMP_PAD = 16
CMP_BAND = 24


def _cparams(sem, vmem=VMEM_LIMIT):
    return pltpu.CompilerParams(dimension_semantics=sem, vmem_limit_bytes=vmem)


def _const_spec(shape):
    nd = len(shape)
    return pl.BlockSpec(shape, lambda *_: (0,) * nd, pipeline_mode=pl.Buffered(1))


def _bucket_table():
    n = np.arange(256)
    max_exact = N_BUCKETS // 2
    nf = np.maximum(n, 1).astype(np.float64)
    large = max_exact + (np.log(nf / max_exact) / math.log(MAX_DISTANCE / max_exact)
                         * (N_BUCKETS - max_exact)).astype(np.int64)
    large = np.minimum(large, N_BUCKETS - 1)
    return np.where(n < max_exact, n, large)


_BUCKET = _bucket_table()


def _proj_kernel(x_ref, g_ref, w_ref, wt_ref, lb_ref,
                 qa_ref, kc_ref, vc_ref, ks_ref, vs_ref, kw_ref, vw_ref,
                 ksb_ref, kwb_ref, ga_ref,
                 qh_ref, lf_ref, kh_ref, vh_ref, gs_ref, sa_ref, sb_ref,
                 qat_ref, gat_ref, vst_ref, vwt_ref):
    x = x_ref[...]
    xn = x * lax.rsqrt(jnp.mean(x * x, axis=-1, keepdims=True) + EPS) * g_ref[...]
    xb = xn.astype(BF16)

    def seg(a, n):
        return jnp.dot(xb, w_ref[:, a:a + n], preferred_element_type=F32)

    qa_ref[...] = (seg(_OFF_Q, Q_DIM) * Q_SCALE).astype(BF16)
    f32_refs = (kc_ref, vc_ref, ks_ref, vs_ref, kw_ref, vw_ref)
    b16_refs = (None, None, ksb_ref, None, kwb_ref, None)
    for i in range(6):
        u = seg(_OFF_KV + i * KV_DIM, KV_DIM)
        f32_refs[i][...] = u
        if b16_refs[i] is not None:
            b16_refs[i][...] = u.astype(BF16)
    ga_ref[...] = jax.nn.sigmoid(seg(_OFF_G, LANES))

    def seg_t(a, n):
        return lax.dot_general(wt_ref[a:a + n, :], xb, (((1,), (1,)), ((), ())), preferred_element_type=F32)

    qat_ref[...] = (seg_t(0, Q_DIM) * Q_SCALE).astype(BF16)
    gat_ref[...] = jax.nn.sigmoid(seg_t(Q_DIM, LANES))
    vst_ref[...] = seg_t(Q_DIM + LANES, KV_DIM).astype(BF16)
    vwt_ref[...] = seg_t(Q_DIM + LANES + KV_DIM, KV_DIM).astype(BF16)

    log_lb = lb_ref[0:1, :]
    log_1m = lb_ref[1:2, :]
    one_m = lb_ref[2:3, :]
    qh_ref[...] = jax.nn.silu(seg(_OFF_HG, HG_W))
    z = seg(_OFF_HG + HG_W, HG_W)
    b = log_1m + (jnp.minimum(z, 0.0) - jnp.log1p(jnp.exp(-jnp.abs(z))))
    hi = jnp.maximum(log_lb, b)
    lf_ref[...] = hi + jnp.log1p(jnp.exp(-jnp.abs(log_lb - b)))
    kh_ref[...] = one_m * jax.nn.sigmoid(-z)
    vh_ref[...] = seg(_OFF_HG + 2 * HG_W, HG_W)
    gs_ref[...] = jax.nn.silu(seg(_OFF_HG + 3 * HG_W, HG_W))
    sa_ref[...] = jax.nn.sigmoid(seg(_OFF_GATE, D_MODEL))
    sb_ref[...] = jax.nn.sigmoid(seg(_OFF_GATE + D_MODEL, D_MODEL))


_PROJ_T = Q_DIM + LANES + 2 * KV_DIM


def _proj(x2d, g, w, wt, lb3, tm):
    rows = x2d.shape[0]
    widths = ([(Q_DIM, BF16)] + [(KV_DIM, F32)] * 6 + [(KV_DIM, BF16)] * 2 + [(LANES, F32)]
              + [(HG_W, F32)] * 5 + [(D_MODEL, F32)] * 2)
    heights = [(Q_DIM, BF16), (LANES, F32), (KV_DIM, BF16), (KV_DIM, BF16)]
    return pl.pallas_call(
        _proj_kernel,
        grid=(rows // tm,),
        in_specs=[pl.BlockSpec((tm, D_MODEL), lambda i: (i, 0)),
                  _const_spec((1, D_MODEL)),
                  _const_spec((D_MODEL, _PROJ_N)),
                  _const_spec((_PROJ_T, D_MODEL)),
                  _const_spec((SUBLANES, HG_W))],
        out_specs=([pl.BlockSpec((tm, n), lambda i: (i, 0)) for n, _ in widths]
                   + [pl.BlockSpec((n, tm), lambda i: (0, i)) for n, _ in heights]),
        out_shape=([jax.ShapeDtypeStruct((rows, n), dt) for n, dt in widths]
                   + [jax.ShapeDtypeStruct((n, rows), dt) for n, dt in heights]),
        compiler_params=_cparams(("arbitrary",)),
        name="proj",
    )(x2d, g, w, wt, lb3)


_CH_W = CMP_STRIDE * KV_DIM
_CH_PER_PAGE = PAGE // CMP_STRIDE


def _compress_kernel(pt_ref, pool_ref, pe_ref, w1_ref, w2_ref, out_ref, outt_ref, buf, xa, hbuf, sem,
                     *, n_pages):
    b = pl.program_id(0)
    nb = pl.num_programs(0)
    slot = b % 2
    n_ch = n_pages * _CH_PER_PAGE

    def page_copy(bb, p, s):
        return pltpu.make_async_copy(pool_ref.at[pt_ref[bb, p]],
                                     buf.at[s, pl.ds(pl.multiple_of(p * PAGE, PAGE), PAGE)],
                                     sem.at[s])

    def start_all(bb, s):
        def body(p, c):
            page_copy(bb, p, s).start()
            return c
        lax.fori_loop(0, n_pages, body, 0)

    @pl.when(b == 0)
    def _():
        start_all(b, slot)

    @pl.when(b + 1 < nb)
    def _():
        start_all(b + 1, 1 - slot)

    def wait_body(p, c):
        page_copy(b, p, slot).wait()
        return c
    lax.fori_loop(0, n_pages, wait_body, 0)

    rows = math.gcd(n_ch, 256)
    for j in range(2):
        for r in range(n_ch // rows):
            for s in range(CMP_STRIDE):
                x = buf[slot, pl.ds(r * rows * CMP_STRIDE + s, rows, stride=CMP_STRIDE), :]
                xa[:, s * KV_DIM:(s + 1) * KV_DIM] = (x + pe_ref[j:j + 1, s * KV_DIM:(s + 1) * KV_DIM]).astype(BF16)
            hbuf[j, r * rows:(r + 1) * rows, :] = jnp.dot(xa[...], w1_ref[j], preferred_element_type=F32)
    h = hbuf[0] + pltpu.roll(hbuf[1], n_ch - 1, 0)
    blocks = jnp.dot(jax.nn.gelu(h).astype(BF16), w2_ref[...], preferred_element_type=F32)
    row = lax.broadcasted_iota(jnp.int32, blocks.shape, 0)
    blocks = jnp.where(row < n_ch - 1, blocks, 0.0)
    out_ref[0] = blocks.astype(BF16)
    outt_ref[0] = blocks.T.astype(BF16)


def _compress(pool, page_table, pe, w1, w2):
    nbatch, n_pages = page_table.shape
    n_ch = n_pages * _CH_PER_PAGE
    rows = math.gcd(n_ch, 256)
    grid_spec = pltpu.PrefetchScalarGridSpec(
        num_scalar_prefetch=1,
        grid=(nbatch,),
        in_specs=[pl.BlockSpec(memory_space=pl.ANY),
                  _const_spec((SUBLANES, _CH_W)),
                  _const_spec((2, _CH_W, 2 * CMP_HIDDEN)),
                  _const_spec((2 * CMP_HIDDEN, KV_DIM))],
        out_specs=[pl.BlockSpec((1, n_ch, KV_DIM), lambda b, pt: (b, 0, 0)),
                   pl.BlockSpec((1, KV_DIM, n_ch), lambda b, pt: (b, 0, 0))],
        scratch_shapes=[pltpu.VMEM((2, n_pages * PAGE, KV_DIM), F32),
                        pltpu.VMEM((rows, _CH_W), BF16),
                        pltpu.VMEM((2, n_ch, 2 * CMP_HIDDEN), F32),
                        pltpu.SemaphoreType.DMA((2,))],
    )
    return pl.pallas_call(
        functools.partial(_compress_kernel, n_pages=n_pages),
        grid_spec=grid_spec,
        out_shape=[jax.ShapeDtypeStruct((nbatch, n_ch, KV_DIM), BF16),
                   jax.ShapeDtypeStruct((nbatch, KV_DIM, n_ch), BF16)],
        compiler_params=_cparams(("arbitrary",)),
        name="compress",
    )(page_table, pool, pe, w1, w2)


def _compress_weights(pe, w1, w2):
    c = CMP_BLOCK // CMP_STRIDE
    pe_r = pe.reshape(c, CMP_STRIDE, 1, HEAD_DIM)
    pe_x = jnp.broadcast_to(pe_r, (c, CMP_STRIDE, N_KV, HEAD_DIM)).reshape(c, _CH_W)
    pe_x = jnp.pad(pe_x, ((0, SUBLANES - c), (0, 0)))
    w1_r = w1.reshape(c, CMP_STRIDE, HEAD_DIM, CMP_HIDDEN)
    eye = jnp.eye(N_KV, dtype=w1.dtype)
    w1_x = jnp.einsum('jsde,kq->jskdqe', w1_r, eye).reshape(c, _CH_W, N_KV * CMP_HIDDEN)
    w2_x = jnp.einsum('ed,kq->keqd', w2, eye).reshape(N_KV * CMP_HIDDEN, KV_DIM)
    return pe_x, w1_x.astype(BF16), w2_x.astype(BF16)


def _hgrn_chunk(q, k, v, lf, st_ref, chunk, sub):
    if chunk > SUBLANES:
        r = lax.broadcasted_iota(jnp.int32, (chunk, chunk), 0)
        c = lax.broadcasted_iota(jnp.int32, (chunk, chunk), 1)
        tri = (r >= c).astype(F32)
        b = jnp.dot(tri, lf, preferred_element_type=F32, precision=HIGHEST)
    else:
        rows = [lf[0:1, :]]
        for t in range(1, chunk):
            rows.append(rows[-1] + lf[t:t + 1, :])
        b = jnp.concatenate(rows, axis=0)
    bl = b[chunk - 1:chunk, :]
    qe = q * jnp.exp(b)
    kd = k * jnp.exp(bl - b)
    ebl = jnp.exp(bl)
    n_sub = chunk // sub
    trow = lax.broadcasted_iota(jnp.int32, (sub, HG_W), 0)
    crow = lax.broadcasted_iota(jnp.int32, (chunk, HG_W), 0)

    diag = []
    for i in range(n_sub):
        qi = q[i * sub:(i + 1) * sub, :]
        bi = b[i * sub:(i + 1) * sub, :]
        acc = [jnp.zeros((sub, HG_DIM), F32) for _ in range(HG_HEADS)]
        for s in range(sub):
            row = i * sub + s
            dec = jnp.exp(jnp.where(trow >= s, bi - b[row:row + 1, :], -jnp.inf))
            prod = qi * k[row:row + 1, :] * dec
            for h in range(HG_HEADS):
                a = jnp.sum(prod[:, h * HG_DIM:(h + 1) * HG_DIM], axis=1, keepdims=True)
                acc[h] = acc[h] + a * v[row:row + 1, h * HG_DIM:(h + 1) * HG_DIM]
        diag.append(acc)

    off = []
    for i in range(n_sub):
        if i == 0:
            off.append(None)
            continue
        b0 = b[i * sub - 1:i * sub, :]
        qs = (q[i * sub:(i + 1) * sub, :] * jnp.exp(b[i * sub:(i + 1) * sub, :] - b0)).astype(BF16)
        ks = (k * jnp.exp(jnp.where(crow < i * sub, b0 - b, -jnp.inf))).astype(BF16)
        off.append((qs, ks))

    vb = v.astype(BF16)
    outs = []
    for h in range(HG_HEADS):
        sl = slice(h * HG_DIM, (h + 1) * HG_DIM)
        st = st_ref[h]
        o_h = lax.dot_general(qe[:, sl].astype(BF16), st.astype(BF16), (((1,), (1,)), ((), ())),
                              preferred_element_type=F32)
        parts = []
        for i in range(n_sub):
            d = diag[i][h]
            if off[i] is not None:
                qs, ks = off[i]
                a = lax.dot_general(qs[:, sl], ks[:, sl], (((1,), (1,)), ((), ())),
                                    preferred_element_type=F32)
                d = d + jnp.dot(a.astype(BF16), vb[:, sl], preferred_element_type=F32)
            parts.append(d)
        intra = parts[0] if n_sub == 1 else jnp.concatenate(parts, axis=0)
        outs.append(o_h + intra)
        st_ref[h] = st * ebl[:, sl] + lax.dot_general(
            vb[:, sl], kd[:, sl].astype(BF16), (((0,), (0,)), ((), ())), preferred_element_type=F32)
    return jnp.concatenate(outs, axis=1)


def _hgrn_finish(o, gs, gn):
    outs = []
    for h in range(HG_HEADS):
        oh = o[:, h * HG_DIM:(h + 1) * HG_DIM]
        y = oh * lax.rsqrt(jnp.mean(oh * oh, axis=-1, keepdims=True) + EPS) * gn
        outs.append(y)
    return (jnp.concatenate(outs, axis=1) * gs).astype(BF16)


def _hgrn_prompt_kernel(q_ref, k_ref, v_ref, lf_ref, gs_ref, gn_ref, o_ref, s_ref, st_ref, *, n_chunks):
    i = pl.program_id(0)

    @pl.when(i == 0)
    def _():
        st_ref[...] = jnp.zeros_like(st_ref)

    def body(c, carry):
        r = pl.ds(pl.multiple_of(c * HG_CHUNK, HG_CHUNK), HG_CHUNK)
        o = _hgrn_chunk(q_ref[r, :], k_ref[r, :], v_ref[r, :], lf_ref[r, :], st_ref, HG_CHUNK, HG_SUB)
        o_ref[r, :] = _hgrn_finish(o, gs_ref[r, :], gn_ref[...])
        return carry
    lax.fori_loop(0, n_chunks, body, 0)

    @pl.when(i == pl.num_programs(0) - 1)
    def _():
        for h in range(HG_HEADS):
            s_ref[h] = st_ref[h].T


def _hgrn_prompt(qh, kh, vh, lf, gs, gn, rows_per_step=512):
    t = qh.shape[0]
    spec = pl.BlockSpec((rows_per_step, HG_W), lambda i: (i, 0))
    return pl.pallas_call(
        functools.partial(_hgrn_prompt_kernel, n_chunks=rows_per_step // HG_CHUNK),
        grid=(t // rows_per_step,),
        in_specs=[spec] * 5 + [_const_spec((1, HG_DIM))],
        out_specs=[spec, pl.BlockSpec((HG_HEADS, HG_DIM, HG_DIM), lambda i: (0, 0, 0))],
        out_shape=[jax.ShapeDtypeStruct((t, HG_W), BF16),
                   jax.ShapeDtypeStruct((HG_HEADS, HG_DIM, HG_DIM), F32)],
        scratch_shapes=[pltpu.VMEM((HG_HEADS, HG_DIM, HG_DIM), F32)],
        compiler_params=_cparams(("arbitrary",)),
        name="hgrn_prompt",
    )(qh, kh, vh, lf, gs, gn)


def _hgrn_sample_kernel(q_ref, k_ref, v_ref, lf_ref, gs_ref, gn_ref, s0_ref, o_ref, s_ref, st_ref, *, t):
    for h in range(HG_HEADS):
        st_ref[h] = s0_ref[0, h].T
    o = _hgrn_chunk(q_ref[0], k_ref[0], v_ref[0], lf_ref[0], st_ref, t, t)
    o_ref[0] = _hgrn_finish(o, gs_ref[0], gn_ref[...])
    for h in range(HG_HEADS):
        s_ref[0, h] = st_ref[h].T


def _hgrn_sample(qh, kh, vh, lf, gs, gn, s0):
    nb, t, _ = qh.shape
    spec = pl.BlockSpec((1, t, HG_W), lambda b: (b, 0, 0))
    sspec = pl.BlockSpec((1, HG_HEADS, HG_DIM, HG_DIM), lambda b: (b, 0, 0, 0))
    return pl.pallas_call(
        functools.partial(_hgrn_sample_kernel, t=t),
        grid=(nb,),
        in_specs=[spec] * 5 + [_const_spec((1, HG_DIM)), sspec],
        out_specs=[spec, sspec],
        out_shape=[jax.ShapeDtypeStruct((nb, t, HG_W), BF16),
                   jax.ShapeDtypeStruct((nb, HG_HEADS, HG_DIM, HG_DIM), F32)],
        scratch_shapes=[pltpu.VMEM((HG_HEADS, HG_DIM, HG_DIM), F32)],
        compiler_params=_cparams(("arbitrary",)),
        name="hgrn_sample",
    )(qh, kh, vh, lf, gs, gn, s0)


def _select_topk(x, blk, n):
    nblk = x.shape[0]
    sel = jnp.zeros_like(x)
    for _ in range(n):
        m = jnp.max(x, axis=0, keepdims=True)
        idx = jnp.min(jnp.where(x == m, blk, float(nblk)), axis=0, keepdims=True)
        pick = blk == idx
        sel = jnp.where(pick, 1.0, sel)
        x = jnp.where(pick, -3e38, x)
    return sel


def _softmax_cols(s, valid):
    m = jnp.max(jnp.where(valid, s, NEG), axis=0, keepdims=True)
    p = jnp.where(valid, jnp.exp2(s - m), 0.0)
    l = jnp.sum(p, axis=0, keepdims=True)
    return p * jnp.where(l > 0.0, 1.0 / l, 0.0)


def _split_dot(a_bf16, x):
    hi = x.astype(BF16)
    lo = (x - hi.astype(F32)).astype(BF16)
    return (jnp.dot(a_bf16, hi, preferred_element_type=F32)
            + jnp.dot(a_bf16, lo, preferred_element_type=F32))


def _nt(a, b):
    return lax.dot_general(a, b, (((1,), (1,)), ((), ())), preferred_element_type=F32)


def _online_update(s, v, m_ref, l_ref, acc_ref, v_transposed=False):
    m_old = m_ref[...]
    m_new = jnp.maximum(m_old, jnp.max(s, axis=1, keepdims=True))
    p = jnp.exp2(s - m_new)
    alpha = jnp.exp2(m_old - m_new)
    l_ref[...] = alpha * l_ref[...] + jnp.sum(p, axis=1, keepdims=True)
    pv = _nt(p.astype(BF16), v) if v_transposed else jnp.dot(p.astype(BF16), v, preferred_element_type=F32)
    acc_ref[...] = alpha * acc_ref[...] + pv
    m_ref[...] = m_new


def _score_matrix(n_rows, row_offset, n_blocks):
    c = CMP_BLOCK // CMP_STRIDE
    ratio = SEL_BLOCK // CMP_STRIDE
    n_ov = ratio + c - 1
    m = np.zeros((n_blocks, n_rows), np.float32)
    for j in range(n_blocks):
        for u in range(n_ov):
            start = CMP_STRIDE * (u - (c - 1))
            w_u = (min(start + CMP_BLOCK, SEL_BLOCK) - max(start, 0)) / CMP_STRIDE
            n = ratio * j + u - (c - 1)
            if 0 <= n and n + row_offset < n_rows:
                m[j, n + row_offset] = w_u
    return m


def _nsa_prompt_kernel(qt_ref, gt_ref, kc_ref, vct_ref, ks_ref, vst_ref, kw_ref, vwt_ref,
                       mt_ref, cc_ref, cs_ref, bw_ref, o_ref,
                       sc_ref, sel_ref, m_ref, l_ref, acc_ref, oc_ref,
                       sa_ref, sb_ref, pa_ref, pb_ref, ala_ref, alb_ref, *, n_blocks):
    qb = pl.program_id(0)
    nrow = kc_ref.shape[0]
    gq = GROUP * Q_BLOCK
    tiles_per_q = SEL_TILE // Q_BLOCK
    frow = lax.broadcasted_iota(jnp.int32, (KV_DIM, Q_BLOCK), 0)

    def lanes4(x):
        return jnp.concatenate([x] * GROUP, axis=1)

    qx = []
    for kv in range(N_KV):
        keep = (frow >= HEAD_DIM) if kv else (frow < HEAD_DIM)
        qx.append(jnp.concatenate(
            [jnp.where(keep, qt_ref[j * KV_DIM:(j + 1) * KV_DIM, :], jnp.zeros((), BF16))
             for j in range(GROUP)], axis=1))

    r = lax.broadcasted_iota(jnp.int32, (nrow, Q_BLOCK), 0)
    qpos_c = qb * Q_BLOCK + lax.broadcasted_iota(jnp.int32, (nrow, Q_BLOCK), 1)
    end_pos = (r - CMP_PAD) * CMP_STRIDE + (CMP_BLOCK - 1)
    vis = (r >= CMP_PAD) & (r < CMP_PAD + n_blocks * (SEL_BLOCK // CMP_STRIDE) - 1) & (end_pos <= qpos_c)
    vis_add = lanes4(jnp.where(vis, 0.0, NEG))
    band = pl.ds(pl.multiple_of(qb * SUBLANES, SUBLANES), CMP_BAND)
    blk_i = lax.broadcasted_iota(jnp.int32, (n_blocks, Q_BLOCK), 0)
    cur = (qb * Q_BLOCK + lax.broadcasted_iota(jnp.int32, (n_blocks, Q_BLOCK), 1)) // SEL_BLOCK
    forced = (blk_i == 0) | (blk_i == cur) | (blk_i == cur - 1)
    scores = []
    for kv in range(N_KV):
        sc_ref[...] =jnp.dot(kc_ref[...], qx[kv], preferred_element_type=F32) + vis_add
        sc_ref[band, :] = sc_ref[band, :] + cc_ref[kv]
        s = sc_ref[...]
        m = jnp.max(s, axis=0, keepdims=True)
        p = jnp.exp2(s - m)
        l = jnp.sum(p, axis=0, keepdims=True)
        pn = p * jnp.where(m > 0.5 * NEG, 1.0 / l, 0.0)
        oc_ref[kv] = jnp.dot(vct_ref[...], pn.astype(BF16), preferred_element_type=F32)
        imp = pn[:, 0:Q_BLOCK]
        for j in range(1, GROUP):
            imp = imp + pn[:, j * Q_BLOCK:(j + 1) * Q_BLOCK]
        score = _split_dot(mt_ref[...], imp)
        scores.append(jnp.where(blk_i <= cur, score + jnp.where(forced, FORCE_BONUS, 0.0), -FORCE_BONUS))
    blk2 = jnp.concatenate([blk_i.astype(F32)] * N_KV, axis=1)
    sel = _select_topk(jnp.concatenate(scores, axis=1), blk2, N_SEL)
    for kv in range(N_KV):
        sel_ref[kv] = (sel[:, kv * Q_BLOCK:(kv + 1) * Q_BLOCK] - 1.0) * (-NEG)

    m_ref[...] = jnp.full_like(m_ref, NEG)
    l_ref[...] = jnp.zeros_like(l_ref)
    acc_ref[...] = jnp.zeros_like(acc_ref)

    def tile(kt, near):
        rows = pl.ds(pl.multiple_of(kt * SEL_TILE, SEL_TILE), SEL_TILE)
        k = ks_ref[rows, :]
        vt = vst_ref[kt]
        d0 = qb - kt * tiles_per_q
        for kv in range(N_KV):
            s = jnp.dot(k, qx[kv], preferred_element_type=F32)
            srow = sel_ref[kv, pl.ds(pl.multiple_of(kt * BLK_PER_TILE, BLK_PER_TILE), BLK_PER_TILE), :]
            mask = jnp.concatenate(
                [jnp.broadcast_to(srow[i:i + 1, :], (SEL_BLOCK, Q_BLOCK)) for i in range(BLK_PER_TILE)], axis=0)
            s = s + lanes4(mask)
            if near:
                s = s + jnp.concatenate(
                    [cs_ref[kv, jnp.clip(d0 - i, -1, 2) + 1] for i in range(tiles_per_q)], axis=0)
            m_old = m_ref[kv]
            m_new = jnp.maximum(m_old, jnp.max(s, axis=0, keepdims=True))
            p = jnp.exp2(s - m_new)
            alpha = jnp.exp2(m_old - m_new)
            l_ref[kv] = alpha * l_ref[kv] + jnp.sum(p, axis=0, keepdims=True)
            acc_ref[kv] = alpha * acc_ref[kv] + jnp.dot(vt, p.astype(BF16), preferred_element_type=F32)
            m_ref[kv] = m_new

    n_far = jnp.maximum(qb - 1, 0) // tiles_per_q
    max_tile = n_blocks // BLK_PER_TILE - 1

    def qk_stage(kt, s_ref):
        k = ks_ref[pl.ds(pl.multiple_of(kt * SEL_TILE, SEL_TILE), SEL_TILE), :]
        for kv in range(N_KV):
            s_ref[kv] = jnp.dot(k, qx[kv], preferred_element_type=F32)

    def pv_stage(kt, p_ref, al_ref):
        vt = vst_ref[kt]
        for kv in range(N_KV):
            acc_ref[kv] = al_ref[kv] * acc_ref[kv] + jnp.dot(vt, p_ref[kv], preferred_element_type=F32)

    def sm_stage(kt, s_ref, p_ref, al_ref, penalty):
        for kv in range(N_KV):
            srow = sel_ref[kv, pl.ds(pl.multiple_of(kt * BLK_PER_TILE, BLK_PER_TILE), BLK_PER_TILE), :]
            srow = srow + penalty
            mask = jnp.concatenate(
                [jnp.broadcast_to(srow[i:i + 1, :], (SEL_BLOCK, Q_BLOCK)) for i in range(BLK_PER_TILE)], axis=0)
            s = s_ref[kv] + lanes4(mask)
            m_old = m_ref[kv]
            m_new = jnp.maximum(m_old, jnp.max(s, axis=0, keepdims=True))
            p = jnp.exp2(s - m_new)
            alpha = jnp.exp2(m_old - m_new)
            l_ref[kv] = alpha * l_ref[kv] + jnp.sum(p, axis=0, keepdims=True)
            p_ref[kv] = p.astype(BF16)
            al_ref[kv] = alpha
            m_ref[kv] = m_new

    pb_ref[...] = jnp.zeros_like(pb_ref)
    alb_ref[...] = jnp.ones_like(alb_ref)
    qk_stage(0, sa_ref)

    def pair_body(u, c):
        t0 = 2 * u
        t1 = jnp.minimum(t0 + 1, max_tile)
        t2 = jnp.minimum(t0 + 2, max_tile)
        qk_stage(t1, sb_ref)
        pv_stage(jnp.maximum(t0 - 1, 0), pb_ref, alb_ref)
        sm_stage(t0, sa_ref, pa_ref, ala_ref, 0.0)
        qk_stage(t2, sa_ref)
        pv_stage(t0, pa_ref, ala_ref)
        sm_stage(t1, sb_ref, pb_ref, alb_ref, jnp.where(t0 + 1 < n_far, 0.0, NEG))
        return c
    n_pairs = (n_far + 1) // 2
    lax.fori_loop(0, n_pairs, pair_body, 0)
    pv_stage(jnp.clip(2 * n_pairs - 1, 0, max_tile), pb_ref, alb_ref)

    def near_body(kt, c):
        tile(kt, True)
        return c
    lax.fori_loop(n_far, qb // tiles_per_q + 1, near_body, 0)

    nband = WINDOW + Q_BLOCK
    kband = kw_ref[pl.ds(pl.multiple_of(qb * Q_BLOCK, Q_BLOCK), nband), :]
    vband = jnp.concatenate([vwt_ref[qb + i] for i in range(nband // Q_BLOCK)], axis=1)
    krow = lax.broadcasted_iota(jnp.int32, (nband, Q_BLOCK), 0)
    pos_add = lanes4(jnp.where(krow >= WINDOW - qb * Q_BLOCK, 0.0, NEG))
    o_kv = []
    for kv in range(N_KV):
        s = jnp.dot(kband, qx[kv], preferred_element_type=F32) + bw_ref[kv] + pos_add
        m = jnp.max(s, axis=0, keepdims=True)
        p = jnp.exp2(s - m)
        o_w = (jnp.dot(vband, p.astype(BF16), preferred_element_type=F32)
               / jnp.sum(p, axis=0, keepdims=True))
        o_s = acc_ref[kv] / l_ref[kv]

        def gate(i):
            return jnp.concatenate(
                [gt_ref[3 * (kv * GROUP + j) + i:3 * (kv * GROUP + j) + i + 1, :] for j in range(GROUP)], axis=1)
        o_kv.append(gate(0) * oc_ref[kv] + gate(1) * o_s + gate(2) * o_w)

    frow4 = lax.broadcasted_iota(jnp.int32, (KV_DIM, gq), 0)
    o_t = jnp.where(frow4 < HEAD_DIM, o_kv[0], o_kv[1])
    for j in range(GROUP):
        o_ref[:, j * LANES:(j + 1) * LANES] = o_t[:, j * Q_BLOCK:(j + 1) * Q_BLOCK].T.astype(BF16)


def _nsa_prompt(qat, gat, kc, vct, ksb, vst3, kwb, vwt3, mt, cc, cs, bw):
    t = qat.shape[1]
    n_blocks = t // SEL_BLOCK
    nrow = kc.shape[0]
    gq = GROUP * Q_BLOCK
    return pl.pallas_call(
        functools.partial(_nsa_prompt_kernel, n_blocks=n_blocks),
        grid=(t // Q_BLOCK,),
        in_specs=[pl.BlockSpec((Q_DIM, Q_BLOCK), lambda i: (0, i)),
                  pl.BlockSpec((LANES, Q_BLOCK), lambda i: (0, i)),
                  _const_spec(kc.shape), _const_spec(vct.shape),
                  _const_spec(ksb.shape), _const_spec(vst3.shape),
                  _const_spec(kwb.shape), _const_spec(vwt3.shape),
                  _const_spec(mt.shape), _const_spec(cc.shape),
                  _const_spec(cs.shape), _const_spec(bw.shape)],
        out_specs=pl.BlockSpec((Q_BLOCK, Q_DIM), lambda i: (i, 0)),
        out_shape=jax.ShapeDtypeStruct((t, Q_DIM), BF16),
        scratch_shapes=[pltpu.VMEM((nrow, gq), F32),
                        pltpu.VMEM((N_KV, n_blocks, Q_BLOCK), F32),
                        pltpu.VMEM((N_KV, 1, gq), F32), pltpu.VMEM((N_KV, 1, gq), F32),
                        pltpu.VMEM((N_KV, KV_DIM, gq), F32),
                        pltpu.VMEM((N_KV, KV_DIM, gq), F32),
                        pltpu.VMEM((N_KV, SEL_TILE, gq), F32), pltpu.VMEM((N_KV, SEL_TILE, gq), F32),
                        pltpu.VMEM((N_KV, SEL_TILE, gq), BF16), pltpu.VMEM((N_KV, SEL_TILE, gq), BF16),
                        pltpu.VMEM((N_KV, 1, gq), F32), pltpu.VMEM((N_KV, 1, gq), F32)],
        compiler_params=_cparams(("arbitrary",)),
        name="nsa_prompt",
    )(qat, gat, kc, vct, ksb, vst3, kwb, vwt3, mt, cc, cs, bw)


S_ROWS = 128
S_CHUNK_PAGES = 32
S_CHUNK = S_CHUNK_PAGES * PAGE


def _nsa_sample_kernel(pt_ref, qx_ref, gm_ref, kc_ref, vct_ref, kpool_ref, vpool_ref,
                       knew_ref, vnew_ref, kwin_ref, vwin_ref, kwnew_ref, vwnew_ref,
                       mt_ref, gsum_ref, ccs_ref, css_ref, cns_ref, cws_ref, o_ref,
                       kbuf, vbuf, sem, mask_ref, m_ref, l_ref, acc_ref, oc_ref, ow_ref,
                       *, n_chunks, n_blocks):
    b = pl.program_id(0)
    c = pl.program_id(1)
    step = b * n_chunks + c
    total = pl.num_programs(0) * n_chunks
    slot = step % 2

    def copies(bb, cc, s, p):
        pg = cc * S_CHUNK_PAGES + p
        dst = pl.ds(pl.multiple_of(p * PAGE, PAGE), PAGE)
        return (pltpu.make_async_copy(kpool_ref.at[pt_ref[bb, pg]], kbuf.at[s, :, dst], sem.at[0, s]),
                pltpu.make_async_copy(vpool_ref.at[pt_ref[bb, pg]], vbuf.at[s, :, dst], sem.at[1, s]))

    def start_all(st, s):
        bb = st // n_chunks
        cc = st % n_chunks

        def body(p, carry):
            ck, cv = copies(bb, cc, s, p)
            ck.start()
            cv.start()
            return carry
        lax.fori_loop(0, S_CHUNK_PAGES, body, 0)

    @pl.when(step == 0)
    def _():
        start_all(step, slot)

    @pl.when(step + 1 < total)
    def _():
        start_all(step + 1, 1 - slot)

    qx = qx_ref[0]

    @pl.when(c == 0)
    def _():
        nrow = kc_ref.shape[1]
        s = _nt(kc_ref[0], qx)
        r = lax.broadcasted_iota(jnp.int32, (nrow, S_ROWS), 0)
        band0 = nrow - CMP_BAND
        s = s + jnp.concatenate([jnp.zeros((band0, S_ROWS), F32), ccs_ref[...]], axis=0)
        pn = _softmax_cols(s, r < nrow - 1)
        oc_ref[...] = jnp.dot(vct_ref[0], pn.astype(BF16), preferred_element_type=F32).T
        imp = jnp.dot(pn, gsum_ref[...], preferred_element_type=F32, precision=HIGHEST)
        score = _split_dot(mt_ref[...], imp)
        blk_i = lax.broadcasted_iota(jnp.int32, (n_blocks, S_ROWS), 0)
        forced = (blk_i == 0) | (blk_i == n_blocks - 1)
        score = score + jnp.where(forced, FORCE_BONUS, 0.0)
        selt = _select_topk(score, blk_i.astype(F32), N_SEL - 1)
        selt = lax.dot_general(selt.astype(BF16), gsum_ref[...].astype(BF16), (((1,), (1,)), ((), ())),
                               preferred_element_type=F32)
        selm1 = selt.T - 1.0
        e_r = lax.broadcasted_iota(jnp.int32, (BLK_PER_TILE, SEL_TILE), 0)
        e_c = lax.broadcasted_iota(jnp.int32, (BLK_PER_TILE, SEL_TILE), 1)
        expand = jnp.where(e_c // SEL_BLOCK == e_r, -NEG, 0.0)
        for kt in range(n_blocks // BLK_PER_TILE):
            mask_ref[:, kt * SEL_TILE:(kt + 1) * SEL_TILE] = jnp.dot(
                selm1[:, kt * BLK_PER_TILE:(kt + 1) * BLK_PER_TILE], expand, preferred_element_type=F32)

        sw = _nt(qx, kwin_ref[0].astype(BF16)) + cws_ref[...]
        sn = _nt(qx, kwnew_ref[0].astype(BF16)) + cns_ref[...]
        m = jnp.maximum(jnp.max(sw, axis=1, keepdims=True), jnp.max(sn, axis=1, keepdims=True))
        pw = jnp.exp2(sw - m)
        pn2 = jnp.exp2(sn - m)
        l = jnp.sum(pw, axis=1, keepdims=True) + jnp.sum(pn2, axis=1, keepdims=True)
        ow = (jnp.dot(pw.astype(BF16), vwin_ref[0].astype(BF16), preferred_element_type=F32)
              + jnp.dot(pn2.astype(BF16), vwnew_ref[0].astype(BF16), preferred_element_type=F32))
        ow_ref[...] = ow / l

        m_ref[...] = jnp.full_like(m_ref, NEG)
        l_ref[...] = jnp.zeros_like(l_ref)
        acc_ref[...] = jnp.zeros_like(acc_ref)

    def wait_body(p, carry):
        ck, cv = copies(b, c, slot, p)
        ck.wait()
        cv.wait()
        return carry
    lax.fori_loop(0, S_CHUNK_PAGES, wait_body, 0)

    col0 = pl.multiple_of(c * S_CHUNK, S_CHUNK)
    s = (jnp.dot(qx, kbuf[slot].astype(BF16), preferred_element_type=F32)
         + mask_ref[:, pl.ds(col0, S_CHUNK)])

    @pl.when(c < n_chunks - 1)
    def _():
        _online_update(s, vbuf[slot].astype(BF16), m_ref, l_ref, acc_ref, v_transposed=True)

    @pl.when(c == n_chunks - 1)
    def _():
        near = jnp.concatenate([jnp.zeros((S_ROWS, S_CHUNK - LANES), F32), css_ref[...]], axis=1)
        _online_update(s + near, vbuf[slot].astype(BF16), m_ref, l_ref, acc_ref, v_transposed=True)
        sn = _nt(qx, knew_ref[0].astype(BF16)) + cns_ref[...]
        _online_update(sn, vnew_ref[0].astype(BF16), m_ref, l_ref, acc_ref)
        o_s = acc_ref[...] / l_ref[...]
        o_ref[0] = gm_ref[0, 0] * oc_ref[...] + gm_ref[0, 1] * o_s + gm_ref[0, 2] * ow_ref[...]


def _nsa_sample(page_table, qx, gm, kc, vct, kpool, vpool, knew, vnew, kwin, vwin, kwnew, vwnew,
                mt, gsum, ccs, css, cns, cws):
    nb = qx.shape[0]
    n_pages = page_table.shape[1]
    n_chunks = n_pages // S_CHUNK_PAGES
    past = n_pages * PAGE
    n_blocks = past // SEL_BLOCK
    nrow = kc.shape[1]

    def bspec(shape):
        nd = len(shape)
        return pl.BlockSpec((1,) + tuple(shape[1:]), lambda b, c, pt: (b,) + (0,) * (nd - 1))

    def cspec(shape):
        nd = len(shape)
        return pl.BlockSpec(tuple(shape), lambda b, c, pt: (0,) * nd, pipeline_mode=pl.Buffered(1))

    grid_spec = pltpu.PrefetchScalarGridSpec(
        num_scalar_prefetch=1,
        grid=(nb, n_chunks),
        in_specs=[bspec(qx.shape), bspec(gm.shape), bspec(kc.shape), bspec(vct.shape),
                  pl.BlockSpec(memory_space=pl.ANY), pl.BlockSpec(memory_space=pl.ANY),
                  bspec(knew.shape), bspec(vnew.shape), bspec(kwin.shape), bspec(vwin.shape),
                  bspec(kwnew.shape), bspec(vwnew.shape),
                  cspec(mt.shape), cspec(gsum.shape), cspec(ccs.shape), cspec(css.shape),
                  cspec(cns.shape), cspec(cws.shape)],
        out_specs=pl.BlockSpec((1, S_ROWS, KV_DIM), lambda b, c, pt: (b, 0, 0)),
        scratch_shapes=[pltpu.VMEM((2, KV_DIM, S_CHUNK), F32),
                        pltpu.VMEM((2, KV_DIM, S_CHUNK), F32),
                        pltpu.SemaphoreType.DMA((2, 2)),
                        pltpu.VMEM((S_ROWS, past), F32),
                        pltpu.VMEM((S_ROWS, 1), F32), pltpu.VMEM((S_ROWS, 1), F32),
                        pltpu.VMEM((S_ROWS, KV_DIM), F32),
                        pltpu.VMEM((S_ROWS, KV_DIM), F32), pltpu.VMEM((S_ROWS, KV_DIM), F32)],
    )
    return pl.pallas_call(
        functools.partial(_nsa_sample_kernel, n_chunks=n_chunks, n_blocks=n_blocks),
        grid_spec=grid_spec,
        out_shape=jax.ShapeDtypeStruct((nb, S_ROWS, KV_DIM), F32),
        compiler_params=_cparams(("arbitrary", "arbitrary")),
        name="nsa_sample",
    )(page_table, qx, gm, kc, vct, kpool, vpool, knew, vnew, kwin, vwin, kwnew, vwnew,
      mt, gsum, ccs, css, cns, cws)


def _ffn_kernel(x_ref, oa_ref, ob_ref, sa_ref, sb_ref, wpa_ref, wpb_ref, wo_ref, nf_ref,
                wg_ref, wu_ref, wd_ref, nl_ref, y_ref):
    pa = jnp.dot(oa_ref[...], wpa_ref[...], preferred_element_type=F32)
    pb = jnp.dot(ob_ref[...], wpb_ref[...], preferred_element_type=F32)
    merged = sa_ref[...] * pa + sb_ref[...] * pb
    x = x_ref[...] + jnp.dot(merged.astype(BF16), wo_ref[...], preferred_element_type=F32)
    hn = (x * lax.rsqrt(jnp.mean(x * x, axis=-1, keepdims=True) + EPS) * nf_ref[...]).astype(BF16)
    gate = jnp.dot(hn, wg_ref[...], preferred_element_type=F32)
    up = jnp.dot(hn, wu_ref[...], preferred_element_type=F32)
    ff = (jax.nn.silu(gate) * up).astype(BF16)
    x = x + jnp.dot(ff, wd_ref[...], preferred_element_type=F32)
    y_ref[...] = x * lax.rsqrt(jnp.mean(x * x, axis=-1, keepdims=True) + EPS) * nl_ref[...]


def _ffn(x2d, oa, ob, sa, sb, wpa, wpb, wo, nf, wg, wu, wd, nl, tm):
    rows = x2d.shape[0]

    def rspec(n):
        return pl.BlockSpec((tm, n), lambda i: (i, 0))

    return pl.pallas_call(
        _ffn_kernel,
        grid=(rows // tm,),
        in_specs=[rspec(D_MODEL), rspec(Q_DIM), rspec(HG_W), rspec(D_MODEL), rspec(D_MODEL),
                  _const_spec(wpa.shape), _const_spec(wpb.shape), _const_spec(wo.shape),
                  _const_spec(nf.shape), _const_spec(wg.shape), _const_spec(wu.shape),
                  _const_spec(wd.shape), _const_spec(nl.shape)],
        out_specs=rspec(D_MODEL),
        out_shape=jax.ShapeDtypeStruct((rows, D_MODEL), F32),
        compiler_params=_cparams(("arbitrary",)),
        name="ffn",
    )(x2d, oa, ob, sa, sb, wpa, wpb, wo, nf, wg, wu, wd, nl)


def _pack_w_in(w_in):
    sizes = (Q_DIM,) + (KV_DIM,) * 6 + (3 * N_HEADS,) + (HG_W,) * 4 + (D_MODEL,) * 2
    offs = np.concatenate([[0], np.cumsum(sizes)])
    q = w_in[:, offs[0]:offs[1]].reshape(D_MODEL, N_HEADS, HEAD_DIM)[:, _HEAD_PERM, :].reshape(D_MODEL, Q_DIM)
    g = jnp.pad(w_in[:, offs[7]:offs[8]], ((0, 0), (0, LANES - 3 * N_HEADS)))
    return jnp.concatenate([q, w_in[:, offs[1]:offs[7]], g, w_in[:, offs[8]:]], axis=1).astype(BF16)


def _strip(bvc, rel, lo=0, hi=None, masked=NEG):
    val = bvc[:, np.clip(rel, 0, 255)]
    ok = rel >= lo
    if hi is not None:
        ok = ok & (rel < hi)
    return jnp.where(jnp.asarray(ok)[None], val, masked)


def _toeplitz(bvc, a, n_rows, n_cols, lo=0, hi=None):
    n = n_rows + n_cols - 1
    u = _strip(bvc, a - (n_rows - 1) + np.arange(n), lo, hi)
    u = jnp.pad(u, ((0, 0), (0, 1)))
    circ = jnp.tile(u, (1, n_rows))[:, :n_rows * n].reshape(N_HEADS, n_rows, n)
    return circ[:, :, n_rows - 1:n_rows - 1 + n_cols]


def _bias_strips_prompt(bvc):
    gq = GROUP * Q_BLOCK

    def lanes(x):
        return x.reshape(N_KV, GROUP, x.shape[1], Q_BLOCK).transpose(0, 2, 1, 3).reshape(N_KV, x.shape[1], gq)

    cs = jnp.stack([lanes(_toeplitz(bvc, Q_BLOCK * d, Q_BLOCK, Q_BLOCK)) for d in (-1, 0, 1, 2)], axis=1)
    bw = lanes(_toeplitz(bvc, WINDOW, WINDOW + Q_BLOCK, Q_BLOCK, 0, WINDOW))
    rr = np.arange(CMP_BAND)[:, None]
    rel_c = np.arange(Q_BLOCK)[None, :] - CMP_STRIDE * (rr - CMP_PAD) - (CMP_BLOCK - 1)
    cc = _strip(bvc, rel_c, masked=0.0).reshape(N_KV, GROUP, CMP_BAND, Q_BLOCK)
    cc = cc.transpose(0, 2, 1, 3).reshape(N_KV, CMP_BAND, gq)
    return cc, cs, bw


def _bias_strips_sample(bvc, past, t):
    def rows(a):
        a = a.reshape(N_HEADS * t, a.shape[-1])
        return jnp.pad(a, ((0, S_ROWS - N_HEADS * t), (0, 0)))
    tt = np.arange(t)[:, None]
    nrow = past // CMP_STRIDE
    n = (nrow - CMP_BAND + np.arange(CMP_BAND))[None, :]
    ccs = rows(_strip(bvc, past + tt - CMP_STRIDE * n - (CMP_BLOCK - 1), masked=0.0)).T
    i = np.arange(LANES)[None, :]
    css = rows(_strip(bvc, LANES + tt - i))
    cns = rows(_strip(bvc, np.where(i < t, tt - i, -1)))
    iw = np.arange(WINDOW)[None, :]
    cws = rows(_strip(bvc, WINDOW + tt - iw, 0, WINDOW))
    return ccs, css, cns, cws


def kernel(x_prompt, x_sample, cache_k_cmp, cache_v_cmp, cache_k_slc, cache_v_slc, state_k_win, state_v_win,
           state_hgrn, page_table, norm_mix, w_in, cmp_pe_k, cmp_w1_k, cmp_w2_k, cmp_pe_v, cmp_w1_v, cmp_w2_v,
           rel_bias, hg_lb_logits, hg_norm, w_proj_a, w_proj_b, w_out, norm_ffn, w_gate, w_up, w_down, norm_final):
    nbp, t_p, _ = x_prompt.shape
    nbs, t_s, _ = x_sample.shape
    assert nbp == 1 and norm_mix.shape[0] == 1
    n_pages = page_table.shape[1]
    past = n_pages * PAGE
    assert state_k_win.shape[2] == WINDOW and past % S_CHUNK == 0 and t_s <= SUBLANES

    lb = jnp.cumsum(jax.nn.softmax(hg_lb_logits.astype(F32), axis=0), axis=0)[0]
    lb3 = jnp.pad(jnp.stack([jnp.log(lb), jnp.log1p(-lb), 1.0 - lb]), ((0, SUBLANES - 3), (0, 0)))
    w_pack = _pack_w_in(w_in[0])
    g_mix = norm_mix[0][None, :]
    wpa = w_proj_a[0].reshape(N_HEADS, HEAD_DIM, D_MODEL)[_HEAD_PERM].reshape(Q_DIM, D_MODEL).astype(BF16)
    wpb = w_proj_b[0].astype(BF16)
    wo = w_out[0].astype(BF16)
    wg, wu, wd = w_gate[0].astype(BF16), w_up[0].astype(BF16), w_down[0].astype(BF16)
    nf, nl = norm_ffn[0][None, :], norm_final[None, :]
    gn = hg_norm[0][None, :]
    pe_k, w1_k, w2_k = _compress_weights(cmp_pe_k[0], cmp_w1_k[0], cmp_w2_k[0])
    pe_v, w1_v, w2_v = _compress_weights(cmp_pe_v[0], cmp_w1_v[0], cmp_w2_v[0])
    bvc = (rel_bias[_BUCKET] - rel_bias[N_BUCKETS - 1][None, :]).T * LOG2E
    cc, cs, bw = _bias_strips_prompt(bvc)
    ccs, css, cns, cws = _bias_strips_sample(bvc, past, t_s)

    xp2 = x_prompt.reshape(t_p, D_MODEL)
    xs2 = x_sample.reshape(nbs * t_s, D_MODEL)
    seg = lambda off, n: w_pack[:, off:off + n]
    w_t = jnp.concatenate([seg(_OFF_Q, Q_DIM), seg(_OFF_G, LANES), seg(_OFF_KV + 3 * KV_DIM, KV_DIM),
                           seg(_OFF_KV + 5 * KV_DIM, KV_DIM)], axis=1).T
    pp = _proj(xp2, g_mix, w_pack, w_t, lb3, 256)
    ps = _proj(xs2, g_mix, w_pack, w_t, lb3, nbs * t_s)
    (_, kc_p, vc_p, ks_p, vs_p, kw_p, vw_p, ksb_p, kwb_p, _,
     qh_p, lf_p, kh_p, vh_p, gs_p, sa_p, sb_p, qat_p, gat_p, vst_p, vwt_p) = pp
    (qa_s, kc_s, vc_s, ks_s, vs_s, kw_s, vw_s, _, _, ga_s,
     qh_s, lf_s, kh_s, vh_s, gs_s, sa_s, sb_s, _, _, _, _) = ps

    ob_p, s_p = _hgrn_prompt(qh_p, kh_p, vh_p, lf_p, gs_p, gn)
    r3 = lambda a: a.reshape(nbs, t_s, a.shape[-1])
    ob_s, s_s = _hgrn_sample(r3(qh_s), r3(kh_s), r3(vh_s), r3(lf_s), r3(gs_s), gn, state_hgrn[0])

    ident = jnp.arange(t_p // PAGE, dtype=jnp.int32)[None, :]
    pool_rows = lambda a: a.reshape(-1, PAGE, KV_DIM)
    kcb_p, _ = _compress(pool_rows(kc_p), ident, pe_k, w1_k, w2_k)
    _, vct_p = _compress(pool_rows(vc_p), ident, pe_v, w1_v, w2_v)
    kcb_p = jnp.pad(kcb_p[0], ((CMP_PAD, 0), (0, 0)))
    vct_p = jnp.pad(vct_p[0], ((0, 0), (CMP_PAD, 0)))
    nrow_p = kcb_p.shape[0]
    mt_p = jnp.asarray(_score_matrix(nrow_p, CMP_PAD, t_p // SEL_BLOCK), BF16)
    kwb_pad = jnp.pad(kwb_p, ((WINDOW, 0), (0, 0)))
    vst3 = vst_p.reshape(KV_DIM, t_p // SEL_TILE, SEL_TILE).transpose(1, 0, 2)
    vwt3 = jnp.pad(vwt_p, ((0, 0), (WINDOW, 0))).reshape(KV_DIM, (t_p + WINDOW) // Q_BLOCK, Q_BLOCK)
    vwt3 = vwt3.transpose(1, 0, 2)
    oa_p = _nsa_prompt(qat_p, gat_p, kcb_p, vct_p, ksb_p, vst3, kwb_pad, vwt3, mt_p, cc, cs, bw)

    kcb_s, _ = _compress(pool_rows(cache_k_cmp[0]), page_table, pe_k, w1_k, w2_k)
    _, vct_s = _compress(pool_rows(cache_v_cmp[0]), page_table, pe_v, w1_v, w2_v)
    nq = N_HEADS * t_s
    qs4 = qa_s.reshape(nbs, t_s, GROUP, N_KV, HEAD_DIM).astype(F32)
    qx = jnp.einsum('btjkd,kq->bkjtqd', qs4, jnp.eye(N_KV, dtype=F32)).reshape(nbs, nq, KV_DIM)
    qx = jnp.pad(qx, ((0, 0), (0, S_ROWS - nq), (0, 0))).astype(BF16)
    g4 = ga_s[:, :3 * N_HEADS].reshape(nbs, t_s, N_KV, GROUP, 3)
    gm = jnp.transpose(g4, (0, 4, 2, 3, 1)).reshape(nbs, 3, nq, 1)
    gm = jnp.broadcast_to(jnp.pad(gm, ((0, 0), (0, 0), (0, S_ROWS - nq), (0, 0))), (nbs, 3, S_ROWS, KV_DIM))
    new_tile = lambda a: jnp.pad(a.reshape(nbs, t_s, KV_DIM), ((0, 0), (0, LANES - t_s), (0, 0)))
    mt_s = jnp.asarray(_score_matrix(past // CMP_STRIDE, 0, past // SEL_BLOCK), BF16)
    pool_t = lambda a: a.transpose(0, 2, 3, 1).reshape(-1, KV_DIM, PAGE)
    gsum = np.zeros((S_ROWS, S_ROWS), np.float32)
    for kv in range(N_KV):
        for j in range(GROUP):
            for t in range(t_s):
                gsum[(kv * GROUP + j) * t_s + t, kv * t_s + t] = 1.0
    o_kv = _nsa_sample(page_table, qx, gm, kcb_s, vct_s,
                       pool_t(cache_k_slc[0]), pool_t(cache_v_slc[0]),
                       new_tile(ks_s), new_tile(vs_s), state_k_win[0].reshape(nbs, WINDOW, KV_DIM),
                       state_v_win[0].reshape(nbs, WINDOW, KV_DIM), new_tile(kw_s), new_tile(vw_s),
                       mt_s, jnp.asarray(gsum), ccs, css, cns, cws)
    o5 = o_kv[:, :nq].reshape(nbs, N_KV, GROUP, t_s, N_KV, HEAD_DIM)
    oa_s = jnp.einsum('bkjtqd,kq->btjkd', o5, jnp.eye(N_KV, dtype=F32)).reshape(nbs * t_s, Q_DIM).astype(BF16)

    y_p = _ffn(xp2, oa_p, ob_p, sa_p, sb_p, wpa, wpb, wo, nf, wg, wu, wd, nl, 256)
    y_s = _ffn(xs2, oa_s, ob_s.reshape(nbs * t_s, HG_W), sa_s, sb_s, wpa, wpb, wo, nf, wg, wu, wd, nl, nbs * t_s)

    kv5 = lambda a, nb_, tt: a.reshape(1, nb_, tt, N_KV, HEAD_DIM)
    wl = min(WINDOW, t_p)
    win = lambda st, new: jnp.concatenate(
        [st[0], new.reshape(nbs, t_s, N_KV, HEAD_DIM)], axis=1)[:, -WINDOW:][None]
    return (y_p.reshape(1, t_p, D_MODEL), y_s.reshape(nbs, t_s, D_MODEL),
            kv5(kc_p, 1, t_p), kv5(vc_p, 1, t_p), kv5(ks_p, 1, t_p), kv5(vs_p, 1, t_p),
            kv5(kw_p[-wl:], 1, wl), kv5(vw_p[-wl:], 1, wl), s_p[None, None],
            kv5(kc_s, nbs, t_s), kv5(vc_s, nbs, t_s), kv5(ks_s, nbs, t_s), kv5(vs_s, nbs, t_s),
            win(state_k_win, kw_s), win(state_v_win, vw_s), s_s[None])
```

```python
import functools
import math

import numpy as np
import jax
import jax.numpy as jnp
from jax import lax
from jax.experimental import pallas as pl
from jax.experimental.pallas import tpu as pltpu

F32 = jnp.float32
BF16 = jnp.bfloat16
HIGHEST = lax.Precision.HIGHEST

D_MODEL = 1024
N_HEADS = 8
N_KV = 2
GROUP = N_HEADS // N_KV
HEAD_DIM = 64
KV_DIM = N_KV * HEAD_DIM
Q_DIM = N_HEADS * HEAD_DIM
CMP_BLOCK = 32
CMP_STRIDE = 16
CMP_HIDDEN = 2 * HEAD_DIM
SEL_BLOCK = 64
N_SEL = 16
WINDOW = 512
Q_BLOCK = 128
FORCE_BONUS = 1e4
N_BUCKETS = 32
MAX_DISTANCE = 128
HG_HEADS = 4
HG_DIM = 128
HG_CHUNK = 64
HG_SUB = 16
HG_W = HG_HEADS * HG_DIM
D_FF = ((8 * D_MODEL // 3 + 255) // 256) * 256
EPS = 1e-6
PAGE = 128
NEG = -1e30
LOG2E = math.log2(math.e)
Q_SCALE = HEAD_DIM ** -0.5 * LOG2E

LANES = 128
SUBLANES = 8
VMEM_LIMIT = 56 * 1024 * 1024

_OFF_Q = 0
_OFF_KV = _OFF_Q + Q_DIM
_OFF_G = _OFF_KV + 6 * KV_DIM
_OFF_HG = _OFF_G + LANES
_OFF_GATE = _OFF_HG + 4 * HG_W
_PROJ_N = _OFF_GATE + 2 * D_MODEL

_HEAD_PERM = np.array([h for j in range(GROUP) for h in (j, GROUP + j)])

SEL_TILE = 512
BLK_PER_TILE = SEL_TILE // SEL_BLOCK
CMP_PAD = 16
CMP_BAND = 24


def _cparams(sem, vmem=VMEM_LIMIT):
    return pltpu.CompilerParams(dimension_semantics=sem, vmem_limit_bytes=vmem)


def _const_spec(shape):
    nd = len(shape)
    return pl.BlockSpec(shape, lambda *_: (0,) * nd, pipeline_mode=pl.Buffered(1))


def _bucket_table():
    n = np.arange(256)
    max_exact = N_BUCKETS // 2
    nf = np.maximum(n, 1).astype(np.float64)
    large = max_exact + (np.log(nf / max_exact) / math.log(MAX_DISTANCE / max_exact)
                         * (N_BUCKETS - max_exact)).astype(np.int64)
    large = np.minimum(large, N_BUCKETS - 1)
    return np.where(n < max_exact, n, large)


_BUCKET = _bucket_table()


def _proj_kernel(x_ref, g_ref, w_ref, wt_ref, lb_ref,
                 qa_ref, kc_ref, vc_ref, ks_ref, vs_ref, kw_ref, vw_ref,
                 ksb_ref, kwb_ref, ga_ref,
                 qh_ref, lf_ref, kh_ref, vh_ref, gs_ref, sa_ref, sb_ref,
                 qat_ref, gat_ref, vst_ref, vwt_ref):
    x = x_ref[...]
    xn = x * lax.rsqrt(jnp.mean(x * x, axis=-1, keepdims=True) + EPS) * g_ref[...]
    xb = xn.astype(BF16)

    def seg(a, n):
        return jnp.dot(xb, w_ref[:, a:a + n], preferred_element_type=F32)

    qa_ref[...] = (seg(_OFF_Q, Q_DIM) * Q_SCALE).astype(BF16)
    f32_refs = (kc_ref, vc_ref, ks_ref, vs_ref, kw_ref, vw_ref)
    b16_refs = (None, None, ksb_ref, None, kwb_ref, None)
    for i in range(6):
        u = seg(_OFF_KV + i * KV_DIM, KV_DIM)
        f32_refs[i][...] = u
        if b16_refs[i] is not None:
            b16_refs[i][...] = u.astype(BF16)
    ga_ref[...] = jax.nn.sigmoid(seg(_OFF_G, LANES))

    def seg_t(a, n):
        return lax.dot_general(wt_ref[a:a + n, :], xb, (((1,), (1,)), ((), ())), preferred_element_type=F32)

    qat_ref[...] = (seg_t(0, Q_DIM) * Q_SCALE).astype(BF16)
    gat_ref[...] = jax.nn.sigmoid(seg_t(Q_DIM, LANES))
    vst_ref[...] = seg_t(Q_DIM + LANES, KV_DIM).astype(BF16)
    vwt_ref[...] = seg_t(Q_DIM + LANES + KV_DIM, KV_DIM).astype(BF16)

    log_lb = lb_ref[0:1, :]
    log_1m = lb_ref[1:2, :]
    one_m = lb_ref[2:3, :]
    qh_ref[...] = jax.nn.silu(seg(_OFF_HG, HG_W))
    z = seg(_OFF_HG + HG_W, HG_W)
    b = log_1m + (jnp.minimum(z, 0.0) - jnp.log1p(jnp.exp(-jnp.abs(z))))
    hi = jnp.maximum(log_lb, b)
    lf_ref[...] = hi + jnp.log1p(jnp.exp(-jnp.abs(log_lb - b)))
    kh_ref[...] = one_m * jax.nn.sigmoid(-z)
    vh_ref[...] = seg(_OFF_HG + 2 * HG_W, HG_W)
    gs_ref[...] = jax.nn.silu(seg(_OFF_HG + 3 * HG_W, HG_W))
    sa_ref[...] = jax.nn.sigmoid(seg(_OFF_GATE, D_MODEL))
    sb_ref[...] = jax.nn.sigmoid(seg(_OFF_GATE + D_MODEL, D_MODEL))


_PROJ_T = Q_DIM + LANES + 2 * KV_DIM


def _proj(x2d, g, w, wt, lb3, tm):
    rows = x2d.shape[0]
    widths = ([(Q_DIM, BF16)] + [(KV_DIM, F32)] * 6 + [(KV_DIM, BF16)] * 2 + [(LANES, F32)]
              + [(HG_W, F32)] * 5 + [(D_MODEL, F32)] * 2)
    heights = [(Q_DIM, BF16), (LANES, F32), (KV_DIM, BF16), (KV_DIM, BF16)]
    return pl.pallas_call(
        _proj_kernel,
        grid=(rows // tm,),
        in_specs=[pl.BlockSpec((tm, D_MODEL), lambda i: (i, 0)),
                  _const_spec((1, D_MODEL)),
                  _const_spec((D_MODEL, _PROJ_N)),
                  _const_spec((_PROJ_T, D_MODEL)),
                  _const_spec((SUBLANES, HG_W))],
        out_specs=([pl.BlockSpec((tm, n), lambda i: (i, 0)) for n, _ in widths]
                   + [pl.BlockSpec((n, tm), lambda i: (0, i)) for n, _ in heights]),
        out_shape=([jax.ShapeDtypeStruct((rows, n), dt) for n, dt in widths]
                   + [jax.ShapeDtypeStruct((n, rows), dt) for n, dt in heights]),
        compiler_params=_cparams(("arbitrary",)),
        name="proj",
    )(x2d, g, w, wt, lb3)


_CH_W = CMP_STRIDE * KV_DIM
_CH_PER_PAGE = PAGE // CMP_STRIDE


def _compress_kernel(pt_ref, pool_ref, pe_ref, w1_ref, w2_ref, out_ref, outt_ref, buf, xa, hbuf, sem,
                     *, n_pages):
    b = pl.program_id(0)
    nb = pl.num_programs(0)
    slot = b % 2
    n_ch = n_pages * _CH_PER_PAGE

    def page_copy(bb, p, s):
        return pltpu.make_async_copy(pool_ref.at[pt_ref[bb, p]],
                                     buf.at[s, pl.ds(pl.multiple_of(p * PAGE, PAGE), PAGE)],
                                     sem.at[s])

    def start_all(bb, s):
        def body(p, c):
            page_copy(bb, p, s).start()
            return c
        lax.fori_loop(0, n_pages, body, 0)

    @pl.when(b == 0)
    def _():
        start_all(b, slot)

    @pl.when(b + 1 < nb)
    def _():
        start_all(b + 1, 1 - slot)

    def wait_body(p, c):
        page_copy(b, p, slot).wait()
        return c
    lax.fori_loop(0, n_pages, wait_body, 0)

    rows = math.gcd(n_ch, 256)
    for j in range(2):
        for r in range(n_ch // rows):
            for s in range(CMP_STRIDE):
                x = buf[slot, pl.ds(r * rows * CMP_STRIDE + s, rows, stride=CMP_STRIDE), :]
                xa[:, s * KV_DIM:(s + 1) * KV_DIM] = (x + pe_ref[j:j + 1, s * KV_DIM:(s + 1) * KV_DIM]).astype(BF16)
            hbuf[j, r * rows:(r + 1) * rows, :] = jnp.dot(xa[...], w1_ref[j], preferred_element_type=F32)
    h = hbuf[0] + pltpu.roll(hbuf[1], n_ch - 1, 0)
    blocks = jnp.dot(jax.nn.gelu(h).astype(BF16), w2_ref[...], preferred_element_type=F32)
    row = lax.broadcasted_iota(jnp.int32, blocks.shape, 0)
    blocks = jnp.where(row < n_ch - 1, blocks, 0.0)
    out_ref[0] = blocks.astype(BF16)
    outt_ref[0] = blocks.T.astype(BF16)


def _compress(pool, page_table, pe, w1, w2):
    nbatch, n_pages = page_table.shape
    n_ch = n_pages * _CH_PER_PAGE
    rows = math.gcd(n_ch, 256)
    grid_spec = pltpu.PrefetchScalarGridSpec(
        num_scalar_prefetch=1,
        grid=(nbatch,),
        in_specs=[pl.BlockSpec(memory_space=pl.ANY),
                  _const_spec((SUBLANES, _CH_W)),
                  _const_spec((2, _CH_W, 2 * CMP_HIDDEN)),
                  _const_spec((2 * CMP_HIDDEN, KV_DIM))],
        out_specs=[pl.BlockSpec((1, n_ch, KV_DIM), lambda b, pt: (b, 0, 0)),
                   pl.BlockSpec((1, KV_DIM, n_ch), lambda b, pt: (b, 0, 0))],
        scratch_shapes=[pltpu.VMEM((2, n_pages * PAGE, KV_DIM), F32),
                        pltpu.VMEM((rows, _CH_W), BF16),
                        pltpu.VMEM((2, n_ch, 2 * CMP_HIDDEN), F32),
                        pltpu.SemaphoreType.DMA((2,))],
    )
    return pl.pallas_call(
        functools.partial(_compress_kernel, n_pages=n_pages),
        grid_spec=grid_spec,
        out_shape=[jax.ShapeDtypeStruct((nbatch, n_ch, KV_DIM), BF16),
                   jax.ShapeDtypeStruct((nbatch, KV_DIM, n_ch), BF16)],
        compiler_params=_cparams(("arbitrary",)),
        name="compress",
    )(page_table, pool, pe, w1, w2)


def _compress_weights(pe, w1, w2):
    c = CMP_BLOCK // CMP_STRIDE
    pe_r = pe.reshape(c, CMP_STRIDE, 1, HEAD_DIM)
    pe_x = jnp.broadcast_to(pe_r, (c, CMP_STRIDE, N_KV, HEAD_DIM)).reshape(c, _CH_W)
    pe_x = jnp.pad(pe_x, ((0, SUBLANES - c), (0, 0)))
    w1_r = w1.reshape(c, CMP_STRIDE, HEAD_DIM, CMP_HIDDEN)
    eye = jnp.eye(N_KV, dtype=w1.dtype)
    w1_x = jnp.einsum('jsde,kq->jskdqe', w1_r, eye).reshape(c, _CH_W, N_KV * CMP_HIDDEN)
    w2_x = jnp.einsum('ed,kq->keqd', w2, eye).reshape(N_KV * CMP_HIDDEN, KV_DIM)
    return pe_x, w1_x.astype(BF16), w2_x.astype(BF16)


def _hgrn_chunk(q, k, v, lf, st_ref, chunk, sub):
    if chunk > SUBLANES:
        r = lax.broadcasted_iota(jnp.int32, (chunk, chunk), 0)
        c = lax.broadcasted_iota(jnp.int32, (chunk, chunk), 1)
        tri = (r >= c).astype(F32)
        b = jnp.dot(tri, lf, preferred_element_type=F32, precision=HIGHEST)
    else:
        rows = [lf[0:1, :]]
        for t in range(1, chunk):
            rows.append(rows[-1] + lf[t:t + 1, :])
        b = jnp.concatenate(rows, axis=0)
    bl = b[chunk - 1:chunk, :]
    qe = q * jnp.exp(b)
    kd = k * jnp.exp(bl - b)
    ebl = jnp.exp(bl)
    n_sub = chunk // sub
    trow = lax.broadcasted_iota(jnp.int32, (sub, HG_W), 0)
    crow = lax.broadcasted_iota(jnp.int32, (chunk, HG_W), 0)

    diag = []
    for i in range(n_sub):
        qi = q[i * sub:(i + 1) * sub, :]
        bi = b[i * sub:(i + 1) * sub, :]
        acc = [jnp.zeros((sub, HG_DIM), F32) for _ in range(HG_HEADS)]
        for s in range(sub):
            row = i * sub + s
            dec = jnp.exp(jnp.where(trow >= s, bi - b[row:row + 1, :], -jnp.inf))
            prod = qi * k[row:row + 1, :] * dec
            for h in range(HG_HEADS):
                a = jnp.sum(prod[:, h * HG_DIM:(h + 1) * HG_DIM], axis=1, keepdims=True)
                acc[h] = acc[h] + a * v[row:row + 1, h * HG_DIM:(h + 1) * HG_DIM]
        diag.append(acc)

    off = []
    for i in range(n_sub):
        if i == 0:
            off.append(None)
            continue
        b0 = b[i * sub - 1:i * sub, :]
        qs = (q[i * sub:(i + 1) * sub, :] * jnp.exp(b[i * sub:(i + 1) * sub, :] - b0)).astype(BF16)
        ks = (k * jnp.exp(jnp.where(crow < i * sub, b0 - b, -jnp.inf))).astype(BF16)
        off.append((qs, ks))

    vb = v.astype(BF16)
    outs = []
    for h in range(HG_HEADS):
        sl = slice(h * HG_DIM, (h + 1) * HG_DIM)
        st = st_ref[h]
        o_h = lax.dot_general(qe[:, sl].astype(BF16), st.astype(BF16), (((1,), (1,)), ((), ())),
                              preferred_element_type=F32)
        parts = []
        for i in range(n_sub):
            d = diag[i][h]
            if off[i] is not None:
                qs, ks = off[i]
                a = lax.dot_general(qs[:, sl], ks[:, sl], (((1,), (1,)), ((), ())),
                                    preferred_element_type=F32)
                d = d + jnp.dot(a.astype(BF16), vb[:, sl], preferred_element_type=F32)
            parts.append(d)
        intra = parts[0] if n_sub == 1 else jnp.concatenate(parts, axis=0)
        outs.append(o_h + intra)
        st_ref[h] = st * ebl[:, sl] + lax.dot_general(
            vb[:, sl], kd[:, sl].astype(BF16), (((0,), (0,)), ((), ())), preferred_element_type=F32)
    return jnp.concatenate(outs, axis=1)


def _hgrn_finish(o, gs, gn):
    outs = []
    for h in range(HG_HEADS):
        oh = o[:, h * HG_DIM:(h + 1) * HG_DIM]
        y = oh * lax.rsqrt(jnp.mean(oh * oh, axis=-1, keepdims=True) + EPS) * gn
        outs.append(y)
    return (jnp.concatenate(outs, axis=1) * gs).astype(BF16)


def _hgrn_prompt_kernel(q_ref, k_ref, v_ref, lf_ref, gs_ref, gn_ref, o_ref, s_ref, st_ref, *, n_chunks):
    i = pl.program_id(0)

    @pl.when(i == 0)
    def _():
        st_ref[...] = jnp.zeros_like(st_ref)

    def body(c, carry):
        r = pl.ds(pl.multiple_of(c * HG_CHUNK, HG_CHUNK), HG_CHUNK)
        o = _hgrn_chunk(q_ref[r, :], k_ref[r, :], v_ref[r, :], lf_ref[r, :], st_ref, HG_CHUNK, HG_SUB)
        o_ref[r, :] = _hgrn_finish(o, gs_ref[r, :], gn_ref[...])
        return carry
    lax.fori_loop(0, n_chunks, body, 0)

    @pl.when(i == pl.num_programs(0) - 1)
    def _():
        for h in range(HG_HEADS):
            s_ref[h] = st_ref[h].T


def _hgrn_prompt(qh, kh, vh, lf, gs, gn, rows_per_step=512):
    t = qh.shape[0]
    spec = pl.BlockSpec((rows_per_step, HG_W), lambda i: (i, 0))
    return pl.pallas_call(
        functools.partial(_hgrn_prompt_kernel, n_chunks=rows_per_step // HG_CHUNK),
        grid=(t // rows_per_step,),
        in_specs=[spec] * 5 + [_const_spec((1, HG_DIM))],
        out_specs=[spec, pl.BlockSpec((HG_HEADS, HG_DIM, HG_DIM), lambda i: (0, 0, 0))],
        out_shape=[jax.ShapeDtypeStruct((t, HG_W), BF16),
                   jax.ShapeDtypeStruct((HG_HEADS, HG_DIM, HG_DIM), F32)],
        scratch_shapes=[pltpu.VMEM((HG_HEADS, HG_DIM, HG_DIM), F32)],
        compiler_params=_cparams(("arbitrary",)),
        name="hgrn_prompt",
    )(qh, kh, vh, lf, gs, gn)


def _hgrn_sample_kernel(q_ref, k_ref, v_ref, lf_ref, gs_ref, gn_ref, s0_ref, o_ref, s_ref, st_ref, *, t):
    for h in range(HG_HEADS):
        st_ref[h] = s0_ref[0, h].T
    o = _hgrn_chunk(q_ref[0], k_ref[0], v_ref[0], lf_ref[0], st_ref, t, t)
    o_ref[0] = _hgrn_finish(o, gs_ref[0], gn_ref[...])
    for h in range(HG_HEADS):
        s_ref[0, h] = st_ref[h].T


def _hgrn_sample(qh, kh, vh, lf, gs, gn, s0):
    nb, t, _ = qh.shape
    spec = pl.BlockSpec((1, t, HG_W), lambda b: (b, 0, 0))
    sspec = pl.BlockSpec((1, HG_HEADS, HG_DIM, HG_DIM), lambda b: (b, 0, 0, 0))
    return pl.pallas_call(
        functools.partial(_hgrn_sample_kernel, t=t),
        grid=(nb,),
        in_specs=[spec] * 5 + [_const_spec((1, HG_DIM)), sspec],
        out_specs=[spec, sspec],
        out_shape=[jax.ShapeDtypeStruct((nb, t, HG_W), BF16),
                   jax.ShapeDtypeStruct((nb, HG_HEADS, HG_DIM, HG_DIM), F32)],
        scratch_shapes=[pltpu.VMEM((HG_HEADS, HG_DIM, HG_DIM), F32)],
        compiler_params=_cparams(("arbitrary",)),
        name="hgrn_sample",
    )(qh, kh, vh, lf, gs, gn, s0)


def _select_topk(x, blk, n):
    nblk = x.shape[0]
    sel = jnp.zeros_like(x)
    for _ in range(n):
        m = jnp.max(x, axis=0, keepdims=True)
        idx = jnp.min(jnp.where(x == m, blk, float(nblk)), axis=0, keepdims=True)
        pick = blk == idx
        sel = jnp.where(pick, 1.0, sel)
        x = jnp.where(pick, -3e38, x)
    return sel


def _softmax_cols(s, valid):
    m = jnp.max(jnp.where(valid, s, NEG), axis=0, keepdims=True)
    p = jnp.where(valid, jnp.exp2(s - m), 0.0)
    l = jnp.sum(p, axis=0, keepdims=True)
    return p * jnp.where(l > 0.0, 1.0 / l, 0.0)


def _split_dot(a_bf16, x):
    hi = x.astype(BF16)
    lo = (x - hi.astype(F32)).astype(BF16)
    return (jnp.dot(a_bf16, hi, preferred_element_type=F32)
            + jnp.dot(a_bf16, lo, preferred_element_type=F32))


def _nt(a, b):
    return lax.dot_general(a, b, (((1,), (1,)), ((), ())), preferred_element_type=F32)


def _online_update(s, v, m_ref, l_ref, acc_ref, v_transposed=False):
    m_old = m_ref[...]
    m_new = jnp.maximum(m_old, jnp.max(s, axis=1, keepdims=True))
    p = jnp.exp2(s - m_new)
    alpha = jnp.exp2(m_old - m_new)
    l_ref[...] = alpha * l_ref[...] + jnp.sum(p, axis=1, keepdims=True)
    pv = _nt(p.astype(BF16), v) if v_transposed else jnp.dot(p.astype(BF16), v, preferred_element_type=F32)
    acc_ref[...] = alpha * acc_ref[...] + pv
    m_ref[...] = m_new


def _score_matrix(n_rows, row_offset, n_blocks):
    c = CMP_BLOCK // CMP_STRIDE
    ratio = SEL_BLOCK // CMP_STRIDE
    n_ov = ratio + c - 1
    m = np.zeros((n_blocks, n_rows), np.float32)
    for j in range(n_blocks):
        for u in range(n_ov):
            start = CMP_STRIDE * (u - (c - 1))
            w_u = (min(start + CMP_BLOCK, SEL_BLOCK) - max(start, 0)) / CMP_STRIDE
            n = ratio * j + u - (c - 1)
            if 0 <= n and n + row_offset < n_rows:
                m[j, n + row_offset] = w_u
    return m


def _nsa_prompt_kernel(qt_ref, gt_ref, kc_ref, vct_ref, ks_ref, vst_ref, kw_ref, vwt_ref,
                       mt_ref, cc_ref, cs_ref, bw_ref, eb_ref, o_ref,
                       sc_ref, sw_ref, sel_ref, m_ref, acc_ref, oc_ref,
                       sa_ref, sb_ref, pa_ref, pb_ref, ala_ref, alb_ref, *, n_blocks):
    qb = pl.program_id(0)
    nrow = kc_ref.shape[0]
    gq = GROUP * Q_BLOCK
    tiles_per_q = SEL_TILE // Q_BLOCK
    nband = WINDOW + Q_BLOCK
    max_tile = n_blocks // BLK_PER_TILE - 1
    frow = lax.broadcasted_iota(jnp.int32, (KV_DIM, Q_BLOCK), 0)

    def lanes4(x):
        return jnp.concatenate([x] * GROUP, axis=1)

    qx = []
    for kv in range(N_KV):
        keep = (frow >= HEAD_DIM) if kv else (frow < HEAD_DIM)
        qx.append(jnp.concatenate(
            [jnp.where(keep, qt_ref[j * KV_DIM:(j + 1) * KV_DIM, :], jnp.zeros((), BF16))
             for j in range(GROUP)], axis=1))

    def qk_stage(kt, s_ref, penalty=0.0):
        k = ks_ref[pl.ds(pl.multiple_of(kt * SEL_TILE, SEL_TILE), SEL_TILE), :]
        k_aug = jnp.concatenate([k, eb_ref[...]], axis=1)
        for kv in range(N_KV):
            srow = sel_ref[kv, pl.ds(pl.multiple_of(kt * BLK_PER_TILE, BLK_PER_TILE), BLK_PER_TILE), :]
            mrows = jnp.concatenate([lanes4(srow + penalty), jnp.zeros((KV_DIM - BLK_PER_TILE, gq), F32)], axis=0)
            q_aug = jnp.concatenate([qx[kv], mrows.astype(BF16)], axis=0)
            s_ref[kv] = jnp.dot(k_aug, q_aug, preferred_element_type=F32)

    def gate(kv, i):
        return jnp.concatenate(
            [gt_ref[3 * (kv * GROUP + j) + i:3 * (kv * GROUP + j) + i + 1, :] for j in range(GROUP)], axis=1)

    for kv in range(N_KV):
        sc_ref[kv] = jnp.dot(kc_ref[...], qx[kv], preferred_element_type=F32)

    r = lax.broadcasted_iota(jnp.int32, (nrow, Q_BLOCK), 0)
    qpos_c = qb * Q_BLOCK + lax.broadcasted_iota(jnp.int32, (nrow, Q_BLOCK), 1)
    end_pos = (r - CMP_PAD) * CMP_STRIDE + (CMP_BLOCK - 1)
    vis = (r >= CMP_PAD) & (r < CMP_PAD + n_blocks * (SEL_BLOCK // CMP_STRIDE) - 1) & (end_pos <= qpos_c)
    vis_add = lanes4(jnp.where(vis, 0.0, NEG))
    band = pl.ds(pl.multiple_of(qb * SUBLANES, SUBLANES), CMP_BAND)
    blk_i = lax.broadcasted_iota(jnp.int32, (n_blocks, Q_BLOCK), 0)
    cur = (qb * Q_BLOCK + lax.broadcasted_iota(jnp.int32, (n_blocks, Q_BLOCK), 1)) // SEL_BLOCK
    forced = (blk_i == 0) | (blk_i == cur) | (blk_i == cur - 1)
    scores = []
    for kv in range(N_KV):
        sc_ref[kv, band, :] = sc_ref[kv, band, :] + cc_ref[kv]
        s = sc_ref[kv] + vis_add
        m = jnp.max(s, axis=0, keepdims=True)
        p = jnp.exp2(s - m)
        l = jnp.sum(p, axis=0, keepdims=True)
        pn = p * jnp.where(m > 0.5 * NEG, 1.0 / l, 0.0)
        oc_ref[kv] = gate(kv, 0) * jnp.dot(vct_ref[...], pn.astype(BF16), preferred_element_type=F32)
        imp = pn[:, 0:Q_BLOCK]
        for j in range(1, GROUP):
            imp = imp + pn[:, j * Q_BLOCK:(j + 1) * Q_BLOCK]
        score = _split_dot(mt_ref[...], imp)
        scores.append(jnp.where(blk_i <= cur, score + jnp.where(forced, FORCE_BONUS, 0.0), -FORCE_BONUS))
    kband = kw_ref[pl.ds(pl.multiple_of(qb * Q_BLOCK, Q_BLOCK), nband), :]
    for kv in range(N_KV):
        sw_ref[kv] = jnp.dot(kband, qx[kv], preferred_element_type=F32)
    blk2 =jnp.concatenate([blk_i.astype(F32)] * N_KV, axis=1)
    sel = _select_topk(jnp.concatenate(scores, axis=1), blk2, N_SEL)
    for kv in range(N_KV):
        sel_ref[kv] = (sel[:, kv * Q_BLOCK:(kv + 1) * Q_BLOCK] - 1.0) * (-NEG)

    m_ref[...] = jnp.full_like(m_ref, NEG)
    acc_ref[...] = jnp.zeros_like(acc_ref)

    def pv_stage(kt, p_ref, al_ref):
        for kv in range(N_KV):
            acc_ref[kv] = al_ref[kv] * acc_ref[kv] + jnp.dot(vst_ref[kv, kt], p_ref[kv],
                                                             preferred_element_type=F32)

    def sm_stage(kt, s_ref, p_ref, al_ref, near):
        d0 = qb - kt * tiles_per_q
        for kv in range(N_KV):
            s = s_ref[kv]
            if near:
                s = s + jnp.concatenate(
                    [cs_ref[kv, jnp.clip(d0 - i, -1, 2) + 1] for i in range(tiles_per_q)], axis=0)
            m_old = m_ref[kv]
            m_new = jnp.maximum(m_old, jnp.max(s, axis=0, keepdims=True))
            p_ref[kv] = jnp.exp2(s - m_new).astype(BF16)
            al_ref[kv] = jnp.exp2(m_old - m_new)
            m_ref[kv] = m_new

    n_far = jnp.maximum(qb - 1, 0) // tiles_per_q
    pb_ref[...] = jnp.zeros_like(pb_ref)
    alb_ref[...] = jnp.ones_like(alb_ref)
    qk_stage(0, sa_ref)

    vband = jnp.concatenate([vwt_ref[qb + i] for i in range(nband // Q_BLOCK)], axis=1)
    krow = lax.broadcasted_iota(jnp.int32, (nband, Q_BLOCK), 0)
    pos_add = lanes4(jnp.where(krow >= WINDOW - qb * Q_BLOCK, 0.0, NEG))
    for kv in range(N_KV):
        s = sw_ref[kv] + bw_ref[kv] + pos_add
        m = jnp.max(s, axis=0, keepdims=True)
        p = jnp.exp2(s - m)
        o_w = (jnp.dot(vband, p.astype(BF16), preferred_element_type=F32)
               / jnp.sum(p, axis=0, keepdims=True))
        oc_ref[kv] = oc_ref[kv] + gate(kv, 2) * o_w

    def pair_body(u, c):
        t0 = 2 * u
        t1 = jnp.minimum(t0 + 1, max_tile)
        t2 = jnp.minimum(t0 + 2, max_tile)
        qk_stage(t1, sb_ref, jnp.where(t0 + 1 < n_far, 0.0, NEG))
        pv_stage(jnp.maximum(t0 - 1, 0), pb_ref, alb_ref)
        sm_stage(t0, sa_ref, pa_ref, ala_ref, False)
        qk_stage(t2, sa_ref)
        pv_stage(t0, pa_ref, ala_ref)
        sm_stage(t1, sb_ref, pb_ref, alb_ref, False)
        return c
    n_pairs = (n_far + 1) // 2
    lax.fori_loop(0, n_pairs, pair_body, 0)

    near_a = n_far
    near_b = jnp.minimum(n_far + 1, max_tile)
    qk_stage(near_a, sa_ref)
    pv_stage(jnp.clip(2 * n_pairs - 1, 0, max_tile), pb_ref, alb_ref)
    qk_stage(near_b, sb_ref, jnp.where(n_far + 1 <= qb // tiles_per_q, 0.0, NEG))
    sm_stage(near_a, sa_ref, pa_ref, ala_ref, True)
    pv_stage(near_a, pa_ref, ala_ref)
    sm_stage(near_b, sb_ref, pb_ref, alb_ref, True)
    pv_stage(near_b, pb_ref, alb_ref)

    o_kv = []
    for kv in range(N_KV):
        l_row = (1 - kv) * HEAD_DIM
        acc = acc_ref[kv]
        o_s = acc / acc[l_row:l_row + 1, :]
        o_kv.append(oc_ref[kv] + gate(kv, 1) * o_s)

    frow4 = lax.broadcasted_iota(jnp.int32, (KV_DIM, gq), 0)
    o_t = jnp.where(frow4 < HEAD_DIM, o_kv[0], o_kv[1])
    for j in range(GROUP):
        o_ref[:, j * LANES:(j + 1) * LANES] = o_t[:, j * Q_BLOCK:(j + 1) * Q_BLOCK].T.astype(BF16)


def _nsa_prompt(qat, gat, kc, vct, ksb, vst3, kwb, vwt3, mt, cc, cs, bw, eb):
    t = qat.shape[1]
    n_blocks = t // SEL_BLOCK
    nrow = kc.shape[0]
    gq = GROUP * Q_BLOCK
    return pl.pallas_call(
        functools.partial(_nsa_prompt_kernel, n_blocks=n_blocks),
        grid=(t // Q_BLOCK,),
        in_specs=[pl.BlockSpec((Q_DIM, Q_BLOCK), lambda i: (0, i)),
                  pl.BlockSpec((LANES, Q_BLOCK), lambda i: (0, i)),
                  _const_spec(kc.shape), _const_spec(vct.shape),
                  _const_spec(ksb.shape), _const_spec(vst3.shape),
                  _const_spec(kwb.shape), _const_spec(vwt3.shape),
                  _const_spec(mt.shape), _const_spec(cc.shape),
                  _const_spec(cs.shape), _const_spec(bw.shape), _const_spec(eb.shape)],
        out_specs=pl.BlockSpec((Q_BLOCK, Q_DIM), lambda i: (i, 0)),
        out_shape=jax.ShapeDtypeStruct((t, Q_DIM), BF16),
        scratch_shapes=[pltpu.VMEM((N_KV, nrow, gq), F32),
                        pltpu.VMEM((N_KV, WINDOW + Q_BLOCK, gq), F32),
                        pltpu.VMEM((N_KV, n_blocks, Q_BLOCK), F32),
                        pltpu.VMEM((N_KV, 1, gq), F32),
                        pltpu.VMEM((N_KV, KV_DIM, gq), F32),
                        pltpu.VMEM((N_KV, KV_DIM, gq), F32),
                        pltpu.VMEM((N_KV, SEL_TILE, gq), F32), pltpu.VMEM((N_KV, SEL_TILE, gq), F32),
                        pltpu.VMEM((N_KV, SEL_TILE, gq), BF16), pltpu.VMEM((N_KV, SEL_TILE, gq), BF16),
                        pltpu.VMEM((N_KV, 1, gq), F32), pltpu.VMEM((N_KV, 1, gq), F32)],
        compiler_params=_cparams(("arbitrary",)),
        name="nsa_prompt",
    )(qat, gat, kc, vct, ksb, vst3, kwb, vwt3, mt, cc, cs, bw, eb)


S_ROWS = 128
S_CHUNK_PAGES = 32
S_CHUNK = S_CHUNK_PAGES * PAGE


def _nsa_sample_kernel(pt_ref, qx_ref, gm_ref, kc_ref, vct_ref, kpool_ref, vpool_ref,
                       knew_ref, vnew_ref, kwin_ref, vwin_ref, kwnew_ref, vwnew_ref,
                       mt_ref, gsum_ref, ccs_ref, css_ref, cns_ref, cws_ref, o_ref,
                       kbuf, vbuf, sem, mask_ref, m_ref, l_ref, acc_ref, oc_ref, ow_ref,
                       *, n_chunks, n_blocks):
    b = pl.program_id(0)
    c = pl.program_id(1)
    step = b * n_chunks + c
    total = pl.num_programs(0) * n_chunks
    slot = step % 2

    def copies(bb, cc, s, p):
        pg = cc * S_CHUNK_PAGES + p
        dst = pl.ds(pl.multiple_of(p * PAGE, PAGE), PAGE)
        return (pltpu.make_async_copy(kpool_ref.at[pt_ref[bb, pg]], kbuf.at[s, :, dst], sem.at[0, s]),
                pltpu.make_async_copy(vpool_ref.at[pt_ref[bb, pg]], vbuf.at[s, :, dst], sem.at[1, s]))

    def start_all(st, s):
        bb = st // n_chunks
        cc = st % n_chunks

        def body(p, carry):
            ck, cv = copies(bb, cc, s, p)
            ck.start()
            cv.start()
            return carry
        lax.fori_loop(0, S_CHUNK_PAGES, body, 0)

    @pl.when(step == 0)
    def _():
        start_all(step, slot)

    @pl.when(step + 1 < total)
    def _():
        start_all(step + 1, 1 - slot)

    qx = qx_ref[0]

    @pl.when(c == 0)
    def _():
        nrow = kc_ref.shape[1]
        s = _nt(kc_ref[0], qx)
        r = lax.broadcasted_iota(jnp.int32, (nrow, S_ROWS), 0)
        band0 = nrow - CMP_BAND
        s = s + jnp.concatenate([jnp.zeros((band0, S_ROWS), F32), ccs_ref[...]], axis=0)
        pn = _softmax_cols(s, r < nrow - 1)
        oc_ref[...] = jnp.dot(vct_ref[0], pn.astype(BF16), preferred_element_type=F32).T
        imp = jnp.dot(pn, gsum_ref[...], preferred_element_type=F32, precision=HIGHEST)
        score = _split_dot(mt_ref[...], imp)
        blk_i = lax.broadcasted_iota(jnp.int32, (n_blocks, S_ROWS), 0)
        forced = (blk_i == 0) | (blk_i == n_blocks - 1)
        score = score + jnp.where(forced, FORCE_BONUS, 0.0)
        selt = _select_topk(score, blk_i.astype(F32), N_SEL - 1)
        selt = lax.dot_general(selt.astype(BF16), gsum_ref[...].astype(BF16), (((1,), (1,)), ((), ())),
                               preferred_element_type=F32)
        selm1 = selt.T - 1.0
        e_r = lax.broadcasted_iota(jnp.int32, (BLK_PER_TILE, SEL_TILE), 0)
        e_c = lax.broadcasted_iota(jnp.int32, (BLK_PER_TILE, SEL_TILE), 1)
        expand = jnp.where(e_c // SEL_BLOCK == e_r, -NEG, 0.0)
        for kt in range(n_blocks // BLK_PER_TILE):
            mask_ref[:, kt * SEL_TILE:(kt + 1) * SEL_TILE] = jnp.dot(
                selm1[:, kt * BLK_PER_TILE:(kt + 1) * BLK_PER_TILE], expand, preferred_element_type=F32)

        sw = _nt(qx, kwin_ref[0].astype(BF16)) + cws_ref[...]
        sn = _nt(qx, kwnew_ref[0].astype(BF16)) + cns_ref[...]
        m = jnp.maximum(jnp.max(sw, axis=1, keepdims=True), jnp.max(sn, axis=1, keepdims=True))
        pw = jnp.exp2(sw - m)
        pn2 = jnp.exp2(sn - m)
        l = jnp.sum(pw, axis=1, keepdims=True) + jnp.sum(pn2, axis=1, keepdims=True)
        ow = (jnp.dot(pw.astype(BF16), vwin_ref[0].astype(BF16), preferred_element_type=F32)
              + jnp.dot(pn2.astype(BF16), vwnew_ref[0].astype(BF16), preferred_element_type=F32))
        ow_ref[...] = ow / l

        m_ref[...] = jnp.full_like(m_ref, NEG)
        l_ref[...] = jnp.zeros_like(l_ref)
        acc_ref[...] = jnp.zeros_like(acc_ref)

    def wait_body(p, carry):
        ck, cv = copies(b, c, slot, p)
        ck.wait()
        cv.wait()
        return carry
    lax.fori_loop(0, S_CHUNK_PAGES, wait_body, 0)

    col0 = pl.multiple_of(c * S_CHUNK, S_CHUNK)
    s = (jnp.dot(qx, kbuf[slot].astype(BF16), preferred_element_type=F32)
         + mask_ref[:, pl.ds(col0, S_CHUNK)])

    @pl.when(c < n_chunks - 1)
    def _():
        _online_update(s, vbuf[slot].astype(BF16), m_ref, l_ref, acc_ref, v_transposed=True)

    @pl.when(c == n_chunks - 1)
    def _():
        near = jnp.concatenate([jnp.zeros((S_ROWS, S_CHUNK - LANES), F32), css_ref[...]], axis=1)
        _online_update(s + near, vbuf[slot].astype(BF16), m_ref, l_ref, acc_ref, v_transposed=True)
        sn = _nt(qx, knew_ref[0].astype(BF16)) + cns_ref[...]
        _online_update(sn, vnew_ref[0].astype(BF16), m_ref, l_ref, acc_ref)
        o_s = acc_ref[...] / l_ref[...]
        o_ref[0] = gm_ref[0, 0] * oc_ref[...] + gm_ref[0, 1] * o_s + gm_ref[0, 2] * ow_ref[...]


def _nsa_sample(page_table, qx, gm, kc, vct, kpool, vpool, knew, vnew, kwin, vwin, kwnew, vwnew,
                mt, gsum, ccs, css, cns, cws):
    nb = qx.shape[0]
    n_pages = page_table.shape[1]
    n_chunks = n_pages // S_CHUNK_PAGES
    past = n_pages * PAGE
    n_blocks = past // SEL_BLOCK
    nrow = kc.shape[1]

    def bspec(shape):
        nd = len(shape)
        return pl.BlockSpec((1,) + tuple(shape[1:]), lambda b, c, pt: (b,) + (0,) * (nd - 1))

    def cspec(shape):
        nd = len(shape)
        return pl.BlockSpec(tuple(shape), lambda b, c, pt: (0,) * nd, pipeline_mode=pl.Buffered(1))

    grid_spec = pltpu.PrefetchScalarGridSpec(
        num_scalar_prefetch=1,
        grid=(nb, n_chunks),
        in_specs=[bspec(qx.shape), bspec(gm.shape), bspec(kc.shape), bspec(vct.shape),
                  pl.BlockSpec(memory_space=pl.ANY), pl.BlockSpec(memory_space=pl.ANY),
                  bspec(knew.shape), bspec(vnew.shape), bspec(kwin.shape), bspec(vwin.shape),
                  bspec(kwnew.shape), bspec(vwnew.shape),
                  cspec(mt.shape), cspec(gsum.shape), cspec(ccs.shape), cspec(css.shape),
                  cspec(cns.shape), cspec(cws.shape)],
        out_specs=pl.BlockSpec((1, S_ROWS, KV_DIM), lambda b, c, pt: (b, 0, 0)),
        scratch_shapes=[pltpu.VMEM((2, KV_DIM, S_CHUNK), F32),
                        pltpu.VMEM((2, KV_DIM, S_CHUNK), F32),
                        pltpu.SemaphoreType.DMA((2, 2)),
                        pltpu.VMEM((S_ROWS, past), F32),
                        pltpu.VMEM((S_ROWS, 1), F32), pltpu.VMEM((S_ROWS, 1), F32),
                        pltpu.VMEM((S_ROWS, KV_DIM), F32),
                        pltpu.VMEM((S_ROWS, KV_DIM), F32), pltpu.VMEM((S_ROWS, KV_DIM), F32)],
    )
    return pl.pallas_call(
        functools.partial(_nsa_sample_kernel, n_chunks=n_chunks, n_blocks=n_blocks),
        grid_spec=grid_spec,
        out_shape=jax.ShapeDtypeStruct((nb, S_ROWS, KV_DIM), F32),
        compiler_params=_cparams(("arbitrary", "arbitrary")),
        name="nsa_sample",
    )(page_table, qx, gm, kc, vct, kpool, vpool, knew, vnew, kwin, vwin, kwnew, vwnew,
      mt, gsum, ccs, css, cns, cws)


def _ffn_kernel(x_ref, oa_ref, ob_ref, sa_ref, sb_ref, wpa_ref, wpb_ref, wo_ref, nf_ref,
                wg_ref, wu_ref, wd_ref, nl_ref, y_ref):
    pa = jnp.dot(oa_ref[...], wpa_ref[...], preferred_element_type=F32)
    pb = jnp.dot(ob_ref[...], wpb_ref[...], preferred_element_type=F32)
    merged = sa_ref[...] * pa + sb_ref[...] * pb
    x = x_ref[...] + jnp.dot(merged.astype(BF16), wo_ref[...], preferred_element_type=F32)
    hn = (x * lax.rsqrt(jnp.mean(x * x, axis=-1, keepdims=True) + EPS) * nf_ref[...]).astype(BF16)
    gate = jnp.dot(hn, wg_ref[...], preferred_element_type=F32)
    up = jnp.dot(hn, wu_ref[...], preferred_element_type=F32)
    ff = (jax.nn.silu(gate) * up).astype(BF16)
    x = x + jnp.dot(ff, wd_ref[...], preferred_element_type=F32)
    y_ref[...] = x * lax.rsqrt(jnp.mean(x * x, axis=-1, keepdims=True) + EPS) * nl_ref[...]


def _ffn(x2d, oa, ob, sa, sb, wpa, wpb, wo, nf, wg, wu, wd, nl, tm):
    rows = x2d.shape[0]

    def rspec(n):
        return pl.BlockSpec((tm, n), lambda i: (i, 0))

    return pl.pallas_call(
        _ffn_kernel,
        grid=(rows // tm,),
        in_specs=[rspec(D_MODEL), rspec(Q_DIM), rspec(HG_W), rspec(D_MODEL), rspec(D_MODEL),
                  _const_spec(wpa.shape), _const_spec(wpb.shape), _const_spec(wo.shape),
                  _const_spec(nf.shape), _const_spec(wg.shape), _const_spec(wu.shape),
                  _const_spec(wd.shape), _const_spec(nl.shape)],
        out_specs=rspec(D_MODEL),
        out_shape=jax.ShapeDtypeStruct((rows, D_MODEL), F32),
        compiler_params=_cparams(("arbitrary",)),
        name="ffn",
    )(x2d, oa, ob, sa, sb, wpa, wpb, wo, nf, wg, wu, wd, nl)


def _pack_w_in(w_in):
    sizes = (Q_DIM,) + (KV_DIM,) * 6 + (3 * N_HEADS,) + (HG_W,) * 4 + (D_MODEL,) * 2
    offs = np.concatenate([[0], np.cumsum(sizes)])
    q = w_in[:, offs[0]:offs[1]].reshape(D_MODEL, N_HEADS, HEAD_DIM)[:, _HEAD_PERM, :].reshape(D_MODEL, Q_DIM)
    g = jnp.pad(w_in[:, offs[7]:offs[8]], ((0, 0), (0, LANES - 3 * N_HEADS)))
    return jnp.concatenate([q, w_in[:, offs[1]:offs[7]], g, w_in[:, offs[8]:]], axis=1).astype(BF16)


def _strip(bvc, rel, lo=0, hi=None, masked=NEG):
    val = bvc[:, np.clip(rel, 0, 255)]
    ok = rel >= lo
    if hi is not None:
        ok = ok & (rel < hi)
    return jnp.where(jnp.asarray(ok)[None], val, masked)


def _toeplitz(bvc, a, n_rows, n_cols, lo=0, hi=None):
    n = n_rows + n_cols - 1
    u = _strip(bvc, a - (n_rows - 1) + np.arange(n), lo, hi)
    u = jnp.pad(u, ((0, 0), (0, 1)))
    circ = jnp.tile(u, (1, n_rows))[:, :n_rows * n].reshape(N_HEADS, n_rows, n)
    return circ[:, :, n_rows - 1:n_rows - 1 + n_cols]


def _bias_strips_prompt(bvc):
    gq = GROUP * Q_BLOCK

    def lanes(x):
        return x.reshape(N_KV, GROUP, x.shape[1], Q_BLOCK).transpose(0, 2, 1, 3).reshape(N_KV, x.shape[1], gq)

    cs = jnp.stack([lanes(_toeplitz(bvc, Q_BLOCK * d, Q_BLOCK, Q_BLOCK)) for d in (-1, 0, 1, 2)], axis=1)
    bw = lanes(_toeplitz(bvc, WINDOW, WINDOW + Q_BLOCK, Q_BLOCK, 0, WINDOW))
    rr = np.arange(CMP_BAND)[:, None]
    rel_c = np.arange(Q_BLOCK)[None, :] - CMP_STRIDE * (rr - CMP_PAD) - (CMP_BLOCK - 1)
    cc = _strip(bvc, rel_c, masked=0.0).reshape(N_KV, GROUP, CMP_BAND, Q_BLOCK)
    cc = cc.transpose(0, 2, 1, 3).reshape(N_KV, CMP_BAND, gq)
    return cc, cs, bw


def _bias_strips_sample(bvc, past, t):
    def rows(a):
        a = a.reshape(N_HEADS * t, a.shape[-1])
        return jnp.pad(a, ((0, S_ROWS - N_HEADS * t), (0, 0)))
    tt = np.arange(t)[:, None]
    nrow = past // CMP_STRIDE
    n = (nrow - CMP_BAND + np.arange(CMP_BAND))[None, :]
    ccs = rows(_strip(bvc, past + tt - CMP_STRIDE * n - (CMP_BLOCK - 1), masked=0.0)).T
    i = np.arange(LANES)[None, :]
    css = rows(_strip(bvc, LANES + tt - i))
    cns = rows(_strip(bvc, np.where(i < t, tt - i, -1)))
    iw = np.arange(WINDOW)[None, :]
    cws = rows(_strip(bvc, WINDOW + tt - iw, 0, WINDOW))
    return ccs, css, cns, cws


def kernel(x_prompt, x_sample, cache_k_cmp, cache_v_cmp, cache_k_slc, cache_v_slc, state_k_win, state_v_win,
           state_hgrn, page_table, norm_mix, w_in, cmp_pe_k, cmp_w1_k, cmp_w2_k, cmp_pe_v, cmp_w1_v, cmp_w2_v,
           rel_bias, hg_lb_logits, hg_norm, w_proj_a, w_proj_b, w_out, norm_ffn, w_gate, w_up, w_down, norm_final):
    nbp, t_p, _ = x_prompt.shape
    nbs, t_s, _ = x_sample.shape
    assert nbp == 1 and norm_mix.shape[0] == 1
    n_pages = page_table.shape[1]
    past = n_pages * PAGE
    assert state_k_win.shape[2] == WINDOW and past % S_CHUNK == 0 and t_s <= SUBLANES

    lb = jnp.cumsum(jax.nn.softmax(hg_lb_logits.astype(F32), axis=0), axis=0)[0]
    lb3 = jnp.pad(jnp.stack([jnp.log(lb), jnp.log1p(-lb), 1.0 - lb]), ((0, SUBLANES - 3), (0, 0)))
    w_pack = _pack_w_in(w_in[0])
    g_mix = norm_mix[0][None, :]
    wpa = w_proj_a[0].reshape(N_HEADS, HEAD_DIM, D_MODEL)[_HEAD_PERM].reshape(Q_DIM, D_MODEL).astype(BF16)
    wpb = w_proj_b[0].astype(BF16)
    wo = w_out[0].astype(BF16)
    wg, wu, wd = w_gate[0].astype(BF16), w_up[0].astype(BF16), w_down[0].astype(BF16)
    nf, nl = norm_ffn[0][None, :], norm_final[None, :]
    gn = hg_norm[0][None, :]
    pe_k, w1_k, w2_k = _compress_weights(cmp_pe_k[0], cmp_w1_k[0], cmp_w2_k[0])
    pe_v, w1_v, w2_v = _compress_weights(cmp_pe_v[0], cmp_w1_v[0], cmp_w2_v[0])
    bvc = (rel_bias[_BUCKET] - rel_bias[N_BUCKETS - 1][None, :]).T * LOG2E
    cc, cs, bw = _bias_strips_prompt(bvc)
    ccs, css, cns, cws = _bias_strips_sample(bvc, past, t_s)

    xp2 = x_prompt.reshape(t_p, D_MODEL)
    xs2 = x_sample.reshape(nbs * t_s, D_MODEL)
    seg = lambda off, n: w_pack[:, off:off + n]
    w_t = jnp.concatenate([seg(_OFF_Q, Q_DIM), seg(_OFF_G, LANES), seg(_OFF_KV + 3 * KV_DIM, KV_DIM),
                           seg(_OFF_KV + 5 * KV_DIM, KV_DIM)], axis=1).T
    pp = _proj(xp2, g_mix, w_pack, w_t, lb3, 256)
    ps = _proj(xs2, g_mix, w_pack, w_t, lb3, nbs * t_s)
    (_, kc_p, vc_p, ks_p, vs_p, kw_p, vw_p, ksb_p, kwb_p, _,
     qh_p, lf_p, kh_p, vh_p, gs_p, sa_p, sb_p, qat_p, gat_p, vst_p, vwt_p) = pp
    (qa_s, kc_s, vc_s, ks_s, vs_s, kw_s, vw_s, _, _, ga_s,
     qh_s, lf_s, kh_s, vh_s, gs_s, sa_s, sb_s, _, _, _, _) = ps

    ob_p, s_p = _hgrn_prompt(qh_p, kh_p, vh_p, lf_p, gs_p, gn)
    r3 = lambda a: a.reshape(nbs, t_s, a.shape[-1])
    ob_s, s_s = _hgrn_sample(r3(qh_s), r3(kh_s), r3(vh_s), r3(lf_s), r3(gs_s), gn, state_hgrn[0])

    ident = jnp.arange(t_p // PAGE, dtype=jnp.int32)[None, :]
    pool_rows = lambda a: a.reshape(-1, PAGE, KV_DIM)
    kcb_p, _ = _compress(pool_rows(kc_p), ident, pe_k, w1_k, w2_k)
    _, vct_p = _compress(pool_rows(vc_p), ident, pe_v, w1_v, w2_v)
    kcb_p = jnp.pad(kcb_p[0], ((CMP_PAD, 0), (0, 0)))
    vct_p = jnp.pad(vct_p[0], ((0, 0), (CMP_PAD, 0)))
    nrow_p = kcb_p.shape[0]
    mt_p = jnp.asarray(_score_matrix(nrow_p, CMP_PAD, t_p // SEL_BLOCK), BF16)
    kwb_pad = jnp.pad(kwb_p, ((WINDOW, 0), (0, 0)))
    vst3 = vst_p.reshape(KV_DIM, t_p // SEL_TILE, SEL_TILE).transpose(1, 0, 2)
    own = (np.arange(KV_DIM)[None, :] // HEAD_DIM == np.arange(N_KV)[:, None])[:, None, :, None]
    vst3 = jnp.where(own, vst3[None], jnp.ones((), BF16))
    vwt3 = jnp.pad(vwt_p, ((0, 0), (WINDOW, 0))).reshape(KV_DIM, (t_p + WINDOW) // Q_BLOCK, Q_BLOCK)
    vwt3 = vwt3.transpose(1, 0, 2)
    eb = (np.arange(LANES)[None, :] == np.arange(SEL_TILE)[:, None] // SEL_BLOCK).astype(np.float32)
    oa_p = _nsa_prompt(qat_p, gat_p, kcb_p, vct_p, ksb_p, vst3, kwb_pad, vwt3, mt_p, cc, cs, bw,
                       jnp.asarray(eb, BF16))

    kcb_s, _ = _compress(pool_rows(cache_k_cmp[0]), page_table, pe_k, w1_k, w2_k)
    _, vct_s = _compress(pool_rows(cache_v_cmp[0]), page_table, pe_v, w1_v, w2_v)
    nq = N_HEADS * t_s
    qs4 = qa_s.reshape(nbs, t_s, GROUP, N_KV, HEAD_DIM).astype(F32)
    qx = jnp.einsum('btjkd,kq->bkjtqd', qs4, jnp.eye(N_KV, dtype=F32)).reshape(nbs, nq, KV_DIM)
    qx = jnp.pad(qx, ((0, 0), (0, S_ROWS - nq), (0, 0))).astype(BF16)
    g4 = ga_s[:, :3 * N_HEADS].reshape(nbs, t_s, N_KV, GROUP, 3)
    gm = jnp.transpose(g4, (0, 4, 2, 3, 1)).reshape(nbs, 3, nq, 1)
    gm = jnp.broadcast_to(jnp.pad(gm, ((0, 0), (0, 0), (0, S_ROWS - nq), (0, 0))), (nbs, 3, S_ROWS, KV_DIM))
    new_tile = lambda a: jnp.pad(a.reshape(nbs, t_s, KV_DIM), ((0, 0), (0, LANES - t_s), (0, 0)))
    mt_s = jnp.asarray(_score_matrix(past // CMP_STRIDE, 0, past // SEL_BLOCK), BF16)
    pool_t = lambda a: a.transpose(0, 2, 3, 1).reshape(-1, KV_DIM, PAGE)
    gsum = np.zeros((S_ROWS, S_ROWS), np.float32)
    for kv in range(N_KV):
        for j in range(GROUP):
            for t in range(t_s):
                gsum[(kv * GROUP + j) * t_s + t, kv * t_s + t] = 1.0
    o_kv = _nsa_sample(page_table, qx, gm, kcb_s, vct_s,
                       pool_t(cache_k_slc[0]), pool_t(cache_v_slc[0]),
                       new_tile(ks_s), new_tile(vs_s), state_k_win[0].reshape(nbs, WINDOW, KV_DIM),
                       state_v_win[0].reshape(nbs, WINDOW, KV_DIM), new_tile(kw_s), new_tile(vw_s),
                       mt_s, jnp.asarray(gsum), ccs, css, cns, cws)
    o5 = o_kv[:, :nq].reshape(nbs, N_KV, GROUP, t_s, N_KV, HEAD_DIM)
    oa_s = jnp.einsum('bkjtqd,kq->btjkd', o5, jnp.eye(N_KV, dtype=F32)).reshape(nbs * t_s, Q_DIM).astype(BF16)

    y_p = _ffn(xp2, oa_p, ob_p, sa_p, sb_p, wpa, wpb, wo, nf, wg, wu, wd, nl, 256)
    y_s = _ffn(xs2, oa_s, ob_s.reshape(nbs * t_s, HG_W), sa_s, sb_s, wpa, wpb, wo, nf, wg, wu, wd, nl, nbs * t_s)

    kv5 = lambda a, nb_, tt: a.reshape(1, nb_, tt, N_KV, HEAD_DIM)
    wl = min(WINDOW, t_p)
    win = lambda st, new: jnp.concatenate(
        [st[0], new.reshape(nbs, t_s, N_KV, HEAD_DIM)], axis=1)[:, -WINDOW:][None]
    return (y_p.reshape(1, t_p, D_MODEL), y_s.reshape(nbs, t_s, D_MODEL),
            kv5(kc_p, 1, t_p), kv5(vc_p, 1, t_p), kv5(ks_p, 1, t_p), kv5(vs_p, 1, t_p),
            kv5(kw_p[-wl:], 1, wl), kv5(vw_p[-wl:], 1, wl), s_p[None, None],
            kv5(kc_s, nbs, t_s), kv5(vc_s, nbs, t_s), kv5(ks_s, nbs, t_s), kv5(vs_s, nbs, t_s),
            win(state_k_win, kw_s), win(state_v_win, vw_s), s_s[None])
```

```python
import functools
import math

import numpy as np
import jax
import jax.numpy as jnp
from jax import lax
from jax.experimental import pallas as pl
from jax.experimental.pallas import tpu as pltpu

F32 = jnp.float32
BF16 = jnp.bfloat16
HIGHEST = lax.Precision.HIGHEST

D_MODEL = 1024
N_HEADS = 8
N_KV = 2
GROUP = N_HEADS // N_KV
HEAD_DIM = 64
KV_DIM = N_KV * HEAD_DIM
Q_DIM = N_HEADS * HEAD_DIM
CMP_BLOCK = 32
CMP_STRIDE = 16
CMP_HIDDEN = 2 * HEAD_DIM
SEL_BLOCK = 64
N_SEL = 16
WINDOW = 512
Q_BLOCK = 128
FORCE_BONUS = 1e4
N_BUCKETS = 32
MAX_DISTANCE = 128
HG_HEADS = 4
HG_DIM = 128
HG_CHUNK = 64
HG_SUB = 16
HG_W = HG_HEADS * HG_DIM
D_FF = ((8 * D_MODEL // 3 + 255) // 256) * 256
EPS = 1e-6
PAGE = 128
NEG = -1e30
LOG2E = math.log2(math.e)
Q_SCALE = HEAD_DIM ** -0.5 * LOG2E

LANES = 128
SUBLANES = 8
VMEM_LIMIT = 56 * 1024 * 1024

_OFF_Q = 0
_OFF_KV = _OFF_Q + Q_DIM
_OFF_G = _OFF_KV + 6 * KV_DIM
_OFF_HG = _OFF_G + LANES
_OFF_GATE = _OFF_HG + 4 * HG_W
_PROJ_N = _OFF_GATE + 2 * D_MODEL

_HEAD_PERM = np.array([h for j in range(GROUP) for h in (j, GROUP + j)])

SEL_TILE = 512
BLK_PER_TILE = SEL_TILE // SEL_BLOCK
CMP_PAD = 16
CMP_BAND = 24


def _cparams(sem, vmem=VMEM_LIMIT):
    return pltpu.CompilerParams(dimension_semantics=sem, vmem_limit_bytes=vmem)


def _const_spec(shape):
    nd = len(shape)
    return pl.BlockSpec(shape, lambda *_: (0,) * nd, pipeline_mode=pl.Buffered(1))


def _bucket_table():
    n = np.arange(256)
    max_exact = N_BUCKETS // 2
    nf = np.maximum(n, 1).astype(np.float64)
    large = max_exact + (np.log(nf / max_exact) / math.log(MAX_DISTANCE / max_exact)
                         * (N_BUCKETS - max_exact)).astype(np.int64)
    large = np.minimum(large, N_BUCKETS - 1)
    return np.where(n < max_exact, n, large)


_BUCKET = _bucket_table()


def _proj_kernel(x_ref, g_ref, w_ref, wt_ref, lb_ref,
                 qa_ref, kc_ref, vc_ref, ks_ref, vs_ref, kw_ref, vw_ref,
                 ksb_ref, kwb_ref, ga_ref,
                 qh_ref, lf_ref, kh_ref, vh_ref, gs_ref, sa_ref, sb_ref,
                 qat_ref, gat_ref, vst_ref, vwt_ref):
    x = x_ref[...]
    xn = x * lax.rsqrt(jnp.mean(x * x, axis=-1, keepdims=True) + EPS) * g_ref[...]
    xb = xn.astype(BF16)

    def seg(a, n):
        return jnp.dot(xb, w_ref[:, a:a + n], preferred_element_type=F32)

    qa_ref[...] = (seg(_OFF_Q, Q_DIM) * Q_SCALE).astype(BF16)
    f32_refs = (kc_ref, vc_ref, ks_ref, vs_ref, kw_ref, vw_ref)
    b16_refs = (None, None, ksb_ref, None, kwb_ref, None)
    for i in range(6):
        u = seg(_OFF_KV + i * KV_DIM, KV_DIM)
        f32_refs[i][...] = u
        if b16_refs[i] is not None:
            b16_refs[i][...] = u.astype(BF16)
    ga_ref[...] = jax.nn.sigmoid(seg(_OFF_G, LANES))

    def seg_t(a, n):
        return lax.dot_general(wt_ref[a:a + n, :], xb, (((1,), (1,)), ((), ())), preferred_element_type=F32)

    qat_ref[...] = (seg_t(0, Q_DIM) * Q_SCALE).astype(BF16)
    gat_ref[...] = jax.nn.sigmoid(seg_t(Q_DIM, LANES))
    vst_ref[...] = seg_t(Q_DIM + LANES, KV_DIM).astype(BF16)
    vwt_ref[...] = seg_t(Q_DIM + LANES + KV_DIM, KV_DIM).astype(BF16)

    log_lb = lb_ref[0:1, :]
    log_1m = lb_ref[1:2, :]
    one_m = lb_ref[2:3, :]
    qh_ref[...] = jax.nn.silu(seg(_OFF_HG, HG_W))
    z = seg(_OFF_HG + HG_W, HG_W)
    b = log_1m + (jnp.minimum(z, 0.0) - jnp.log1p(jnp.exp(-jnp.abs(z))))
    hi = jnp.maximum(log_lb, b)
    lf_ref[...] = hi + jnp.log1p(jnp.exp(-jnp.abs(log_lb - b)))
    kh_ref[...] = one_m * jax.nn.sigmoid(-z)
    vh_ref[...] = seg(_OFF_HG + 2 * HG_W, HG_W)
    gs_ref[...] = jax.nn.silu(seg(_OFF_HG + 3 * HG_W, HG_W))
    sa_ref[...] = jax.nn.sigmoid(seg(_OFF_GATE, D_MODEL))
    sb_ref[...] = jax.nn.sigmoid(seg(_OFF_GATE + D_MODEL, D_MODEL))


_PROJ_T = Q_DIM + LANES + 2 * KV_DIM


def _proj(x2d, g, w, wt, lb3, tm):
    rows = x2d.shape[0]
    widths = ([(Q_DIM, BF16)] + [(KV_DIM, F32)] * 6 + [(KV_DIM, BF16)] * 2 + [(LANES, F32)]
              + [(HG_W, F32)] * 5 + [(D_MODEL, F32)] * 2)
    heights = [(Q_DIM, BF16), (LANES, F32), (KV_DIM, BF16), (KV_DIM, BF16)]
    return pl.pallas_call(
        _proj_kernel,
        grid=(rows // tm,),
        in_specs=[pl.BlockSpec((tm, D_MODEL), lambda i: (i, 0)),
                  _const_spec((1, D_MODEL)),
                  _const_spec((D_MODEL, _PROJ_N)),
                  _const_spec((_PROJ_T, D_MODEL)),
                  _const_spec((SUBLANES, HG_W))],
        out_specs=([pl.BlockSpec((tm, n), lambda i: (i, 0)) for n, _ in widths]
                   + [pl.BlockSpec((n, tm), lambda i: (0, i)) for n, _ in heights]),
        out_shape=([jax.ShapeDtypeStruct((rows, n), dt) for n, dt in widths]
                   + [jax.ShapeDtypeStruct((n, rows), dt) for n, dt in heights]),
        compiler_params=_cparams(("arbitrary",)),
        name="proj",
    )(x2d, g, w, wt, lb3)


_CH_W = CMP_STRIDE * KV_DIM
_CH_PER_PAGE = PAGE // CMP_STRIDE


def _compress_kernel(pt_ref, pool_ref, pe_ref, w1_ref, w2_ref, out_ref, outt_ref, buf, rbuf, xa, hbuf, sem,
                     *, n_pages, pages_transposed):
    b = pl.program_id(0)
    nb = pl.num_programs(0)
    slot = b % 2
    n_ch = n_pages * _CH_PER_PAGE

    def page_copy(bb, p, s):
        return pltpu.make_async_copy(pool_ref.at[pt_ref[bb, p]], buf.at[s, p], sem.at[s])

    def start_all(bb, s):
        def body(p, c):
            page_copy(bb, p, s).start()
            return c
        lax.fori_loop(0, n_pages, body, 0)

    @pl.when(b == 0)
    def _():
        start_all(b, slot)

    @pl.when(b + 1 < nb)
    def _():
        start_all(b + 1, 1 - slot)

    def wait_body(p, c):
        page_copy(b, p, slot).wait()
        return c
    lax.fori_loop(0, n_pages, wait_body, 0)

    def to_rows(p, c):
        page = buf[slot, p]
        rbuf[pl.ds(pl.multiple_of(p * PAGE, PAGE), PAGE), :] = page.T if pages_transposed else page
        return c
    lax.fori_loop(0, n_pages, to_rows, 0, unroll=8)

    rows = math.gcd(n_ch, 256)
    for j in range(2):
        for r in range(n_ch // rows):
            for s in range(CMP_STRIDE):
                x = rbuf[pl.ds(r * rows * CMP_STRIDE + s, rows, stride=CMP_STRIDE), :]
                xa[:, s * KV_DIM:(s + 1) * KV_DIM] = (x + pe_ref[j:j + 1, s * KV_DIM:(s + 1) * KV_DIM]).astype(BF16)
            hbuf[j, r * rows:(r + 1) * rows, :] = jnp.dot(xa[...], w1_ref[j], preferred_element_type=F32)
    h = hbuf[0] + pltpu.roll(hbuf[1], n_ch - 1, 0)
    blocks = jnp.dot(jax.nn.gelu(h).astype(BF16), w2_ref[...], preferred_element_type=F32)
    row = lax.broadcasted_iota(jnp.int32, blocks.shape, 0)
    blocks = jnp.where(row < n_ch - 1, blocks, 0.0)
    out_ref[0] = blocks.astype(BF16)
    outt_ref[0] = blocks.T.astype(BF16)


def _compress(pool, page_table, pe, w1, w2, pages_transposed):
    nbatch, n_pages = page_table.shape
    n_ch = n_pages * _CH_PER_PAGE
    rows = math.gcd(n_ch, 256)
    grid_spec = pltpu.PrefetchScalarGridSpec(
        num_scalar_prefetch=1,
        grid=(nbatch,),
        in_specs=[pl.BlockSpec(memory_space=pl.ANY),
                  _const_spec((SUBLANES, _CH_W)),
                  _const_spec((2, _CH_W, 2 * CMP_HIDDEN)),
                  _const_spec((2 * CMP_HIDDEN, KV_DIM))],
        out_specs=[pl.BlockSpec((1, n_ch, KV_DIM), lambda b, pt: (b, 0, 0)),
                   pl.BlockSpec((1, KV_DIM, n_ch), lambda b, pt: (b, 0, 0))],
        scratch_shapes=[pltpu.VMEM((2, n_pages, PAGE, KV_DIM), F32),
                        pltpu.VMEM((n_pages * PAGE, KV_DIM), F32),
                        pltpu.VMEM((rows, _CH_W), BF16),
                        pltpu.VMEM((2, n_ch, 2 * CMP_HIDDEN), F32),
                        pltpu.SemaphoreType.DMA((2,))],
    )
    return pl.pallas_call(
        functools.partial(_compress_kernel, n_pages=n_pages, pages_transposed=pages_transposed),
        grid_spec=grid_spec,
        out_shape=[jax.ShapeDtypeStruct((nbatch, n_ch, KV_DIM), BF16),
                   jax.ShapeDtypeStruct((nbatch, KV_DIM, n_ch), BF16)],
        compiler_params=_cparams(("arbitrary",)),
        name="compress",
    )(page_table, pool, pe, w1, w2)


def _compress_weights(pe, w1, w2):
    c = CMP_BLOCK // CMP_STRIDE
    pe_r = pe.reshape(c, CMP_STRIDE, 1, HEAD_DIM)
    pe_x = jnp.broadcast_to(pe_r, (c, CMP_STRIDE, N_KV, HEAD_DIM)).reshape(c, _CH_W)
    pe_x = jnp.pad(pe_x, ((0, SUBLANES - c), (0, 0)))
    w1_r = w1.reshape(c, CMP_STRIDE, HEAD_DIM, CMP_HIDDEN)
    eye = jnp.eye(N_KV, dtype=w1.dtype)
    w1_x = jnp.einsum('jsde,kq->jskdqe', w1_r, eye).reshape(c, _CH_W, N_KV * CMP_HIDDEN)
    w2_x = jnp.einsum('ed,kq->keqd', w2, eye).reshape(N_KV * CMP_HIDDEN, KV_DIM)
    return pe_x, w1_x.astype(BF16), w2_x.astype(BF16)


def _hgrn_chunk(q, k, v, lf, st_ref, chunk, sub):
    if chunk > SUBLANES:
        r = lax.broadcasted_iota(jnp.int32, (chunk, chunk), 0)
        c = lax.broadcasted_iota(jnp.int32, (chunk, chunk), 1)
        tri = (r >= c).astype(F32)
        b = jnp.dot(tri, lf, preferred_element_type=F32, precision=HIGHEST)
    else:
        rows = [lf[0:1, :]]
        for t in range(1, chunk):
            rows.append(rows[-1] + lf[t:t + 1, :])
        b = jnp.concatenate(rows, axis=0)
    bl = b[chunk - 1:chunk, :]
    qe = q * jnp.exp(b)
    kd = k * jnp.exp(bl - b)
    ebl = jnp.exp(bl)
    n_sub = chunk // sub
    trow = lax.broadcasted_iota(jnp.int32, (sub, HG_W), 0)
    crow = lax.broadcasted_iota(jnp.int32, (chunk, HG_W), 0)

    diag = []
    for i in range(n_sub):
        qi = q[i * sub:(i + 1) * sub, :]
        bi = b[i * sub:(i + 1) * sub, :]
        acc = [jnp.zeros((sub, HG_DIM), F32) for _ in range(HG_HEADS)]
        for s in range(sub):
            row = i * sub + s
            dec = jnp.exp(jnp.where(trow >= s, bi - b[row:row + 1, :], -jnp.inf))
            prod = qi * k[row:row + 1, :] * dec
            for h in range(HG_HEADS):
                a = jnp.sum(prod[:, h * HG_DIM:(h + 1) * HG_DIM], axis=1, keepdims=True)
                acc[h] = acc[h] + a * v[row:row + 1, h * HG_DIM:(h + 1) * HG_DIM]
        diag.append(acc)

    off = []
    for i in range(n_sub):
        if i == 0:
            off.append(None)
            continue
        b0 = b[i * sub - 1:i * sub, :]
        qs = (q[i * sub:(i + 1) * sub, :] * jnp.exp(b[i * sub:(i + 1) * sub, :] - b0)).astype(BF16)
        ks = (k * jnp.exp(jnp.where(crow < i * sub, b0 - b, -jnp.inf))).astype(BF16)
        off.append((qs, ks))

    vb = v.astype(BF16)
    outs = []
    for h in range(HG_HEADS):
        sl = slice(h * HG_DIM, (h + 1) * HG_DIM)
        st = st_ref[h]
        o_h = lax.dot_general(qe[:, sl].astype(BF16), st.astype(BF16), (((1,), (1,)), ((), ())),
                              preferred_element_type=F32)
        parts = []
        for i in range(n_sub):
            d = diag[i][h]
            if off[i] is not None:
                qs, ks = off[i]
                a = lax.dot_general(qs[:, sl], ks[:, sl], (((1,), (1,)), ((), ())),
                                    preferred_element_type=F32)
                d = d + jnp.dot(a.astype(BF16), vb[:, sl], preferred_element_type=F32)
            parts.append(d)
        intra = parts[0] if n_sub == 1 else jnp.concatenate(parts, axis=0)
        outs.append(o_h + intra)
        st_ref[h] = st * ebl[:, sl] + lax.dot_general(
            vb[:, sl], kd[:, sl].astype(BF16), (((0,), (0,)), ((), ())), preferred_element_type=F32)
    return jnp.concatenate(outs, axis=1)


def _hgrn_finish(o, gs, gn):
    outs = []
    for h in range(HG_HEADS):
        oh = o[:, h * HG_DIM:(h + 1) * HG_DIM]
        y = oh * lax.rsqrt(jnp.mean(oh * oh, axis=-1, keepdims=True) + EPS) * gn
        outs.append(y)
    return (jnp.concatenate(outs, axis=1) * gs).astype(BF16)


def _hgrn_prompt_kernel(q_ref, k_ref, v_ref, lf_ref, gs_ref, gn_ref, o_ref, s_ref, st_ref, *, n_chunks):
    i = pl.program_id(0)

    @pl.when(i == 0)
    def _():
        st_ref[...] = jnp.zeros_like(st_ref)

    def body(c, carry):
        r = pl.ds(pl.multiple_of(c * HG_CHUNK, HG_CHUNK), HG_CHUNK)
        o = _hgrn_chunk(q_ref[r, :], k_ref[r, :], v_ref[r, :], lf_ref[r, :], st_ref, HG_CHUNK, HG_SUB)
        o_ref[r, :] = _hgrn_finish(o, gs_ref[r, :], gn_ref[...])
        return carry
    lax.fori_loop(0, n_chunks, body, 0, unroll=4)

    @pl.when(i == pl.num_programs(0) - 1)
    def _():
        for h in range(HG_HEADS):
            s_ref[h] = st_ref[h].T


def _hgrn_prompt(qh, kh, vh, lf, gs, gn, rows_per_step=512):
    t = qh.shape[0]
    spec = pl.BlockSpec((rows_per_step, HG_W), lambda i: (i, 0))
    return pl.pallas_call(
        functools.partial(_hgrn_prompt_kernel, n_chunks=rows_per_step // HG_CHUNK),
        grid=(t // rows_per_step,),
        in_specs=[spec] * 5 + [_const_spec((1, HG_DIM))],
        out_specs=[spec, pl.BlockSpec((HG_HEADS, HG_DIM, HG_DIM), lambda i: (0, 0, 0))],
        out_shape=[jax.ShapeDtypeStruct((t, HG_W), BF16),
                   jax.ShapeDtypeStruct((HG_HEADS, HG_DIM, HG_DIM), F32)],
        scratch_shapes=[pltpu.VMEM((HG_HEADS, HG_DIM, HG_DIM), F32)],
        compiler_params=_cparams(("arbitrary",)),
        name="hgrn_prompt",
    )(qh, kh, vh, lf, gs, gn)


def _hgrn_sample_kernel(q_ref, k_ref, v_ref, lf_ref, gs_ref, gn_ref, s0_ref, o_ref, s_ref, st_ref, *, t):
    for h in range(HG_HEADS):
        st_ref[h] = s0_ref[0, h].T
    o = _hgrn_chunk(q_ref[0], k_ref[0], v_ref[0], lf_ref[0], st_ref, t, t)
    o_ref[0] = _hgrn_finish(o, gs_ref[0], gn_ref[...])
    for h in range(HG_HEADS):
        s_ref[0, h] = st_ref[h].T


def _hgrn_sample(qh, kh, vh, lf, gs, gn, s0):
    nb, t, _ = qh.shape
    spec = pl.BlockSpec((1, t, HG_W), lambda b: (b, 0, 0))
    sspec = pl.BlockSpec((1, HG_HEADS, HG_DIM, HG_DIM), lambda b: (b, 0, 0, 0))
    return pl.pallas_call(
        functools.partial(_hgrn_sample_kernel, t=t),
        grid=(nb,),
        in_specs=[spec] * 5 + [_const_spec((1, HG_DIM)), sspec],
        out_specs=[spec, sspec],
        out_shape=[jax.ShapeDtypeStruct((nb, t, HG_W), BF16),
                   jax.ShapeDtypeStruct((nb, HG_HEADS, HG_DIM, HG_DIM), F32)],
        scratch_shapes=[pltpu.VMEM((HG_HEADS, HG_DIM, HG_DIM), F32)],
        compiler_params=_cparams(("arbitrary",)),
        name="hgrn_sample",
    )(qh, kh, vh, lf, gs, gn, s0)


def _select_topk(x, blk, n):
    nblk = x.shape[0]
    sel = jnp.zeros_like(x)
    for _ in range(n):
        m = jnp.max(x, axis=0, keepdims=True)
        idx = jnp.min(jnp.where(x == m, blk, float(nblk)), axis=0, keepdims=True)
        pick = blk == idx
        sel = jnp.where(pick, 1.0, sel)
        x = jnp.where(pick, -3e38, x)
    return sel


def _softmax_cols(s, valid):
    m = jnp.max(jnp.where(valid, s, NEG), axis=0, keepdims=True)
    p = jnp.where(valid, jnp.exp2(s - m), 0.0)
    l = jnp.sum(p, axis=0, keepdims=True)
    return p * jnp.where(l > 0.0, 1.0 / l, 0.0)


def _split_dot(a_bf16, x):
    hi = x.astype(BF16)
    lo = (x - hi.astype(F32)).astype(BF16)
    return (jnp.dot(a_bf16, hi, preferred_element_type=F32)
            + jnp.dot(a_bf16, lo, preferred_element_type=F32))


def _nt(a, b):
    return lax.dot_general(a, b, (((1,), (1,)), ((), ())), preferred_element_type=F32)


def _online_update(s, v, m_ref, l_ref, acc_ref, v_transposed=False):
    m_old = m_ref[...]
    m_new = jnp.maximum(m_old, jnp.max(s, axis=1, keepdims=True))
    p = jnp.exp2(s - m_new)
    alpha = jnp.exp2(m_old - m_new)
    l_ref[...] = alpha * l_ref[...] + jnp.sum(p, axis=1, keepdims=True)
    pv = _nt(p.astype(BF16), v) if v_transposed else jnp.dot(p.astype(BF16), v, preferred_element_type=F32)
    acc_ref[...] = alpha * acc_ref[...] + pv
    m_ref[...] = m_new


def _score_matrix(n_rows, row_offset, n_blocks):
    c = CMP_BLOCK // CMP_STRIDE
    ratio = SEL_BLOCK // CMP_STRIDE
    n_ov = ratio + c - 1
    m = np.zeros((n_blocks, n_rows), np.float32)
    for j in range(n_blocks):
        for u in range(n_ov):
            start = CMP_STRIDE * (u - (c - 1))
            w_u = (min(start + CMP_BLOCK, SEL_BLOCK) - max(start, 0)) / CMP_STRIDE
            n = ratio * j + u - (c - 1)
            if 0 <= n and n + row_offset < n_rows:
                m[j, n + row_offset] = w_u
    return m


def _nsa_prompt_kernel(qt_ref, gt_ref, kc_ref, vct_ref, ks_ref, vst_ref, kw_ref, vwt_ref,
                       mt_ref, cc_ref, cs_ref, bw_ref, eb_ref, o_ref,
                       sc_ref, sw_ref, sel_ref, m_ref, acc_ref, oc_ref,
                       sa_ref, sb_ref, pa_ref, pb_ref, ala_ref, alb_ref, *, n_blocks):
    qb = pl.program_id(0)
    nrow = kc_ref.shape[0]
    gq = GROUP * Q_BLOCK
    tiles_per_q = SEL_TILE // Q_BLOCK
    nband = WINDOW + Q_BLOCK
    max_tile = n_blocks // BLK_PER_TILE - 1
    frow = lax.broadcasted_iota(jnp.int32, (KV_DIM, Q_BLOCK), 0)

    def lanes4(x):
        return jnp.concatenate([x] * GROUP, axis=1)

    qx = []
    for kv in range(N_KV):
        keep = (frow >= HEAD_DIM) if kv else (frow < HEAD_DIM)
        qx.append(jnp.concatenate(
            [jnp.where(keep, qt_ref[j * KV_DIM:(j + 1) * KV_DIM, :], jnp.zeros((), BF16))
             for j in range(GROUP)], axis=1))

    def qk_stage(kt, s_ref, penalty=0.0):
        k = ks_ref[pl.ds(pl.multiple_of(kt * SEL_TILE, SEL_TILE), SEL_TILE), :]
        k_aug = jnp.concatenate([k, eb_ref[...]], axis=1)
        for kv in range(N_KV):
            srow = sel_ref[kv, pl.ds(pl.multiple_of(kt * BLK_PER_TILE, BLK_PER_TILE), BLK_PER_TILE), :]
            mrows = jnp.concatenate([lanes4(srow + penalty), jnp.zeros((KV_DIM - BLK_PER_TILE, gq), F32)], axis=0)
            q_aug = jnp.concatenate([qx[kv], mrows.astype(BF16)], axis=0)
            s_ref[kv] = jnp.dot(k_aug, q_aug, preferred_element_type=F32)

    def gate(kv, i):
        return jnp.concatenate(
            [gt_ref[3 * (kv * GROUP + j) + i:3 * (kv * GROUP + j) + i + 1, :] for j in range(GROUP)], axis=1)

    for kv in range(N_KV):
        sc_ref[kv] = jnp.dot(kc_ref[...], qx[kv], preferred_element_type=F32)

    r = lax.broadcasted_iota(jnp.int32, (nrow, Q_BLOCK), 0)
    qpos_c = qb * Q_BLOCK + lax.broadcasted_iota(jnp.int32, (nrow, Q_BLOCK), 1)
    end_pos = (r - CMP_PAD) * CMP_STRIDE + (CMP_BLOCK - 1)
    vis = (r >= CMP_PAD) & (r < CMP_PAD + n_blocks * (SEL_BLOCK // CMP_STRIDE) - 1) & (end_pos <= qpos_c)
    vis_add = lanes4(jnp.where(vis, 0.0, NEG))
    band = pl.ds(pl.multiple_of(qb * SUBLANES, SUBLANES), CMP_BAND)
    blk_i = lax.broadcasted_iota(jnp.int32, (n_blocks, Q_BLOCK), 0)
    cur = (qb * Q_BLOCK + lax.broadcasted_iota(jnp.int32, (n_blocks, Q_BLOCK), 1)) // SEL_BLOCK
    forced = (blk_i == 0) | (blk_i == cur) | (blk_i == cur - 1)
    scores = []
    for kv in range(N_KV):
        sc_ref[kv, band, :] = sc_ref[kv, band, :] + cc_ref[kv]
        s = sc_ref[kv] + vis_add
        m = jnp.max(s, axis=0, keepdims=True)
        p = jnp.exp2(s - m)
        l = jnp.sum(p, axis=0, keepdims=True)
        pn = p * jnp.where(m > 0.5 * NEG, 1.0 / l, 0.0)
        oc_ref[kv] = gate(kv, 0) * jnp.dot(vct_ref[...], pn.astype(BF16), preferred_element_type=F32)
        imp = pn[:, 0:Q_BLOCK]
        for j in range(1, GROUP):
            imp = imp + pn[:, j * Q_BLOCK:(j + 1) * Q_BLOCK]
        score = _split_dot(mt_ref[...], imp)
        scores.append(jnp.where(blk_i <= cur, score + jnp.where(forced, FORCE_BONUS, 0.0), -FORCE_BONUS))
    kband = kw_ref[pl.ds(pl.multiple_of(qb * Q_BLOCK, Q_BLOCK), nband), :]
    for kv in range(N_KV):
        sw_ref[kv] = jnp.dot(kband, qx[kv], preferred_element_type=F32)
    blk2 =jnp.concatenate([blk_i.astype(F32)] * N_KV, axis=1)
    sel = _select_topk(jnp.concatenate(scores, axis=1), blk2, N_SEL)
    for kv in range(N_KV):
        sel_ref[kv] = (sel[:, kv * Q_BLOCK:(kv + 1) * Q_BLOCK] - 1.0) * (-NEG)

    m_ref[...] = jnp.full_like(m_ref, NEG)
    acc_ref[...] = jnp.zeros_like(acc_ref)

    def pv_stage(kt, p_ref, al_ref):
        for kv in range(N_KV):
            acc_ref[kv] = al_ref[kv] * acc_ref[kv] + jnp.dot(vst_ref[kv, kt], p_ref[kv],
                                                             preferred_element_type=F32)

    def sm_stage(kt, s_ref, p_ref, al_ref, near):
        d0 = qb - kt * tiles_per_q
        for kv in range(N_KV):
            s = s_ref[kv]
            if near:
                s = s + jnp.concatenate(
                    [cs_ref[kv, jnp.clip(d0 - i, -1, 2) + 1] for i in range(tiles_per_q)], axis=0)
            m_old = m_ref[kv]
            m_new = jnp.maximum(m_old, jnp.max(s, axis=0, keepdims=True))
            p_ref[kv] = jnp.exp2(s - m_new).astype(BF16)
            al_ref[kv] = jnp.exp2(m_old - m_new)
            m_ref[kv] = m_new

    n_far = jnp.maximum(qb - 1, 0) // tiles_per_q
    pb_ref[...] = jnp.zeros_like(pb_ref)
    alb_ref[...] = jnp.ones_like(alb_ref)
    qk_stage(0, sa_ref)

    vband = jnp.concatenate([vwt_ref[qb + i] for i in range(nband // Q_BLOCK)], axis=1)
    krow = lax.broadcasted_iota(jnp.int32, (nband, Q_BLOCK), 0)
    pos_add = lanes4(jnp.where(krow >= WINDOW - qb * Q_BLOCK, 0.0, NEG))
    for kv in range(N_KV):
        s = sw_ref[kv] + bw_ref[kv] + pos_add
        m = jnp.max(s, axis=0, keepdims=True)
        p = jnp.exp2(s - m)
        o_w = (jnp.dot(vband, p.astype(BF16), preferred_element_type=F32)
               / jnp.sum(p, axis=0, keepdims=True))
        oc_ref[kv] = oc_ref[kv] + gate(kv, 2) * o_w

    def pair_body(u, c):
        t0 = 2 * u
        t1 = jnp.minimum(t0 + 1, max_tile)
        t2 = jnp.minimum(t0 + 2, max_tile)
        qk_stage(t1, sb_ref, jnp.where(t0 + 1 < n_far, 0.0, NEG))
        pv_stage(jnp.maximum(t0 - 1, 0), pb_ref, alb_ref)
        sm_stage(t0, sa_ref, pa_ref, ala_ref, False)
        qk_stage(t2, sa_ref)
        pv_stage(t0, pa_ref, ala_ref)
        sm_stage(t1, sb_ref, pb_ref, alb_ref, False)
        return c
    n_pairs = (n_far + 1) // 2
    lax.fori_loop(0, n_pairs, pair_body, 0)

    near_a = n_far
    near_b = jnp.minimum(n_far + 1, max_tile)
    qk_stage(near_a, sa_ref)
    pv_stage(jnp.clip(2 * n_pairs - 1, 0, max_tile), pb_ref, alb_ref)
    qk_stage(near_b, sb_ref, jnp.where(n_far + 1 <= qb // tiles_per_q, 0.0, NEG))
    sm_stage(near_a, sa_ref, pa_ref, ala_ref, True)
    pv_stage(near_a, pa_ref, ala_ref)
    sm_stage(near_b, sb_ref, pb_ref, alb_ref, True)
    pv_stage(near_b, pb_ref, alb_ref)

    o_kv = []
    for kv in range(N_KV):
        l_row = (1 - kv) * HEAD_DIM
        acc = acc_ref[kv]
        o_s = acc / acc[l_row:l_row + 1, :]
        o_kv.append(oc_ref[kv] + gate(kv, 1) * o_s)

    frow4 = lax.broadcasted_iota(jnp.int32, (KV_DIM, gq), 0)
    o_t = jnp.where(frow4 < HEAD_DIM, o_kv[0], o_kv[1])
    for j in range(GROUP):
        o_ref[:, j * LANES:(j + 1) * LANES] = o_t[:, j * Q_BLOCK:(j + 1) * Q_BLOCK].T.astype(BF16)


def _nsa_prompt(qat, gat, kc, vct, ksb, vst3, kwb, vwt3, mt, cc, cs, bw, eb):
    t = qat.shape[1]
    n_blocks = t // SEL_BLOCK
    nrow = kc.shape[0]
    gq = GROUP * Q_BLOCK
    return pl.pallas_call(
        functools.partial(_nsa_prompt_kernel, n_blocks=n_blocks),
        grid=(t // Q_BLOCK,),
        in_specs=[pl.BlockSpec((Q_DIM, Q_BLOCK), lambda i: (0, i)),
                  pl.BlockSpec((LANES, Q_BLOCK), lambda i: (0, i)),
                  _const_spec(kc.shape), _const_spec(vct.shape),
                  _const_spec(ksb.shape), _const_spec(vst3.shape),
                  _const_spec(kwb.shape), _const_spec(vwt3.shape),
                  _const_spec(mt.shape), _const_spec(cc.shape),
                  _const_spec(cs.shape), _const_spec(bw.shape), _const_spec(eb.shape)],
        out_specs=pl.BlockSpec((Q_BLOCK, Q_DIM), lambda i: (i, 0)),
        out_shape=jax.ShapeDtypeStruct((t, Q_DIM), BF16),
        scratch_shapes=[pltpu.VMEM((N_KV, nrow, gq), F32),
                        pltpu.VMEM((N_KV, WINDOW + Q_BLOCK, gq), F32),
                        pltpu.VMEM((N_KV, n_blocks, Q_BLOCK), F32),
                        pltpu.VMEM((N_KV, 1, gq), F32),
                        pltpu.VMEM((N_KV, KV_DIM, gq), F32),
                        pltpu.VMEM((N_KV, KV_DIM, gq), F32),
                        pltpu.VMEM((N_KV, SEL_TILE, gq), F32), pltpu.VMEM((N_KV, SEL_TILE, gq), F32),
                        pltpu.VMEM((N_KV, SEL_TILE, gq), BF16), pltpu.VMEM((N_KV, SEL_TILE, gq), BF16),
                        pltpu.VMEM((N_KV, 1, gq), F32), pltpu.VMEM((N_KV, 1, gq), F32)],
        compiler_params=_cparams(("arbitrary",)),
        name="nsa_prompt",
    )(qat, gat, kc, vct, ksb, vst3, kwb, vwt3, mt, cc, cs, bw, eb)


S_ROWS = 128
S_CHUNK_PAGES = 32
S_CHUNK = S_CHUNK_PAGES * PAGE


def _nsa_sample_kernel(pt_ref, qx_ref, gm_ref, kc_ref, vct_ref, kpool_ref, vpool_ref,
                       knew_ref, vnew_ref, kwin_ref, vwin_ref, kwnew_ref, vwnew_ref,
                       mt_ref, gsum_ref, ccs_ref, css_ref, cns_ref, cws_ref, o_ref,
                       kbuf, vbuf, sem, mask_ref, m_ref, l_ref, acc_ref, oc_ref, ow_ref,
                       *, n_chunks, n_blocks):
    b = pl.program_id(0)
    c = pl.program_id(1)
    step = b * n_chunks + c
    total = pl.num_programs(0) * n_chunks
    slot = step % 2

    def copies(bb, cc, s, p):
        pg = cc * S_CHUNK_PAGES + p
        dst = pl.ds(pl.multiple_of(p * PAGE, PAGE), PAGE)
        return (pltpu.make_async_copy(kpool_ref.at[pt_ref[bb, pg]], kbuf.at[s, :, dst], sem.at[0, s]),
                pltpu.make_async_copy(vpool_ref.at[pt_ref[bb, pg]], vbuf.at[s, :, dst], sem.at[1, s]))

    def start_all(st, s):
        bb = st // n_chunks
        cc = st % n_chunks

        def body(p, carry):
            ck, cv = copies(bb, cc, s, p)
            ck.start()
            cv.start()
            return carry
        lax.fori_loop(0, S_CHUNK_PAGES, body, 0)

    @pl.when(step == 0)
    def _():
        start_all(step, slot)

    @pl.when(step + 1 < total)
    def _():
        start_all(step + 1, 1 - slot)

    qx = qx_ref[0]
    n_q = o_ref.shape[1]
    qq = qx[:n_q]

    @pl.when(c == 0)
    def _():
        nrow = kc_ref.shape[1]
        s = _nt(kc_ref[0], qx)
        r = lax.broadcasted_iota(jnp.int32, (nrow, S_ROWS), 0)
        band0 = nrow - CMP_BAND
        s = s + jnp.concatenate([jnp.zeros((band0, S_ROWS), F32), ccs_ref[...]], axis=0)
        pn = _softmax_cols(s, r < nrow - 1)
        oc_ref[...] = jnp.dot(vct_ref[0], pn.astype(BF16), preferred_element_type=F32).T[:n_q]
        imp = jnp.dot(pn, gsum_ref[...], preferred_element_type=F32, precision=HIGHEST)
        score = _split_dot(mt_ref[...], imp)
        blk_i = lax.broadcasted_iota(jnp.int32, (n_blocks, S_ROWS), 0)
        forced = (blk_i == 0) | (blk_i == n_blocks - 1)
        score = score + jnp.where(forced, FORCE_BONUS, 0.0)
        selt = _select_topk(score, blk_i.astype(F32), N_SEL - 1)
        selt = lax.dot_general(selt.astype(BF16), gsum_ref[...].astype(BF16), (((1,), (1,)), ((), ())),
                               preferred_element_type=F32)
        selm1 = selt.T[:n_q] - 1.0
        e_r = lax.broadcasted_iota(jnp.int32, (BLK_PER_TILE, SEL_TILE), 0)
        e_c = lax.broadcasted_iota(jnp.int32, (BLK_PER_TILE, SEL_TILE), 1)
        expand = jnp.where(e_c // SEL_BLOCK == e_r, -NEG, 0.0)
        for kt in range(n_blocks // BLK_PER_TILE):
            mask_ref[:, kt * SEL_TILE:(kt + 1) * SEL_TILE] = jnp.dot(
                selm1[:, kt * BLK_PER_TILE:(kt + 1) * BLK_PER_TILE], expand, preferred_element_type=F32)

        sw = _nt(qq, kwin_ref[0].astype(BF16)) + cws_ref[:n_q, :]
        sn = _nt(qq, kwnew_ref[0].astype(BF16)) + cns_ref[:n_q, :]
        m = jnp.maximum(jnp.max(sw, axis=1, keepdims=True), jnp.max(sn, axis=1, keepdims=True))
        pw = jnp.exp2(sw - m)
        pn2 = jnp.exp2(sn - m)
        l = jnp.sum(pw, axis=1, keepdims=True) + jnp.sum(pn2, axis=1, keepdims=True)
        ow = (jnp.dot(pw.astype(BF16), vwin_ref[0].astype(BF16), preferred_element_type=F32)
              + jnp.dot(pn2.astype(BF16), vwnew_ref[0].astype(BF16), preferred_element_type=F32))
        ow_ref[...] = ow / l

        m_ref[...] = jnp.full_like(m_ref, NEG)
        l_ref[...] = jnp.zeros_like(l_ref)
        acc_ref[...] = jnp.zeros_like(acc_ref)

    def wait_body(p, carry):
        ck, cv = copies(b, c, slot, p)
        ck.wait()
        cv.wait()
        return carry
    lax.fori_loop(0, S_CHUNK_PAGES, wait_body, 0)

    col0 = pl.multiple_of(c * S_CHUNK, S_CHUNK)
    s = (jnp.dot(qq, kbuf[slot].astype(BF16), preferred_element_type=F32)
         + mask_ref[:, pl.ds(col0, S_CHUNK)])

    @pl.when(c < n_chunks - 1)
    def _():
        _online_update(s, vbuf[slot].astype(BF16), m_ref, l_ref, acc_ref, v_transposed=True)

    @pl.when(c == n_chunks - 1)
    def _():
        near = jnp.concatenate([jnp.zeros((n_q, S_CHUNK - LANES), F32), css_ref[:n_q, :]], axis=1)
        _online_update(s + near, vbuf[slot].astype(BF16), m_ref, l_ref, acc_ref, v_transposed=True)
        sn = _nt(qq, knew_ref[0].astype(BF16)) + cns_ref[:n_q, :]
        _online_update(sn, vnew_ref[0].astype(BF16), m_ref, l_ref, acc_ref)
        o_s = acc_ref[...] / l_ref[...]
        o_ref[0] = gm_ref[0, 0] * oc_ref[...] + gm_ref[0, 1] * o_s + gm_ref[0, 2] * ow_ref[...]


def _nsa_sample(page_table, qx, gm, kc, vct, kpool, vpool, knew, vnew, kwin, vwin, kwnew, vwnew,
                mt, gsum, ccs, css, cns, cws):
    nb = qx.shape[0]
    n_pages = page_table.shape[1]
    n_chunks = n_pages // S_CHUNK_PAGES
    past = n_pages * PAGE
    n_blocks = past // SEL_BLOCK
    n_q = gm.shape[2]

    def bspec(shape):
        nd = len(shape)
        return pl.BlockSpec((1,) + tuple(shape[1:]), lambda b, c, pt: (b,) + (0,) * (nd - 1))

    def cspec(shape):
        nd = len(shape)
        return pl.BlockSpec(tuple(shape), lambda b, c, pt: (0,) * nd, pipeline_mode=pl.Buffered(1))

    grid_spec = pltpu.PrefetchScalarGridSpec(
        num_scalar_prefetch=1,
        grid=(nb, n_chunks),
        in_specs=[bspec(qx.shape), bspec(gm.shape), bspec(kc.shape), bspec(vct.shape),
                  pl.BlockSpec(memory_space=pl.ANY), pl.BlockSpec(memory_space=pl.ANY),
                  bspec(knew.shape), bspec(vnew.shape), bspec(kwin.shape), bspec(vwin.shape),
                  bspec(kwnew.shape), bspec(vwnew.shape),
                  cspec(mt.shape), cspec(gsum.shape), cspec(ccs.shape), cspec(css.shape),
                  cspec(cns.shape), cspec(cws.shape)],
        out_specs=pl.BlockSpec((1, n_q, KV_DIM), lambda b, c, pt: (b, 0, 0)),
        scratch_shapes=[pltpu.VMEM((2, KV_DIM, S_CHUNK), F32),
                        pltpu.VMEM((2, KV_DIM, S_CHUNK), F32),
                        pltpu.SemaphoreType.DMA((2, 2)),
                        pltpu.VMEM((n_q, past), F32),
                        pltpu.VMEM((n_q, 1), F32), pltpu.VMEM((n_q, 1), F32),
                        pltpu.VMEM((n_q, KV_DIM), F32),
                        pltpu.VMEM((n_q, KV_DIM), F32), pltpu.VMEM((n_q, KV_DIM), F32)],
    )
    return pl.pallas_call(
        functools.partial(_nsa_sample_kernel, n_chunks=n_chunks, n_blocks=n_blocks),
        grid_spec=grid_spec,
        out_shape=jax.ShapeDtypeStruct((nb, n_q, KV_DIM), F32),
        compiler_params=_cparams(("arbitrary", "arbitrary")),
        name="nsa_sample",
    )(page_table, qx, gm, kc, vct, kpool, vpool, knew, vnew, kwin, vwin, kwnew, vwnew,
      mt, gsum, ccs, css, cns, cws)


def _ffn_kernel(x_ref, oa_ref, ob_ref, sa_ref, sb_ref, wpa_ref, wpb_ref, wo_ref, nf_ref,
                wg_ref, wu_ref, wd_ref, nl_ref, y_ref):
    pa = jnp.dot(oa_ref[...], wpa_ref[...], preferred_element_type=F32)
    pb = jnp.dot(ob_ref[...], wpb_ref[...], preferred_element_type=F32)
    merged = sa_ref[...] * pa + sb_ref[...] * pb
    x = x_ref[...] + jnp.dot(merged.astype(BF16), wo_ref[...], preferred_element_type=F32)
    hn = (x * lax.rsqrt(jnp.mean(x * x, axis=-1, keepdims=True) + EPS) * nf_ref[...]).astype(BF16)
    gate = jnp.dot(hn, wg_ref[...], preferred_element_type=F32)
    up = jnp.dot(hn, wu_ref[...], preferred_element_type=F32)
    ff = (jax.nn.silu(gate) * up).astype(BF16)
    x = x + jnp.dot(ff, wd_ref[...], preferred_element_type=F32)
    y_ref[...] = x * lax.rsqrt(jnp.mean(x * x, axis=-1, keepdims=True) + EPS) * nl_ref[...]


def _ffn(x2d, oa, ob, sa, sb, wpa, wpb, wo, nf, wg, wu, wd, nl, tm):
    rows = x2d.shape[0]

    def rspec(n):
        return pl.BlockSpec((tm, n), lambda i: (i, 0))

    return pl.pallas_call(
        _ffn_kernel,
        grid=(rows // tm,),
        in_specs=[rspec(D_MODEL), rspec(Q_DIM), rspec(HG_W), rspec(D_MODEL), rspec(D_MODEL),
                  _const_spec(wpa.shape), _const_spec(wpb.shape), _const_spec(wo.shape),
                  _const_spec(nf.shape), _const_spec(wg.shape), _const_spec(wu.shape),
                  _const_spec(wd.shape), _const_spec(nl.shape)],
        out_specs=rspec(D_MODEL),
        out_shape=jax.ShapeDtypeStruct((rows, D_MODEL), F32),
        compiler_params=_cparams(("arbitrary",)),
        name="ffn",
    )(x2d, oa, ob, sa, sb, wpa, wpb, wo, nf, wg, wu, wd, nl)


def _pack_w_in(w_in):
    sizes = (Q_DIM,) + (KV_DIM,) * 6 + (3 * N_HEADS,) + (HG_W,) * 4 + (D_MODEL,) * 2
    offs = np.concatenate([[0], np.cumsum(sizes)])
    q = w_in[:, offs[0]:offs[1]].reshape(D_MODEL, N_HEADS, HEAD_DIM)[:, _HEAD_PERM, :].reshape(D_MODEL, Q_DIM)
    g = jnp.pad(w_in[:, offs[7]:offs[8]], ((0, 0), (0, LANES - 3 * N_HEADS)))
    return jnp.concatenate([q, w_in[:, offs[1]:offs[7]], g, w_in[:, offs[8]:]], axis=1).astype(BF16)


def _strip(bvc, rel, lo=0, hi=None, masked=NEG):
    val = bvc[:, np.clip(rel, 0, 255)]
    ok = rel >= lo
    if hi is not None:
        ok = ok & (rel < hi)
    return jnp.where(jnp.asarray(ok)[None], val, masked)


def _toeplitz(bvc, a, n_rows, n_cols, lo=0, hi=None):
    n = n_rows + n_cols - 1
    u = _strip(bvc, a - (n_rows - 1) + np.arange(n), lo, hi)
    u = jnp.pad(u, ((0, 0), (0, 1)))
    circ = jnp.tile(u, (1, n_rows))[:, :n_rows * n].reshape(N_HEADS, n_rows, n)
    return circ[:, :, n_rows - 1:n_rows - 1 + n_cols]


def _bias_strips_prompt(bvc):
    gq = GROUP * Q_BLOCK

    def lanes(x):
        return x.reshape(N_KV, GROUP, x.shape[1], Q_BLOCK).transpose(0, 2, 1, 3).reshape(N_KV, x.shape[1], gq)

    cs = jnp.stack([lanes(_toeplitz(bvc, Q_BLOCK * d, Q_BLOCK, Q_BLOCK)) for d in (-1, 0, 1, 2)], axis=1)
    bw = lanes(_toeplitz(bvc, WINDOW, WINDOW + Q_BLOCK, Q_BLOCK, 0, WINDOW))
    rr = np.arange(CMP_BAND)[:, None]
    rel_c = np.arange(Q_BLOCK)[None, :] - CMP_STRIDE * (rr - CMP_PAD) - (CMP_BLOCK - 1)
    cc = _strip(bvc, rel_c, masked=0.0).reshape(N_KV, GROUP, CMP_BAND, Q_BLOCK)
    cc = cc.transpose(0, 2, 1, 3).reshape(N_KV, CMP_BAND, gq)
    return cc, cs, bw


def _bias_strips_sample(bvc, past, t):
    def rows(a):
        a = a.reshape(N_HEADS * t, a.shape[-1])
        return jnp.pad(a, ((0, S_ROWS - N_HEADS * t), (0, 0)))
    tt = np.arange(t)[:, None]
    nrow = past // CMP_STRIDE
    n = (nrow - CMP_BAND + np.arange(CMP_BAND))[None, :]
    ccs = rows(_strip(bvc, past + tt - CMP_STRIDE * n - (CMP_BLOCK - 1), masked=0.0)).T
    i = np.arange(LANES)[None, :]
    css = rows(_strip(bvc, LANES + tt - i))
    cns = rows(_strip(bvc, np.where(i < t, tt - i, -1)))
    iw = np.arange(WINDOW)[None, :]
    cws = rows(_strip(bvc, WINDOW + tt - iw, 0, WINDOW))
    return ccs, css, cns, cws


def kernel(x_prompt, x_sample, cache_k_cmp, cache_v_cmp, cache_k_slc, cache_v_slc, state_k_win, state_v_win,
           state_hgrn, page_table, norm_mix, w_in, cmp_pe_k, cmp_w1_k, cmp_w2_k, cmp_pe_v, cmp_w1_v, cmp_w2_v,
           rel_bias, hg_lb_logits, hg_norm, w_proj_a, w_proj_b, w_out, norm_ffn, w_gate, w_up, w_down, norm_final):
    nbp, t_p, _ = x_prompt.shape
    nbs, t_s, _ = x_sample.shape
    assert nbp == 1 and norm_mix.shape[0] == 1
    n_pages = page_table.shape[1]
    past = n_pages * PAGE
    assert state_k_win.shape[2] == WINDOW and past % S_CHUNK == 0 and t_s <= SUBLANES

    lb = jnp.cumsum(jax.nn.softmax(hg_lb_logits.astype(F32), axis=0), axis=0)[0]
    lb3 = jnp.pad(jnp.stack([jnp.log(lb), jnp.log1p(-lb), 1.0 - lb]), ((0, SUBLANES - 3), (0, 0)))
    w_pack = _pack_w_in(w_in[0])
    g_mix = norm_mix[0][None, :]
    wpa = w_proj_a[0].reshape(N_HEADS, HEAD_DIM, D_MODEL)[_HEAD_PERM].reshape(Q_DIM, D_MODEL).astype(BF16)
    wpb = w_proj_b[0].astype(BF16)
    wo = w_out[0].astype(BF16)
    wg, wu, wd = w_gate[0].astype(BF16), w_up[0].astype(BF16), w_down[0].astype(BF16)
    nf, nl = norm_ffn[0][None, :], norm_final[None, :]
    gn = hg_norm[0][None, :]
    pe_k, w1_k, w2_k = _compress_weights(cmp_pe_k[0], cmp_w1_k[0], cmp_w2_k[0])
    pe_v, w1_v, w2_v = _compress_weights(cmp_pe_v[0], cmp_w1_v[0], cmp_w2_v[0])
    bvc = (rel_bias[_BUCKET] - rel_bias[N_BUCKETS - 1][None, :]).T * LOG2E
    cc, cs, bw = _bias_strips_prompt(bvc)
    ccs, css, cns, cws = _bias_strips_sample(bvc, past, t_s)

    xp2 = x_prompt.reshape(t_p, D_MODEL)
    xs2 = x_sample.reshape(nbs * t_s, D_MODEL)
    seg = lambda off, n: w_pack[:, off:off + n]
    w_t = jnp.concatenate([seg(_OFF_Q, Q_DIM), seg(_OFF_G, LANES), seg(_OFF_KV + 3 * KV_DIM, KV_DIM),
                           seg(_OFF_KV + 5 * KV_DIM, KV_DIM)], axis=1).T
    pp = _proj(xp2, g_mix, w_pack, w_t, lb3, 256)
    ps = _proj(xs2, g_mix, w_pack, w_t, lb3, nbs * t_s)
    (_, kc_p, vc_p, ks_p, vs_p, kw_p, vw_p, ksb_p, kwb_p, _,
     qh_p, lf_p, kh_p, vh_p, gs_p, sa_p, sb_p, qat_p, gat_p, vst_p, vwt_p) = pp
    (qa_s, kc_s, vc_s, ks_s, vs_s, kw_s, vw_s, _, _, ga_s,
     qh_s, lf_s, kh_s, vh_s, gs_s, sa_s, sb_s, _, _, _, _) = ps

    ob_p, s_p = _hgrn_prompt(qh_p, kh_p, vh_p, lf_p, gs_p, gn)
    r3 = lambda a: a.reshape(nbs, t_s, a.shape[-1])
    ob_s, s_s = _hgrn_sample(r3(qh_s), r3(kh_s), r3(vh_s), r3(lf_s), r3(gs_s), gn, state_hgrn[0])

    ident = jnp.arange(t_p // PAGE, dtype=jnp.int32)[None, :]
    pool_rows = lambda a: a.reshape(-1, PAGE, KV_DIM)
    kcb_p, _ = _compress(pool_rows(kc_p), ident, pe_k, w1_k, w2_k, False)
    _, vct_p = _compress(pool_rows(vc_p), ident, pe_v, w1_v, w2_v, False)
    kcb_p = jnp.pad(kcb_p[0], ((CMP_PAD, 0), (0, 0)))
    vct_p = jnp.pad(vct_p[0], ((0, 0), (CMP_PAD, 0)))
    nrow_p = kcb_p.shape[0]
    mt_p = jnp.asarray(_score_matrix(nrow_p, CMP_PAD, t_p // SEL_BLOCK), BF16)
    kwb_pad = jnp.pad(kwb_p, ((WINDOW, 0), (0, 0)))
    vst3 = vst_p.reshape(KV_DIM, t_p // SEL_TILE, SEL_TILE).transpose(1, 0, 2)
    own = (np.arange(KV_DIM)[None, :] // HEAD_DIM == np.arange(N_KV)[:, None])[:, None, :, None]
    vst3 = jnp.where(own, vst3[None], jnp.ones((), BF16))
    vwt3 = jnp.pad(vwt_p, ((0, 0), (WINDOW, 0))).reshape(KV_DIM, (t_p + WINDOW) // Q_BLOCK, Q_BLOCK)
    vwt3 = vwt3.transpose(1, 0, 2)
    eb = (np.arange(LANES)[None, :] == np.arange(SEL_TILE)[:, None] // SEL_BLOCK).astype(np.float32)
    oa_p = _nsa_prompt(qat_p, gat_p, kcb_p, vct_p, ksb_p, vst3, kwb_pad, vwt3, mt_p, cc, cs, bw,
                       jnp.asarray(eb, BF16))

    pool_t = lambda a: a.transpose(0, 2, 3, 1).reshape(-1, KV_DIM, PAGE)
    kcb_s, _ = _compress(pool_t(cache_k_cmp[0]), page_table, pe_k, w1_k, w2_k, True)
    _, vct_s = _compress(pool_t(cache_v_cmp[0]), page_table, pe_v, w1_v, w2_v, True)
    nq = N_HEADS * t_s
    qs4 = qa_s.reshape(nbs, t_s, GROUP, N_KV, HEAD_DIM).astype(F32)
    qx = jnp.einsum('btjkd,kq->bkjtqd', qs4, jnp.eye(N_KV, dtype=F32)).reshape(nbs, nq, KV_DIM)
    qx = jnp.pad(qx, ((0, 0), (0, S_ROWS - nq), (0, 0))).astype(BF16)
    g4 = ga_s[:, :3 * N_HEADS].reshape(nbs, t_s, N_KV, GROUP, 3)
    gm = jnp.transpose(g4, (0, 4, 2, 3, 1)).reshape(nbs, 3, nq, 1)
    gm = jnp.broadcast_to(gm, (nbs, 3, nq, KV_DIM))
    new_tile = lambda a: jnp.pad(a.reshape(nbs, t_s, KV_DIM), ((0, 0), (0, LANES - t_s), (0, 0)))
    mt_s = jnp.asarray(_score_matrix(past // CMP_STRIDE, 0, past // SEL_BLOCK), BF16)
    gsum = np.zeros((S_ROWS, S_ROWS), np.float32)
    for kv in range(N_KV):
        for j in range(GROUP):
            for t in range(t_s):
                gsum[(kv * GROUP + j) * t_s + t, kv * t_s + t] = 1.0
    o_kv = _nsa_sample(page_table, qx, gm, kcb_s, vct_s,
                       pool_t(cache_k_slc[0]), pool_t(cache_v_slc[0]),
                       new_tile(ks_s), new_tile(vs_s), state_k_win[0].reshape(nbs, WINDOW, KV_DIM),
                       state_v_win[0].reshape(nbs, WINDOW, KV_DIM), new_tile(kw_s), new_tile(vw_s),
                       mt_s, jnp.asarray(gsum), ccs, css, cns, cws)
    o5 = o_kv.reshape(nbs, N_KV, GROUP, t_s, N_KV, HEAD_DIM)
    oa_s = jnp.einsum('bkjtqd,kq->btjkd', o5, jnp.eye(N_KV, dtype=F32)).reshape(nbs * t_s, Q_DIM).astype(BF16)

    y_p = _ffn(xp2, oa_p, ob_p, sa_p, sb_p, wpa, wpb, wo, nf, wg, wu, wd, nl, 256)
    y_s = _ffn(xs2, oa_s, ob_s.reshape(nbs * t_s, HG_W), sa_s, sb_s, wpa, wpb, wo, nf, wg, wu, wd, nl, nbs * t_s)

    kv5 = lambda a, nb_, tt: a.reshape(1, nb_, tt, N_KV, HEAD_DIM)
    wl = min(WINDOW, t_p)
    win = lambda st, new: jnp.concatenate(
        [st[0], new.reshape(nbs, t_s, N_KV, HEAD_DIM)], axis=1)[:, -WINDOW:][None]
    return (y_p.reshape(1, t_p, D_MODEL), y_s.reshape(nbs, t_s, D_MODEL),
            kv5(kc_p, 1, t_p), kv5(vc_p, 1, t_p), kv5(ks_p, 1, t_p), kv5(vs_p, 1, t_p),
            kv5(kw_p[-wl:], 1, wl), kv5(vw_p[-wl:], 1, wl), s_p[None, None],
            kv5(kc_s, nbs, t_s), kv5(vc_s, nbs, t_s), kv5(ks_s, nbs, t_s), kv5(vs_s, nbs, t_s),
            win(state_k_win, kw_s), win(state_v_win, vw_s), s_s[None])
```

```python
import functools
import math

import numpy as np
import jax
import jax.numpy as jnp
from jax import lax
from jax.experimental import pallas as pl
from jax.experimental.pallas import tpu as pltpu

F32 = jnp.float32
BF16 = jnp.bfloat16
HIGHEST = lax.Precision.HIGHEST

D_MODEL = 1024
N_HEADS = 8
N_KV = 2
GROUP = N_HEADS // N_KV
HEAD_DIM = 64
KV_DIM = N_KV * HEAD_DIM
Q_DIM = N_HEADS * HEAD_DIM
CMP_BLOCK = 32
CMP_STRIDE = 16
CMP_HIDDEN = 2 * HEAD_DIM
SEL_BLOCK = 64
N_SEL = 16
WINDOW = 512
Q_BLOCK = 128
FORCE_BONUS = 1e4
N_BUCKETS = 32
MAX_DISTANCE = 128
HG_HEADS = 4
HG_DIM = 128
HG_CHUNK = 64
HG_SUB = 16
HG_W = HG_HEADS * HG_DIM
D_FF = ((8 * D_MODEL // 3 + 255) // 256) * 256
EPS = 1e-6
PAGE = 128
NEG = -1e30
LOG2E = math.log2(math.e)
Q_SCALE = HEAD_DIM ** -0.5 * LOG2E

LANES = 128
SUBLANES = 8
VMEM_LIMIT = 56 * 1024 * 1024

_OFF_Q = 0
_OFF_KV = _OFF_Q + Q_DIM
_OFF_G = _OFF_KV + 6 * KV_DIM
_OFF_HG = _OFF_G + LANES
_OFF_GATE = _OFF_HG + 4 * HG_W
_PROJ_N = _OFF_GATE + 2 * D_MODEL

_HEAD_PERM = np.array([h for j in range(GROUP) for h in (j, GROUP + j)])

SEL_TILE = 512
BLK_PER_TILE = SEL_TILE // SEL_BLOCK
CMP_PAD = 16
CMP_BAND = 24


def _cparams(sem, vmem=VMEM_LIMIT):
    return pltpu.CompilerParams(dimension_semantics=sem, vmem_limit_bytes=vmem)


def _const_spec(shape):
    nd = len(shape)
    return pl.BlockSpec(shape, lambda *_: (0,) * nd, pipeline_mode=pl.Buffered(1))


def _bucket_table():
    n = np.arange(256)
    max_exact = N_BUCKETS // 2
    nf = np.maximum(n, 1).astype(np.float64)
    large = max_exact + (np.log(nf / max_exact) / math.log(MAX_DISTANCE / max_exact)
                         * (N_BUCKETS - max_exact)).astype(np.int64)
    large = np.minimum(large, N_BUCKETS - 1)
    return np.where(n < max_exact, n, large)


_BUCKET = _bucket_table()


def _proj_kernel(x_ref, g_ref, w_ref, wt_ref, lb_ref,
                 qa_ref, kc_ref, vc_ref, ks_ref, vs_ref, kw_ref, vw_ref,
                 ksb_ref, kwb_ref, ga_ref,
                 qh_ref, lf_ref, kh_ref, vh_ref, gs_ref, sa_ref, sb_ref,
                 qat_ref, gat_ref, vst_ref, vwt_ref):
    x = x_ref[...]
    xn = x * lax.rsqrt(jnp.mean(x * x, axis=-1, keepdims=True) + EPS) * g_ref[...]
    xb = xn.astype(BF16)

    def seg(a, n):
        return jnp.dot(xb, w_ref[:, a:a + n], preferred_element_type=F32)

    qa_ref[...] = (seg(_OFF_Q, Q_DIM) * Q_SCALE).astype(BF16)
    f32_refs = (kc_ref, vc_ref, ks_ref, vs_ref, kw_ref, vw_ref)
    b16_refs = (None, None, ksb_ref, None, kwb_ref, None)
    for i in range(6):
        u = seg(_OFF_KV + i * KV_DIM, KV_DIM)
        f32_refs[i][...] = u
        if b16_refs[i] is not None:
            b16_refs[i][...] = u.astype(BF16)
    ga_ref[...] = jax.nn.sigmoid(seg(_OFF_G, LANES))

    def seg_t(a, n):
        return lax.dot_general(wt_ref[a:a + n, :], xb, (((1,), (1,)), ((), ())), preferred_element_type=F32)

    qat_ref[...] = (seg_t(0, Q_DIM) * Q_SCALE).astype(BF16)
    gat_ref[...] = jax.nn.sigmoid(seg_t(Q_DIM, LANES))
    vst_ref[...] = seg_t(Q_DIM + LANES, KV_DIM).astype(BF16)
    vwt_ref[...] = seg_t(Q_DIM + LANES + KV_DIM, KV_DIM).astype(BF16)

    log_lb = lb_ref[0:1, :]
    log_1m = lb_ref[1:2, :]
    one_m = lb_ref[2:3, :]
    qh_ref[...] = jax.nn.silu(seg(_OFF_HG, HG_W))
    z = seg(_OFF_HG + HG_W, HG_W)
    b = log_1m + (jnp.minimum(z, 0.0) - jnp.log1p(jnp.exp(-jnp.abs(z))))
    hi = jnp.maximum(log_lb, b)
    lf_ref[...] = hi + jnp.log1p(jnp.exp(-jnp.abs(log_lb - b)))
    kh_ref[...] = one_m * jax.nn.sigmoid(-z)
    vh_ref[...] = seg(_OFF_HG + 2 * HG_W, HG_W)
    gs_ref[...] = jax.nn.silu(seg(_OFF_HG + 3 * HG_W, HG_W))
    sa_ref[...] = jax.nn.sigmoid(seg(_OFF_GATE, D_MODEL))
    sb_ref[...] = jax.nn.sigmoid(seg(_OFF_GATE + D_MODEL, D_MODEL))


_PROJ_T = Q_DIM + LANES + 2 * KV_DIM


def _proj(x2d, g, w, wt, lb3, tm):
    rows = x2d.shape[0]
    widths = ([(Q_DIM, BF16)] + [(KV_DIM, F32)] * 6 + [(KV_DIM, BF16)] * 2 + [(LANES, F32)]
              + [(HG_W, F32)] * 5 + [(D_MODEL, F32)] * 2)
    heights = [(Q_DIM, BF16), (LANES, F32), (KV_DIM, BF16), (KV_DIM, BF16)]
    return pl.pallas_call(
        _proj_kernel,
        grid=(rows // tm,),
        in_specs=[pl.BlockSpec((tm, D_MODEL), lambda i: (i, 0)),
                  _const_spec((1, D_MODEL)),
                  _const_spec((D_MODEL, _PROJ_N)),
                  _const_spec((_PROJ_T, D_MODEL)),
                  _const_spec((SUBLANES, HG_W))],
        out_specs=([pl.BlockSpec((tm, n), lambda i: (i, 0)) for n, _ in widths]
                   + [pl.BlockSpec((n, tm), lambda i: (0, i)) for n, _ in heights]),
        out_shape=([jax.ShapeDtypeStruct((rows, n), dt) for n, dt in widths]
                   + [jax.ShapeDtypeStruct((n, rows), dt) for n, dt in heights]),
        compiler_params=_cparams(("arbitrary",)),
        name="proj",
    )(x2d, g, w, wt, lb3)


_CH_W = CMP_STRIDE * KV_DIM
_CH_PER_PAGE = PAGE // CMP_STRIDE
_CH_PITCH = CMP_STRIDE + SUBLANES


def _compress_kernel(pt_ref, pool_ref, pe_ref, w1_ref, w2_ref, out_ref, outt_ref, buf, rbuf, xa, hbuf, sem,
                     *, n_pages, pages_transposed):
    b = pl.program_id(0)
    nb = pl.num_programs(0)
    slot = b % 2
    n_ch = n_pages * _CH_PER_PAGE

    def page_copy(bb, p, s):
        return pltpu.make_async_copy(pool_ref.at[pt_ref[bb, p]], buf.at[s, p], sem.at[s])

    def start_all(bb, s):
        def body(p, c):
            page_copy(bb, p, s).start()
            return c
        lax.fori_loop(0, n_pages, body, 0)

    @pl.when(b == 0)
    def _():
        start_all(b, slot)

    @pl.when(b + 1 < nb)
    def _():
        start_all(b + 1, 1 - slot)

    def wait_body(p, c):
        page_copy(b, p, slot).wait()
        return c
    lax.fori_loop(0, n_pages, wait_body, 0)

    def to_rows(p, c):
        page = buf[slot, p]
        page = page.T if pages_transposed else page
        for i in range(_CH_PER_PAGE):
            dst = pl.multiple_of((p * _CH_PER_PAGE + i) * _CH_PITCH, SUBLANES)
            rbuf[pl.ds(dst, CMP_STRIDE), :] = page[i * CMP_STRIDE:(i + 1) * CMP_STRIDE, :]
        return c
    lax.fori_loop(0, n_pages, to_rows, 0, unroll=8)

    rows = math.gcd(n_ch, 256)
    for r in range(n_ch // rows):
        for s in range(CMP_STRIDE):
            x = rbuf[pl.ds(r * rows * _CH_PITCH + s, rows, stride=_CH_PITCH), :]
            xa[:, s * KV_DIM:(s + 1) * KV_DIM] = x.astype(BF16)
        hbuf[r * rows:(r + 1) * rows, :] = jnp.dot(xa[...], w1_ref[...], preferred_element_type=F32)
    pw = _split_dot_rhs(pe_ref[...], w1_ref[...])
    nh = N_KV * CMP_HIDDEN
    bias = pw[0:1, 0:nh] + pw[1:2, nh:2 * nh]
    h = hbuf[:, 0:nh] + pltpu.roll(hbuf[:, nh:2 * nh], n_ch - 1, 0) + bias
    blocks = jnp.dot(jax.nn.gelu(h).astype(BF16), w2_ref[...], preferred_element_type=F32)
    row = lax.broadcasted_iota(jnp.int32, blocks.shape, 0)
    blocks = jnp.where(row < n_ch - 1, blocks, 0.0)
    out_ref[0] = blocks.astype(BF16)
    outt_ref[0] = blocks.T.astype(BF16)


def _compress(pool, page_table, pe, w1, w2, pages_transposed):
    nbatch, n_pages = page_table.shape
    n_ch = n_pages * _CH_PER_PAGE
    rows = math.gcd(n_ch, 256)
    grid_spec = pltpu.PrefetchScalarGridSpec(
        num_scalar_prefetch=1,
        grid=(nbatch,),
        in_specs=[pl.BlockSpec(memory_space=pl.ANY),
                  _const_spec((SUBLANES, _CH_W)),
                  _const_spec((_CH_W, 4 * CMP_HIDDEN)),
                  _const_spec((2 * CMP_HIDDEN, KV_DIM))],
        out_specs=[pl.BlockSpec((1, n_ch, KV_DIM), lambda b, pt: (b, 0, 0)),
                   pl.BlockSpec((1, KV_DIM, n_ch), lambda b, pt: (b, 0, 0))],
        scratch_shapes=[pltpu.VMEM((2, n_pages, PAGE, KV_DIM), F32),
                        pltpu.VMEM((n_ch * _CH_PITCH, KV_DIM), F32),
                        pltpu.VMEM((rows, _CH_W), BF16),
                        pltpu.VMEM((n_ch, 4 * CMP_HIDDEN), F32),
                        pltpu.SemaphoreType.DMA((2,))],
    )
    return pl.pallas_call(
        functools.partial(_compress_kernel, n_pages=n_pages, pages_transposed=pages_transposed),
        grid_spec=grid_spec,
        out_shape=[jax.ShapeDtypeStruct((nbatch, n_ch, KV_DIM), BF16),
                   jax.ShapeDtypeStruct((nbatch, KV_DIM, n_ch), BF16)],
        compiler_params=_cparams(("arbitrary",)),
        name="compress",
    )(page_table, pool, pe, w1, w2)


def _compress_weights(pe, w1, w2):
    c = CMP_BLOCK // CMP_STRIDE
    pe_r = pe.reshape(c, CMP_STRIDE, 1, HEAD_DIM)
    pe_x = jnp.broadcast_to(pe_r, (c, CMP_STRIDE, N_KV, HEAD_DIM)).reshape(c, _CH_W)
    pe_x = jnp.pad(pe_x, ((0, SUBLANES - c), (0, 0)))
    w1_r = w1.reshape(c, CMP_STRIDE, HEAD_DIM, CMP_HIDDEN)
    eye = jnp.eye(N_KV, dtype=w1.dtype)
    w1_x = jnp.einsum('jsde,kq->skdjqe', w1_r, eye).reshape(_CH_W, c * N_KV * CMP_HIDDEN)
    w2_x = jnp.einsum('ed,kq->keqd', w2, eye).reshape(N_KV * CMP_HIDDEN, KV_DIM)
    return pe_x, w1_x.astype(BF16), w2_x.astype(BF16)


def _hgrn_chunk(q, k, v, lf, st_ref, chunk, sub):
    if chunk > SUBLANES:
        r = lax.broadcasted_iota(jnp.int32, (chunk, chunk), 0)
        c = lax.broadcasted_iota(jnp.int32, (chunk, chunk), 1)
        tri = (r >= c).astype(F32)
        b = jnp.dot(tri, lf, preferred_element_type=F32, precision=HIGHEST)
    else:
        rows = [lf[0:1, :]]
        for t in range(1, chunk):
            rows.append(rows[-1] + lf[t:t + 1, :])
        b = jnp.concatenate(rows, axis=0)
    bl = b[chunk - 1:chunk, :]
    qe = q * jnp.exp(b)
    kd = k * jnp.exp(bl - b)
    ebl = jnp.exp(bl)
    n_sub = chunk // sub
    trow = lax.broadcasted_iota(jnp.int32, (sub, HG_W), 0)
    crow = lax.broadcasted_iota(jnp.int32, (chunk, HG_W), 0)

    diag = []
    for i in range(n_sub):
        qi = q[i * sub:(i + 1) * sub, :]
        bi = b[i * sub:(i + 1) * sub, :]
        acc = [jnp.zeros((sub, HG_DIM), F32) for _ in range(HG_HEADS)]
        for s in range(sub):
            row = i * sub + s
            dec = jnp.exp(jnp.where(trow >= s, bi - b[row:row + 1, :], -jnp.inf))
            prod = qi * k[row:row + 1, :] * dec
            for h in range(HG_HEADS):
                a = jnp.sum(prod[:, h * HG_DIM:(h + 1) * HG_DIM], axis=1, keepdims=True)
                acc[h] = acc[h] + a * v[row:row + 1, h * HG_DIM:(h + 1) * HG_DIM]
        diag.append(acc)

    off = []
    for i in range(n_sub):
        if i == 0:
            off.append(None)
            continue
        b0 = b[i * sub - 1:i * sub, :]
        qs = (q[i * sub:(i + 1) * sub, :] * jnp.exp(b[i * sub:(i + 1) * sub, :] - b0)).astype(BF16)
        ks = (k * jnp.exp(jnp.where(crow < i * sub, b0 - b, -jnp.inf))).astype(BF16)
        off.append((qs, ks))

    vb = v.astype(BF16)
    outs = []
    for h in range(HG_HEADS):
        sl = slice(h * HG_DIM, (h + 1) * HG_DIM)
        st = st_ref[h]
        o_h = lax.dot_general(qe[:, sl].astype(BF16), st.astype(BF16), (((1,), (1,)), ((), ())),
                              preferred_element_type=F32)
        parts = []
        for i in range(n_sub):
            d = diag[i][h]
            if off[i] is not None:
                qs, ks = off[i]
                a = lax.dot_general(qs[:, sl], ks[:, sl], (((1,), (1,)), ((), ())),
                                    preferred_element_type=F32)
                d = d + jnp.dot(a.astype(BF16), vb[:, sl], preferred_element_type=F32)
            parts.append(d)
        intra = parts[0] if n_sub == 1 else jnp.concatenate(parts, axis=0)
        outs.append(o_h + intra)
        st_ref[h] = st * ebl[:, sl] + lax.dot_general(
            vb[:, sl], kd[:, sl].astype(BF16), (((0,), (0,)), ((), ())), preferred_element_type=F32)
    return jnp.concatenate(outs, axis=1)


def _hgrn_finish(o, gs, gn):
    outs = []
    for h in range(HG_HEADS):
        oh = o[:, h * HG_DIM:(h + 1) * HG_DIM]
        y = oh * lax.rsqrt(jnp.mean(oh * oh, axis=-1, keepdims=True) + EPS) * gn
        outs.append(y)
    return (jnp.concatenate(outs, axis=1) * gs).astype(BF16)


def _hgrn_prompt_kernel(q_ref, k_ref, v_ref, lf_ref, gs_ref, gn_ref, o_ref, s_ref, st_ref, *, n_chunks):
    i = pl.program_id(0)

    @pl.when(i == 0)
    def _():
        st_ref[...] = jnp.zeros_like(st_ref)

    def body(c, carry):
        r = pl.ds(pl.multiple_of(c * HG_CHUNK, HG_CHUNK), HG_CHUNK)
        o = _hgrn_chunk(q_ref[r, :], k_ref[r, :], v_ref[r, :], lf_ref[r, :], st_ref, HG_CHUNK, HG_SUB)
        o_ref[r, :] = _hgrn_finish(o, gs_ref[r, :], gn_ref[...])
        return carry
    lax.fori_loop(0, n_chunks, body, 0, unroll=4)

    @pl.when(i == pl.num_programs(0) - 1)
    def _():
        for h in range(HG_HEADS):
            s_ref[h] = st_ref[h].T


def _hgrn_prompt(qh, kh, vh, lf, gs, gn, rows_per_step=512):
    t = qh.shape[0]
    spec = pl.BlockSpec((rows_per_step, HG_W), lambda i: (i, 0))
    return pl.pallas_call(
        functools.partial(_hgrn_prompt_kernel, n_chunks=rows_per_step // HG_CHUNK),
        grid=(t // rows_per_step,),
        in_specs=[spec] * 5 + [_const_spec((1, HG_DIM))],
        out_specs=[spec, pl.BlockSpec((HG_HEADS, HG_DIM, HG_DIM), lambda i: (0, 0, 0))],
        out_shape=[jax.ShapeDtypeStruct((t, HG_W), BF16),
                   jax.ShapeDtypeStruct((HG_HEADS, HG_DIM, HG_DIM), F32)],
        scratch_shapes=[pltpu.VMEM((HG_HEADS, HG_DIM, HG_DIM), F32)],
        compiler_params=_cparams(("arbitrary",)),
        name="hgrn_prompt",
    )(qh, kh, vh, lf, gs, gn)


def _hgrn_sample_kernel(q_ref, k_ref, v_ref, lf_ref, gs_ref, gn_ref, s0_ref, o_ref, s_ref, st_ref, *, t):
    for h in range(HG_HEADS):
        st_ref[h] = s0_ref[0, h].T
    o = _hgrn_chunk(q_ref[0], k_ref[0], v_ref[0], lf_ref[0], st_ref, t, t)
    o_ref[0] = _hgrn_finish(o, gs_ref[0], gn_ref[...])
    for h in range(HG_HEADS):
        s_ref[0, h] = st_ref[h].T


def _hgrn_sample(qh, kh, vh, lf, gs, gn, s0):
    nb, t, _ = qh.shape
    spec = pl.BlockSpec((1, t, HG_W), lambda b: (b, 0, 0))
    sspec = pl.BlockSpec((1, HG_HEADS, HG_DIM, HG_DIM), lambda b: (b, 0, 0, 0))
    return pl.pallas_call(
        functools.partial(_hgrn_sample_kernel, t=t),
        grid=(nb,),
        in_specs=[spec] * 5 + [_const_spec((1, HG_DIM)), sspec],
        out_specs=[spec, sspec],
        out_shape=[jax.ShapeDtypeStruct((nb, t, HG_W), BF16),
                   jax.ShapeDtypeStruct((nb, HG_HEADS, HG_DIM, HG_DIM), F32)],
        scratch_shapes=[pltpu.VMEM((HG_HEADS, HG_DIM, HG_DIM), F32)],
        compiler_params=_cparams(("arbitrary",)),
        name="hgrn_sample",
    )(qh, kh, vh, lf, gs, gn, s0)


def _select_topk(x, blk, n):
    nblk = x.shape[0]
    sel = jnp.zeros_like(x)
    for _ in range(n):
        m = jnp.max(x, axis=0, keepdims=True)
        idx = jnp.min(jnp.where(x == m, blk, float(nblk)), axis=0, keepdims=True)
        pick = blk == idx
        sel = jnp.where(pick, 1.0, sel)
        x = jnp.where(pick, -3e38, x)
    return sel


def _softmax_cols(s, valid):
    m = jnp.max(jnp.where(valid, s, NEG), axis=0, keepdims=True)
    p = jnp.where(valid, jnp.exp2(s - m), 0.0)
    l = jnp.sum(p, axis=0, keepdims=True)
    return p * jnp.where(l > 0.0, 1.0 / l, 0.0)


def _split_dot(a_bf16, x):
    hi = x.astype(BF16)
    lo = (x - hi.astype(F32)).astype(BF16)
    return (jnp.dot(a_bf16, hi, preferred_element_type=F32)
            + jnp.dot(a_bf16, lo, preferred_element_type=F32))


def _split_dot_rhs(x, w_bf16):
    hi = x.astype(BF16)
    lo = (x - hi.astype(F32)).astype(BF16)
    return (jnp.dot(hi, w_bf16, preferred_element_type=F32)
            + jnp.dot(lo, w_bf16, preferred_element_type=F32))


def _nt(a, b):
    return lax.dot_general(a, b, (((1,), (1,)), ((), ())), preferred_element_type=F32)


def _online_update(s, v, m_ref, l_ref, acc_ref, v_transposed=False):
    m_old = m_ref[...]
    m_new = jnp.maximum(m_old, jnp.max(s, axis=1, keepdims=True))
    p = jnp.exp2(s - m_new)
    alpha = jnp.exp2(m_old - m_new)
    l_ref[...] = alpha * l_ref[...] + jnp.sum(p, axis=1, keepdims=True)
    pv = _nt(p.astype(BF16), v) if v_transposed else jnp.dot(p.astype(BF16), v, preferred_element_type=F32)
    acc_ref[...] = alpha * acc_ref[...] + pv
    m_ref[...] = m_new


def _score_matrix(n_rows, row_offset, n_blocks):
    c = CMP_BLOCK // CMP_STRIDE
    ratio = SEL_BLOCK // CMP_STRIDE
    n_ov = ratio + c - 1
    m = np.zeros((n_blocks, n_rows), np.float32)
    for j in range(n_blocks):
        for u in range(n_ov):
            start = CMP_STRIDE * (u - (c - 1))
            w_u = (min(start + CMP_BLOCK, SEL_BLOCK) - max(start, 0)) / CMP_STRIDE
            n = ratio * j + u - (c - 1)
            if 0 <= n and n + row_offset < n_rows:
                m[j, n + row_offset] = w_u
    return m


def _nsa_prompt_kernel(qt_ref, gt_ref, kc_ref, vct_ref, ks_ref, vst_ref, kw_ref, vwt_ref,
                       mt_ref, cc_ref, cs_ref, bw_ref, eb_ref, o_ref,
                       sc_ref, sw_ref, sel_ref, m_ref, acc_ref, oc_ref,
                       sa_ref, sb_ref, pa_ref, pb_ref, ala_ref, alb_ref, *, n_blocks):
    qb = pl.program_id(0)
    nrow = kc_ref.shape[0]
    gq = GROUP * Q_BLOCK
    tiles_per_q = SEL_TILE // Q_BLOCK
    nband = WINDOW + Q_BLOCK
    max_tile = n_blocks // BLK_PER_TILE - 1
    frow = lax.broadcasted_iota(jnp.int32, (KV_DIM, Q_BLOCK), 0)

    def lanes4(x):
        return jnp.concatenate([x] * GROUP, axis=1)

    qx = []
    for kv in range(N_KV):
        keep = (frow >= HEAD_DIM) if kv else (frow < HEAD_DIM)
        qx.append(jnp.concatenate(
            [jnp.where(keep, qt_ref[j * KV_DIM:(j + 1) * KV_DIM, :], jnp.zeros((), BF16))
             for j in range(GROUP)], axis=1))

    def qk_stage(kt, s_ref, penalty=0.0):
        k = ks_ref[pl.ds(pl.multiple_of(kt * SEL_TILE, SEL_TILE), SEL_TILE), :]
        k_aug = jnp.concatenate([k, eb_ref[...]], axis=1)
        for kv in range(N_KV):
            srow = sel_ref[kv, pl.ds(pl.multiple_of(kt * BLK_PER_TILE, BLK_PER_TILE), BLK_PER_TILE), :]
            mrows = jnp.concatenate([lanes4(srow + penalty), jnp.zeros((KV_DIM - BLK_PER_TILE, gq), F32)], axis=0)
            q_aug = jnp.concatenate([qx[kv], mrows.astype(BF16)], axis=0)
            s_ref[kv] = jnp.dot(k_aug, q_aug, preferred_element_type=F32)

    def gate(kv, i):
        return jnp.concatenate(
            [gt_ref[3 * (kv * GROUP + j) + i:3 * (kv * GROUP + j) + i + 1, :] for j in range(GROUP)], axis=1)

    for kv in range(N_KV):
        sc_ref[kv] = jnp.dot(kc_ref[...], qx[kv], preferred_element_type=F32)

    r = lax.broadcasted_iota(jnp.int32, (nrow, Q_BLOCK), 0)
    qpos_c = qb * Q_BLOCK + lax.broadcasted_iota(jnp.int32, (nrow, Q_BLOCK), 1)
    end_pos = (r - CMP_PAD) * CMP_STRIDE + (CMP_BLOCK - 1)
    vis = (r >= CMP_PAD) & (r < CMP_PAD + n_blocks * (SEL_BLOCK // CMP_STRIDE) - 1) & (end_pos <= qpos_c)
    vis_add = lanes4(jnp.where(vis, 0.0, NEG))
    band = pl.ds(pl.multiple_of(qb * SUBLANES, SUBLANES), CMP_BAND)
    blk_i = lax.broadcasted_iota(jnp.int32, (n_blocks, Q_BLOCK), 0)
    cur = (qb * Q_BLOCK + lax.broadcasted_iota(jnp.int32, (n_blocks, Q_BLOCK), 1)) // SEL_BLOCK
    forced = (blk_i == 0) | (blk_i == cur) | (blk_i == cur - 1)
    scores = []
    for kv in range(N_KV):
        sc_ref[kv, band, :] = sc_ref[kv, band, :] + cc_ref[kv]
        s = sc_ref[kv] + vis_add
        m = jnp.max(s, axis=0, keepdims=True)
        p = jnp.exp2(s - m)
        l = jnp.sum(p, axis=0, keepdims=True)
        pn = p * jnp.where(m > 0.5 * NEG, 1.0 / l, 0.0)
        oc_ref[kv] = gate(kv, 0) * jnp.dot(vct_ref[...], pn.astype(BF16), preferred_element_type=F32)
        imp = pn[:, 0:Q_BLOCK]
        for j in range(1, GROUP):
            imp = imp + pn[:, j * Q_BLOCK:(j + 1) * Q_BLOCK]
        score = _split_dot(mt_ref[...], imp)
        scores.append(jnp.where(blk_i <= cur, score + jnp.where(forced, FORCE_BONUS, 0.0), -FORCE_BONUS))
    kband = kw_ref[pl.ds(pl.multiple_of(qb * Q_BLOCK, Q_BLOCK), nband), :]
    for kv in range(N_KV):
        sw_ref[kv] = jnp.dot(kband, qx[kv], preferred_element_type=F32)
    blk2 =jnp.concatenate([blk_i.astype(F32)] * N_KV, axis=1)
    sel = _select_topk(jnp.concatenate(scores, axis=1), blk2, N_SEL)
    for kv in range(N_KV):
        sel_ref[kv] = (sel[:, kv * Q_BLOCK:(kv + 1) * Q_BLOCK] - 1.0) * (-NEG)

    m_ref[...] = jnp.full_like(m_ref, NEG)
    acc_ref[...] = jnp.zeros_like(acc_ref)

    def pv_stage(kt, p_ref, al_ref):
        for kv in range(N_KV):
            acc_ref[kv] = al_ref[kv] * acc_ref[kv] + jnp.dot(vst_ref[kv, kt], p_ref[kv],
                                                             preferred_element_type=F32)

    def sm_stage(kt, s_ref, p_ref, al_ref, near):
        d0 = qb - kt * tiles_per_q
        for kv in range(N_KV):
            s = s_ref[kv]
            if near:
                s = s + jnp.concatenate(
                    [cs_ref[kv, jnp.clip(d0 - i, -1, 2) + 1] for i in range(tiles_per_q)], axis=0)
            m_old = m_ref[kv]
            m_new = jnp.maximum(m_old, jnp.max(s, axis=0, keepdims=True))
            p_ref[kv] = jnp.exp2(s - m_new).astype(BF16)
            al_ref[kv] = jnp.exp2(m_old - m_new)
            m_ref[kv] = m_new

    n_far = jnp.maximum(qb - 1, 0) // tiles_per_q
    pb_ref[...] = jnp.zeros_like(pb_ref)
    alb_ref[...] = jnp.ones_like(alb_ref)
    qk_stage(0, sa_ref)

    vband = jnp.concatenate([vwt_ref[qb + i] for i in range(nband // Q_BLOCK)], axis=1)
    krow = lax.broadcasted_iota(jnp.int32, (nband, Q_BLOCK), 0)
    pos_add = lanes4(jnp.where(krow >= WINDOW - qb * Q_BLOCK, 0.0, NEG))
    for kv in range(N_KV):
        s = sw_ref[kv] + bw_ref[kv] + pos_add
        m = jnp.max(s, axis=0, keepdims=True)
        p = jnp.exp2(s - m)
        o_w = (jnp.dot(vband, p.astype(BF16), preferred_element_type=F32)
               / jnp.sum(p, axis=0, keepdims=True))
        oc_ref[kv] = oc_ref[kv] + gate(kv, 2) * o_w

    def pair_body(u, c):
        t0 = 2 * u
        t1 = jnp.minimum(t0 + 1, max_tile)
        t2 = jnp.minimum(t0 + 2, max_tile)
        qk_stage(t1, sb_ref, jnp.where(t0 + 1 < n_far, 0.0, NEG))
        pv_stage(jnp.maximum(t0 - 1, 0), pb_ref, alb_ref)
        sm_stage(t0, sa_ref, pa_ref, ala_ref, False)
        qk_stage(t2, sa_ref)
        pv_stage(t0, pa_ref, ala_ref)
        sm_stage(t1, sb_ref, pb_ref, alb_ref, False)
        return c
    n_pairs = (n_far + 1) // 2
    lax.fori_loop(0, n_pairs, pair_body, 0)

    near_a = n_far
    near_b = jnp.minimum(n_far + 1, max_tile)
    qk_stage(near_a, sa_ref)
    pv_stage(jnp.clip(2 * n_pairs - 1, 0, max_tile), pb_ref, alb_ref)
    qk_stage(near_b, sb_ref, jnp.where(n_far + 1 <= qb // tiles_per_q, 0.0, NEG))
    sm_stage(near_a, sa_ref, pa_ref, ala_ref, True)
    pv_stage(near_a, pa_ref, ala_ref)
    sm_stage(near_b, sb_ref, pb_ref, alb_ref, True)
    pv_stage(near_b, pb_ref, alb_ref)

    o_kv = []
    for kv in range(N_KV):
        acc = acc_ref[kv]
        o_s = acc[0:HEAD_DIM, :] / acc[HEAD_DIM:HEAD_DIM + 1, :]
        o_kv.append(oc_ref[kv, kv * HEAD_DIM:(kv + 1) * HEAD_DIM, :] + gate(kv, 1) * o_s)
    o_t = jnp.concatenate(o_kv, axis=0)
    for j in range(GROUP):
        o_ref[:, j * LANES:(j + 1) * LANES] = o_t[:, j * Q_BLOCK:(j + 1) * Q_BLOCK].T.astype(BF16)


def _nsa_prompt(qat, gat, kc, vct, ksb, vst3, kwb, vwt3, mt, cc, cs, bw, eb):
    t = qat.shape[1]
    n_blocks = t // SEL_BLOCK
    nrow = kc.shape[0]
    gq = GROUP * Q_BLOCK
    return pl.pallas_call(
        functools.partial(_nsa_prompt_kernel, n_blocks=n_blocks),
        grid=(t // Q_BLOCK,),
        in_specs=[pl.BlockSpec((Q_DIM, Q_BLOCK), lambda i: (0, i)),
                  pl.BlockSpec((LANES, Q_BLOCK), lambda i: (0, i)),
                  _const_spec(kc.shape), _const_spec(vct.shape),
                  _const_spec(ksb.shape), _const_spec(vst3.shape),
                  _const_spec(kwb.shape), _const_spec(vwt3.shape),
                  _const_spec(mt.shape), _const_spec(cc.shape),
                  _const_spec(cs.shape), _const_spec(bw.shape), _const_spec(eb.shape)],
        out_specs=pl.BlockSpec((Q_BLOCK, Q_DIM), lambda i: (i, 0)),
        out_shape=jax.ShapeDtypeStruct((t, Q_DIM), BF16),
        scratch_shapes=[pltpu.VMEM((N_KV, nrow, gq), F32),
                        pltpu.VMEM((N_KV, WINDOW + Q_BLOCK, gq), F32),
                        pltpu.VMEM((N_KV, n_blocks, Q_BLOCK), F32),
                        pltpu.VMEM((N_KV, 1, gq), F32),
                        pltpu.VMEM((N_KV, vst3.shape[2], gq), F32),
                        pltpu.VMEM((N_KV, KV_DIM, gq), F32),
                        pltpu.VMEM((N_KV, SEL_TILE, gq), F32), pltpu.VMEM((N_KV, SEL_TILE, gq), F32),
                        pltpu.VMEM((N_KV, SEL_TILE, gq), BF16), pltpu.VMEM((N_KV, SEL_TILE, gq), BF16),
                        pltpu.VMEM((N_KV, 1, gq), F32), pltpu.VMEM((N_KV, 1, gq), F32)],
        compiler_params=_cparams(("arbitrary",)),
        name="nsa_prompt",
    )(qat, gat, kc, vct, ksb, vst3, kwb, vwt3, mt, cc, cs, bw, eb)


S_ROWS = 128
S_CHUNK_PAGES = 32
S_CHUNK = S_CHUNK_PAGES * PAGE


def _nsa_sample_kernel(pt_ref, qx_ref, gm_ref, kc_ref, vct_ref, kpool_ref, vpool_ref,
                       knew_ref, vnew_ref, kwin_ref, vwin_ref, kwnew_ref, vwnew_ref,
                       mt_ref, gsum_ref, ccs_ref, css_ref, cns_ref, cws_ref, o_ref,
                       kbuf, vbuf, sem, mask_ref, m_ref, l_ref, acc_ref, oc_ref, ow_ref,
                       *, n_chunks, n_blocks):
    b = pl.program_id(0)
    c = pl.program_id(1)
    step = b * n_chunks + c
    total = pl.num_programs(0) * n_chunks
    slot = step % 2

    def copies(bb, cc, s, p):
        pg = cc * S_CHUNK_PAGES + p
        dst = pl.ds(pl.multiple_of(p * PAGE, PAGE), PAGE)
        return (pltpu.make_async_copy(kpool_ref.at[pt_ref[bb, pg]], kbuf.at[s, :, dst], sem.at[0, s]),
                pltpu.make_async_copy(vpool_ref.at[pt_ref[bb, pg]], vbuf.at[s, :, dst], sem.at[1, s]))

    def start_all(st, s):
        bb = st // n_chunks
        cc = st % n_chunks

        def body(p, carry):
            ck, cv = copies(bb, cc, s, p)
            ck.start()
            cv.start()
            return carry
        lax.fori_loop(0, S_CHUNK_PAGES, body, 0)

    @pl.when(step == 0)
    def _():
        start_all(step, slot)

    @pl.when(step + 1 < total)
    def _():
        start_all(step + 1, 1 - slot)

    qx = qx_ref[0]
    n_q = o_ref.shape[1]
    qq = qx[:n_q]

    @pl.when(c == 0)
    def _():
        nrow = kc_ref.shape[1]
        s = _nt(kc_ref[0], qx)
        r = lax.broadcasted_iota(jnp.int32, (nrow, S_ROWS), 0)
        band0 = nrow - CMP_BAND
        s = s + jnp.concatenate([jnp.zeros((band0, S_ROWS), F32), ccs_ref[...]], axis=0)
        pn = _softmax_cols(s, r < nrow - 1)
        oc_ref[...] = jnp.dot(vct_ref[0], pn.astype(BF16), preferred_element_type=F32).T[:n_q]
        imp = jnp.dot(pn, gsum_ref[...], preferred_element_type=F32, precision=HIGHEST)
        score = _split_dot(mt_ref[...], imp)
        blk_i = lax.broadcasted_iota(jnp.int32, (n_blocks, S_ROWS), 0)
        forced = (blk_i == 0) | (blk_i == n_blocks - 1)
        score = score + jnp.where(forced, FORCE_BONUS, 0.0)
        selt = _select_topk(score, blk_i.astype(F32), N_SEL - 1)
        selt = lax.dot_general(selt.astype(BF16), gsum_ref[...].astype(BF16), (((1,), (1,)), ((), ())),
                               preferred_element_type=F32)
        selm1 = selt.T[:n_q] - 1.0
        e_r = lax.broadcasted_iota(jnp.int32, (BLK_PER_TILE, SEL_TILE), 0)
        e_c = lax.broadcasted_iota(jnp.int32, (BLK_PER_TILE, SEL_TILE), 1)
        expand = jnp.where(e_c // SEL_BLOCK == e_r, -NEG, 0.0)
        for kt in range(n_blocks // BLK_PER_TILE):
            mask_ref[:, kt * SEL_TILE:(kt + 1) * SEL_TILE] = jnp.dot(
                selm1[:, kt * BLK_PER_TILE:(kt + 1) * BLK_PER_TILE], expand, preferred_element_type=F32)

        sw = _nt(qq, kwin_ref[0].astype(BF16)) + cws_ref[:n_q, :]
        sn = _nt(qq, kwnew_ref[0].astype(BF16)) + cns_ref[:n_q, :]
        m = jnp.maximum(jnp.max(sw, axis=1, keepdims=True), jnp.max(sn, axis=1, keepdims=True))
        pw = jnp.exp2(sw - m)
        pn2 = jnp.exp2(sn - m)
        l = jnp.sum(pw, axis=1, keepdims=True) + jnp.sum(pn2, axis=1, keepdims=True)
        ow = (jnp.dot(pw.astype(BF16), vwin_ref[0].astype(BF16), preferred_element_type=F32)
              + jnp.dot(pn2.astype(BF16), vwnew_ref[0].astype(BF16), preferred_element_type=F32))
        ow_ref[...] = ow / l

        m_ref[...] = jnp.full_like(m_ref, NEG)
        l_ref[...] = jnp.zeros_like(l_ref)
        acc_ref[...] = jnp.zeros_like(acc_ref)

    def wait_body(p, carry):
        ck, cv = copies(b, c, slot, p)
        ck.wait()
        cv.wait()
        return carry
    lax.fori_loop(0, S_CHUNK_PAGES, wait_body, 0)

    col0 = pl.multiple_of(c * S_CHUNK, S_CHUNK)
    s = (jnp.dot(qq, kbuf[slot].astype(BF16), preferred_element_type=F32)
         + mask_ref[:, pl.ds(col0, S_CHUNK)])

    @pl.when(c < n_chunks - 1)
    def _():
        _online_update(s, vbuf[slot].astype(BF16), m_ref, l_ref, acc_ref, v_transposed=True)

    @pl.when(c == n_chunks - 1)
    def _():
        near = jnp.concatenate([jnp.zeros((n_q, S_CHUNK - LANES), F32), css_ref[:n_q, :]], axis=1)
        _online_update(s + near, vbuf[slot].astype(BF16), m_ref, l_ref, acc_ref, v_transposed=True)
        sn = _nt(qq, knew_ref[0].astype(BF16)) + cns_ref[:n_q, :]
        _online_update(sn, vnew_ref[0].astype(BF16), m_ref, l_ref, acc_ref)
        o_s = acc_ref[...] / l_ref[...]
        o_ref[0] = gm_ref[0, 0] * oc_ref[...] + gm_ref[0, 1] * o_s + gm_ref[0, 2] * ow_ref[...]


def _nsa_sample(page_table, qx, gm, kc, vct, kpool, vpool, knew, vnew, kwin, vwin, kwnew, vwnew,
                mt, gsum, ccs, css, cns, cws):
    nb = qx.shape[0]
    n_pages = page_table.shape[1]
    n_chunks = n_pages // S_CHUNK_PAGES
    past = n_pages * PAGE
    n_blocks = past // SEL_BLOCK
    n_q = gm.shape[2]

    def bspec(shape):
        nd = len(shape)
        return pl.BlockSpec((1,) + tuple(shape[1:]), lambda b, c, pt: (b,) + (0,) * (nd - 1))

    def cspec(shape):
        nd = len(shape)
        return pl.BlockSpec(tuple(shape), lambda b, c, pt: (0,) * nd, pipeline_mode=pl.Buffered(1))

    grid_spec = pltpu.PrefetchScalarGridSpec(
        num_scalar_prefetch=1,
        grid=(nb, n_chunks),
        in_specs=[bspec(qx.shape), bspec(gm.shape), bspec(kc.shape), bspec(vct.shape),
                  pl.BlockSpec(memory_space=pl.ANY), pl.BlockSpec(memory_space=pl.ANY),
                  bspec(knew.shape), bspec(vnew.shape), bspec(kwin.shape), bspec(vwin.shape),
                  bspec(kwnew.shape), bspec(vwnew.shape),
                  cspec(mt.shape), cspec(gsum.shape), cspec(ccs.shape), cspec(css.shape),
                  cspec(cns.shape), cspec(cws.shape)],
        out_specs=pl.BlockSpec((1, n_q, KV_DIM), lambda b, c, pt: (b, 0, 0)),
        scratch_shapes=[pltpu.VMEM((2, KV_DIM, S_CHUNK), F32),
                        pltpu.VMEM((2, KV_DIM, S_CHUNK), F32),
                        pltpu.SemaphoreType.DMA((2, 2)),
                        pltpu.VMEM((n_q, past), F32),
                        pltpu.VMEM((n_q, 1), F32), pltpu.VMEM((n_q, 1), F32),
                        pltpu.VMEM((n_q, KV_DIM), F32),
                        pltpu.VMEM((n_q, KV_DIM), F32), pltpu.VMEM((n_q, KV_DIM), F32)],
    )
    return pl.pallas_call(
        functools.partial(_nsa_sample_kernel, n_chunks=n_chunks, n_blocks=n_blocks),
        grid_spec=grid_spec,
        out_shape=jax.ShapeDtypeStruct((nb, n_q, KV_DIM), F32),
        compiler_params=_cparams(("arbitrary", "arbitrary")),
        name="nsa_sample",
    )(page_table, qx, gm, kc, vct, kpool, vpool, knew, vnew, kwin, vwin, kwnew, vwnew,
      mt, gsum, ccs, css, cns, cws)


def _ffn_kernel(x_ref, oa_ref, ob_ref, sa_ref, sb_ref, wpa_ref, wpb_ref, wo_ref, nf_ref,
                wg_ref, wu_ref, wd_ref, nl_ref, y_ref):
    pa = jnp.dot(oa_ref[...], wpa_ref[...], preferred_element_type=F32)
    pb = jnp.dot(ob_ref[...], wpb_ref[...], preferred_element_type=F32)
    merged = sa_ref[...] * pa + sb_ref[...] * pb
    x = x_ref[...] + jnp.dot(merged.astype(BF16), wo_ref[...], preferred_element_type=F32)
    hn = (x * lax.rsqrt(jnp.mean(x * x, axis=-1, keepdims=True) + EPS) * nf_ref[...]).astype(BF16)
    gate = jnp.dot(hn, wg_ref[...], preferred_element_type=F32)
    up = jnp.dot(hn, wu_ref[...], preferred_element_type=F32)
    ff = (jax.nn.silu(gate) * up).astype(BF16)
    x = x + jnp.dot(ff, wd_ref[...], preferred_element_type=F32)
    y_ref[...] = x * lax.rsqrt(jnp.mean(x * x, axis=-1, keepdims=True) + EPS) * nl_ref[...]


def _ffn(x2d, oa, ob, sa, sb, wpa, wpb, wo, nf, wg, wu, wd, nl, tm):
    rows = x2d.shape[0]

    def rspec(n):
        return pl.BlockSpec((tm, n), lambda i: (i, 0))

    return pl.pallas_call(
        _ffn_kernel,
        grid=(rows // tm,),
        in_specs=[rspec(D_MODEL), rspec(Q_DIM), rspec(HG_W), rspec(D_MODEL), rspec(D_MODEL),
                  _const_spec(wpa.shape), _const_spec(wpb.shape), _const_spec(wo.shape),
                  _const_spec(nf.shape), _const_spec(wg.shape), _const_spec(wu.shape),
                  _const_spec(wd.shape), _const_spec(nl.shape)],
        out_specs=rspec(D_MODEL),
        out_shape=jax.ShapeDtypeStruct((rows, D_MODEL), F32),
        compiler_params=_cparams(("arbitrary",)),
        name="ffn",
    )(x2d, oa, ob, sa, sb, wpa, wpb, wo, nf, wg, wu, wd, nl)


def _pack_w_in(w_in):
    sizes = (Q_DIM,) + (KV_DIM,) * 6 + (3 * N_HEADS,) + (HG_W,) * 4 + (D_MODEL,) * 2
    offs = np.concatenate([[0], np.cumsum(sizes)])
    q = w_in[:, offs[0]:offs[1]].reshape(D_MODEL, N_HEADS, HEAD_DIM)[:, _HEAD_PERM, :].reshape(D_MODEL, Q_DIM)
    g = jnp.pad(w_in[:, offs[7]:offs[8]], ((0, 0), (0, LANES - 3 * N_HEADS)))
    return jnp.concatenate([q, w_in[:, offs[1]:offs[7]], g, w_in[:, offs[8]:]], axis=1).astype(BF16)


def _strip(bvc, rel, lo=0, hi=None, masked=NEG):
    val = bvc[:, np.clip(rel, 0, 255)]
    ok = rel >= lo
    if hi is not None:
        ok = ok & (rel < hi)
    return jnp.where(jnp.asarray(ok)[None], val, masked)


def _toeplitz(bvc, a, n_rows, n_cols, lo=0, hi=None):
    n = n_rows + n_cols - 1
    u = _strip(bvc, a - (n_rows - 1) + np.arange(n), lo, hi)
    u = jnp.pad(u, ((0, 0), (0, 1)))
    circ = jnp.tile(u, (1, n_rows))[:, :n_rows * n].reshape(N_HEADS, n_rows, n)
    return circ[:, :, n_rows - 1:n_rows - 1 + n_cols]


def _bias_strips_prompt(bvc):
    gq = GROUP * Q_BLOCK

    def lanes(x):
        return x.reshape(N_KV, GROUP, x.shape[1], Q_BLOCK).transpose(0, 2, 1, 3).reshape(N_KV, x.shape[1], gq)

    cs = jnp.stack([lanes(_toeplitz(bvc, Q_BLOCK * d, Q_BLOCK, Q_BLOCK)) for d in (-1, 0, 1, 2)], axis=1)
    bw = lanes(_toeplitz(bvc, WINDOW, WINDOW + Q_BLOCK, Q_BLOCK, 0, WINDOW))
    rr = np.arange(CMP_BAND)[:, None]
    rel_c = np.arange(Q_BLOCK)[None, :] - CMP_STRIDE * (rr - CMP_PAD) - (CMP_BLOCK - 1)
    cc = _strip(bvc, rel_c, masked=0.0).reshape(N_KV, GROUP, CMP_BAND, Q_BLOCK)
    cc = cc.transpose(0, 2, 1, 3).reshape(N_KV, CMP_BAND, gq)
    return cc, cs, bw


def _bias_strips_sample(bvc, past, t):
    def rows(a):
        a = a.reshape(N_HEADS * t, a.shape[-1])
        return jnp.pad(a, ((0, S_ROWS - N_HEADS * t), (0, 0)))
    tt = np.arange(t)[:, None]
    nrow = past // CMP_STRIDE
    n = (nrow - CMP_BAND + np.arange(CMP_BAND))[None, :]
    ccs = rows(_strip(bvc, past + tt - CMP_STRIDE * n - (CMP_BLOCK - 1), masked=0.0)).T
    i = np.arange(LANES)[None, :]
    css = rows(_strip(bvc, LANES + tt - i))
    cns = rows(_strip(bvc, np.where(i < t, tt - i, -1)))
    iw = np.arange(WINDOW)[None, :]
    cws = rows(_strip(bvc, WINDOW + tt - iw, 0, WINDOW))
    return ccs, css, cns, cws


def kernel(x_prompt, x_sample, cache_k_cmp, cache_v_cmp, cache_k_slc, cache_v_slc, state_k_win, state_v_win,
           state_hgrn, page_table, norm_mix, w_in, cmp_pe_k, cmp_w1_k, cmp_w2_k, cmp_pe_v, cmp_w1_v, cmp_w2_v,
           rel_bias, hg_lb_logits, hg_norm, w_proj_a, w_proj_b, w_out, norm_ffn, w_gate, w_up, w_down, norm_final):
    nbp, t_p, _ = x_prompt.shape
    nbs, t_s, _ = x_sample.shape
    assert nbp == 1 and norm_mix.shape[0] == 1
    n_pages = page_table.shape[1]
    past = n_pages * PAGE
    assert state_k_win.shape[2] == WINDOW and past % S_CHUNK == 0 and t_s <= SUBLANES

    lb = jnp.cumsum(jax.nn.softmax(hg_lb_logits.astype(F32), axis=0), axis=0)[0]
    lb3 = jnp.pad(jnp.stack([jnp.log(lb), jnp.log1p(-lb), 1.0 - lb]), ((0, SUBLANES - 3), (0, 0)))
    w_pack = _pack_w_in(w_in[0])
    g_mix = norm_mix[0][None, :]
    wpa = w_proj_a[0].reshape(N_HEADS, HEAD_DIM, D_MODEL)[_HEAD_PERM].reshape(Q_DIM, D_MODEL).astype(BF16)
    wpb = w_proj_b[0].astype(BF16)
    wo = w_out[0].astype(BF16)
    wg, wu, wd = w_gate[0].astype(BF16), w_up[0].astype(BF16), w_down[0].astype(BF16)
    nf, nl = norm_ffn[0][None, :], norm_final[None, :]
    gn = hg_norm[0][None, :]
    pe_k, w1_k, w2_k = _compress_weights(cmp_pe_k[0], cmp_w1_k[0], cmp_w2_k[0])
    pe_v, w1_v, w2_v = _compress_weights(cmp_pe_v[0], cmp_w1_v[0], cmp_w2_v[0])
    bvc = (rel_bias[_BUCKET] - rel_bias[N_BUCKETS - 1][None, :]).T * LOG2E
    cc, cs, bw = _bias_strips_prompt(bvc)
    ccs, css, cns, cws = _bias_strips_sample(bvc, past, t_s)

    xp2 = x_prompt.reshape(t_p, D_MODEL)
    xs2 = x_sample.reshape(nbs * t_s, D_MODEL)
    seg = lambda off, n: w_pack[:, off:off + n]
    w_t = jnp.concatenate([seg(_OFF_Q, Q_DIM), seg(_OFF_G, LANES), seg(_OFF_KV + 3 * KV_DIM, KV_DIM),
                           seg(_OFF_KV + 5 * KV_DIM, KV_DIM)], axis=1).T
    pp = _proj(xp2, g_mix, w_pack, w_t, lb3, 256)
    ps = _proj(xs2, g_mix, w_pack, w_t, lb3, nbs * t_s)
    (_, kc_p, vc_p, ks_p, vs_p, kw_p, vw_p, ksb_p, kwb_p, _,
     qh_p, lf_p, kh_p, vh_p, gs_p, sa_p, sb_p, qat_p, gat_p, vst_p, vwt_p) = pp
    (qa_s, kc_s, vc_s, ks_s, vs_s, kw_s, vw_s, _, _, ga_s,
     qh_s, lf_s, kh_s, vh_s, gs_s, sa_s, sb_s, _, _, _, _) = ps

    ob_p, s_p = _hgrn_prompt(qh_p, kh_p, vh_p, lf_p, gs_p, gn)
    r3 = lambda a: a.reshape(nbs, t_s, a.shape[-1])
    ob_s, s_s = _hgrn_sample(r3(qh_s), r3(kh_s), r3(vh_s), r3(lf_s), r3(gs_s), gn, state_hgrn[0])

    ident = jnp.arange(t_p // PAGE, dtype=jnp.int32)[None, :]
    pool_rows = lambda a: a.reshape(-1, PAGE, KV_DIM)
    kcb_p, _ = _compress(pool_rows(kc_p), ident, pe_k, w1_k, w2_k, False)
    _, vct_p = _compress(pool_rows(vc_p), ident, pe_v, w1_v, w2_v, False)
    kcb_p = jnp.pad(kcb_p[0], ((CMP_PAD, 0), (0, 0)))
    vct_p = jnp.pad(vct_p[0], ((0, 0), (CMP_PAD, 0)))
    nrow_p = kcb_p.shape[0]
    mt_p = jnp.asarray(_score_matrix(nrow_p, CMP_PAD, t_p // SEL_BLOCK), BF16)
    kwb_pad = jnp.pad(kwb_p, ((WINDOW, 0), (0, 0)))
    vst3 = vst_p.reshape(KV_DIM, t_p // SEL_TILE, SEL_TILE).transpose(1, 0, 2)
    ones_rows = jnp.ones((t_p // SEL_TILE, 2 * SUBLANES, SEL_TILE), BF16)
    vst3 = jnp.stack([jnp.concatenate([vst3[:, kv * HEAD_DIM:(kv + 1) * HEAD_DIM, :], ones_rows], axis=1)
                      for kv in range(N_KV)])
    vwt3 = jnp.pad(vwt_p, ((0, 0), (WINDOW, 0))).reshape(KV_DIM, (t_p + WINDOW) // Q_BLOCK, Q_BLOCK)
    vwt3 = vwt3.transpose(1, 0, 2)
    eb = (np.arange(LANES)[None, :] == np.arange(SEL_TILE)[:, None] // SEL_BLOCK).astype(np.float32)
    oa_p = _nsa_prompt(qat_p, gat_p, kcb_p, vct_p, ksb_p, vst3, kwb_pad, vwt3, mt_p, cc, cs, bw,
                       jnp.asarray(eb, BF16))

    pool_t = lambda a: a.transpose(0, 2, 3, 1).reshape(-1, KV_DIM, PAGE)
    kcb_s, _ = _compress(pool_t(cache_k_cmp[0]), page_table, pe_k, w1_k, w2_k, True)
    _, vct_s = _compress(pool_t(cache_v_cmp[0]), page_table, pe_v, w1_v, w2_v, True)
    nq = N_HEADS * t_s
    qs4 = qa_s.reshape(nbs, t_s, GROUP, N_KV, HEAD_DIM).astype(F32)
    qx = jnp.einsum('btjkd,kq->bkjtqd', qs4, jnp.eye(N_KV, dtype=F32)).reshape(nbs, nq, KV_DIM)
    qx = jnp.pad(qx, ((0, 0), (0, S_ROWS - nq), (0, 0))).astype(BF16)
    g4 = ga_s[:, :3 * N_HEADS].reshape(nbs, t_s, N_KV, GROUP, 3)
    gm = jnp.transpose(g4, (0, 4, 2, 3, 1)).reshape(nbs, 3, nq, 1)
    gm = jnp.broadcast_to(gm, (nbs, 3, nq, KV_DIM))
    new_tile = lambda a: jnp.pad(a.reshape(nbs, t_s, KV_DIM), ((0, 0), (0, LANES - t_s), (0, 0)))
    mt_s = jnp.asarray(_score_matrix(past // CMP_STRIDE, 0, past // SEL_BLOCK), BF16)
    gsum = np.zeros((S_ROWS, S_ROWS), np.float32)
    for kv in range(N_KV):
        for j in range(GROUP):
            for t in range(t_s):
                gsum[(kv * GROUP + j) * t_s + t, kv * t_s + t] = 1.0
    o_kv = _nsa_sample(page_table, qx, gm, kcb_s, vct_s,
                       pool_t(cache_k_slc[0]), pool_t(cache_v_slc[0]),
                       new_tile(ks_s), new_tile(vs_s), state_k_win[0].reshape(nbs, WINDOW, KV_DIM),
                       state_v_win[0].reshape(nbs, WINDOW, KV_DIM), new_tile(kw_s), new_tile(vw_s),
                       mt_s, jnp.asarray(gsum), ccs, css, cns, cws)
    o5 = o_kv.reshape(nbs, N_KV, GROUP, t_s, N_KV, HEAD_DIM)
    oa_s = jnp.einsum('bkjtqd,kq->btjkd', o5, jnp.eye(N_KV, dtype=F32)).reshape(nbs * t_s, Q_DIM).astype(BF16)

    y_p = _ffn(xp2, oa_p, ob_p, sa_p, sb_p, wpa, wpb, wo, nf, wg, wu, wd, nl, 256)
    y_s = _ffn(xs2, oa_s, ob_s.reshape(nbs * t_s, HG_W), sa_s, sb_s, wpa, wpb, wo, nf, wg, wu, wd, nl, nbs * t_s)

    kv5 = lambda a, nb_, tt: a.reshape(1, nb_, tt, N_KV, HEAD_DIM)
    wl = min(WINDOW, t_p)
    win = lambda st, new: jnp.concatenate(
        [st[0], new.reshape(nbs, t_s, N_KV, HEAD_DIM)], axis=1)[:, -WINDOW:][None]
    return (y_p.reshape(1, t_p, D_MODEL), y_s.reshape(nbs, t_s, D_MODEL),
            kv5(kc_p, 1, t_p), kv5(vc_p, 1, t_p), kv5(ks_p, 1, t_p), kv5(vs_p, 1, t_p),
            kv5(kw_p[-wl:], 1, wl), kv5(vw_p[-wl:], 1, wl), s_p[None, None],
            kv5(kc_s, nbs, t_s), kv5(vc_s, nbs, t_s), kv5(ks_s, nbs, t_s), kv5(vs_s, nbs, t_s),
            win(state_k_win, kw_s), win(state_v_win, vw_s), s_s[None])
```

```python
import functools
import math

import numpy as np
import jax
import jax.numpy as jnp
from jax import lax
from jax.experimental import pallas as pl
from jax.experimental.pallas import tpu as pltpu

F32 = jnp.float32
BF16 = jnp.bfloat16
HIGHEST = lax.Precision.HIGHEST

D_MODEL = 1024
N_HEADS = 8
N_KV = 2
GROUP = N_HEADS // N_KV
HEAD_DIM = 64
KV_DIM = N_KV * HEAD_DIM
Q_DIM = N_HEADS * HEAD_DIM
CMP_BLOCK = 32
CMP_STRIDE = 16
CMP_HIDDEN = 2 * HEAD_DIM
SEL_BLOCK = 64
N_SEL = 16
WINDOW = 512
Q_BLOCK = 128
FORCE_BONUS = 1e4
N_BUCKETS = 32
MAX_DISTANCE = 128
HG_HEADS = 4
HG_DIM = 128
HG_CHUNK = 64
HG_SUB = 16
HG_W = HG_HEADS * HG_DIM
D_FF = ((8 * D_MODEL // 3 + 255) // 256) * 256
EPS = 1e-6
PAGE = 128
NEG = -1e30
LOG2E = math.log2(math.e)
Q_SCALE = HEAD_DIM ** -0.5 * LOG2E

LANES = 128
SUBLANES = 8
VMEM_LIMIT = 56 * 1024 * 1024

_OFF_Q = 0
_OFF_KV = _OFF_Q + Q_DIM
_OFF_G = _OFF_KV + 6 * KV_DIM
_OFF_HG = _OFF_G + LANES
_OFF_GATE = _OFF_HG + 4 * HG_W
_PROJ_N = _OFF_GATE + 2 * D_MODEL

_HEAD_PERM = np.array([h for j in range(GROUP) for h in (j, GROUP + j)])

SEL_TILE = 512
BLK_PER_TILE = SEL_TILE // SEL_BLOCK
CMP_PAD = 16
CMP_BAND = 24


def _cparams(sem, vmem=VMEM_LIMIT):
    return pltpu.CompilerParams(dimension_semantics=sem, vmem_limit_bytes=vmem)


def _const_spec(shape):
    nd = len(shape)
    return pl.BlockSpec(shape, lambda *_: (0,) * nd, pipeline_mode=pl.Buffered(1))


def _bucket_table():
    n = np.arange(256)
    max_exact = N_BUCKETS // 2
    nf = np.maximum(n, 1).astype(np.float64)
    large = max_exact + (np.log(nf / max_exact) / math.log(MAX_DISTANCE / max_exact)
                         * (N_BUCKETS - max_exact)).astype(np.int64)
    large = np.minimum(large, N_BUCKETS - 1)
    return np.where(n < max_exact, n, large)


_BUCKET = _bucket_table()


def _proj_kernel(x_ref, g_ref, w_ref, wt_ref, lb_ref,
                 qa_ref, kc_ref, vc_ref, ks_ref, vs_ref, kw_ref, vw_ref,
                 ksb_ref, kwb_ref, ga_ref,
                 qh_ref, lf_ref, kh_ref, vh_ref, gs_ref, sa_ref, sb_ref,
                 qat_ref, gat_ref, vst_ref, vwt_ref):
    x = x_ref[...]
    xn = x * lax.rsqrt(jnp.mean(x * x, axis=-1, keepdims=True) + EPS) * g_ref[...]
    xb = xn.astype(BF16)

    def seg(a, n):
        return jnp.dot(xb, w_ref[:, a:a + n], preferred_element_type=F32)

    qa_ref[...] = (seg(_OFF_Q, Q_DIM) * Q_SCALE).astype(BF16)
    f32_refs = (kc_ref, vc_ref, ks_ref, vs_ref, kw_ref, vw_ref)
    b16_refs = (None, None, ksb_ref, None, kwb_ref, None)
    for i in range(6):
        u = seg(_OFF_KV + i * KV_DIM, KV_DIM)
        f32_refs[i][...] = u
        if b16_refs[i] is not None:
            b16_refs[i][...] = u.astype(BF16)
    ga_ref[...] = jax.nn.sigmoid(seg(_OFF_G, LANES))

    def seg_t(a, n):
        return lax.dot_general(wt_ref[a:a + n, :], xb, (((1,), (1,)), ((), ())), preferred_element_type=F32)

    qat_ref[...] = (seg_t(0, Q_DIM) * Q_SCALE).astype(BF16)
    gat_ref[...] = jax.nn.sigmoid(seg_t(Q_DIM, LANES))
    vst_ref[...] = seg_t(Q_DIM + LANES, KV_DIM).astype(BF16)
    vwt_ref[...] = seg_t(Q_DIM + LANES + KV_DIM, KV_DIM).astype(BF16)

    log_lb = lb_ref[0:1, :]
    log_1m = lb_ref[1:2, :]
    one_m = lb_ref[2:3, :]
    qh_ref[...] = jax.nn.silu(seg(_OFF_HG, HG_W))
    z = seg(_OFF_HG + HG_W, HG_W)
    b = log_1m + (jnp.minimum(z, 0.0) - jnp.log1p(jnp.exp(-jnp.abs(z))))
    hi = jnp.maximum(log_lb, b)
    lf_ref[...] = hi + jnp.log1p(jnp.exp(-jnp.abs(log_lb - b)))
    kh_ref[...] = one_m * jax.nn.sigmoid(-z)
    vh_ref[...] = seg(_OFF_HG + 2 * HG_W, HG_W)
    gs_ref[...] = jax.nn.silu(seg(_OFF_HG + 3 * HG_W, HG_W))
    sa_ref[...] = jax.nn.sigmoid(seg(_OFF_GATE, D_MODEL))
    sb_ref[...] = jax.nn.sigmoid(seg(_OFF_GATE + D_MODEL, D_MODEL))


_PROJ_T = Q_DIM + LANES + 2 * KV_DIM


def _proj(x2d, g, w, wt, lb3, tm):
    rows = x2d.shape[0]
    widths = ([(Q_DIM, BF16)] + [(KV_DIM, F32)] * 6 + [(KV_DIM, BF16)] * 2 + [(LANES, F32)]
              + [(HG_W, F32)] * 5 + [(D_MODEL, F32)] * 2)
    heights = [(Q_DIM, BF16), (LANES, F32), (KV_DIM, BF16), (KV_DIM, BF16)]
    return pl.pallas_call(
        _proj_kernel,
        grid=(rows // tm,),
        in_specs=[pl.BlockSpec((tm, D_MODEL), lambda i: (i, 0)),
                  _const_spec((1, D_MODEL)),
                  _const_spec((D_MODEL, _PROJ_N)),
                  _const_spec((_PROJ_T, D_MODEL)),
                  _const_spec((SUBLANES, HG_W))],
        out_specs=([pl.BlockSpec((tm, n), lambda i: (i, 0)) for n, _ in widths]
                   + [pl.BlockSpec((n, tm), lambda i: (0, i)) for n, _ in heights]),
        out_shape=([jax.ShapeDtypeStruct((rows, n), dt) for n, dt in widths]
                   + [jax.ShapeDtypeStruct((n, rows), dt) for n, dt in heights]),
        compiler_params=_cparams(("arbitrary",)),
        name="proj",
    )(x2d, g, w, wt, lb3)


_CH_W = CMP_STRIDE * KV_DIM
_CH_PER_PAGE = PAGE // CMP_STRIDE
_CH_PITCH = CMP_STRIDE + SUBLANES


def _compress_kernel(pt_ref, pool_ref, pe_ref, w1_ref, w2_ref, out_ref, outt_ref, buf, rbuf, xa, hbuf, sem,
                     *, n_pages, pages_transposed):
    b = pl.program_id(0)
    nb = pl.num_programs(0)
    slot = b % 2
    n_ch = n_pages * _CH_PER_PAGE

    def page_copy(bb, p, s):
        return pltpu.make_async_copy(pool_ref.at[pt_ref[bb, p]], buf.at[s, p], sem.at[s])

    def start_all(bb, s):
        def body(p, c):
            page_copy(bb, p, s).start()
            return c
        lax.fori_loop(0, n_pages, body, 0)

    @pl.when(b == 0)
    def _():
        start_all(b, slot)

    @pl.when(b + 1 < nb)
    def _():
        start_all(b + 1, 1 - slot)

    def wait_body(p, c):
        page_copy(b, p, slot).wait()
        return c
    lax.fori_loop(0, n_pages, wait_body, 0)

    def to_rows(p, c):
        page = buf[slot, p]
        page = page.T if pages_transposed else page
        for i in range(_CH_PER_PAGE):
            dst = pl.multiple_of((p * _CH_PER_PAGE + i) * _CH_PITCH, SUBLANES)
            rbuf[pl.ds(dst, CMP_STRIDE), :] = page[i * CMP_STRIDE:(i + 1) * CMP_STRIDE, :]
        return c
    lax.fori_loop(0, n_pages, to_rows, 0, unroll=8)

    rows = math.gcd(n_ch, 256)
    for r in range(n_ch // rows):
        for s in range(CMP_STRIDE):
            x = rbuf[pl.ds(r * rows * _CH_PITCH + s, rows, stride=_CH_PITCH), :]
            xa[:, s * KV_DIM:(s + 1) * KV_DIM] = x.astype(BF16)
        hbuf[r * rows:(r + 1) * rows, :] = jnp.dot(xa[...], w1_ref[...], preferred_element_type=F32)
    pw = _split_dot_rhs(pe_ref[...], w1_ref[...])
    nh = N_KV * CMP_HIDDEN
    bias = pw[0:1, 0:nh] + pw[1:2, nh:2 * nh]
    h = hbuf[:, 0:nh] + pltpu.roll(hbuf[:, nh:2 * nh], n_ch - 1, 0) + bias
    blocks = jnp.dot(jax.nn.gelu(h).astype(BF16), w2_ref[...], preferred_element_type=F32)
    row = lax.broadcasted_iota(jnp.int32, blocks.shape, 0)
    blocks = jnp.where(row < n_ch - 1, blocks, 0.0)
    out_ref[0] = blocks.astype(BF16)
    outt_ref[0] = blocks.T.astype(BF16)


def _compress(pool, page_table, pe, w1, w2, pages_transposed):
    nbatch, n_pages = page_table.shape
    n_ch = n_pages * _CH_PER_PAGE
    rows = math.gcd(n_ch, 256)
    grid_spec = pltpu.PrefetchScalarGridSpec(
        num_scalar_prefetch=1,
        grid=(nbatch,),
        in_specs=[pl.BlockSpec(memory_space=pl.ANY),
                  _const_spec((SUBLANES, _CH_W)),
                  _const_spec((_CH_W, 4 * CMP_HIDDEN)),
                  _const_spec((2 * CMP_HIDDEN, KV_DIM))],
        out_specs=[pl.BlockSpec((1, n_ch, KV_DIM), lambda b, pt: (b, 0, 0)),
                   pl.BlockSpec((1, KV_DIM, n_ch), lambda b, pt: (b, 0, 0))],
        scratch_shapes=[pltpu.VMEM((2, n_pages, PAGE, KV_DIM), F32),
                        pltpu.VMEM((n_ch * _CH_PITCH, KV_DIM), F32),
                        pltpu.VMEM((rows, _CH_W), BF16),
                        pltpu.VMEM((n_ch, 4 * CMP_HIDDEN), F32),
                        pltpu.SemaphoreType.DMA((2,))],
    )
    return pl.pallas_call(
        functools.partial(_compress_kernel, n_pages=n_pages, pages_transposed=pages_transposed),
        grid_spec=grid_spec,
        out_shape=[jax.ShapeDtypeStruct((nbatch, n_ch, KV_DIM), BF16),
                   jax.ShapeDtypeStruct((nbatch, KV_DIM, n_ch), BF16)],
        compiler_params=_cparams(("arbitrary",)),
        name="compress",
    )(page_table, pool, pe, w1, w2)


def _compress_weights(pe, w1, w2):
    c = CMP_BLOCK // CMP_STRIDE
    pe_r = pe.reshape(c, CMP_STRIDE, 1, HEAD_DIM)
    pe_x = jnp.broadcast_to(pe_r, (c, CMP_STRIDE, N_KV, HEAD_DIM)).reshape(c, _CH_W)
    pe_x = jnp.pad(pe_x, ((0, SUBLANES - c), (0, 0)))
    w1_r = w1.reshape(c, CMP_STRIDE, HEAD_DIM, CMP_HIDDEN)
    eye = jnp.eye(N_KV, dtype=w1.dtype)
    w1_x = jnp.einsum('jsde,kq->skdjqe', w1_r, eye).reshape(_CH_W, c * N_KV * CMP_HIDDEN)
    w2_x = jnp.einsum('ed,kq->keqd', w2, eye).reshape(N_KV * CMP_HIDDEN, KV_DIM)
    return pe_x, w1_x.astype(BF16), w2_x.astype(BF16)


def _hgrn_chunk(q, k, v, lf, st_ref, chunk, sub):
    if chunk > SUBLANES:
        r = lax.broadcasted_iota(jnp.int32, (chunk, chunk), 0)
        c = lax.broadcasted_iota(jnp.int32, (chunk, chunk), 1)
        tri = (r >= c).astype(F32)
        b = jnp.dot(tri, lf, preferred_element_type=F32, precision=HIGHEST)
    else:
        rows = [lf[0:1, :]]
        for t in range(1, chunk):
            rows.append(rows[-1] + lf[t:t + 1, :])
        b = jnp.concatenate(rows, axis=0)
    bl = b[chunk - 1:chunk, :]
    qe = q * jnp.exp(b)
    kd = k * jnp.exp(bl - b)
    ebl = jnp.exp(bl)
    n_sub = chunk // sub
    trow = lax.broadcasted_iota(jnp.int32, (sub, HG_W), 0)
    crow = lax.broadcasted_iota(jnp.int32, (chunk, HG_W), 0)

    diag = []
    for i in range(n_sub):
        qi = q[i * sub:(i + 1) * sub, :]
        bi = b[i * sub:(i + 1) * sub, :]
        acc = [jnp.zeros((sub, HG_DIM), F32) for _ in range(HG_HEADS)]
        for s in range(sub):
            row = i * sub + s
            dec = jnp.exp(jnp.where(trow >= s, bi - b[row:row + 1, :], -jnp.inf))
            prod = qi * k[row:row + 1, :] * dec
            for h in range(HG_HEADS):
                a = jnp.sum(prod[:, h * HG_DIM:(h + 1) * HG_DIM], axis=1, keepdims=True)
                acc[h] = acc[h] + a * v[row:row + 1, h * HG_DIM:(h + 1) * HG_DIM]
        diag.append(acc)

    off = []
    for i in range(n_sub):
        if i == 0:
            off.append(None)
            continue
        b0 = b[i * sub - 1:i * sub, :]
        qs = (q[i * sub:(i + 1) * sub, :] * jnp.exp(b[i * sub:(i + 1) * sub, :] - b0)).astype(BF16)
        ks = (k * jnp.exp(jnp.where(crow < i * sub, b0 - b, -jnp.inf))).astype(BF16)
        off.append((qs, ks))

    vb = v.astype(BF16)
    outs = []
    for h in range(HG_HEADS):
        sl = slice(h * HG_DIM, (h + 1) * HG_DIM)
        st = st_ref[h]
        o_h = lax.dot_general(qe[:, sl].astype(BF16), st.astype(BF16), (((1,), (1,)), ((), ())),
                              preferred_element_type=F32)
        parts = []
        for i in range(n_sub):
            d = diag[i][h]
            if off[i] is not None:
                qs, ks = off[i]
                a = lax.dot_general(qs[:, sl], ks[:, sl], (((1,), (1,)), ((), ())),
                                    preferred_element_type=F32)
                d = d + jnp.dot(a.astype(BF16), vb[:, sl], preferred_element_type=F32)
            parts.append(d)
        intra = parts[0] if n_sub == 1 else jnp.concatenate(parts, axis=0)
        outs.append(o_h + intra)
        st_ref[h] = st * ebl[:, sl] + lax.dot_general(
            vb[:, sl], kd[:, sl].astype(BF16), (((0,), (0,)), ((), ())), preferred_element_type=F32)
    return jnp.concatenate(outs, axis=1)


def _hgrn_finish(o, gs, gn):
    outs = []
    for h in range(HG_HEADS):
        oh = o[:, h * HG_DIM:(h + 1) * HG_DIM]
        y = oh * lax.rsqrt(jnp.mean(oh * oh, axis=-1, keepdims=True) + EPS) * gn
        outs.append(y)
    return (jnp.concatenate(outs, axis=1) * gs).astype(BF16)


def _hgrn_prompt_kernel(q_ref, k_ref, v_ref, lf_ref, gs_ref, gn_ref, o_ref, s_ref, st_ref, *, n_chunks):
    i = pl.program_id(0)

    @pl.when(i == 0)
    def _():
        st_ref[...] = jnp.zeros_like(st_ref)

    def body(c, carry):
        r = pl.ds(pl.multiple_of(c * HG_CHUNK, HG_CHUNK), HG_CHUNK)
        o = _hgrn_chunk(q_ref[r, :], k_ref[r, :], v_ref[r, :], lf_ref[r, :], st_ref, HG_CHUNK, HG_SUB)
        o_ref[r, :] = _hgrn_finish(o, gs_ref[r, :], gn_ref[...])
        return carry
    lax.fori_loop(0, n_chunks, body, 0, unroll=4)

    @pl.when(i == pl.num_programs(0) - 1)
    def _():
        for h in range(HG_HEADS):
            s_ref[h] = st_ref[h].T


def _hgrn_prompt(qh, kh, vh, lf, gs, gn, rows_per_step=512):
    t = qh.shape[0]
    spec = pl.BlockSpec((rows_per_step, HG_W), lambda i: (i, 0))
    return pl.pallas_call(
        functools.partial(_hgrn_prompt_kernel, n_chunks=rows_per_step // HG_CHUNK),
        grid=(t // rows_per_step,),
        in_specs=[spec] * 5 + [_const_spec((1, HG_DIM))],
        out_specs=[spec, pl.BlockSpec((HG_HEADS, HG_DIM, HG_DIM), lambda i: (0, 0, 0))],
        out_shape=[jax.ShapeDtypeStruct((t, HG_W), BF16),
                   jax.ShapeDtypeStruct((HG_HEADS, HG_DIM, HG_DIM), F32)],
        scratch_shapes=[pltpu.VMEM((HG_HEADS, HG_DIM, HG_DIM), F32)],
        compiler_params=_cparams(("arbitrary",)),
        name="hgrn_prompt",
    )(qh, kh, vh, lf, gs, gn)


def _hgrn_sample_kernel(q_ref, k_ref, v_ref, lf_ref, gs_ref, gn_ref, s0_ref, o_ref, s_ref, st_ref, *, t):
    for h in range(HG_HEADS):
        st_ref[h] = s0_ref[0, h].T
    o = _hgrn_chunk(q_ref[0], k_ref[0], v_ref[0], lf_ref[0], st_ref, t, t)
    o_ref[0] = _hgrn_finish(o, gs_ref[0], gn_ref[...])
    for h in range(HG_HEADS):
        s_ref[0, h] = st_ref[h].T


def _hgrn_sample(qh, kh, vh, lf, gs, gn, s0):
    nb, t, _ = qh.shape
    spec = pl.BlockSpec((1, t, HG_W), lambda b: (b, 0, 0))
    sspec = pl.BlockSpec((1, HG_HEADS, HG_DIM, HG_DIM), lambda b: (b, 0, 0, 0))
    return pl.pallas_call(
        functools.partial(_hgrn_sample_kernel, t=t),
        grid=(nb,),
        in_specs=[spec] * 5 + [_const_spec((1, HG_DIM)), sspec],
        out_specs=[spec, sspec],
        out_shape=[jax.ShapeDtypeStruct((nb, t, HG_W), BF16),
                   jax.ShapeDtypeStruct((nb, HG_HEADS, HG_DIM, HG_DIM), F32)],
        scratch_shapes=[pltpu.VMEM((HG_HEADS, HG_DIM, HG_DIM), F32)],
        compiler_params=_cparams(("arbitrary",)),
        name="hgrn_sample",
    )(qh, kh, vh, lf, gs, gn, s0)


def _select_topk(x, blk, n, always=None):
    nblk = x.shape[0]
    if always is None:
        sel = jnp.zeros_like(x)
    else:
        sel = jnp.where(always, 1.0, 0.0)
        x = jnp.where(always, -3e38, x)
    for _ in range(n):
        m = jnp.max(x, axis=0, keepdims=True)
        idx = jnp.min(jnp.where(x == m, blk, float(nblk)), axis=0, keepdims=True)
        pick = blk == idx
        sel = jnp.where(pick, 1.0, sel)
        x = jnp.where(pick, -3e38, x)
    return sel


def _softmax_cols(s, valid):
    m = jnp.max(jnp.where(valid, s, NEG), axis=0, keepdims=True)
    p = jnp.where(valid, jnp.exp2(s - m), 0.0)
    l = jnp.sum(p, axis=0, keepdims=True)
    return p * jnp.where(l > 0.0, 1.0 / l, 0.0)


def _split_dot(a_bf16, x):
    hi = x.astype(BF16)
    lo = (x - hi.astype(F32)).astype(BF16)
    return (jnp.dot(a_bf16, hi, preferred_element_type=F32)
            + jnp.dot(a_bf16, lo, preferred_element_type=F32))


def _split_dot_rhs(x, w_bf16):
    hi = x.astype(BF16)
    lo = (x - hi.astype(F32)).astype(BF16)
    return (jnp.dot(hi, w_bf16, preferred_element_type=F32)
            + jnp.dot(lo, w_bf16, preferred_element_type=F32))


def _nt(a, b):
    return lax.dot_general(a, b, (((1,), (1,)), ((), ())), preferred_element_type=F32)


def _online_update(s, v, m_ref, l_ref, acc_ref, v_transposed=False):
    m_old = m_ref[...]
    m_new = jnp.maximum(m_old, jnp.max(s, axis=1, keepdims=True))
    p = jnp.exp2(s - m_new)
    alpha = jnp.exp2(m_old - m_new)
    l_ref[...] = alpha * l_ref[...] + jnp.sum(p, axis=1, keepdims=True)
    pv = _nt(p.astype(BF16), v) if v_transposed else jnp.dot(p.astype(BF16), v, preferred_element_type=F32)
    acc_ref[...] = alpha * acc_ref[...] + pv
    m_ref[...] = m_new


def _score_matrix(n_rows, row_offset, n_blocks):
    c = CMP_BLOCK // CMP_STRIDE
    ratio = SEL_BLOCK // CMP_STRIDE
    n_ov = ratio + c - 1
    m = np.zeros((n_blocks, n_rows), np.float32)
    for j in range(n_blocks):
        for u in range(n_ov):
            start = CMP_STRIDE * (u - (c - 1))
            w_u = (min(start + CMP_BLOCK, SEL_BLOCK) - max(start, 0)) / CMP_STRIDE
            n = ratio * j + u - (c - 1)
            if 0 <= n and n + row_offset < n_rows:
                m[j, n + row_offset] = w_u
    return m


def _nsa_prompt_kernel(qt_ref, gt_ref, kc_ref, vct_ref, ks_ref, vst_ref, kw_ref, vwt_ref,
                       mt_ref, cc_ref, cs_ref, bw_ref, eb_ref, o_ref,
                       sc_ref, sw_ref, sel_ref, m_ref, acc_ref, oc_ref,
                       sa_ref, sb_ref, pa_ref, pb_ref, ala_ref, alb_ref, *, n_blocks, variants):
    qb = pl.program_id(0)
    nrow = kc_ref.shape[0]
    gq = GROUP * Q_BLOCK
    tiles_per_q = SEL_TILE // Q_BLOCK
    nband = WINDOW + Q_BLOCK
    max_tile = n_blocks // BLK_PER_TILE - 1
    frow = lax.broadcasted_iota(jnp.int32, (KV_DIM, Q_BLOCK), 0)

    def lanes4(x):
        return jnp.concatenate([x] * GROUP, axis=1)

    qx = []
    for kv in range(N_KV):
        keep = (frow >= HEAD_DIM) if kv else (frow < HEAD_DIM)
        qx.append(jnp.concatenate(
            [jnp.where(keep, qt_ref[j * KV_DIM:(j + 1) * KV_DIM, :], jnp.zeros((), BF16))
             for j in range(GROUP)], axis=1))

    def qk_stage(kt, s_ref, penalty=0.0):
        k = ks_ref[pl.ds(pl.multiple_of(kt * SEL_TILE, SEL_TILE), SEL_TILE), :]
        k_aug = jnp.concatenate([k, eb_ref[...]], axis=1)
        for kv in range(N_KV):
            srow = sel_ref[kv, pl.ds(pl.multiple_of(kt * BLK_PER_TILE, BLK_PER_TILE), BLK_PER_TILE), :]
            mrows = jnp.concatenate([lanes4(srow + penalty), jnp.zeros((KV_DIM - BLK_PER_TILE, gq), F32)], axis=0)
            q_aug = jnp.concatenate([qx[kv], mrows.astype(BF16)], axis=0)
            s_ref[kv] = jnp.dot(k_aug, q_aug, preferred_element_type=F32)

    def gate(kv, i):
        return jnp.concatenate(
            [gt_ref[3 * (kv * GROUP + j) + i:3 * (kv * GROUP + j) + i + 1, :] for j in range(GROUP)], axis=1)

    def compressed_and_select(n_r, n_b):
        for kv in range(N_KV):
            sc_ref[kv, 0:n_r, :] = jnp.dot(kc_ref[0:n_r, :], qx[kv], preferred_element_type=F32)
        r = lax.broadcasted_iota(jnp.int32, (n_r, Q_BLOCK), 0)
        qpos_c = qb * Q_BLOCK + lax.broadcasted_iota(jnp.int32, (n_r, Q_BLOCK), 1)
        end_pos = (r - CMP_PAD) * CMP_STRIDE + (CMP_BLOCK - 1)
        vis = (r >= CMP_PAD) & (r < CMP_PAD + n_blocks * (SEL_BLOCK // CMP_STRIDE) - 1) & (end_pos <= qpos_c)
        vis_add = lanes4(jnp.where(vis, 0.0, NEG))
        band = pl.ds(pl.multiple_of(qb * SUBLANES, SUBLANES), CMP_BAND)
        blk_i = lax.broadcasted_iota(jnp.int32, (n_b, Q_BLOCK), 0)
        cur = (qb * Q_BLOCK + lax.broadcasted_iota(jnp.int32, (n_b, Q_BLOCK), 1)) // SEL_BLOCK
        forced = (blk_i == 0) | (blk_i == cur) | (blk_i == cur - 1)
        scores = []
        for kv in range(N_KV):
            sc_ref[kv, band, :] = sc_ref[kv, band, :] + cc_ref[kv]
            s = sc_ref[kv, 0:n_r, :] + vis_add
            m = jnp.max(s, axis=0, keepdims=True)
            p = jnp.exp2(s - m)
            l = jnp.sum(p, axis=0, keepdims=True)
            pn = p * jnp.where(m > 0.5 * NEG, 1.0 / l, 0.0)
            oc_ref[kv] = gate(kv, 0) * jnp.dot(vct_ref[:, 0:n_r], pn.astype(BF16), preferred_element_type=F32)
            imp = pn[:, 0:Q_BLOCK]
            for j in range(1, GROUP):
                imp = imp + pn[:, j * Q_BLOCK:(j + 1) * Q_BLOCK]
            score = _split_dot(mt_ref[0:n_b, 0:n_r], imp)
            scores.append(jnp.where(blk_i <= cur, score, -FORCE_BONUS))
        kband = kw_ref[pl.ds(pl.multiple_of(qb * Q_BLOCK, Q_BLOCK), nband), :]
        for kv in range(N_KV):
            sw_ref[kv] = jnp.dot(kband, qx[kv], preferred_element_type=F32)
        blk2 = jnp.concatenate([blk_i.astype(F32)] * N_KV, axis=1)
        forced2 = jnp.concatenate([forced] * N_KV, axis=1)
        sel = _select_topk(jnp.concatenate(scores, axis=1), blk2, N_SEL - 3, always=forced2)
        for kv in range(N_KV):
            sel_ref[kv, 0:n_b, :] = (sel[:, kv * Q_BLOCK:(kv + 1) * Q_BLOCK] - 1.0) * (-NEG)

    sel_ref[...] = jnp.full_like(sel_ref, NEG)
    lo = 0
    for q_hi, n_r, n_b in variants:
        @pl.when((qb >= lo) & (qb <= q_hi))
        def _(n_r=n_r, n_b=n_b):
            compressed_and_select(n_r, n_b)
        lo = q_hi + 1

    m_ref[...] = jnp.full_like(m_ref, NEG)
    acc_ref[...] = jnp.zeros_like(acc_ref)

    def pv_stage(kt, p_ref, al_ref):
        for kv in range(N_KV):
            acc_ref[kv] = al_ref[kv] * acc_ref[kv] + jnp.dot(vst_ref[kv, kt], p_ref[kv],
                                                             preferred_element_type=F32)

    def sm_stage(kt, s_ref, p_ref, al_ref, near):
        d0 = qb - kt * tiles_per_q
        for kv in range(N_KV):
            s = s_ref[kv]
            if near:
                s = s + jnp.concatenate(
                    [cs_ref[kv, jnp.clip(d0 - i, -1, 2) + 1] for i in range(tiles_per_q)], axis=0)
            m_old = m_ref[kv]
            m_new = jnp.maximum(m_old, jnp.max(s, axis=0, keepdims=True))
            p_ref[kv] = jnp.exp2(s - m_new).astype(BF16)
            al_ref[kv] = jnp.exp2(m_old - m_new)
            m_ref[kv] = m_new

    n_far = jnp.maximum(qb - 1, 0) // tiles_per_q
    pb_ref[...] = jnp.zeros_like(pb_ref)
    alb_ref[...] = jnp.ones_like(alb_ref)
    qk_stage(0, sa_ref)

    vband = jnp.concatenate([vwt_ref[qb + i] for i in range(nband // Q_BLOCK)], axis=1)
    krow = lax.broadcasted_iota(jnp.int32, (nband, Q_BLOCK), 0)
    pos_add = lanes4(jnp.where(krow >= WINDOW - qb * Q_BLOCK, 0.0, NEG))
    for kv in range(N_KV):
        s = sw_ref[kv] + bw_ref[kv] + pos_add
        m = jnp.max(s, axis=0, keepdims=True)
        p = jnp.exp2(s - m)
        o_w = (jnp.dot(vband, p.astype(BF16), preferred_element_type=F32)
               / jnp.sum(p, axis=0, keepdims=True))
        oc_ref[kv] = oc_ref[kv] + gate(kv, 2) * o_w

    def pair_body(u, c):
        t0 = 2 * u
        t1 = jnp.minimum(t0 + 1, max_tile)
        t2 = jnp.minimum(t0 + 2, max_tile)
        qk_stage(t1, sb_ref, jnp.where(t0 + 1 < n_far, 0.0, NEG))
        pv_stage(jnp.maximum(t0 - 1, 0), pb_ref, alb_ref)
        sm_stage(t0, sa_ref, pa_ref, ala_ref, False)
        qk_stage(t2, sa_ref)
        pv_stage(t0, pa_ref, ala_ref)
        sm_stage(t1, sb_ref, pb_ref, alb_ref, False)
        return c
    n_pairs = (n_far + 1) // 2
    lax.fori_loop(0, n_pairs, pair_body, 0)

    near_a = n_far
    near_b = jnp.minimum(n_far + 1, max_tile)
    qk_stage(near_a, sa_ref)
    pv_stage(jnp.clip(2 * n_pairs - 1, 0, max_tile), pb_ref, alb_ref)
    sm_stage(near_a, sa_ref, pa_ref, ala_ref, True)
    pv_stage(near_a, pa_ref, ala_ref)

    @pl.when(n_far + 1 <= qb // tiles_per_q)
    def _():
        qk_stage(near_b, sb_ref)
        sm_stage(near_b, sb_ref, pb_ref, alb_ref, True)
        pv_stage(near_b, pb_ref, alb_ref)

    o_kv = []
    for kv in range(N_KV):
        acc = acc_ref[kv]
        o_s = acc[0:HEAD_DIM, :] / acc[HEAD_DIM:HEAD_DIM + 1, :]
        o_kv.append(oc_ref[kv, kv * HEAD_DIM:(kv + 1) * HEAD_DIM, :] + gate(kv, 1) * o_s)
    o_t = jnp.concatenate(o_kv, axis=0)
    for j in range(GROUP):
        o_ref[:, j * LANES:(j + 1) * LANES] = o_t[:, j * Q_BLOCK:(j + 1) * Q_BLOCK].T.astype(BF16)


def _nsa_prompt(qat, gat, kc, vct, ksb, vst3, kwb, vwt3, mt, cc, cs, bw, eb):
    t = qat.shape[1]
    n_blocks = t // SEL_BLOCK
    nrow = kc.shape[0]
    gq = GROUP * Q_BLOCK
    n_q = t // Q_BLOCK
    variants = []
    for n_r in list(range(2 * LANES, nrow - LANES, 2 * LANES)) + [nrow]:
        q_hi = n_q - 1 if n_r == nrow else min((n_r - CMP_BAND) // SUBLANES, n_q - 1)
        n_b = min(-(-(2 * q_hi + 2) // (2 * SUBLANES)) * (2 * SUBLANES), n_blocks)
        variants.append((q_hi, n_r, n_b))
    return pl.pallas_call(
        functools.partial(_nsa_prompt_kernel, n_blocks=n_blocks, variants=tuple(variants)),
        grid=(t // Q_BLOCK,),
        in_specs=[pl.BlockSpec((Q_DIM, Q_BLOCK), lambda i: (0, i)),
                  pl.BlockSpec((LANES, Q_BLOCK), lambda i: (0, i)),
                  _const_spec(kc.shape), _const_spec(vct.shape),
                  _const_spec(ksb.shape), _const_spec(vst3.shape),
                  _const_spec(kwb.shape), _const_spec(vwt3.shape),
                  _const_spec(mt.shape), _const_spec(cc.shape),
                  _const_spec(cs.shape), _const_spec(bw.shape), _const_spec(eb.shape)],
        out_specs=pl.BlockSpec((Q_BLOCK, Q_DIM), lambda i: (i, 0)),
        out_shape=jax.ShapeDtypeStruct((t, Q_DIM), BF16),
        scratch_shapes=[pltpu.VMEM((N_KV, nrow, gq), F32),
                        pltpu.VMEM((N_KV, WINDOW + Q_BLOCK, gq), F32),
                        pltpu.VMEM((N_KV, n_blocks, Q_BLOCK), F32),
                        pltpu.VMEM((N_KV, 1, gq), F32),
                        pltpu.VMEM((N_KV, vst3.shape[2], gq), F32),
                        pltpu.VMEM((N_KV, KV_DIM, gq), F32),
                        pltpu.VMEM((N_KV, SEL_TILE, gq), F32), pltpu.VMEM((N_KV, SEL_TILE, gq), F32),
                        pltpu.VMEM((N_KV, SEL_TILE, gq), BF16), pltpu.VMEM((N_KV, SEL_TILE, gq), BF16),
                        pltpu.VMEM((N_KV, 1, gq), F32), pltpu.VMEM((N_KV, 1, gq), F32)],
        compiler_params=_cparams(("arbitrary",)),
        name="nsa_prompt",
    )(qat, gat, kc, vct, ksb, vst3, kwb, vwt3, mt, cc, cs, bw, eb)


S_ROWS = 128
S_CHUNK_PAGES = 32
S_CHUNK = S_CHUNK_PAGES * PAGE


def _nsa_sample_kernel(pt_ref, qx_ref, gm_ref, kc_ref, vct_ref, kpool_ref, vpool_ref,
                       knew_ref, vnew_ref, kwin_ref, vwin_ref, kwnew_ref, vwnew_ref,
                       mt_ref, gsum_ref, ccs_ref, css_ref, cns_ref, cws_ref, o_ref,
                       kbuf, vbuf, sem, mask_ref, m_ref, l_ref, acc_ref, oc_ref, ow_ref,
                       *, n_chunks, n_blocks):
    b = pl.program_id(0)
    c = pl.program_id(1)
    step = b * n_chunks + c
    total = pl.num_programs(0) * n_chunks
    slot = step % 2

    def copies(bb, cc, s, p):
        pg = cc * S_CHUNK_PAGES + p
        dst = pl.ds(pl.multiple_of(p * PAGE, PAGE), PAGE)
        return (pltpu.make_async_copy(kpool_ref.at[pt_ref[bb, pg]], kbuf.at[s, :, dst], sem.at[0, s]),
                pltpu.make_async_copy(vpool_ref.at[pt_ref[bb, pg]], vbuf.at[s, :, dst], sem.at[1, s]))

    def start_all(st, s):
        bb = st // n_chunks
        cc = st % n_chunks

        def body(p, carry):
            ck, cv = copies(bb, cc, s, p)
            ck.start()
            cv.start()
            return carry
        lax.fori_loop(0, S_CHUNK_PAGES, body, 0)

    @pl.when(step == 0)
    def _():
        start_all(step, slot)

    @pl.when(step + 1 < total)
    def _():
        start_all(step + 1, 1 - slot)

    qx = qx_ref[0]
    n_q = o_ref.shape[1]
    qq = qx[:n_q]

    @pl.when(c == 0)
    def _():
        nrow = kc_ref.shape[1]
        s = _nt(kc_ref[0], qx)
        r = lax.broadcasted_iota(jnp.int32, (nrow, S_ROWS), 0)
        band0 = nrow - CMP_BAND
        s = s + jnp.concatenate([jnp.zeros((band0, S_ROWS), F32), ccs_ref[...]], axis=0)
        pn = _softmax_cols(s, r < nrow - 1)
        oc_ref[...] = jnp.dot(vct_ref[0], pn.astype(BF16), preferred_element_type=F32).T[:n_q]
        imp = jnp.dot(pn, gsum_ref[...], preferred_element_type=F32, precision=HIGHEST)
        score = _split_dot(mt_ref[...], imp)
        blk_i = lax.broadcasted_iota(jnp.int32, (n_blocks, S_ROWS), 0)
        forced = (blk_i == 0) | (blk_i == n_blocks - 1)
        selt = _select_topk(score, blk_i.astype(F32), N_SEL - 3, always=forced)
        selt = lax.dot_general(selt.astype(BF16), gsum_ref[...].astype(BF16), (((1,), (1,)), ((), ())),
                               preferred_element_type=F32)
        selm1 = selt.T[:n_q] - 1.0
        e_r = lax.broadcasted_iota(jnp.int32, (BLK_PER_TILE, SEL_TILE), 0)
        e_c = lax.broadcasted_iota(jnp.int32, (BLK_PER_TILE, SEL_TILE), 1)
        expand = jnp.where(e_c // SEL_BLOCK == e_r, -NEG, 0.0)
        for kt in range(n_blocks // BLK_PER_TILE):
            mask_ref[:, kt * SEL_TILE:(kt + 1) * SEL_TILE] = jnp.dot(
                selm1[:, kt * BLK_PER_TILE:(kt + 1) * BLK_PER_TILE], expand, preferred_element_type=F32)

        sw = _nt(qq, kwin_ref[0].astype(BF16)) + cws_ref[:n_q, :]
        sn = _nt(qq, kwnew_ref[0].astype(BF16)) + cns_ref[:n_q, :]
        m = jnp.maximum(jnp.max(sw, axis=1, keepdims=True), jnp.max(sn, axis=1, keepdims=True))
        pw = jnp.exp2(sw - m)
        pn2 = jnp.exp2(sn - m)
        l = jnp.sum(pw, axis=1, keepdims=True) + jnp.sum(pn2, axis=1, keepdims=True)
        ow = (jnp.dot(pw.astype(BF16), vwin_ref[0].astype(BF16), preferred_element_type=F32)
              + jnp.dot(pn2.astype(BF16), vwnew_ref[0].astype(BF16), preferred_element_type=F32))
        ow_ref[...] = ow / l

        m_ref[...] = jnp.full_like(m_ref, NEG)
        l_ref[...] = jnp.zeros_like(l_ref)
        acc_ref[...] = jnp.zeros_like(acc_ref)

    def wait_body(p, carry):
        ck, cv = copies(b, c, slot, p)
        ck.wait()
        cv.wait()
        return carry
    lax.fori_loop(0, S_CHUNK_PAGES, wait_body, 0)

    col0 = pl.multiple_of(c * S_CHUNK, S_CHUNK)
    s = (jnp.dot(qq, kbuf[slot].astype(BF16), preferred_element_type=F32)
         + mask_ref[:, pl.ds(col0, S_CHUNK)])

    @pl.when(c < n_chunks - 1)
    def _():
        _online_update(s, vbuf[slot].astype(BF16), m_ref, l_ref, acc_ref, v_transposed=True)

    @pl.when(c == n_chunks - 1)
    def _():
        near = jnp.concatenate([jnp.zeros((n_q, S_CHUNK - LANES), F32), css_ref[:n_q, :]], axis=1)
        _online_update(s + near, vbuf[slot].astype(BF16), m_ref, l_ref, acc_ref, v_transposed=True)
        sn = _nt(qq, knew_ref[0].astype(BF16)) + cns_ref[:n_q, :]
        _online_update(sn, vnew_ref[0].astype(BF16), m_ref, l_ref, acc_ref)
        o_s = acc_ref[...] / l_ref[...]
        o_ref[0] = gm_ref[0, 0] * oc_ref[...] + gm_ref[0, 1] * o_s + gm_ref[0, 2] * ow_ref[...]


def _nsa_sample(page_table, qx, gm, kc, vct, kpool, vpool, knew, vnew, kwin, vwin, kwnew, vwnew,
                mt, gsum, ccs, css, cns, cws):
    nb = qx.shape[0]
    n_pages = page_table.shape[1]
    n_chunks = n_pages // S_CHUNK_PAGES
    past = n_pages * PAGE
    n_blocks = past // SEL_BLOCK
    n_q = gm.shape[2]

    def bspec(shape):
        nd = len(shape)
        return pl.BlockSpec((1,) + tuple(shape[1:]), lambda b, c, pt: (b,) + (0,) * (nd - 1))

    def cspec(shape):
        nd = len(shape)
        return pl.BlockSpec(tuple(shape), lambda b, c, pt: (0,) * nd, pipeline_mode=pl.Buffered(1))

    grid_spec = pltpu.PrefetchScalarGridSpec(
        num_scalar_prefetch=1,
        grid=(nb, n_chunks),
        in_specs=[bspec(qx.shape), bspec(gm.shape), bspec(kc.shape), bspec(vct.shape),
                  pl.BlockSpec(memory_space=pl.ANY), pl.BlockSpec(memory_space=pl.ANY),
                  bspec(knew.shape), bspec(vnew.shape), bspec(kwin.shape), bspec(vwin.shape),
                  bspec(kwnew.shape), bspec(vwnew.shape),
                  cspec(mt.shape), cspec(gsum.shape), cspec(ccs.shape), cspec(css.shape),
                  cspec(cns.shape), cspec(cws.shape)],
        out_specs=pl.BlockSpec((1, n_q, KV_DIM), lambda b, c, pt: (b, 0, 0)),
        scratch_shapes=[pltpu.VMEM((2, KV_DIM, S_CHUNK), F32),
                        pltpu.VMEM((2, KV_DIM, S_CHUNK), F32),
                        pltpu.SemaphoreType.DMA((2, 2)),
                        pltpu.VMEM((n_q, past), F32),
                        pltpu.VMEM((n_q, 1), F32), pltpu.VMEM((n_q, 1), F32),
                        pltpu.VMEM((n_q, KV_DIM), F32),
                        pltpu.VMEM((n_q, KV_DIM), F32), pltpu.VMEM((n_q, KV_DIM), F32)],
    )
    return pl.pallas_call(
        functools.partial(_nsa_sample_kernel, n_chunks=n_chunks, n_blocks=n_blocks),
        grid_spec=grid_spec,
        out_shape=jax.ShapeDtypeStruct((nb, n_q, KV_DIM), F32),
        compiler_params=_cparams(("arbitrary", "arbitrary")),
        name="nsa_sample",
    )(page_table, qx, gm, kc, vct, kpool, vpool, knew, vnew, kwin, vwin, kwnew, vwnew,
      mt, gsum, ccs, css, cns, cws)


def _ffn_kernel(x_ref, oa_ref, ob_ref, sa_ref, sb_ref, wpa_ref, wpb_ref, wo_ref, nf_ref,
                wg_ref, wu_ref, wd_ref, nl_ref, y_ref):
    pa = jnp.dot(oa_ref[...], wpa_ref[...], preferred_element_type=F32)
    pb = jnp.dot(ob_ref[...], wpb_ref[...], preferred_element_type=F32)
    merged = sa_ref[...] * pa + sb_ref[...] * pb
    x = x_ref[...] + jnp.dot(merged.astype(BF16), wo_ref[...], preferred_element_type=F32)
    hn = (x * lax.rsqrt(jnp.mean(x * x, axis=-1, keepdims=True) + EPS) * nf_ref[...]).astype(BF16)
    gate = jnp.dot(hn, wg_ref[...], preferred_element_type=F32)
    up = jnp.dot(hn, wu_ref[...], preferred_element_type=F32)
    ff = (jax.nn.silu(gate) * up).astype(BF16)
    x = x + jnp.dot(ff, wd_ref[...], preferred_element_type=F32)
    y_ref[...] = x * lax.rsqrt(jnp.mean(x * x, axis=-1, keepdims=True) + EPS) * nl_ref[...]


def _ffn(x2d, oa, ob, sa, sb, wpa, wpb, wo, nf, wg, wu, wd, nl, tm):
    rows = x2d.shape[0]

    def rspec(n):
        return pl.BlockSpec((tm, n), lambda i: (i, 0))

    return pl.pallas_call(
        _ffn_kernel,
        grid=(rows // tm,),
        in_specs=[rspec(D_MODEL), rspec(Q_DIM), rspec(HG_W), rspec(D_MODEL), rspec(D_MODEL),
                  _const_spec(wpa.shape), _const_spec(wpb.shape), _const_spec(wo.shape),
                  _const_spec(nf.shape), _const_spec(wg.shape), _const_spec(wu.shape),
                  _const_spec(wd.shape), _const_spec(nl.shape)],
        out_specs=rspec(D_MODEL),
        out_shape=jax.ShapeDtypeStruct((rows, D_MODEL), F32),
        compiler_params=_cparams(("arbitrary",)),
        name="ffn",
    )(x2d, oa, ob, sa, sb, wpa, wpb, wo, nf, wg, wu, wd, nl)


def _pack_w_in(w_in):
    sizes = (Q_DIM,) + (KV_DIM,) * 6 + (3 * N_HEADS,) + (HG_W,) * 4 + (D_MODEL,) * 2
    offs = np.concatenate([[0], np.cumsum(sizes)])
    q = w_in[:, offs[0]:offs[1]].reshape(D_MODEL, N_HEADS, HEAD_DIM)[:, _HEAD_PERM, :].reshape(D_MODEL, Q_DIM)
    g = jnp.pad(w_in[:, offs[7]:offs[8]], ((0, 0), (0, LANES - 3 * N_HEADS)))
    return jnp.concatenate([q, w_in[:, offs[1]:offs[7]], g, w_in[:, offs[8]:]], axis=1).astype(BF16)


def _strip(bvc, rel, lo=0, hi=None, masked=NEG):
    val = bvc[:, np.clip(rel, 0, 255)]
    ok = rel >= lo
    if hi is not None:
        ok = ok & (rel < hi)
    return jnp.where(jnp.asarray(ok)[None], val, masked)


def _toeplitz(bvc, a, n_rows, n_cols, lo=0, hi=None):
    n = n_rows + n_cols - 1
    u = _strip(bvc, a - (n_rows - 1) + np.arange(n), lo, hi)
    u = jnp.pad(u, ((0, 0), (0, 1)))
    circ = jnp.tile(u, (1, n_rows))[:, :n_rows * n].reshape(N_HEADS, n_rows, n)
    return circ[:, :, n_rows - 1:n_rows - 1 + n_cols]


def _bias_strips_prompt(bvc):
    gq = GROUP * Q_BLOCK

    def lanes(x):
        return x.reshape(N_KV, GROUP, x.shape[1], Q_BLOCK).transpose(0, 2, 1, 3).reshape(N_KV, x.shape[1], gq)

    cs = jnp.stack([lanes(_toeplitz(bvc, Q_BLOCK * d, Q_BLOCK, Q_BLOCK)) for d in (-1, 0, 1, 2)], axis=1)
    bw = lanes(_toeplitz(bvc, WINDOW, WINDOW + Q_BLOCK, Q_BLOCK, 0, WINDOW))
    rr = np.arange(CMP_BAND)[:, None]
    rel_c = np.arange(Q_BLOCK)[None, :] - CMP_STRIDE * (rr - CMP_PAD) - (CMP_BLOCK - 1)
    cc = _strip(bvc, rel_c, masked=0.0).reshape(N_KV, GROUP, CMP_BAND, Q_BLOCK)
    cc = cc.transpose(0, 2, 1, 3).reshape(N_KV, CMP_BAND, gq)
    return cc, cs, bw


def _bias_strips_sample(bvc, past, t):
    def rows(a):
        a = a.reshape(N_HEADS * t, a.shape[-1])
        return jnp.pad(a, ((0, S_ROWS - N_HEADS * t), (0, 0)))
    tt = np.arange(t)[:, None]
    nrow = past // CMP_STRIDE
    n = (nrow - CMP_BAND + np.arange(CMP_BAND))[None, :]
    ccs = rows(_strip(bvc, past + tt - CMP_STRIDE * n - (CMP_BLOCK - 1), masked=0.0)).T
    i = np.arange(LANES)[None, :]
    css = rows(_strip(bvc, LANES + tt - i))
    cns = rows(_strip(bvc, np.where(i < t, tt - i, -1)))
    iw = np.arange(WINDOW)[None, :]
    cws = rows(_strip(bvc, WINDOW + tt - iw, 0, WINDOW))
    return ccs, css, cns, cws


def kernel(x_prompt, x_sample, cache_k_cmp, cache_v_cmp, cache_k_slc, cache_v_slc, state_k_win, state_v_win,
           state_hgrn, page_table, norm_mix, w_in, cmp_pe_k, cmp_w1_k, cmp_w2_k, cmp_pe_v, cmp_w1_v, cmp_w2_v,
           rel_bias, hg_lb_logits, hg_norm, w_proj_a, w_proj_b, w_out, norm_ffn, w_gate, w_up, w_down, norm_final):
    nbp, t_p, _ = x_prompt.shape
    nbs, t_s, _ = x_sample.shape
    assert nbp == 1 and norm_mix.shape[0] == 1
    n_pages = page_table.shape[1]
    past = n_pages * PAGE
    assert state_k_win.shape[2] == WINDOW and past % S_CHUNK == 0 and t_s <= SUBLANES

    lb = jnp.cumsum(jax.nn.softmax(hg_lb_logits.astype(F32), axis=0), axis=0)[0]
    lb3 = jnp.pad(jnp.stack([jnp.log(lb), jnp.log1p(-lb), 1.0 - lb]), ((0, SUBLANES - 3), (0, 0)))
    w_pack = _pack_w_in(w_in[0])
    g_mix = norm_mix[0][None, :]
    wpa = w_proj_a[0].reshape(N_HEADS, HEAD_DIM, D_MODEL)[_HEAD_PERM].reshape(Q_DIM, D_MODEL).astype(BF16)
    wpb = w_proj_b[0].astype(BF16)
    wo = w_out[0].astype(BF16)
    wg, wu, wd = w_gate[0].astype(BF16), w_up[0].astype(BF16), w_down[0].astype(BF16)
    nf, nl = norm_ffn[0][None, :], norm_final[None, :]
    gn = hg_norm[0][None, :]
    pe_k, w1_k, w2_k = _compress_weights(cmp_pe_k[0], cmp_w1_k[0], cmp_w2_k[0])
    pe_v, w1_v, w2_v = _compress_weights(cmp_pe_v[0], cmp_w1_v[0], cmp_w2_v[0])
    bvc = (rel_bias[_BUCKET] - rel_bias[N_BUCKETS - 1][None, :]).T * LOG2E
    cc, cs, bw = _bias_strips_prompt(bvc)
    ccs, css, cns, cws = _bias_strips_sample(bvc, past, t_s)

    xp2 = x_prompt.reshape(t_p, D_MODEL)
    xs2 = x_sample.reshape(nbs * t_s, D_MODEL)
    seg = lambda off, n: w_pack[:, off:off + n]
    w_t = jnp.concatenate([seg(_OFF_Q, Q_DIM), seg(_OFF_G, LANES), seg(_OFF_KV + 3 * KV_DIM, KV_DIM),
                           seg(_OFF_KV + 5 * KV_DIM, KV_DIM)], axis=1).T
    pp = _proj(xp2, g_mix, w_pack, w_t, lb3, 512)
    ps = _proj(xs2, g_mix, w_pack, w_t, lb3, nbs * t_s)
    (_, kc_p, vc_p, ks_p, vs_p, kw_p, vw_p, ksb_p, kwb_p, _,
     qh_p, lf_p, kh_p, vh_p, gs_p, sa_p, sb_p, qat_p, gat_p, vst_p, vwt_p) = pp
    (qa_s, kc_s, vc_s, ks_s, vs_s, kw_s, vw_s, _, _, ga_s,
     qh_s, lf_s, kh_s, vh_s, gs_s, sa_s, sb_s, _, _, _, _) = ps

    ob_p, s_p = _hgrn_prompt(qh_p, kh_p, vh_p, lf_p, gs_p, gn)
    r3 = lambda a: a.reshape(nbs, t_s, a.shape[-1])
    ob_s, s_s = _hgrn_sample(r3(qh_s), r3(kh_s), r3(vh_s), r3(lf_s), r3(gs_s), gn, state_hgrn[0])

    ident = jnp.arange(t_p // PAGE, dtype=jnp.int32)[None, :]
    pool_rows = lambda a: a.reshape(-1, PAGE, KV_DIM)
    kcb_p, _ = _compress(pool_rows(kc_p), ident, pe_k, w1_k, w2_k, False)
    _, vct_p = _compress(pool_rows(vc_p), ident, pe_v, w1_v, w2_v, False)
    back = -(CMP_PAD + kcb_p.shape[1]) % LANES
    kcb_p = jnp.pad(kcb_p[0], ((CMP_PAD, back), (0, 0)))
    vct_p = jnp.pad(vct_p[0], ((0, 0), (CMP_PAD, back)))
    nrow_p = kcb_p.shape[0]
    mt_p = jnp.asarray(_score_matrix(nrow_p, CMP_PAD, t_p // SEL_BLOCK), BF16)
    kwb_pad = jnp.pad(kwb_p, ((WINDOW, 0), (0, 0)))
    vst3 = vst_p.reshape(KV_DIM, t_p // SEL_TILE, SEL_TILE).transpose(1, 0, 2)
    ones_rows = jnp.ones((t_p // SEL_TILE, 2 * SUBLANES, SEL_TILE), BF16)
    vst3 = jnp.stack([jnp.concatenate([vst3[:, kv * HEAD_DIM:(kv + 1) * HEAD_DIM, :], ones_rows], axis=1)
                      for kv in range(N_KV)])
    vwt3 = jnp.pad(vwt_p, ((0, 0), (WINDOW, 0))).reshape(KV_DIM, (t_p + WINDOW) // Q_BLOCK, Q_BLOCK)
    vwt3 = vwt3.transpose(1, 0, 2)
    eb = (np.arange(LANES)[None, :] == np.arange(SEL_TILE)[:, None] // SEL_BLOCK).astype(np.float32)
    oa_p = _nsa_prompt(qat_p, gat_p, kcb_p, vct_p, ksb_p, vst3, kwb_pad, vwt3, mt_p, cc, cs, bw,
                       jnp.asarray(eb, BF16))

    pool_t = lambda a: a.transpose(0, 2, 3, 1).reshape(-1, KV_DIM, PAGE)
    kcb_s, _ = _compress(pool_t(cache_k_cmp[0]), page_table, pe_k, w1_k, w2_k, True)
    _, vct_s = _compress(pool_t(cache_v_cmp[0]), page_table, pe_v, w1_v, w2_v, True)
    nq = N_HEADS * t_s
    qs4 = qa_s.reshape(nbs, t_s, GROUP, N_KV, HEAD_DIM).astype(F32)
    qx = jnp.einsum('btjkd,kq->bkjtqd', qs4, jnp.eye(N_KV, dtype=F32)).reshape(nbs, nq, KV_DIM)
    qx = jnp.pad(qx, ((0, 0), (0, S_ROWS - nq), (0, 0))).astype(BF16)
    g4 = ga_s[:, :3 * N_HEADS].reshape(nbs, t_s, N_KV, GROUP, 3)
    gm = jnp.transpose(g4, (0, 4, 2, 3, 1)).reshape(nbs, 3, nq, 1)
    gm = jnp.broadcast_to(gm, (nbs, 3, nq, KV_DIM))
    new_tile = lambda a: jnp.pad(a.reshape(nbs, t_s, KV_DIM), ((0, 0), (0, LANES - t_s), (0, 0)))
    mt_s = jnp.asarray(_score_matrix(past // CMP_STRIDE, 0, past // SEL_BLOCK), BF16)
    gsum = np.zeros((S_ROWS, S_ROWS), np.float32)
    for kv in range(N_KV):
        for j in range(GROUP):
            for t in range(t_s):
                gsum[(kv * GROUP + j) * t_s + t, kv * t_s + t] = 1.0
    o_kv = _nsa_sample(page_table, qx, gm, kcb_s, vct_s,
                       pool_t(cache_k_slc[0]), pool_t(cache_v_slc[0]),
                       new_tile(ks_s), new_tile(vs_s), state_k_win[0].reshape(nbs, WINDOW, KV_DIM),
                       state_v_win[0].reshape(nbs, WINDOW, KV_DIM), new_tile(kw_s), new_tile(vw_s),
                       mt_s, jnp.asarray(gsum), ccs, css, cns, cws)
    o5 = o_kv.reshape(nbs, N_KV, GROUP, t_s, N_KV, HEAD_DIM)
    oa_s = jnp.einsum('bkjtqd,kq->btjkd', o5, jnp.eye(N_KV, dtype=F32)).reshape(nbs * t_s, Q_DIM).astype(BF16)

    y_p = _ffn(xp2, oa_p, ob_p, sa_p, sb_p, wpa, wpb, wo, nf, wg, wu, wd, nl, 256)
    y_s = _ffn(xs2, oa_s, ob_s.reshape(nbs * t_s, HG_W), sa_s, sb_s, wpa, wpb, wo, nf, wg, wu, wd, nl, nbs * t_s)

    kv5 = lambda a, nb_, tt: a.reshape(1, nb_, tt, N_KV, HEAD_DIM)
    wl = min(WINDOW, t_p)
    win = lambda st, new: jnp.concatenate(
        [st[0], new.reshape(nbs, t_s, N_KV, HEAD_DIM)], axis=1)[:, -WINDOW:][None]
    return (y_p.reshape(1, t_p, D_MODEL), y_s.reshape(nbs, t_s, D_MODEL),
            kv5(kc_p, 1, t_p), kv5(vc_p, 1, t_p), kv5(ks_p, 1, t_p), kv5(vs_p, 1, t_p),
            kv5(kw_p[-wl:], 1, wl), kv5(vw_p[-wl:], 1, wl), s_p[None, None],
            kv5(kc_s, nbs, t_s), kv5(vc_s, nbs, t_s), kv5(ks_s, nbs, t_s), kv5(vs_s, nbs, t_s),
            win(state_k_win, kw_s), win(state_v_win, vw_s), s_s[None])
```

```python
import functools
import math

import numpy as np
import jax
import jax.numpy as jnp
from jax import lax
from jax.experimental import pallas as pl
from jax.experimental.pallas import tpu as pltpu

F32 = jnp.float32
BF16 = jnp.bfloat16
HIGHEST = lax.Precision.HIGHEST

D_MODEL = 1024
N_HEADS = 8
N_KV = 2
GROUP = N_HEADS // N_KV
HEAD_DIM = 64
KV_DIM = N_KV * HEAD_DIM
Q_DIM = N_HEADS * HEAD_DIM
CMP_BLOCK = 32
CMP_STRIDE = 16
CMP_HIDDEN = 2 * HEAD_DIM
SEL_BLOCK = 64
N_SEL = 16
WINDOW = 512
Q_BLOCK = 128
FORCE_BONUS = 1e4
N_BUCKETS = 32
MAX_DISTANCE = 128
HG_HEADS = 4
HG_DIM = 128
HG_CHUNK = 64
HG_SUB = 16
HG_W = HG_HEADS * HG_DIM
D_FF = ((8 * D_MODEL // 3 + 255) // 256) * 256
EPS = 1e-6
PAGE = 128
NEG = -1e30
LOG2E = math.log2(math.e)
Q_SCALE = HEAD_DIM ** -0.5 * LOG2E

LANES = 128
SUBLANES = 8
VMEM_LIMIT = 56 * 1024 * 1024

_OFF_Q = 0
_OFF_KV = _OFF_Q + Q_DIM
_OFF_G = _OFF_KV + 6 * KV_DIM
_OFF_HG = _OFF_G + LANES
_OFF_GATE = _OFF_HG + 4 * HG_W
_PROJ_N = _OFF_GATE + 2 * D_MODEL

_HEAD_PERM = np.array([h for j in range(GROUP) for h in (j, GROUP + j)])

SEL_TILE = 512
BLK_PER_TILE = SEL_TILE // SEL_BLOCK
CMP_PAD = 16
CMP_BAND = 24


def _cparams(sem, vmem=VMEM_LIMIT):
    return pltpu.CompilerParams(dimension_semantics=sem, vmem_limit_bytes=vmem)


def _const_spec(shape):
    nd = len(shape)
    return pl.BlockSpec(shape, lambda *_: (0,) * nd, pipeline_mode=pl.Buffered(1))


def _bucket_table():
    n = np.arange(256)
    max_exact = N_BUCKETS // 2
    nf = np.maximum(n, 1).astype(np.float64)
    large = max_exact + (np.log(nf / max_exact) / math.log(MAX_DISTANCE / max_exact)
                         * (N_BUCKETS - max_exact)).astype(np.int64)
    large = np.minimum(large, N_BUCKETS - 1)
    return np.where(n < max_exact, n, large)


_BUCKET = _bucket_table()


def _proj_kernel(x_ref, g_ref, w_ref, wt_ref, lb_ref,
                 qa_ref, kc_ref, vc_ref, ks_ref, vs_ref, kw_ref, vw_ref,
                 ksb_ref, kwb_ref, ga_ref,
                 qh_ref, lf_ref, kh_ref, vh_ref, gs_ref, sa_ref, sb_ref,
                 qat_ref, gat_ref, vst_ref, vwt_ref):
    x = x_ref[...]
    xn = x * lax.rsqrt(jnp.mean(x * x, axis=-1, keepdims=True) + EPS) * g_ref[...]
    xb = xn.astype(BF16)

    def seg(a, n):
        return jnp.dot(xb, w_ref[:, a:a + n], preferred_element_type=F32)

    qa_ref[...] = (seg(_OFF_Q, Q_DIM) * Q_SCALE).astype(BF16)
    f32_refs = (kc_ref, vc_ref, ks_ref, vs_ref, kw_ref, vw_ref)
    b16_refs = (None, None, ksb_ref, None, kwb_ref, None)
    for i in range(6):
        u = seg(_OFF_KV + i * KV_DIM, KV_DIM)
        f32_refs[i][...] = u
        if b16_refs[i] is not None:
            b16_refs[i][...] = u.astype(BF16)
    ga_ref[...] = jax.nn.sigmoid(seg(_OFF_G, LANES))

    def seg_t(a, n):
        return lax.dot_general(wt_ref[a:a + n, :], xb, (((1,), (1,)), ((), ())), preferred_element_type=F32)

    qat_ref[...] = (seg_t(0, Q_DIM) * Q_SCALE).astype(BF16)
    gat_ref[...] = jax.nn.sigmoid(seg_t(Q_DIM, LANES))
    vst_ref[...] = seg_t(Q_DIM + LANES, KV_DIM).astype(BF16)
    vwt_ref[...] = seg_t(Q_DIM + LANES + KV_DIM, KV_DIM).astype(BF16)

    log_lb = lb_ref[0:1, :]
    log_1m = lb_ref[1:2, :]
    one_m = lb_ref[2:3, :]
    qh_ref[...] = jax.nn.silu(seg(_OFF_HG, HG_W))
    z = seg(_OFF_HG + HG_W, HG_W)
    b = log_1m + (jnp.minimum(z, 0.0) - jnp.log1p(jnp.exp(-jnp.abs(z))))
    hi = jnp.maximum(log_lb, b)
    lf_ref[...] = hi + jnp.log1p(jnp.exp(-jnp.abs(log_lb - b)))
    kh_ref[...] = one_m * jax.nn.sigmoid(-z)
    vh_ref[...] = seg(_OFF_HG + 2 * HG_W, HG_W)
    gs_ref[...] = jax.nn.silu(seg(_OFF_HG + 3 * HG_W, HG_W))
    sa_ref[...] = jax.nn.sigmoid(seg(_OFF_GATE, D_MODEL))
    sb_ref[...] = jax.nn.sigmoid(seg(_OFF_GATE + D_MODEL, D_MODEL))


_PROJ_T = Q_DIM + LANES + 2 * KV_DIM


def _proj(x2d, g, w, wt, lb3, tm):
    rows = x2d.shape[0]
    widths = ([(Q_DIM, BF16)] + [(KV_DIM, F32)] * 6 + [(KV_DIM, BF16)] * 2 + [(LANES, F32)]
              + [(HG_W, F32)] * 5 + [(D_MODEL, F32)] * 2)
    heights = [(Q_DIM, BF16), (LANES, F32), (KV_DIM, BF16), (KV_DIM, BF16)]
    return pl.pallas_call(
        _proj_kernel,
        grid=(rows // tm,),
        in_specs=[pl.BlockSpec((tm, D_MODEL), lambda i: (i, 0)),
                  _const_spec((1, D_MODEL)),
                  _const_spec((D_MODEL, _PROJ_N)),
                  _const_spec((_PROJ_T, D_MODEL)),
                  _const_spec((SUBLANES, HG_W))],
        out_specs=([pl.BlockSpec((tm, n), lambda i: (i, 0)) for n, _ in widths]
                   + [pl.BlockSpec((n, tm), lambda i: (0, i)) for n, _ in heights]),
        out_shape=([jax.ShapeDtypeStruct((rows, n), dt) for n, dt in widths]
                   + [jax.ShapeDtypeStruct((n, rows), dt) for n, dt in heights]),
        compiler_params=_cparams(("arbitrary",)),
        name="proj",
    )(x2d, g, w, wt, lb3)


_CH_W = CMP_STRIDE * KV_DIM
_CH_PER_PAGE = PAGE // CMP_STRIDE
_CH_PITCH = CMP_STRIDE + SUBLANES


def _compress_kernel(pt_ref, pool_ref, pe_ref, w1_ref, w2_ref, out_ref, outt_ref, buf, rbuf, xa, hbuf, sem,
                     *, n_pages, pages_transposed):
    b = pl.program_id(0)
    nb = pl.num_programs(0)
    slot = b % 2
    n_ch = n_pages * _CH_PER_PAGE

    def page_copy(bb, p, s):
        return pltpu.make_async_copy(pool_ref.at[pt_ref[bb, p]], buf.at[s, p], sem.at[s])

    def start_all(bb, s):
        def body(p, c):
            page_copy(bb, p, s).start()
            return c
        lax.fori_loop(0, n_pages, body, 0)

    @pl.when(b == 0)
    def _():
        start_all(b, slot)

    @pl.when(b + 1 < nb)
    def _():
        start_all(b + 1, 1 - slot)

    def wait_body(p, c):
        page_copy(b, p, slot).wait()
        return c
    lax.fori_loop(0, n_pages, wait_body, 0)

    rows = math.gcd(n_ch, 256)
    pages_per_group = rows // _CH_PER_PAGE

    def to_rows(g):
        for p in range(g * pages_per_group, (g + 1) * pages_per_group):
            page = buf[slot, p]
            page = page.T if pages_transposed else page
            for i in range(_CH_PER_PAGE):
                dst = (p * _CH_PER_PAGE + i) * _CH_PITCH
                rbuf[dst:dst + CMP_STRIDE, :] = page[i * CMP_STRIDE:(i + 1) * CMP_STRIDE, :]

    to_rows(0)
    for r in range(n_ch // rows):
        if r + 1 < n_ch // rows:
            to_rows(r + 1)
        for s in range(CMP_STRIDE):
            x = rbuf[pl.ds(r * rows * _CH_PITCH + s, rows, stride=_CH_PITCH), :]
            xa[:, s * KV_DIM:(s + 1) * KV_DIM] = x.astype(BF16)
        hbuf[r * rows:(r + 1) * rows, :] = jnp.dot(xa[...], w1_ref[...], preferred_element_type=F32)
    pw = _split_dot_rhs(pe_ref[...], w1_ref[...])
    nh = N_KV * CMP_HIDDEN
    bias = pw[0:1, 0:nh] + pw[1:2, nh:2 * nh]
    h = hbuf[:, 0:nh] + pltpu.roll(hbuf[:, nh:2 * nh], n_ch - 1, 0) + bias
    blocks = jnp.dot(jax.nn.gelu(h).astype(BF16), w2_ref[...], preferred_element_type=F32)
    row = lax.broadcasted_iota(jnp.int32, blocks.shape, 0)
    blocks = jnp.where(row < n_ch - 1, blocks, 0.0)
    out_ref[0] = blocks.astype(BF16)
    outt_ref[0] = blocks.T.astype(BF16)


def _compress(pool, page_table, pe, w1, w2, pages_transposed):
    nbatch, n_pages = page_table.shape
    n_ch = n_pages * _CH_PER_PAGE
    rows = math.gcd(n_ch, 256)
    grid_spec = pltpu.PrefetchScalarGridSpec(
        num_scalar_prefetch=1,
        grid=(nbatch,),
        in_specs=[pl.BlockSpec(memory_space=pl.ANY),
                  _const_spec((SUBLANES, _CH_W)),
                  _const_spec((_CH_W, 4 * CMP_HIDDEN)),
                  _const_spec((2 * CMP_HIDDEN, KV_DIM))],
        out_specs=[pl.BlockSpec((1, n_ch, KV_DIM), lambda b, pt: (b, 0, 0)),
                   pl.BlockSpec((1, KV_DIM, n_ch), lambda b, pt: (b, 0, 0))],
        scratch_shapes=[pltpu.VMEM((2, n_pages, PAGE, KV_DIM), F32),
                        pltpu.VMEM((n_ch * _CH_PITCH, KV_DIM), F32),
                        pltpu.VMEM((rows, _CH_W), BF16),
                        pltpu.VMEM((n_ch, 4 * CMP_HIDDEN), F32),
                        pltpu.SemaphoreType.DMA((2,))],
    )
    return pl.pallas_call(
        functools.partial(_compress_kernel, n_pages=n_pages, pages_transposed=pages_transposed),
        grid_spec=grid_spec,
        out_shape=[jax.ShapeDtypeStruct((nbatch, n_ch, KV_DIM), BF16),
                   jax.ShapeDtypeStruct((nbatch, KV_DIM, n_ch), BF16)],
        compiler_params=_cparams(("arbitrary",)),
        name="compress",
    )(page_table, pool, pe, w1, w2)


def _compress_weights(pe, w1, w2):
    c = CMP_BLOCK // CMP_STRIDE
    pe_r = pe.reshape(c, CMP_STRIDE, 1, HEAD_DIM)
    pe_x = jnp.broadcast_to(pe_r, (c, CMP_STRIDE, N_KV, HEAD_DIM)).reshape(c, _CH_W)
    pe_x = jnp.pad(pe_x, ((0, SUBLANES - c), (0, 0)))
    w1_r = w1.reshape(c, CMP_STRIDE, HEAD_DIM, CMP_HIDDEN)
    eye = jnp.eye(N_KV, dtype=w1.dtype)
    w1_x = jnp.einsum('jsde,kq->skdjqe', w1_r, eye).reshape(_CH_W, c * N_KV * CMP_HIDDEN)
    w2_x = jnp.einsum('ed,kq->keqd', w2, eye).reshape(N_KV * CMP_HIDDEN, KV_DIM)
    return pe_x, w1_x.astype(BF16), w2_x.astype(BF16)


def _hgrn_chunk(q, k, v, lf, st_ref, chunk, sub):
    if chunk > SUBLANES:
        r = lax.broadcasted_iota(jnp.int32, (chunk, chunk), 0)
        c = lax.broadcasted_iota(jnp.int32, (chunk, chunk), 1)
        tri = (r >= c).astype(F32)
        b = jnp.dot(tri, lf, preferred_element_type=F32, precision=HIGHEST)
    else:
        rows = [lf[0:1, :]]
        for t in range(1, chunk):
            rows.append(rows[-1] + lf[t:t + 1, :])
        b = jnp.concatenate(rows, axis=0)
    bl = b[chunk - 1:chunk, :]
    qe = q * jnp.exp(b)
    kd = k * jnp.exp(bl - b)
    ebl = jnp.exp(bl)
    n_sub = chunk // sub
    trow = lax.broadcasted_iota(jnp.int32, (sub, HG_W), 0)
    crow = lax.broadcasted_iota(jnp.int32, (chunk, HG_W), 0)

    diag = []
    for i in range(n_sub):
        qi = q[i * sub:(i + 1) * sub, :]
        bi = b[i * sub:(i + 1) * sub, :]
        acc = [jnp.zeros((sub, HG_DIM), F32) for _ in range(HG_HEADS)]
        for s in range(sub):
            row = i * sub + s
            dec = jnp.exp(jnp.where(trow >= s, bi - b[row:row + 1, :], -jnp.inf))
            prod = qi * k[row:row + 1, :] * dec
            for h in range(HG_HEADS):
                a = jnp.sum(prod[:, h * HG_DIM:(h + 1) * HG_DIM], axis=1, keepdims=True)
                acc[h] = acc[h] + a * v[row:row + 1, h * HG_DIM:(h + 1) * HG_DIM]
        diag.append(acc)

    off = []
    for i in range(n_sub):
        if i == 0:
            off.append(None)
            continue
        b0 = b[i * sub - 1:i * sub, :]
        qs = (q[i * sub:(i + 1) * sub, :] * jnp.exp(b[i * sub:(i + 1) * sub, :] - b0)).astype(BF16)
        ks = (k * jnp.exp(jnp.where(crow < i * sub, b0 - b, -jnp.inf))).astype(BF16)
        off.append((qs, ks))

    vb = v.astype(BF16)
    outs = []
    for h in range(HG_HEADS):
        sl = slice(h * HG_DIM, (h + 1) * HG_DIM)
        st = st_ref[h]
        o_h = lax.dot_general(qe[:, sl].astype(BF16), st.astype(BF16), (((1,), (1,)), ((), ())),
                              preferred_element_type=F32)
        parts = []
        for i in range(n_sub):
            d = diag[i][h]
            if off[i] is not None:
                qs, ks = off[i]
                a = lax.dot_general(qs[:, sl], ks[:, sl], (((1,), (1,)), ((), ())),
                                    preferred_element_type=F32)
                d = d + jnp.dot(a.astype(BF16), vb[:, sl], preferred_element_type=F32)
            parts.append(d)
        intra = parts[0] if n_sub == 1 else jnp.concatenate(parts, axis=0)
        outs.append(o_h + intra)
        st_ref[h] = st * ebl[:, sl] + lax.dot_general(
            vb[:, sl], kd[:, sl].astype(BF16), (((0,), (0,)), ((), ())), preferred_element_type=F32)
    return jnp.concatenate(outs, axis=1)


def _hgrn_finish(o, gs, gn):
    outs = []
    for h in range(HG_HEADS):
        oh = o[:, h * HG_DIM:(h + 1) * HG_DIM]
        y = oh * lax.rsqrt(jnp.mean(oh * oh, axis=-1, keepdims=True) + EPS) * gn
        outs.append(y)
    return (jnp.concatenate(outs, axis=1) * gs).astype(BF16)


def _hgrn_prompt_kernel(q_ref, k_ref, v_ref, lf_ref, gs_ref, gn_ref, o_ref, s_ref, st_ref, *, n_chunks):
    i = pl.program_id(0)

    @pl.when(i == 0)
    def _():
        st_ref[...] = jnp.zeros_like(st_ref)

    def body(c, carry):
        r = pl.ds(pl.multiple_of(c * HG_CHUNK, HG_CHUNK), HG_CHUNK)
        o = _hgrn_chunk(q_ref[r, :], k_ref[r, :], v_ref[r, :], lf_ref[r, :], st_ref, HG_CHUNK, HG_SUB)
        o_ref[r, :] = _hgrn_finish(o, gs_ref[r, :], gn_ref[...])
        return carry
    lax.fori_loop(0, n_chunks, body, 0, unroll=4)

    @pl.when(i == pl.num_programs(0) - 1)
    def _():
        for h in range(HG_HEADS):
            s_ref[h] = st_ref[h].T


def _hgrn_prompt(qh, kh, vh, lf, gs, gn, rows_per_step=512):
    t = qh.shape[0]
    spec = pl.BlockSpec((rows_per_step, HG_W), lambda i: (i, 0))
    return pl.pallas_call(
        functools.partial(_hgrn_prompt_kernel, n_chunks=rows_per_step // HG_CHUNK),
        grid=(t // rows_per_step,),
        in_specs=[spec] * 5 + [_const_spec((1, HG_DIM))],
        out_specs=[spec, pl.BlockSpec((HG_HEADS, HG_DIM, HG_DIM), lambda i: (0, 0, 0))],
        out_shape=[jax.ShapeDtypeStruct((t, HG_W), BF16),
                   jax.ShapeDtypeStruct((HG_HEADS, HG_DIM, HG_DIM), F32)],
        scratch_shapes=[pltpu.VMEM((HG_HEADS, HG_DIM, HG_DIM), F32)],
        compiler_params=_cparams(("arbitrary",)),
        name="hgrn_prompt",
    )(qh, kh, vh, lf, gs, gn)


def _hgrn_sample_kernel(q_ref, k_ref, v_ref, lf_ref, gs_ref, gn_ref, s0_ref, o_ref, s_ref, st_ref, *, t):
    for h in range(HG_HEADS):
        st_ref[h] = s0_ref[0, h].T
    o = _hgrn_chunk(q_ref[0], k_ref[0], v_ref[0], lf_ref[0], st_ref, t, t)
    o_ref[0] = _hgrn_finish(o, gs_ref[0], gn_ref[...])
    for h in range(HG_HEADS):
        s_ref[0, h] = st_ref[h].T


def _hgrn_sample(qh, kh, vh, lf, gs, gn, s0):
    nb, t, _ = qh.shape
    spec = pl.BlockSpec((1, t, HG_W), lambda b: (b, 0, 0))
    sspec = pl.BlockSpec((1, HG_HEADS, HG_DIM, HG_DIM), lambda b: (b, 0, 0, 0))
    return pl.pallas_call(
        functools.partial(_hgrn_sample_kernel, t=t),
        grid=(nb,),
        in_specs=[spec] * 5 + [_const_spec((1, HG_DIM)), sspec],
        out_specs=[spec, sspec],
        out_shape=[jax.ShapeDtypeStruct((nb, t, HG_W), BF16),
                   jax.ShapeDtypeStruct((nb, HG_HEADS, HG_DIM, HG_DIM), F32)],
        scratch_shapes=[pltpu.VMEM((HG_HEADS, HG_DIM, HG_DIM), F32)],
        compiler_params=_cparams(("arbitrary",)),
        name="hgrn_sample",
    )(qh, kh, vh, lf, gs, gn, s0)


def _select_topk(x, blk, n, always=None):
    nblk = x.shape[0]
    if always is None:
        sel = jnp.zeros_like(x)
    else:
        sel = jnp.where(always, 1.0, 0.0)
        x = jnp.where(always, -3e38, x)
    for _ in range(n):
        m = jnp.max(x, axis=0, keepdims=True)
        idx = jnp.min(jnp.where(x == m, blk, float(nblk)), axis=0, keepdims=True)
        pick = blk == idx
        sel = jnp.where(pick, 1.0, sel)
        x = jnp.where(pick, -3e38, x)
    return sel


def _softmax_cols(s, valid):
    m = jnp.max(jnp.where(valid, s, NEG), axis=0, keepdims=True)
    p = jnp.where(valid, jnp.exp2(s - m), 0.0)
    l = jnp.sum(p, axis=0, keepdims=True)
    return p * jnp.where(l > 0.0, 1.0 / l, 0.0)


def _split_dot(a_bf16, x):
    hi = x.astype(BF16)
    lo = (x - hi.astype(F32)).astype(BF16)
    return (jnp.dot(a_bf16, hi, preferred_element_type=F32)
            + jnp.dot(a_bf16, lo, preferred_element_type=F32))


def _split_dot_rhs(x, w_bf16):
    hi = x.astype(BF16)
    lo = (x - hi.astype(F32)).astype(BF16)
    return (jnp.dot(hi, w_bf16, preferred_element_type=F32)
            + jnp.dot(lo, w_bf16, preferred_element_type=F32))


def _nt(a, b):
    return lax.dot_general(a, b, (((1,), (1,)), ((), ())), preferred_element_type=F32)


def _online_update(s, v, m_ref, l_ref, acc_ref, v_transposed=False):
    m_old = m_ref[...]
    m_new = jnp.maximum(m_old, jnp.max(s, axis=1, keepdims=True))
    p = jnp.exp2(s - m_new)
    alpha = jnp.exp2(m_old - m_new)
    l_ref[...] = alpha * l_ref[...] + jnp.sum(p, axis=1, keepdims=True)
    pv = _nt(p.astype(BF16), v) if v_transposed else jnp.dot(p.astype(BF16), v, preferred_element_type=F32)
    acc_ref[...] = alpha * acc_ref[...] + pv
    m_ref[...] = m_new


def _score_matrix(n_rows, row_offset, n_blocks):
    c = CMP_BLOCK // CMP_STRIDE
    ratio = SEL_BLOCK // CMP_STRIDE
    n_ov = ratio + c - 1
    m = np.zeros((n_blocks, n_rows), np.float32)
    for j in range(n_blocks):
        for u in range(n_ov):
            start = CMP_STRIDE * (u - (c - 1))
            w_u = (min(start + CMP_BLOCK, SEL_BLOCK) - max(start, 0)) / CMP_STRIDE
            n = ratio * j + u - (c - 1)
            if 0 <= n and n + row_offset < n_rows:
                m[j, n + row_offset] = w_u
    return m


def _nsa_prompt_kernel(qt_ref, gt_ref, kc_ref, vct_ref, ks_ref, vst_ref, kw_ref, vwt_ref,
                       mt_ref, cc_ref, cs_ref, bw_ref, eb_ref, o_ref,
                       sc_ref, sw_ref, sel_ref, m_ref, acc_ref, oc_ref,
                       sa_ref, sb_ref, pa_ref, pb_ref, ala_ref, alb_ref, *, n_blocks, variants):
    qb = pl.program_id(0)
    nrow = kc_ref.shape[0]
    gq = GROUP * Q_BLOCK
    tiles_per_q = SEL_TILE // Q_BLOCK
    nband = WINDOW + Q_BLOCK
    max_tile = n_blocks // BLK_PER_TILE - 1
    frow = lax.broadcasted_iota(jnp.int32, (KV_DIM, Q_BLOCK), 0)

    def lanes4(x):
        return jnp.concatenate([x] * GROUP, axis=1)

    qx = []
    for kv in range(N_KV):
        keep = (frow >= HEAD_DIM) if kv else (frow < HEAD_DIM)
        qx.append(jnp.concatenate(
            [jnp.where(keep, qt_ref[j * KV_DIM:(j + 1) * KV_DIM, :], jnp.zeros((), BF16))
             for j in range(GROUP)], axis=1))

    def qk_stage(kt, s_ref, penalty=0.0):
        k = ks_ref[pl.ds(pl.multiple_of(kt * SEL_TILE, SEL_TILE), SEL_TILE), :]
        k_aug = jnp.concatenate([k, eb_ref[...]], axis=1)
        for kv in range(N_KV):
            srow = sel_ref[kv, pl.ds(pl.multiple_of(kt * BLK_PER_TILE, BLK_PER_TILE), BLK_PER_TILE), :]
            mrows = jnp.concatenate([lanes4(srow + penalty), jnp.zeros((KV_DIM - BLK_PER_TILE, gq), F32)], axis=0)
            q_aug = jnp.concatenate([qx[kv], mrows.astype(BF16)], axis=0)
            s_ref[kv] = jnp.dot(k_aug, q_aug, preferred_element_type=F32)

    def gate(kv, i):
        return jnp.concatenate(
            [gt_ref[3 * (kv * GROUP + j) + i:3 * (kv * GROUP + j) + i + 1, :] for j in range(GROUP)], axis=1)

    def compressed_and_select(n_r, n_b):
        for kv in range(N_KV):
            sc_ref[kv, 0:n_r, :] = jnp.dot(kc_ref[0:n_r, :], qx[kv], preferred_element_type=F32)
        r = lax.broadcasted_iota(jnp.int32, (n_r, Q_BLOCK), 0)
        qpos_c = qb * Q_BLOCK + lax.broadcasted_iota(jnp.int32, (n_r, Q_BLOCK), 1)
        end_pos = (r - CMP_PAD) * CMP_STRIDE + (CMP_BLOCK - 1)
        vis = (r >= CMP_PAD) & (r < CMP_PAD + n_blocks * (SEL_BLOCK // CMP_STRIDE) - 1) & (end_pos <= qpos_c)
        vis_add = lanes4(jnp.where(vis, 0.0, NEG))
        band = pl.ds(pl.multiple_of(qb * SUBLANES, SUBLANES), CMP_BAND)
        blk_i = lax.broadcasted_iota(jnp.int32, (n_b, Q_BLOCK), 0)
        cur = (qb * Q_BLOCK + lax.broadcasted_iota(jnp.int32, (n_b, Q_BLOCK), 1)) // SEL_BLOCK
        forced = (blk_i == 0) | (blk_i == cur) | (blk_i == cur - 1)
        scores = []
        for kv in range(N_KV):
            sc_ref[kv, band, :] = sc_ref[kv, band, :] + cc_ref[kv]
            s = sc_ref[kv, 0:n_r, :] + vis_add
            m = jnp.max(s, axis=0, keepdims=True)
            p = jnp.exp2(s - m)
            l = jnp.sum(p, axis=0, keepdims=True)
            pn = p * jnp.where(m > 0.5 * NEG, 1.0 / l, 0.0)
            oc_ref[kv] = gate(kv, 0) * jnp.dot(vct_ref[:, 0:n_r], pn.astype(BF16), preferred_element_type=F32)
            imp = pn[:, 0:Q_BLOCK]
            for j in range(1, GROUP):
                imp = imp + pn[:, j * Q_BLOCK:(j + 1) * Q_BLOCK]
            score = _split_dot(mt_ref[0:n_b, 0:n_r], imp)
            scores.append(jnp.where(blk_i <= cur, score, -FORCE_BONUS))
        kband = kw_ref[pl.ds(pl.multiple_of(qb * Q_BLOCK, Q_BLOCK), nband), :]
        for kv in range(N_KV):
            sw_ref[kv] = jnp.dot(kband, qx[kv], preferred_element_type=F32)
        blk2 = jnp.concatenate([blk_i.astype(F32)] * N_KV, axis=1)
        forced2 = jnp.concatenate([forced] * N_KV, axis=1)
        sel = _select_topk(jnp.concatenate(scores, axis=1), blk2, N_SEL - 3, always=forced2)
        for kv in range(N_KV):
            sel_ref[kv, 0:n_b, :] = (sel[:, kv * Q_BLOCK:(kv + 1) * Q_BLOCK] - 1.0) * (-NEG)

    sel_ref[...] = jnp.full_like(sel_ref, NEG)
    lo = 0
    for q_hi, n_r, n_b in variants:
        @pl.when((qb >= lo) & (qb <= q_hi))
        def _(n_r=n_r, n_b=n_b):
            compressed_and_select(n_r, n_b)
        lo = q_hi + 1

    m_ref[...] = jnp.full_like(m_ref, NEG)
    acc_ref[...] = jnp.zeros_like(acc_ref)

    def pv_stage(kt, p_ref, al_ref):
        for kv in range(N_KV):
            acc_ref[kv] = al_ref[kv] * acc_ref[kv] + jnp.dot(vst_ref[kv, kt], p_ref[kv],
                                                             preferred_element_type=F32)

    def sm_stage(kt, s_ref, p_ref, al_ref, near):
        d0 = qb - kt * tiles_per_q
        for kv in range(N_KV):
            s = s_ref[kv]
            if near:
                s = s + jnp.concatenate(
                    [cs_ref[kv, jnp.clip(d0 - i, -1, 2) + 1] for i in range(tiles_per_q)], axis=0)
            m_old = m_ref[kv]
            m_new = jnp.maximum(m_old, jnp.max(s, axis=0, keepdims=True))
            p_ref[kv] = jnp.exp2(s - m_new).astype(BF16)
            al_ref[kv] = jnp.exp2(m_old - m_new)
            m_ref[kv] = m_new

    n_far = jnp.maximum(qb - 1, 0) // tiles_per_q
    pb_ref[...] = jnp.zeros_like(pb_ref)
    alb_ref[...] = jnp.ones_like(alb_ref)
    qk_stage(0, sa_ref)

    vband = jnp.concatenate([vwt_ref[qb + i] for i in range(nband // Q_BLOCK)], axis=1)
    krow = lax.broadcasted_iota(jnp.int32, (nband, Q_BLOCK), 0)
    pos_add = lanes4(jnp.where(krow >= WINDOW - qb * Q_BLOCK, 0.0, NEG))
    for kv in range(N_KV):
        s = sw_ref[kv] + bw_ref[kv] + pos_add
        m = jnp.max(s, axis=0, keepdims=True)
        p = jnp.exp2(s - m)
        o_w = (jnp.dot(vband, p.astype(BF16), preferred_element_type=F32)
               / jnp.sum(p, axis=0, keepdims=True))
        oc_ref[kv] = oc_ref[kv] + gate(kv, 2) * o_w

    def pair_body(u, c):
        t0 = 2 * u
        t1 = jnp.minimum(t0 + 1, max_tile)
        t2 = jnp.minimum(t0 + 2, max_tile)
        qk_stage(t1, sb_ref, jnp.where(t0 + 1 < n_far, 0.0, NEG))
        pv_stage(jnp.maximum(t0 - 1, 0), pb_ref, alb_ref)
        sm_stage(t0, sa_ref, pa_ref, ala_ref, False)
        qk_stage(t2, sa_ref)
        pv_stage(t0, pa_ref, ala_ref)
        sm_stage(t1, sb_ref, pb_ref, alb_ref, False)
        return c
    n_pairs = (n_far + 1) // 2
    lax.fori_loop(0, n_pairs, pair_body, 0)

    near_a = n_far
    near_b = jnp.minimum(n_far + 1, max_tile)
    qk_stage(near_a, sa_ref)
    pv_stage(jnp.clip(2 * n_pairs - 1, 0, max_tile), pb_ref, alb_ref)
    sm_stage(near_a, sa_ref, pa_ref, ala_ref, True)
    pv_stage(near_a, pa_ref, ala_ref)

    @pl.when(n_far + 1 <= qb // tiles_per_q)
    def _():
        qk_stage(near_b, sb_ref)
        sm_stage(near_b, sb_ref, pb_ref, alb_ref, True)
        pv_stage(near_b, pb_ref, alb_ref)

    o_kv = []
    for kv in range(N_KV):
        acc = acc_ref[kv]
        o_s = acc[0:HEAD_DIM, :] / acc[HEAD_DIM:HEAD_DIM + 1, :]
        o_kv.append(oc_ref[kv, kv * HEAD_DIM:(kv + 1) * HEAD_DIM, :] + gate(kv, 1) * o_s)
    o_t = jnp.concatenate(o_kv, axis=0)
    for j in range(GROUP):
        o_ref[:, j * LANES:(j + 1) * LANES] = o_t[:, j * Q_BLOCK:(j + 1) * Q_BLOCK].T.astype(BF16)


def _nsa_prompt(qat, gat, kc, vct, ksb, vst3, kwb, vwt3, mt, cc, cs, bw, eb):
    t = qat.shape[1]
    n_blocks = t // SEL_BLOCK
    nrow = kc.shape[0]
    gq = GROUP * Q_BLOCK
    n_q = t // Q_BLOCK
    variants = []
    for n_r in list(range(2 * LANES, nrow - LANES, 2 * LANES)) + [nrow]:
        q_hi = n_q - 1 if n_r == nrow else min((n_r - CMP_BAND) // SUBLANES, n_q - 1)
        n_b = min(-(-(2 * q_hi + 2) // (2 * SUBLANES)) * (2 * SUBLANES), n_blocks)
        variants.append((q_hi, n_r, n_b))
    return pl.pallas_call(
        functools.partial(_nsa_prompt_kernel, n_blocks=n_blocks, variants=tuple(variants)),
        grid=(t // Q_BLOCK,),
        in_specs=[pl.BlockSpec((Q_DIM, Q_BLOCK), lambda i: (0, i)),
                  pl.BlockSpec((LANES, Q_BLOCK), lambda i: (0, i)),
                  _const_spec(kc.shape), _const_spec(vct.shape),
                  _const_spec(ksb.shape), _const_spec(vst3.shape),
                  _const_spec(kwb.shape), _const_spec(vwt3.shape),
                  _const_spec(mt.shape), _const_spec(cc.shape),
                  _const_spec(cs.shape), _const_spec(bw.shape), _const_spec(eb.shape)],
        out_specs=pl.BlockSpec((Q_BLOCK, Q_DIM), lambda i: (i, 0)),
        out_shape=jax.ShapeDtypeStruct((t, Q_DIM), BF16),
        scratch_shapes=[pltpu.VMEM((N_KV, nrow, gq), F32),
                        pltpu.VMEM((N_KV, WINDOW + Q_BLOCK, gq), F32),
                        pltpu.VMEM((N_KV, n_blocks, Q_BLOCK), F32),
                        pltpu.VMEM((N_KV, 1, gq), F32),
                        pltpu.VMEM((N_KV, vst3.shape[2], gq), F32),
                        pltpu.VMEM((N_KV, KV_DIM, gq), F32),
                        pltpu.VMEM((N_KV, SEL_TILE, gq), F32), pltpu.VMEM((N_KV, SEL_TILE, gq), F32),
                        pltpu.VMEM((N_KV, SEL_TILE, gq), BF16), pltpu.VMEM((N_KV, SEL_TILE, gq), BF16),
                        pltpu.VMEM((N_KV, 1, gq), F32), pltpu.VMEM((N_KV, 1, gq), F32)],
        compiler_params=_cparams(("arbitrary",)),
        name="nsa_prompt",
    )(qat, gat, kc, vct, ksb, vst3, kwb, vwt3, mt, cc, cs, bw, eb)


S_ROWS = 128
S_CHUNK_PAGES = 64
S_CHUNK = S_CHUNK_PAGES * PAGE


def _nsa_sample_kernel(pt_ref, qx_ref, gm_ref, kc_ref, vct_ref, kpool_ref, vpool_ref,
                       knew_ref, vnew_ref, kwin_ref, vwin_ref, kwnew_ref, vwnew_ref,
                       mt_ref, gsum_ref, ccs_ref, css_ref, cns_ref, cws_ref, o_ref,
                       kbuf, vbuf, sem, mask_ref, m_ref, l_ref, acc_ref, oc_ref, ow_ref,
                       *, n_chunks, n_blocks):
    b = pl.program_id(0)
    c = pl.program_id(1)
    step = b * n_chunks + c
    total = pl.num_programs(0) * n_chunks
    slot = step % 2

    def copies(bb, cc, s, p):
        pg = cc * S_CHUNK_PAGES + p
        dst = pl.ds(pl.multiple_of(p * PAGE, PAGE), PAGE)
        return (pltpu.make_async_copy(kpool_ref.at[pt_ref[bb, pg]], kbuf.at[s, :, dst], sem.at[0, s]),
                pltpu.make_async_copy(vpool_ref.at[pt_ref[bb, pg]], vbuf.at[s, :, dst], sem.at[1, s]))

    def start_all(st, s):
        bb = st // n_chunks
        cc = st % n_chunks

        def body(p, carry):
            ck, cv = copies(bb, cc, s, p)
            ck.start()
            cv.start()
            return carry
        lax.fori_loop(0, S_CHUNK_PAGES, body, 0)

    @pl.when(step == 0)
    def _():
        start_all(step, slot)

    @pl.when(step + 1 < total)
    def _():
        start_all(step + 1, 1 - slot)

    qx = qx_ref[0]
    n_q = o_ref.shape[1]
    qq = qx[:n_q]

    @pl.when(c == 0)
    def _():
        nrow = kc_ref.shape[1]
        s = _nt(kc_ref[0], qx)
        r = lax.broadcasted_iota(jnp.int32, (nrow, S_ROWS), 0)
        band0 = nrow - CMP_BAND
        s = s + jnp.concatenate([jnp.zeros((band0, S_ROWS), F32), ccs_ref[...]], axis=0)
        pn = _softmax_cols(s, r < nrow - 1)
        oc_ref[...] = jnp.dot(vct_ref[0], pn.astype(BF16), preferred_element_type=F32).T[:n_q]
        imp = jnp.dot(pn, gsum_ref[...], preferred_element_type=F32, precision=HIGHEST)
        score = _split_dot(mt_ref[...], imp)
        blk_i = lax.broadcasted_iota(jnp.int32, (n_blocks, S_ROWS), 0)
        forced = (blk_i == 0) | (blk_i == n_blocks - 1)
        selt = _select_topk(score, blk_i.astype(F32), N_SEL - 3, always=forced)
        selt = lax.dot_general(selt.astype(BF16), gsum_ref[...].astype(BF16), (((1,), (1,)), ((), ())),
                               preferred_element_type=F32)
        selm1 = selt.T[:n_q] - 1.0
        e_r = lax.broadcasted_iota(jnp.int32, (BLK_PER_TILE, SEL_TILE), 0)
        e_c = lax.broadcasted_iota(jnp.int32, (BLK_PER_TILE, SEL_TILE), 1)
        expand = jnp.where(e_c // SEL_BLOCK == e_r, -NEG, 0.0)
        for kt in range(n_blocks // BLK_PER_TILE):
            mask_ref[:, kt * SEL_TILE:(kt + 1) * SEL_TILE] = jnp.dot(
                selm1[:, kt * BLK_PER_TILE:(kt + 1) * BLK_PER_TILE], expand, preferred_element_type=F32)

        sw = _nt(qq, kwin_ref[0].astype(BF16)) + cws_ref[:n_q, :]
        sn = _nt(qq, kwnew_ref[0].astype(BF16)) + cns_ref[:n_q, :]
        m = jnp.maximum(jnp.max(sw, axis=1, keepdims=True), jnp.max(sn, axis=1, keepdims=True))
        pw = jnp.exp2(sw - m)
        pn2 = jnp.exp2(sn - m)
        l = jnp.sum(pw, axis=1, keepdims=True) + jnp.sum(pn2, axis=1, keepdims=True)
        ow = (jnp.dot(pw.astype(BF16), vwin_ref[0].astype(BF16), preferred_element_type=F32)
              + jnp.dot(pn2.astype(BF16), vwnew_ref[0].astype(BF16), preferred_element_type=F32))
        ow_ref[...] = ow / l

        m_ref[...] = jnp.full_like(m_ref, NEG)
        l_ref[...] = jnp.zeros_like(l_ref)
        acc_ref[...] = jnp.zeros_like(acc_ref)

    def wait_body(p, carry):
        ck, cv = copies(b, c, slot, p)
        ck.wait()
        cv.wait()
        return carry
    lax.fori_loop(0, S_CHUNK_PAGES, wait_body, 0)

    col0 = pl.multiple_of(c * S_CHUNK, S_CHUNK)
    s = (jnp.dot(qq, kbuf[slot].astype(BF16), preferred_element_type=F32)
         + mask_ref[:, pl.ds(col0, S_CHUNK)])

    @pl.when(c < n_chunks - 1)
    def _():
        _online_update(s, vbuf[slot].astype(BF16), m_ref, l_ref, acc_ref, v_transposed=True)

    @pl.when(c == n_chunks - 1)
    def _():
        near = jnp.concatenate([jnp.zeros((n_q, S_CHUNK - LANES), F32), css_ref[:n_q, :]], axis=1)
        _online_update(s + near, vbuf[slot].astype(BF16), m_ref, l_ref, acc_ref, v_transposed=True)
        sn = _nt(qq, knew_ref[0].astype(BF16)) + cns_ref[:n_q, :]
        _online_update(sn, vnew_ref[0].astype(BF16), m_ref, l_ref, acc_ref)
        o_s = acc_ref[...] / l_ref[...]
        o_ref[0] = gm_ref[0, 0] * oc_ref[...] + gm_ref[0, 1] * o_s + gm_ref[0, 2] * ow_ref[...]


def _nsa_sample(page_table, qx, gm, kc, vct, kpool, vpool, knew, vnew, kwin, vwin, kwnew, vwnew,
                mt, gsum, ccs, css, cns, cws):
    nb = qx.shape[0]
    n_pages = page_table.shape[1]
    n_chunks = n_pages // S_CHUNK_PAGES
    past = n_pages * PAGE
    n_blocks = past // SEL_BLOCK
    n_q = gm.shape[2]

    def bspec(shape):
        nd = len(shape)
        return pl.BlockSpec((1,) + tuple(shape[1:]), lambda b, c, pt: (b,) + (0,) * (nd - 1))

    def cspec(shape):
        nd = len(shape)
        return pl.BlockSpec(tuple(shape), lambda b, c, pt: (0,) * nd, pipeline_mode=pl.Buffered(1))

    grid_spec = pltpu.PrefetchScalarGridSpec(
        num_scalar_prefetch=1,
        grid=(nb, n_chunks),
        in_specs=[bspec(qx.shape), bspec(gm.shape), bspec(kc.shape), bspec(vct.shape),
                  pl.BlockSpec(memory_space=pl.ANY), pl.BlockSpec(memory_space=pl.ANY),
                  bspec(knew.shape), bspec(vnew.shape), bspec(kwin.shape), bspec(vwin.shape),
                  bspec(kwnew.shape), bspec(vwnew.shape),
                  cspec(mt.shape), cspec(gsum.shape), cspec(ccs.shape), cspec(css.shape),
                  cspec(cns.shape), cspec(cws.shape)],
        out_specs=pl.BlockSpec((1, n_q, KV_DIM), lambda b, c, pt: (b, 0, 0)),
        scratch_shapes=[pltpu.VMEM((2, KV_DIM, S_CHUNK), F32),
                        pltpu.VMEM((2, KV_DIM, S_CHUNK), F32),
                        pltpu.SemaphoreType.DMA((2, 2)),
                        pltpu.VMEM((n_q, past), F32),
                        pltpu.VMEM((n_q, 1), F32), pltpu.VMEM((n_q, 1), F32),
                        pltpu.VMEM((n_q, KV_DIM), F32),
                        pltpu.VMEM((n_q, KV_DIM), F32), pltpu.VMEM((n_q, KV_DIM), F32)],
    )
    return pl.pallas_call(
        functools.partial(_nsa_sample_kernel, n_chunks=n_chunks, n_blocks=n_blocks),
        grid_spec=grid_spec,
        out_shape=jax.ShapeDtypeStruct((nb, n_q, KV_DIM), F32),
        compiler_params=_cparams(("arbitrary", "arbitrary")),
        name="nsa_sample",
    )(page_table, qx, gm, kc, vct, kpool, vpool, knew, vnew, kwin, vwin, kwnew, vwnew,
      mt, gsum, ccs, css, cns, cws)


def _ffn_kernel(x_ref, oa_ref, ob_ref, sa_ref, sb_ref, wpa_ref, wpb_ref, wo_ref, nf_ref,
                wg_ref, wu_ref, wd_ref, nl_ref, y_ref):
    pa = jnp.dot(oa_ref[...], wpa_ref[...], preferred_element_type=F32)
    pb = jnp.dot(ob_ref[...], wpb_ref[...], preferred_element_type=F32)
    merged = sa_ref[...] * pa + sb_ref[...] * pb
    x = x_ref[...] + jnp.dot(merged.astype(BF16), wo_ref[...], preferred_element_type=F32)
    hn = (x * lax.rsqrt(jnp.mean(x * x, axis=-1, keepdims=True) + EPS) * nf_ref[...]).astype(BF16)
    gate = jnp.dot(hn, wg_ref[...], preferred_element_type=F32)
    up = jnp.dot(hn, wu_ref[...], preferred_element_type=F32)
    ff = (jax.nn.silu(gate) * up).astype(BF16)
    x = x + jnp.dot(ff, wd_ref[...], preferred_element_type=F32)
    y_ref[...] = x * lax.rsqrt(jnp.mean(x * x, axis=-1, keepdims=True) + EPS) * nl_ref[...]


def _ffn(x2d, oa, ob, sa, sb, wpa, wpb, wo, nf, wg, wu, wd, nl, tm):
    rows = x2d.shape[0]

    def rspec(n):
        return pl.BlockSpec((tm, n), lambda i: (i, 0))

    return pl.pallas_call(
        _ffn_kernel,
        grid=(rows // tm,),
        in_specs=[rspec(D_MODEL), rspec(Q_DIM), rspec(HG_W), rspec(D_MODEL), rspec(D_MODEL),
                  _const_spec(wpa.shape), _const_spec(wpb.shape), _const_spec(wo.shape),
                  _const_spec(nf.shape), _const_spec(wg.shape), _const_spec(wu.shape),
                  _const_spec(wd.shape), _const_spec(nl.shape)],
        out_specs=rspec(D_MODEL),
        out_shape=jax.ShapeDtypeStruct((rows, D_MODEL), F32),
        compiler_params=_cparams(("arbitrary",)),
        name="ffn",
    )(x2d, oa, ob, sa, sb, wpa, wpb, wo, nf, wg, wu, wd, nl)


def _pack_w_in(w_in):
    sizes = (Q_DIM,) + (KV_DIM,) * 6 + (3 * N_HEADS,) + (HG_W,) * 4 + (D_MODEL,) * 2
    offs = np.concatenate([[0], np.cumsum(sizes)])
    q = w_in[:, offs[0]:offs[1]].reshape(D_MODEL, N_HEADS, HEAD_DIM)[:, _HEAD_PERM, :].reshape(D_MODEL, Q_DIM)
    g = jnp.pad(w_in[:, offs[7]:offs[8]], ((0, 0), (0, LANES - 3 * N_HEADS)))
    return jnp.concatenate([q, w_in[:, offs[1]:offs[7]], g, w_in[:, offs[8]:]], axis=1).astype(BF16)


def _strip(bvc, rel, lo=0, hi=None, masked=NEG):
    val = bvc[:, np.clip(rel, 0, 255)]
    ok = rel >= lo
    if hi is not None:
        ok = ok & (rel < hi)
    return jnp.where(jnp.asarray(ok)[None], val, masked)


def _toeplitz(bvc, a, n_rows, n_cols, lo=0, hi=None):
    n = n_rows + n_cols - 1
    u = _strip(bvc, a - (n_rows - 1) + np.arange(n), lo, hi)
    u = jnp.pad(u, ((0, 0), (0, 1)))
    circ = jnp.tile(u, (1, n_rows))[:, :n_rows * n].reshape(N_HEADS, n_rows, n)
    return circ[:, :, n_rows - 1:n_rows - 1 + n_cols]


def _bias_strips_prompt(bvc):
    gq = GROUP * Q_BLOCK

    def lanes(x):
        return x.reshape(N_KV, GROUP, x.shape[1], Q_BLOCK).transpose(0, 2, 1, 3).reshape(N_KV, x.shape[1], gq)

    cs = jnp.stack([lanes(_toeplitz(bvc, Q_BLOCK * d, Q_BLOCK, Q_BLOCK)) for d in (-1, 0, 1, 2)], axis=1)
    bw = lanes(jnp.concatenate(
        [_toeplitz(bvc, Q_BLOCK * d, Q_BLOCK, Q_BLOCK, 0, WINDOW) for d in range(WINDOW // Q_BLOCK, -1, -1)], axis=1))
    rr = np.arange(CMP_BAND)[:, None]
    rel_c = np.arange(Q_BLOCK)[None, :] - CMP_STRIDE * (rr - CMP_PAD) - (CMP_BLOCK - 1)
    cc = _strip(bvc, rel_c, masked=0.0).reshape(N_KV, GROUP, CMP_BAND, Q_BLOCK)
    cc = cc.transpose(0, 2, 1, 3).reshape(N_KV, CMP_BAND, gq)
    return cc, cs, bw


def _bias_strips_sample(bvc, past, t):
    def rows(a):
        a = a.reshape(N_HEADS * t, a.shape[-1])
        return jnp.pad(a, ((0, S_ROWS - N_HEADS * t), (0, 0)))
    tt = np.arange(t)[:, None]
    nrow = past // CMP_STRIDE
    n = (nrow - CMP_BAND + np.arange(CMP_BAND))[None, :]
    ccs = rows(_strip(bvc, past + tt - CMP_STRIDE * n - (CMP_BLOCK - 1), masked=0.0)).T
    i = np.arange(LANES)[None, :]
    css = rows(_strip(bvc, LANES + tt - i))
    cns = rows(_strip(bvc, np.where(i < t, tt - i, -1)))
    iw = np.arange(WINDOW)[None, :]
    cws = rows(_strip(bvc, WINDOW + tt - iw, 0, WINDOW))
    return ccs, css, cns, cws


def kernel(x_prompt, x_sample, cache_k_cmp, cache_v_cmp, cache_k_slc, cache_v_slc, state_k_win, state_v_win,
           state_hgrn, page_table, norm_mix, w_in, cmp_pe_k, cmp_w1_k, cmp_w2_k, cmp_pe_v, cmp_w1_v, cmp_w2_v,
           rel_bias, hg_lb_logits, hg_norm, w_proj_a, w_proj_b, w_out, norm_ffn, w_gate, w_up, w_down, norm_final):
    nbp, t_p, _ = x_prompt.shape
    nbs, t_s, _ = x_sample.shape
    assert nbp == 1 and norm_mix.shape[0] == 1
    n_pages = page_table.shape[1]
    past = n_pages * PAGE
    assert state_k_win.shape[2] == WINDOW and past % S_CHUNK == 0 and t_s <= SUBLANES

    lb = jnp.cumsum(jax.nn.softmax(hg_lb_logits.astype(F32), axis=0), axis=0)[0]
    lb3 = jnp.pad(jnp.stack([jnp.log(lb), jnp.log1p(-lb), 1.0 - lb]), ((0, SUBLANES - 3), (0, 0)))
    w_pack = _pack_w_in(w_in[0])
    g_mix = norm_mix[0][None, :]
    wpa = w_proj_a[0].reshape(N_HEADS, HEAD_DIM, D_MODEL)[_HEAD_PERM].reshape(Q_DIM, D_MODEL).astype(BF16)
    wpb = w_proj_b[0].astype(BF16)
    wo = w_out[0].astype(BF16)
    wg, wu, wd = w_gate[0].astype(BF16), w_up[0].astype(BF16), w_down[0].astype(BF16)
    nf, nl = norm_ffn[0][None, :], norm_final[None, :]
    gn = hg_norm[0][None, :]
    pe_k, w1_k, w2_k = _compress_weights(cmp_pe_k[0], cmp_w1_k[0], cmp_w2_k[0])
    pe_v, w1_v, w2_v = _compress_weights(cmp_pe_v[0], cmp_w1_v[0], cmp_w2_v[0])
    bvc = (rel_bias[_BUCKET] - rel_bias[N_BUCKETS - 1][None, :]).T * LOG2E
    cc, cs, bw = _bias_strips_prompt(bvc)
    ccs, css, cns, cws = _bias_strips_sample(bvc, past, t_s)

    xp2 = x_prompt.reshape(t_p, D_MODEL)
    xs2 = x_sample.reshape(nbs * t_s, D_MODEL)
    seg = lambda off, n: w_pack[:, off:off + n]
    w_t = jnp.concatenate([seg(_OFF_Q, Q_DIM), seg(_OFF_G, LANES), seg(_OFF_KV + 3 * KV_DIM, KV_DIM),
                           seg(_OFF_KV + 5 * KV_DIM, KV_DIM)], axis=1).T
    pp = _proj(xp2, g_mix, w_pack, w_t, lb3, 512)
    ps = _proj(xs2, g_mix, w_pack, w_t, lb3, nbs * t_s)
    (_, kc_p, vc_p, ks_p, vs_p, kw_p, vw_p, ksb_p, kwb_p, _,
     qh_p, lf_p, kh_p, vh_p, gs_p, sa_p, sb_p, qat_p, gat_p, vst_p, vwt_p) = pp
    (qa_s, kc_s, vc_s, ks_s, vs_s, kw_s, vw_s, _, _, ga_s,
     qh_s, lf_s, kh_s, vh_s, gs_s, sa_s, sb_s, _, _, _, _) = ps

    ob_p, s_p = _hgrn_prompt(qh_p, kh_p, vh_p, lf_p, gs_p, gn)
    r3 = lambda a: a.reshape(nbs, t_s, a.shape[-1])
    ob_s, s_s = _hgrn_sample(r3(qh_s), r3(kh_s), r3(vh_s), r3(lf_s), r3(gs_s), gn, state_hgrn[0])

    ident = jnp.arange(t_p // PAGE, dtype=jnp.int32)[None, :]
    pool_rows = lambda a: a.reshape(-1, PAGE, KV_DIM)
    kcb_p, _ = _compress(pool_rows(kc_p), ident, pe_k, w1_k, w2_k, False)
    _, vct_p = _compress(pool_rows(vc_p), ident, pe_v, w1_v, w2_v, False)
    back = -(CMP_PAD + kcb_p.shape[1]) % LANES
    kcb_p = jnp.pad(kcb_p[0], ((CMP_PAD, back), (0, 0)))
    vct_p = jnp.pad(vct_p[0], ((0, 0), (CMP_PAD, back)))
    nrow_p = kcb_p.shape[0]
    mt_p = jnp.asarray(_score_matrix(nrow_p, CMP_PAD, t_p // SEL_BLOCK), BF16)
    kwb_pad = jnp.pad(kwb_p, ((WINDOW, 0), (0, 0)))
    vst3 = vst_p.reshape(KV_DIM, t_p // SEL_TILE, SEL_TILE).transpose(1, 0, 2)
    ones_rows = jnp.ones((t_p // SEL_TILE, 2 * SUBLANES, SEL_TILE), BF16)
    vst3 = jnp.stack([jnp.concatenate([vst3[:, kv * HEAD_DIM:(kv + 1) * HEAD_DIM, :], ones_rows], axis=1)
                      for kv in range(N_KV)])
    vwt3 = jnp.pad(vwt_p, ((0, 0), (WINDOW, 0))).reshape(KV_DIM, (t_p + WINDOW) // Q_BLOCK, Q_BLOCK)
    vwt3 = vwt3.transpose(1, 0, 2)
    eb = (np.arange(LANES)[None, :] == np.arange(SEL_TILE)[:, None] // SEL_BLOCK).astype(np.float32)
    oa_p = _nsa_prompt(qat_p, gat_p, kcb_p, vct_p, ksb_p, vst3, kwb_pad, vwt3, mt_p, cc, cs, bw,
                       jnp.asarray(eb, BF16))

    pool_t = lambda a: a.transpose(0, 2, 3, 1).reshape(-1, KV_DIM, PAGE)
    kcb_s, _ = _compress(pool_t(cache_k_cmp[0]), page_table, pe_k, w1_k, w2_k, True)
    _, vct_s = _compress(pool_t(cache_v_cmp[0]), page_table, pe_v, w1_v, w2_v, True)
    nq = N_HEADS * t_s
    qs4 = qa_s.reshape(nbs, t_s, GROUP, N_KV, HEAD_DIM).astype(F32)
    qx = jnp.einsum('btjkd,kq->bkjtqd', qs4, jnp.eye(N_KV, dtype=F32)).reshape(nbs, nq, KV_DIM)
    qx = jnp.pad(qx, ((0, 0), (0, S_ROWS - nq), (0, 0))).astype(BF16)
    g4 = ga_s[:, :3 * N_HEADS].reshape(nbs, t_s, N_KV, GROUP, 3)
    gm = jnp.transpose(g4, (0, 4, 2, 3, 1)).reshape(nbs, 3, nq, 1)
    gm = jnp.broadcast_to(gm, (nbs, 3, nq, KV_DIM))
    new_tile = lambda a: jnp.pad(a.reshape(nbs, t_s, KV_DIM), ((0, 0), (0, LANES - t_s), (0, 0)))
    mt_s = jnp.asarray(_score_matrix(past // CMP_STRIDE, 0, past // SEL_BLOCK), BF16)
    gsum = np.zeros((S_ROWS, S_ROWS), np.float32)
    for kv in range(N_KV):
        for j in range(GROUP):
            for t in range(t_s):
                gsum[(kv * GROUP + j) * t_s + t, kv * t_s + t] = 1.0
    o_kv = _nsa_sample(page_table, qx, gm, kcb_s, vct_s,
                       pool_t(cache_k_slc[0]), pool_t(cache_v_slc[0]),
                       new_tile(ks_s), new_tile(vs_s), state_k_win[0].reshape(nbs, WINDOW, KV_DIM),
                       state_v_win[0].reshape(nbs, WINDOW, KV_DIM), new_tile(kw_s), new_tile(vw_s),
                       mt_s, jnp.asarray(gsum), ccs, css, cns, cws)
    o5 = o_kv.reshape(nbs, N_KV, GROUP, t_s, N_KV, HEAD_DIM)
    oa_s = jnp.einsum('bkjtqd,kq->btjkd', o5, jnp.eye(N_KV, dtype=F32)).reshape(nbs * t_s, Q_DIM).astype(BF16)

    y_p = _ffn(xp2, oa_p, ob_p, sa_p, sb_p, wpa, wpb, wo, nf, wg, wu, wd, nl, 256)
    y_s = _ffn(xs2, oa_s, ob_s.reshape(nbs * t_s, HG_W), sa_s, sb_s, wpa, wpb, wo, nf, wg, wu, wd, nl, nbs * t_s)

    kv5 = lambda a, nb_, tt: a.reshape(1, nb_, tt, N_KV, HEAD_DIM)
    wl = min(WINDOW, t_p)
    win = lambda st, new: jnp.concatenate(
        [st[0], new.reshape(nbs, t_s, N_KV, HEAD_DIM)], axis=1)[:, -WINDOW:][None]
    return (y_p.reshape(1, t_p, D_MODEL), y_s.reshape(nbs, t_s, D_MODEL),
            kv5(kc_p, 1, t_p), kv5(vc_p, 1, t_p), kv5(ks_p, 1, t_p), kv5(vs_p, 1, t_p),
            kv5(kw_p[-wl:], 1, wl), kv5(vw_p[-wl:], 1, wl), s_p[None, None],
            kv5(kc_s, nbs, t_s), kv5(vc_s, nbs, t_s), kv5(ks_s, nbs, t_s), kv5(vs_s, nbs, t_s),
            win(state_k_win, kw_s), win(state_v_win, vw_s), s_s[None])
```

```python
import functools
import math

import numpy as np
import jax
import jax.numpy as jnp
from jax import lax
from jax.experimental import pallas as pl
from jax.experimental.pallas import tpu as pltpu

F32 = jnp.float32
BF16 = jnp.bfloat16
HIGHEST = lax.Precision.HIGHEST

D_MODEL = 1024
N_HEADS = 8
N_KV = 2
GROUP = N_HEADS // N_KV
HEAD_DIM = 64
KV_DIM = N_KV * HEAD_DIM
Q_DIM = N_HEADS * HEAD_DIM
CMP_BLOCK = 32
CMP_STRIDE = 16
CMP_HIDDEN = 2 * HEAD_DIM
SEL_BLOCK = 64
N_SEL = 16
WINDOW = 512
Q_BLOCK = 128
FORCE_BONUS = 1e4
N_BUCKETS = 32
MAX_DISTANCE = 128
HG_HEADS = 4
HG_DIM = 128
HG_CHUNK = 64
HG_SUB = 16
HG_W = HG_HEADS * HG_DIM
D_FF = ((8 * D_MODEL // 3 + 255) // 256) * 256
EPS = 1e-6
PAGE = 128
NEG = -1e30
LOG2E = math.log2(math.e)
Q_SCALE = HEAD_DIM ** -0.5 * LOG2E

LANES = 128
SUBLANES = 8
VMEM_LIMIT = 56 * 1024 * 1024

_OFF_Q = 0
_OFF_KV = _OFF_Q + Q_DIM
_OFF_G = _OFF_KV + 6 * KV_DIM
_OFF_HG = _OFF_G + LANES
_OFF_GATE = _OFF_HG + 4 * HG_W
_PROJ_N = _OFF_GATE + 2 * D_MODEL

_HEAD_PERM = np.array([h for j in range(GROUP) for h in (j, GROUP + j)])

SEL_TILE = 512
BLK_PER_TILE = SEL_TILE // SEL_BLOCK
CMP_PAD = 16
CMP_BAND = 24


def _cparams(sem, vmem=VMEM_LIMIT):
    return pltpu.CompilerParams(dimension_semantics=sem, vmem_limit_bytes=vmem)


def _const_spec(shape):
    nd = len(shape)
    return pl.BlockSpec(shape, lambda *_: (0,) * nd, pipeline_mode=pl.Buffered(1))


def _bucket_table():
    n = np.arange(256)
    max_exact = N_BUCKETS // 2
    nf = np.maximum(n, 1).astype(np.float64)
    large = max_exact + (np.log(nf / max_exact) / math.log(MAX_DISTANCE / max_exact)
                         * (N_BUCKETS - max_exact)).astype(np.int64)
    large = np.minimum(large, N_BUCKETS - 1)
    return np.where(n < max_exact, n, large)


_BUCKET = _bucket_table()


def _proj_kernel(x_ref, g_ref, w_ref, wt_ref, lb_ref, *out_refs, prompt):
    if prompt:
        (kc_ref, vc_ref, ks_ref, vs_ref, kw_ref, vw_ref, ksb_ref, kwb_ref,
         qh_ref, lf_ref, kh_ref, vh_ref, gs_ref, sa_ref, sb_ref,
         qat_ref, gat_ref, vst_ref, vwt_ref) = out_refs
    else:
        (qa_ref, kc_ref, vc_ref, ks_ref, vs_ref, kw_ref, vw_ref, ga_ref,
         qh_ref, lf_ref, kh_ref, vh_ref, gs_ref, sa_ref, sb_ref) = out_refs
    x = x_ref[...]
    xn = x * lax.rsqrt(jnp.mean(x * x, axis=-1, keepdims=True) + EPS) * g_ref[...]
    xb = xn.astype(BF16)

    def seg(a, n):
        return jnp.dot(xb, w_ref[:, a:a + n], preferred_element_type=F32)

    f32_refs = (kc_ref, vc_ref, ks_ref, vs_ref, kw_ref, vw_ref)
    for i in range(0, 6, 2):
        u = seg(_OFF_KV + i * KV_DIM, 2 * KV_DIM)
        f32_refs[i][...] = u[:, :KV_DIM]
        f32_refs[i + 1][...] = u[:, KV_DIM:]
        if prompt and i > 0:
            (ksb_ref if i == 2 else kwb_ref)[...] = u[:, :KV_DIM].astype(BF16)

    if prompt:
        def seg_t(a, n):
            return lax.dot_general(wt_ref[a:a + n, :], xb, (((1,), (1,)), ((), ())),
                                   preferred_element_type=F32)

        qat_ref[...] = (seg_t(0, Q_DIM) * Q_SCALE).astype(BF16)
        gat_ref[...] = jax.nn.sigmoid(seg_t(Q_DIM, LANES))
        vt = seg_t(Q_DIM + LANES, 2 * KV_DIM)
        vst_ref[...] = vt[:KV_DIM, :].astype(BF16)
        vwt_ref[...] = vt[KV_DIM:, :].astype(BF16)
    else:
        qa_ref[...] = (seg(_OFF_Q, Q_DIM) * Q_SCALE).astype(BF16)
        ga_ref[...] = jax.nn.sigmoid(seg(_OFF_G, LANES))

    log_lb = lb_ref[0:1, :]
    log_1m = lb_ref[1:2, :]
    one_m = lb_ref[2:3, :]
    qh_ref[...] = jax.nn.silu(seg(_OFF_HG, HG_W))
    z = seg(_OFF_HG + HG_W, HG_W)
    b = log_1m + (jnp.minimum(z, 0.0) - jnp.log1p(jnp.exp(-jnp.abs(z))))
    hi = jnp.maximum(log_lb, b)
    lf_ref[...] = hi + jnp.log1p(jnp.exp(-jnp.abs(log_lb - b)))
    kh_ref[...] = one_m * jax.nn.sigmoid(-z)
    vh_ref[...] = seg(_OFF_HG + 2 * HG_W, HG_W)
    gs_ref[...] = jax.nn.silu(seg(_OFF_HG + 3 * HG_W, HG_W))
    sa_ref[...] = jax.nn.sigmoid(seg(_OFF_GATE, D_MODEL))
    sb_ref[...] = jax.nn.sigmoid(seg(_OFF_GATE + D_MODEL, D_MODEL))


_PROJ_T = Q_DIM + LANES + 2 * KV_DIM


def _proj(x2d, g, w, wt, lb3, tm, prompt):
    rows = x2d.shape[0]
    tail = [(HG_W, F32)] * 5 + [(D_MODEL, F32)] * 2
    if prompt:
        widths = [(KV_DIM, F32)] * 6 + [(KV_DIM, BF16)] * 2 + tail
        heights = [(Q_DIM, BF16), (LANES, F32), (KV_DIM, BF16), (KV_DIM, BF16)]
    else:
        widths = [(Q_DIM, BF16)] + [(KV_DIM, F32)] * 6 + [(LANES, F32)] + tail
        heights = []
    return pl.pallas_call(
        functools.partial(_proj_kernel, prompt=prompt),
        grid=(rows // tm,),
        in_specs=[pl.BlockSpec((tm, D_MODEL), lambda i: (i, 0)),
                  _const_spec((1, D_MODEL)),
                  _const_spec((D_MODEL, _PROJ_N)),
                  _const_spec((_PROJ_T, D_MODEL)),
                  _const_spec((SUBLANES, HG_W))],
        out_specs=([pl.BlockSpec((tm, n), lambda i: (i, 0)) for n, _ in widths]
                   + [pl.BlockSpec((n, tm), lambda i: (0, i)) for n, _ in heights]),
        out_shape=([jax.ShapeDtypeStruct((rows, n), dt) for n, dt in widths]
                   + [jax.ShapeDtypeStruct((n, rows), dt) for n, dt in heights]),
        compiler_params=_cparams(("arbitrary",)),
        name="proj",
    )(x2d, g, w, wt, lb3)


_CH_W = CMP_STRIDE * KV_DIM
_CH_PER_PAGE = PAGE // CMP_STRIDE
_CH_PITCH = CMP_STRIDE + SUBLANES


def _compress_kernel(pt_ref, pool_ref, pe_ref, w1_ref, w2_ref, out_ref, outt_ref, buf, rbuf, xa, hbuf, sem,
                     *, n_pages, pages_transposed):
    b = pl.program_id(0)
    nb = pl.num_programs(0)
    slot = b % 2
    n_ch = n_pages * _CH_PER_PAGE

    def page_copy(bb, p, s):
        return pltpu.make_async_copy(pool_ref.at[pt_ref[bb, p]], buf.at[s, p], sem.at[s])

    def start_all(bb, s):
        def body(p, c):
            page_copy(bb, p, s).start()
            return c
        lax.fori_loop(0, n_pages, body, 0)

    @pl.when(b == 0)
    def _():
        start_all(b, slot)

    @pl.when(b + 1 < nb)
    def _():
        start_all(b + 1, 1 - slot)

    def wait_body(p, c):
        page_copy(b, p, slot).wait()
        return c
    lax.fori_loop(0, n_pages, wait_body, 0)

    rows = math.gcd(n_ch, 256)
    pages_per_group = rows // _CH_PER_PAGE

    def to_rows(g):
        for p in range(g * pages_per_group, (g + 1) * pages_per_group):
            page = buf[slot, p]
            page = page.T if pages_transposed else page
            for i in range(_CH_PER_PAGE):
                dst = (p * _CH_PER_PAGE + i) * _CH_PITCH
                rbuf[dst:dst + CMP_STRIDE, :] = page[i * CMP_STRIDE:(i + 1) * CMP_STRIDE, :]

    to_rows(0)
    for r in range(n_ch // rows):
        if r + 1 < n_ch // rows:
            to_rows(r + 1)
        for s in range(CMP_STRIDE):
            x = rbuf[pl.ds(r * rows * _CH_PITCH + s, rows, stride=_CH_PITCH), :]
            xa[:, s * KV_DIM:(s + 1) * KV_DIM] = x.astype(BF16)
        hbuf[r * rows:(r + 1) * rows, :] = jnp.dot(xa[...], w1_ref[...], preferred_element_type=F32)
    pw = _split_dot_rhs(pe_ref[...], w1_ref[...])
    nh = N_KV * CMP_HIDDEN
    bias = pw[0:1, 0:nh] + pw[1:2, nh:2 * nh]
    h = hbuf[:, 0:nh] + pltpu.roll(hbuf[:, nh:2 * nh], n_ch - 1, 0) + bias
    blocks = jnp.dot(jax.nn.gelu(h).astype(BF16), w2_ref[...], preferred_element_type=F32)
    row = lax.broadcasted_iota(jnp.int32, blocks.shape, 0)
    blocks = jnp.where(row < n_ch - 1, blocks, 0.0)
    out_ref[0] = blocks.astype(BF16)
    outt_ref[0] = blocks.T.astype(BF16)


def _compress(pool, page_table, pe, w1, w2, pages_transposed):
    nbatch, n_pages = page_table.shape
    n_ch = n_pages * _CH_PER_PAGE
    rows = math.gcd(n_ch, 256)
    grid_spec = pltpu.PrefetchScalarGridSpec(
        num_scalar_prefetch=1,
        grid=(nbatch,),
        in_specs=[pl.BlockSpec(memory_space=pl.ANY),
                  _const_spec((SUBLANES, _CH_W)),
                  _const_spec((_CH_W, 4 * CMP_HIDDEN)),
                  _const_spec((2 * CMP_HIDDEN, KV_DIM))],
        out_specs=[pl.BlockSpec((1, n_ch, KV_DIM), lambda b, pt: (b, 0, 0)),
                   pl.BlockSpec((1, KV_DIM, n_ch), lambda b, pt: (b, 0, 0))],
        scratch_shapes=[pltpu.VMEM((2, n_pages, PAGE, KV_DIM), F32),
                        pltpu.VMEM((n_ch * _CH_PITCH, KV_DIM), F32),
                        pltpu.VMEM((rows, _CH_W), BF16),
                        pltpu.VMEM((n_ch, 4 * CMP_HIDDEN), F32),
                        pltpu.SemaphoreType.DMA((2,))],
    )
    return pl.pallas_call(
        functools.partial(_compress_kernel, n_pages=n_pages, pages_transposed=pages_transposed),
        grid_spec=grid_spec,
        out_shape=[jax.ShapeDtypeStruct((nbatch, n_ch, KV_DIM), BF16),
                   jax.ShapeDtypeStruct((nbatch, KV_DIM, n_ch), BF16)],
        compiler_params=_cparams(("arbitrary",)),
        name="compress",
    )(page_table, pool, pe, w1, w2)


def _compress_weights(pe, w1, w2):
    c = CMP_BLOCK // CMP_STRIDE
    pe_r = pe.reshape(c, CMP_STRIDE, 1, HEAD_DIM)
    pe_x = jnp.broadcast_to(pe_r, (c, CMP_STRIDE, N_KV, HEAD_DIM)).reshape(c, _CH_W)
    pe_x = jnp.pad(pe_x, ((0, SUBLANES - c), (0, 0)))
    w1_r = w1.reshape(c, CMP_STRIDE, HEAD_DIM, CMP_HIDDEN)
    eye = jnp.eye(N_KV, dtype=w1.dtype)
    w1_x = jnp.einsum('jsde,kq->skdjqe', w1_r, eye).reshape(_CH_W, c * N_KV * CMP_HIDDEN)
    w2_x = jnp.einsum('ed,kq->keqd', w2, eye).reshape(N_KV * CMP_HIDDEN, KV_DIM)
    return pe_x, w1_x.astype(BF16), w2_x.astype(BF16)


def _hgrn_chunk(q, k, v, lf, st_ref, chunk, sub):
    if chunk > SUBLANES:
        r = lax.broadcasted_iota(jnp.int32, (chunk, chunk), 0)
        c = lax.broadcasted_iota(jnp.int32, (chunk, chunk), 1)
        tri = (r >= c).astype(F32)
        b = jnp.dot(tri, lf, preferred_element_type=F32, precision=HIGHEST)
    else:
        rows = [lf[0:1, :]]
        for t in range(1, chunk):
            rows.append(rows[-1] + lf[t:t + 1, :])
        b = jnp.concatenate(rows, axis=0)
    bl = b[chunk - 1:chunk, :]
    qe = q * jnp.exp(b)
    kd = k * jnp.exp(bl - b)
    ebl = jnp.exp(bl)
    n_sub = chunk // sub
    trow = lax.broadcasted_iota(jnp.int32, (sub, HG_W), 0)
    crow = lax.broadcasted_iota(jnp.int32, (chunk, HG_W), 0)

    diag = []
    for i in range(n_sub):
        qi = q[i * sub:(i + 1) * sub, :]
        bi = b[i * sub:(i + 1) * sub, :]
        acc = [jnp.zeros((sub, HG_DIM), F32) for _ in range(HG_HEADS)]
        for s in range(sub):
            row = i * sub + s
            dec = jnp.exp(jnp.where(trow >= s, bi - b[row:row + 1, :], -jnp.inf))
            prod = qi * k[row:row + 1, :] * dec
            for h in range(HG_HEADS):
                a = jnp.sum(prod[:, h * HG_DIM:(h + 1) * HG_DIM], axis=1, keepdims=True)
                acc[h] = acc[h] + a * v[row:row + 1, h * HG_DIM:(h + 1) * HG_DIM]
        diag.append(acc)

    off = []
    for i in range(n_sub):
        if i == 0:
            off.append(None)
            continue
        b0 = b[i * sub - 1:i * sub, :]
        qs = (q[i * sub:(i + 1) * sub, :] * jnp.exp(b[i * sub:(i + 1) * sub, :] - b0)).astype(BF16)
        ks = (k * jnp.exp(jnp.where(crow < i * sub, b0 - b, -jnp.inf))).astype(BF16)
        off.append((qs, ks))

    vb = v.astype(BF16)
    outs = []
    for h in range(HG_HEADS):
        sl = slice(h * HG_DIM, (h + 1) * HG_DIM)
        st = st_ref[h]
        o_h = lax.dot_general(qe[:, sl].astype(BF16), st.astype(BF16), (((1,), (1,)), ((), ())),
                              preferred_element_type=F32)
        parts = []
        for i in range(n_sub):
            d = diag[i][h]
            if off[i] is not None:
                qs, ks = off[i]
                a = lax.dot_general(qs[:, sl], ks[:, sl], (((1,), (1,)), ((), ())),
                                    preferred_element_type=F32)
                d = d + jnp.dot(a.astype(BF16), vb[:, sl], preferred_element_type=F32)
            parts.append(d)
        intra = parts[0] if n_sub == 1 else jnp.concatenate(parts, axis=0)
        outs.append(o_h + intra)
        st_ref[h] = st * ebl[:, sl] + lax.dot_general(
            vb[:, sl], kd[:, sl].astype(BF16), (((0,), (0,)), ((), ())), preferred_element_type=F32)
    return jnp.concatenate(outs, axis=1)


def _hgrn_finish(o, gs, gn):
    outs = []
    for h in range(HG_HEADS):
        oh = o[:, h * HG_DIM:(h + 1) * HG_DIM]
        y = oh * lax.rsqrt(jnp.mean(oh * oh, axis=-1, keepdims=True) + EPS) * gn
        outs.append(y)
    return (jnp.concatenate(outs, axis=1) * gs).astype(BF16)


def _hgrn_prompt_kernel(q_ref, k_ref, v_ref, lf_ref, gs_ref, gn_ref, o_ref, s_ref, st_ref, *, n_chunks):
    i = pl.program_id(0)

    @pl.when(i == 0)
    def _():
        st_ref[...] = jnp.zeros_like(st_ref)

    def body(c, carry):
        r = pl.ds(pl.multiple_of(c * HG_CHUNK, HG_CHUNK), HG_CHUNK)
        o = _hgrn_chunk(q_ref[r, :], k_ref[r, :], v_ref[r, :], lf_ref[r, :], st_ref, HG_CHUNK, HG_SUB)
        o_ref[r, :] = _hgrn_finish(o, gs_ref[r, :], gn_ref[...])
        return carry
    lax.fori_loop(0, n_chunks, body, 0, unroll=4)

    @pl.when(i == pl.num_programs(0) - 1)
    def _():
        for h in range(HG_HEADS):
            s_ref[h] = st_ref[h].T


def _hgrn_prompt(qh, kh, vh, lf, gs, gn, rows_per_step=512):
    t = qh.shape[0]
    spec = pl.BlockSpec((rows_per_step, HG_W), lambda i: (i, 0))
    return pl.pallas_call(
        functools.partial(_hgrn_prompt_kernel, n_chunks=rows_per_step // HG_CHUNK),
        grid=(t // rows_per_step,),
        in_specs=[spec] * 5 + [_const_spec((1, HG_DIM))],
        out_specs=[spec, pl.BlockSpec((HG_HEADS, HG_DIM, HG_DIM), lambda i: (0, 0, 0))],
        out_shape=[jax.ShapeDtypeStruct((t, HG_W), BF16),
                   jax.ShapeDtypeStruct((HG_HEADS, HG_DIM, HG_DIM), F32)],
        scratch_shapes=[pltpu.VMEM((HG_HEADS, HG_DIM, HG_DIM), F32)],
        compiler_params=_cparams(("arbitrary",)),
        name="hgrn_prompt",
    )(qh, kh, vh, lf, gs, gn)


def _hgrn_sample_kernel(q_ref, k_ref, v_ref, lf_ref, gs_ref, gn_ref, s0_ref, o_ref, s_ref, st_ref, *, t):
    for h in range(HG_HEADS):
        st_ref[h] = s0_ref[0, h].T
    o = _hgrn_chunk(q_ref[0], k_ref[0], v_ref[0], lf_ref[0], st_ref, t, t)
    o_ref[0] = _hgrn_finish(o, gs_ref[0], gn_ref[...])
    for h in range(HG_HEADS):
        s_ref[0, h] = st_ref[h].T


def _hgrn_sample(qh, kh, vh, lf, gs, gn, s0):
    nb, t, _ = qh.shape
    spec = pl.BlockSpec((1, t, HG_W), lambda b: (b, 0, 0))
    sspec = pl.BlockSpec((1, HG_HEADS, HG_DIM, HG_DIM), lambda b: (b, 0, 0, 0))
    return pl.pallas_call(
        functools.partial(_hgrn_sample_kernel, t=t),
        grid=(nb,),
        in_specs=[spec] * 5 + [_const_spec((1, HG_DIM)), sspec],
        out_specs=[spec, sspec],
        out_shape=[jax.ShapeDtypeStruct((nb, t, HG_W), BF16),
                   jax.ShapeDtypeStruct((nb, HG_HEADS, HG_DIM, HG_DIM), F32)],
        scratch_shapes=[pltpu.VMEM((HG_HEADS, HG_DIM, HG_DIM), F32)],
        compiler_params=_cparams(("arbitrary",)),
        name="hgrn_sample",
    )(qh, kh, vh, lf, gs, gn, s0)


def _select_topk(x, blk, n, always=None):
    nblk = x.shape[0]
    if always is None:
        sel = jnp.zeros_like(x)
    else:
        sel = jnp.where(always, 1.0, 0.0)
        x = jnp.where(always, -3e38, x)
    for _ in range(n):
        m = jnp.max(x, axis=0, keepdims=True)
        idx = jnp.min(jnp.where(x == m, blk, float(nblk)), axis=0, keepdims=True)
        pick = blk == idx
        sel = jnp.where(pick, 1.0, sel)
        x = jnp.where(pick, -3e38, x)
    return sel


def _softmax_cols(s, valid):
    m = jnp.max(jnp.where(valid, s, NEG), axis=0, keepdims=True)
    p = jnp.where(valid, jnp.exp2(s - m), 0.0)
    l = jnp.sum(p, axis=0, keepdims=True)
    return p * jnp.where(l > 0.0, 1.0 / l, 0.0)


def _split_dot(a_bf16, x):
    hi = x.astype(BF16)
    lo = (x - hi.astype(F32)).astype(BF16)
    return (jnp.dot(a_bf16, hi, preferred_element_type=F32)
            + jnp.dot(a_bf16, lo, preferred_element_type=F32))


def _split_dot_rhs(x, w_bf16):
    hi = x.astype(BF16)
    lo = (x - hi.astype(F32)).astype(BF16)
    return (jnp.dot(hi, w_bf16, preferred_element_type=F32)
            + jnp.dot(lo, w_bf16, preferred_element_type=F32))


def _nt(a, b):
    return lax.dot_general(a, b, (((1,), (1,)), ((), ())), preferred_element_type=F32)


def _online_update(s, v, m_ref, l_ref, acc_ref, v_transposed=False):
    m_old = m_ref[...]
    m_new = jnp.maximum(m_old, jnp.max(s, axis=1, keepdims=True))
    p = jnp.exp2(s - m_new)
    alpha = jnp.exp2(m_old - m_new)
    l_ref[...] = alpha * l_ref[...] + jnp.sum(p, axis=1, keepdims=True)
    pv = _nt(p.astype(BF16), v) if v_transposed else jnp.dot(p.astype(BF16), v, preferred_element_type=F32)
    acc_ref[...] = alpha * acc_ref[...] + pv
    m_ref[...] = m_new


def _score_matrix(n_rows, row_offset, n_blocks):
    c = CMP_BLOCK // CMP_STRIDE
    ratio = SEL_BLOCK // CMP_STRIDE
    n_ov = ratio + c - 1
    m = np.zeros((n_blocks, n_rows), np.float32)
    for j in range(n_blocks):
        for u in range(n_ov):
            start = CMP_STRIDE * (u - (c - 1))
            w_u = (min(start + CMP_BLOCK, SEL_BLOCK) - max(start, 0)) / CMP_STRIDE
            n = ratio * j + u - (c - 1)
            if 0 <= n and n + row_offset < n_rows:
                m[j, n + row_offset] = w_u
    return m


def _nsa_prompt_kernel(qt_ref, gt_ref, kc_ref, vct_ref, ks_ref, vst_ref, kw_ref, vwt_ref,
                       mt_ref, cc_ref, cs_ref, bw_ref, eb_ref, o_ref,
                       sc_ref, sw_ref, sel_ref, m_ref, acc_ref, oc_ref,
                       sa_ref, sb_ref, pa_ref, pb_ref, ala_ref, alb_ref, *, n_blocks, variants):
    qb = pl.program_id(0)
    nrow = kc_ref.shape[0]
    gq = GROUP * Q_BLOCK
    tiles_per_q = SEL_TILE // Q_BLOCK
    nband = WINDOW + Q_BLOCK
    max_tile = n_blocks // BLK_PER_TILE - 1
    frow = lax.broadcasted_iota(jnp.int32, (KV_DIM, Q_BLOCK), 0)

    def lanes4(x):
        return jnp.concatenate([x] * GROUP, axis=1)

    qx = []
    for kv in range(N_KV):
        keep = (frow >= HEAD_DIM) if kv else (frow < HEAD_DIM)
        qx.append(jnp.concatenate(
            [jnp.where(keep, qt_ref[j * KV_DIM:(j + 1) * KV_DIM, :], jnp.zeros((), BF16))
             for j in range(GROUP)], axis=1))

    def qk_stage(kt, s_ref, penalty=0.0):
        k = ks_ref[pl.ds(pl.multiple_of(kt * SEL_TILE, SEL_TILE), SEL_TILE), :]
        k_aug = jnp.concatenate([k, eb_ref[...]], axis=1)
        for kv in range(N_KV):
            srow = sel_ref[kv, pl.ds(pl.multiple_of(kt * BLK_PER_TILE, BLK_PER_TILE), BLK_PER_TILE), :]
            mrows = jnp.concatenate([lanes4(srow + penalty), jnp.zeros((KV_DIM - BLK_PER_TILE, gq), F32)], axis=0)
            q_aug = jnp.concatenate([qx[kv], mrows.astype(BF16)], axis=0)
            s_ref[kv] = jnp.dot(k_aug, q_aug, preferred_element_type=F32)

    def gate(kv, i):
        return jnp.concatenate(
            [gt_ref[3 * (kv * GROUP + j) + i:3 * (kv * GROUP + j) + i + 1, :] for j in range(GROUP)], axis=1)

    def compressed_and_select(n_r, n_b):
        for kv in range(N_KV):
            sc_ref[kv, 0:n_r, :] = jnp.dot(kc_ref[0:n_r, :], qx[kv], preferred_element_type=F32)
        r = lax.broadcasted_iota(jnp.int32, (n_r, Q_BLOCK), 0)
        qpos_c = qb * Q_BLOCK + lax.broadcasted_iota(jnp.int32, (n_r, Q_BLOCK), 1)
        end_pos = (r - CMP_PAD) * CMP_STRIDE + (CMP_BLOCK - 1)
        vis = (r >= CMP_PAD) & (r < CMP_PAD + n_blocks * (SEL_BLOCK // CMP_STRIDE) - 1) & (end_pos <= qpos_c)
        vis_add = lanes4(jnp.where(vis, 0.0, NEG))
        band = pl.ds(pl.multiple_of(qb * SUBLANES, SUBLANES), CMP_BAND)
        blk_i = lax.broadcasted_iota(jnp.int32, (n_b, Q_BLOCK), 0)
        cur = (qb * Q_BLOCK + lax.broadcasted_iota(jnp.int32, (n_b, Q_BLOCK), 1)) // SEL_BLOCK
        forced = (blk_i == 0) | (blk_i == cur) | (blk_i == cur - 1)
        scores = []
        for kv in range(N_KV):
            sc_ref[kv, band, :] = sc_ref[kv, band, :] + cc_ref[kv]
            s = sc_ref[kv, 0:n_r, :] + vis_add
            m = jnp.max(s, axis=0, keepdims=True)
            p = jnp.exp2(s - m)
            l = jnp.sum(p, axis=0, keepdims=True)
            pn = p * jnp.where(m > 0.5 * NEG, 1.0 / l, 0.0)
            oc_ref[kv] = gate(kv, 0) * jnp.dot(vct_ref[:, 0:n_r], pn.astype(BF16), preferred_element_type=F32)
            imp = pn[:, 0:Q_BLOCK]
            for j in range(1, GROUP):
                imp = imp + pn[:, j * Q_BLOCK:(j + 1) * Q_BLOCK]
            score = _split_dot(mt_ref[0:n_b, 0:n_r], imp)
            scores.append(jnp.where(blk_i <= cur, score, -FORCE_BONUS))
        kband = kw_ref[pl.ds(pl.multiple_of(qb * Q_BLOCK, Q_BLOCK), nband), :]
        for kv in range(N_KV):
            sw_ref[kv] = jnp.dot(kband, qx[kv], preferred_element_type=F32)
        blk2 = jnp.concatenate([blk_i.astype(F32)] * N_KV, axis=1)
        forced2 = jnp.concatenate([forced] * N_KV, axis=1)
        sel = _select_topk(jnp.concatenate(scores, axis=1), blk2, N_SEL - 3, always=forced2)
        for kv in range(N_KV):
            sel_ref[kv, 0:n_b, :] = (sel[:, kv * Q_BLOCK:(kv + 1) * Q_BLOCK] - 1.0) * (-NEG)

    sel_ref[...] = jnp.full_like(sel_ref, NEG)
    lo = 0
    for q_hi, n_r, n_b in variants:
        @pl.when((qb >= lo) & (qb <= q_hi))
        def _(n_r=n_r, n_b=n_b):
            compressed_and_select(n_r, n_b)
        lo = q_hi + 1

    m_ref[...] = jnp.full_like(m_ref, NEG)
    acc_ref[...] = jnp.zeros_like(acc_ref)

    def pv_stage(kt, p_ref, al_ref):
        for kv in range(N_KV):
            acc_ref[kv] = al_ref[kv] * acc_ref[kv] + jnp.dot(vst_ref[kv, kt], p_ref[kv],
                                                             preferred_element_type=F32)

    def sm_stage(kt, s_ref, p_ref, al_ref, near):
        d0 = qb - kt * tiles_per_q
        for kv in range(N_KV):
            s = s_ref[kv]
            if near:
                s = s + jnp.concatenate(
                    [cs_ref[kv, jnp.clip(d0 - i, -1, 2) + 1] for i in range(tiles_per_q)], axis=0)
            m_old = m_ref[kv]
            m_new = jnp.maximum(m_old, jnp.max(s, axis=0, keepdims=True))
            p_ref[kv] = jnp.exp2(s - m_new).astype(BF16)
            al_ref[kv] = jnp.exp2(m_old - m_new)
            m_ref[kv] = m_new

    n_far = jnp.maximum(qb - 1, 0) // tiles_per_q
    pb_ref[...] = jnp.zeros_like(pb_ref)
    alb_ref[...] = jnp.ones_like(alb_ref)
    qk_stage(0, sa_ref)

    def window(kv):
        vband = jnp.concatenate([vwt_ref[qb + i] for i in range(nband // Q_BLOCK)], axis=1)
        krow = lax.broadcasted_iota(jnp.int32, (nband, Q_BLOCK), 0)
        pos_add = lanes4(jnp.where(krow >= WINDOW - qb * Q_BLOCK, 0.0, NEG))
        s = sw_ref[kv] + bw_ref[kv] + pos_add
        m = jnp.max(s, axis=0, keepdims=True)
        p = jnp.exp2(s - m)
        o_w = (jnp.dot(vband, p.astype(BF16), preferred_element_type=F32)
               / jnp.sum(p, axis=0, keepdims=True))
        oc_ref[kv] = oc_ref[kv] + gate(kv, 2) * o_w

    window(0)

    def pair_body(u, c):
        t0 = 2 * u
        t1 = jnp.minimum(t0 + 1, max_tile)
        t2 = jnp.minimum(t0 + 2, max_tile)
        qk_stage(t1, sb_ref, jnp.where(t0 + 1 < n_far, 0.0, NEG))
        pv_stage(jnp.maximum(t0 - 1, 0), pb_ref, alb_ref)
        sm_stage(t0, sa_ref, pa_ref, ala_ref, False)
        qk_stage(t2, sa_ref)
        pv_stage(t0, pa_ref, ala_ref)
        sm_stage(t1, sb_ref, pb_ref, alb_ref, False)
        return c
    n_pairs = (n_far + 1) // 2
    lax.fori_loop(0, n_pairs, pair_body, 0)

    near_a = n_far
    near_b = jnp.minimum(n_far + 1, max_tile)
    qk_stage(near_a, sa_ref)
    pv_stage(jnp.clip(2 * n_pairs - 1, 0, max_tile), pb_ref, alb_ref)
    window(1)
    sm_stage(near_a, sa_ref, pa_ref, ala_ref, True)
    pv_stage(near_a, pa_ref, ala_ref)

    @pl.when(n_far + 1 <= qb // tiles_per_q)
    def _():
        qk_stage(near_b, sb_ref)
        sm_stage(near_b, sb_ref, pb_ref, alb_ref, True)
        pv_stage(near_b, pb_ref, alb_ref)

    o_kv = []
    for kv in range(N_KV):
        acc = acc_ref[kv]
        o_s = acc[0:HEAD_DIM, :] / acc[HEAD_DIM:HEAD_DIM + 1, :]
        o_kv.append(oc_ref[kv, kv * HEAD_DIM:(kv + 1) * HEAD_DIM, :] + gate(kv, 1) * o_s)
    o_t = jnp.concatenate(o_kv, axis=0)
    for j in range(GROUP):
        o_ref[:, j * LANES:(j + 1) * LANES] = o_t[:, j * Q_BLOCK:(j + 1) * Q_BLOCK].T.astype(BF16)


def _nsa_prompt(qat, gat, kc, vct, ksb, vst3, kwb, vwt3, mt, cc, cs, bw, eb):
    t = qat.shape[1]
    n_blocks = t // SEL_BLOCK
    nrow = kc.shape[0]
    gq = GROUP * Q_BLOCK
    n_q = t // Q_BLOCK
    variants = []
    for n_r in list(range(2 * LANES, nrow - LANES, 2 * LANES)) + [nrow]:
        q_hi = n_q - 1 if n_r == nrow else min((n_r - CMP_BAND) // SUBLANES, n_q - 1)
        n_b = min(-(-(2 * q_hi + 2) // (2 * SUBLANES)) * (2 * SUBLANES), n_blocks)
        variants.append((q_hi, n_r, n_b))
    return pl.pallas_call(
        functools.partial(_nsa_prompt_kernel, n_blocks=n_blocks, variants=tuple(variants)),
        grid=(t // Q_BLOCK,),
        in_specs=[pl.BlockSpec((Q_DIM, Q_BLOCK), lambda i: (0, i)),
                  pl.BlockSpec((LANES, Q_BLOCK), lambda i: (0, i)),
                  _const_spec(kc.shape), _const_spec(vct.shape),
                  _const_spec(ksb.shape), _const_spec(vst3.shape),
                  _const_spec(kwb.shape), _const_spec(vwt3.shape),
                  _const_spec(mt.shape), _const_spec(cc.shape),
                  _const_spec(cs.shape), _const_spec(bw.shape), _const_spec(eb.shape)],
        out_specs=pl.BlockSpec((Q_BLOCK, Q_DIM), lambda i: (i, 0)),
        out_shape=jax.ShapeDtypeStruct((t, Q_DIM), BF16),
        scratch_shapes=[pltpu.VMEM((N_KV, nrow, gq), F32),
                        pltpu.VMEM((N_KV, WINDOW + Q_BLOCK, gq), F32),
                        pltpu.VMEM((N_KV, n_blocks, Q_BLOCK), F32),
                        pltpu.VMEM((N_KV, 1, gq), F32),
                        pltpu.VMEM((N_KV, vst3.shape[2], gq), F32),
                        pltpu.VMEM((N_KV, KV_DIM, gq), F32),
                        pltpu.VMEM((N_KV, SEL_TILE, gq), F32), pltpu.VMEM((N_KV, SEL_TILE, gq), F32),
                        pltpu.VMEM((N_KV, SEL_TILE, gq), BF16), pltpu.VMEM((N_KV, SEL_TILE, gq), BF16),
                        pltpu.VMEM((N_KV, 1, gq), F32), pltpu.VMEM((N_KV, 1, gq), F32)],
        compiler_params=_cparams(("arbitrary",)),
        name="nsa_prompt",
    )(qat, gat, kc, vct, ksb, vst3, kwb, vwt3, mt, cc, cs, bw, eb)


S_ROWS = 128
S_CHUNK_PAGES = 64
S_CHUNK = S_CHUNK_PAGES * PAGE


def _nsa_sample_kernel(pt_ref, qx_ref, gm_ref, kc_ref, vct_ref, kpool_ref, vpool_ref,
                       knew_ref, vnew_ref, kwin_ref, vwin_ref, kwnew_ref, vwnew_ref,
                       mt_ref, gsum_ref, ccs_ref, css_ref, cns_ref, cws_ref, o_ref,
                       kbuf, vbuf, sem, mask_ref, m_ref, l_ref, acc_ref, oc_ref, ow_ref,
                       *, n_chunks, n_blocks):
    b = pl.program_id(0)
    c = pl.program_id(1)
    step = b * n_chunks + c
    total = pl.num_programs(0) * n_chunks
    slot = step % 2

    def copies(bb, cc, s, p):
        pg = cc * S_CHUNK_PAGES + p
        dst = pl.ds(pl.multiple_of(p * PAGE, PAGE), PAGE)
        return (pltpu.make_async_copy(kpool_ref.at[pt_ref[bb, pg]], kbuf.at[s, :, dst], sem.at[0, s]),
                pltpu.make_async_copy(vpool_ref.at[pt_ref[bb, pg]], vbuf.at[s, :, dst], sem.at[1, s]))

    def start_all(st, s):
        bb = st // n_chunks
        cc = st % n_chunks

        def body(p, carry):
            ck, cv = copies(bb, cc, s, p)
            ck.start()
            cv.start()
            return carry
        lax.fori_loop(0, S_CHUNK_PAGES, body, 0)

    @pl.when(step == 0)
    def _():
        start_all(step, slot)

    @pl.when(step + 1 < total)
    def _():
        start_all(step + 1, 1 - slot)

    qx = qx_ref[0]
    n_q = o_ref.shape[1]
    qq = qx[:n_q]

    @pl.when(c == 0)
    def _():
        nrow = kc_ref.shape[1]
        s = _nt(kc_ref[0], qx)
        r = lax.broadcasted_iota(jnp.int32, (nrow, S_ROWS), 0)
        band0 = nrow - CMP_BAND
        s = s + jnp.concatenate([jnp.zeros((band0, S_ROWS), F32), ccs_ref[...]], axis=0)
        pn = _softmax_cols(s, r < nrow - 1)
        oc_ref[...] = jnp.dot(vct_ref[0], pn.astype(BF16), preferred_element_type=F32).T[:n_q]
        imp = jnp.dot(pn, gsum_ref[...], preferred_element_type=F32, precision=HIGHEST)
        score = _split_dot(mt_ref[...], imp)
        blk_i = lax.broadcasted_iota(jnp.int32, (n_blocks, S_ROWS), 0)
        forced = (blk_i == 0) | (blk_i == n_blocks - 1)
        selt = _select_topk(score, blk_i.astype(F32), N_SEL - 3, always=forced)
        selt = lax.dot_general(selt.astype(BF16), gsum_ref[...].astype(BF16), (((1,), (1,)), ((), ())),
                               preferred_element_type=F32)
        selm1 = selt.T[:n_q] - 1.0
        e_r = lax.broadcasted_iota(jnp.int32, (BLK_PER_TILE, SEL_TILE), 0)
        e_c = lax.broadcasted_iota(jnp.int32, (BLK_PER_TILE, SEL_TILE), 1)
        expand = jnp.where(e_c // SEL_BLOCK == e_r, -NEG, 0.0)
        for kt in range(n_blocks // BLK_PER_TILE):
            mask_ref[:, kt * SEL_TILE:(kt + 1) * SEL_TILE] = jnp.dot(
                selm1[:, kt * BLK_PER_TILE:(kt + 1) * BLK_PER_TILE], expand, preferred_element_type=F32)

        sw = _nt(qq, kwin_ref[0].astype(BF16)) + cws_ref[:n_q, :]
        sn = _nt(qq, kwnew_ref[0].astype(BF16)) + cns_ref[:n_q, :]
        m = jnp.maximum(jnp.max(sw, axis=1, keepdims=True), jnp.max(sn, axis=1, keepdims=True))
        pw = jnp.exp2(sw - m)
        pn2 = jnp.exp2(sn - m)
        l = jnp.sum(pw, axis=1, keepdims=True) + jnp.sum(pn2, axis=1, keepdims=True)
        ow = (jnp.dot(pw.astype(BF16), vwin_ref[0].astype(BF16), preferred_element_type=F32)
              + jnp.dot(pn2.astype(BF16), vwnew_ref[0].astype(BF16), preferred_element_type=F32))
        ow_ref[...] = ow / l

        m_ref[...] = jnp.full_like(m_ref, NEG)
        l_ref[...] = jnp.zeros_like(l_ref)
        acc_ref[...] = jnp.zeros_like(acc_ref)

    def wait_body(p, carry):
        ck, cv = copies(b, c, slot, p)
        ck.wait()
        cv.wait()
        return carry
    lax.fori_loop(0, S_CHUNK_PAGES, wait_body, 0)

    col0 = pl.multiple_of(c * S_CHUNK, S_CHUNK)
    s = (jnp.dot(qq, kbuf[slot].astype(BF16), preferred_element_type=F32)
         + mask_ref[:, pl.ds(col0, S_CHUNK)])

    @pl.when(c < n_chunks - 1)
    def _():
        _online_update(s, vbuf[slot].astype(BF16), m_ref, l_ref, acc_ref, v_transposed=True)

    @pl.when(c == n_chunks - 1)
    def _():
        near = jnp.concatenate([jnp.zeros((n_q, S_CHUNK - LANES), F32), css_ref[:n_q, :]], axis=1)
        _online_update(s + near, vbuf[slot].astype(BF16), m_ref, l_ref, acc_ref, v_transposed=True)
        sn = _nt(qq, knew_ref[0].astype(BF16)) + cns_ref[:n_q, :]
        _online_update(sn, vnew_ref[0].astype(BF16), m_ref, l_ref, acc_ref)
        o_s = acc_ref[...] / l_ref[...]
        o_ref[0] = gm_ref[0, 0] * oc_ref[...] + gm_ref[0, 1] * o_s + gm_ref[0, 2] * ow_ref[...]


def _nsa_sample(page_table, qx, gm, kc, vct, kpool, vpool, knew, vnew, kwin, vwin, kwnew, vwnew,
                mt, gsum, ccs, css, cns, cws):
    nb = qx.shape[0]
    n_pages = page_table.shape[1]
    n_chunks = n_pages // S_CHUNK_PAGES
    past = n_pages * PAGE
    n_blocks = past // SEL_BLOCK
    n_q = gm.shape[2]

    def bspec(shape):
        nd = len(shape)
        return pl.BlockSpec((1,) + tuple(shape[1:]), lambda b, c, pt: (b,) + (0,) * (nd - 1))

    def cspec(shape):
        nd = len(shape)
        return pl.BlockSpec(tuple(shape), lambda b, c, pt: (0,) * nd, pipeline_mode=pl.Buffered(1))

    grid_spec = pltpu.PrefetchScalarGridSpec(
        num_scalar_prefetch=1,
        grid=(nb, n_chunks),
        in_specs=[bspec(qx.shape), bspec(gm.shape), bspec(kc.shape), bspec(vct.shape),
                  pl.BlockSpec(memory_space=pl.ANY), pl.BlockSpec(memory_space=pl.ANY),
                  bspec(knew.shape), bspec(vnew.shape), bspec(kwin.shape), bspec(vwin.shape),
                  bspec(kwnew.shape), bspec(vwnew.shape),
                  cspec(mt.shape), cspec(gsum.shape), cspec(ccs.shape), cspec(css.shape),
                  cspec(cns.shape), cspec(cws.shape)],
        out_specs=pl.BlockSpec((1, n_q, KV_DIM), lambda b, c, pt: (b, 0, 0)),
        scratch_shapes=[pltpu.VMEM((2, KV_DIM, S_CHUNK), F32),
                        pltpu.VMEM((2, KV_DIM, S_CHUNK), F32),
                        pltpu.SemaphoreType.DMA((2, 2)),
                        pltpu.VMEM((n_q, past), F32),
                        pltpu.VMEM((n_q, 1), F32), pltpu.VMEM((n_q, 1), F32),
                        pltpu.VMEM((n_q, KV_DIM), F32),
                        pltpu.VMEM((n_q, KV_DIM), F32), pltpu.VMEM((n_q, KV_DIM), F32)],
    )
    return pl.pallas_call(
        functools.partial(_nsa_sample_kernel, n_chunks=n_chunks, n_blocks=n_blocks),
        grid_spec=grid_spec,
        out_shape=jax.ShapeDtypeStruct((nb, n_q, KV_DIM), F32),
        compiler_params=_cparams(("arbitrary", "arbitrary")),
        name="nsa_sample",
    )(page_table, qx, gm, kc, vct, kpool, vpool, knew, vnew, kwin, vwin, kwnew, vwnew,
      mt, gsum, ccs, css, cns, cws)


def _ffn_kernel(x_ref, oa_ref, ob_ref, sa_ref, sb_ref, wpa_ref, wpb_ref, wo_ref, nf_ref,
                wg_ref, wu_ref, wd_ref, nl_ref, y_ref):
    pa = jnp.dot(oa_ref[...], wpa_ref[...], preferred_element_type=F32)
    pb = jnp.dot(ob_ref[...], wpb_ref[...], preferred_element_type=F32)
    merged = sa_ref[...] * pa + sb_ref[...] * pb
    x = x_ref[...] + jnp.dot(merged.astype(BF16), wo_ref[...], preferred_element_type=F32)
    hn = (x * lax.rsqrt(jnp.mean(x * x, axis=-1, keepdims=True) + EPS) * nf_ref[...]).astype(BF16)
    gate = jnp.dot(hn, wg_ref[...], preferred_element_type=F32)
    up = jnp.dot(hn, wu_ref[...], preferred_element_type=F32)
    ff = (jax.nn.silu(gate) * up).astype(BF16)
    x = x + jnp.dot(ff, wd_ref[...], preferred_element_type=F32)
    y_ref[...] = x * lax.rsqrt(jnp.mean(x * x, axis=-1, keepdims=True) + EPS) * nl_ref[...]


def _ffn(x2d, oa, ob, sa, sb, wpa, wpb, wo, nf, wg, wu, wd, nl, tm):
    rows = x2d.shape[0]

    def rspec(n):
        return pl.BlockSpec((tm, n), lambda i: (i, 0))

    return pl.pallas_call(
        _ffn_kernel,
        grid=(rows // tm,),
        in_specs=[rspec(D_MODEL), rspec(Q_DIM), rspec(HG_W), rspec(D_MODEL), rspec(D_MODEL),
                  _const_spec(wpa.shape), _const_spec(wpb.shape), _const_spec(wo.shape),
                  _const_spec(nf.shape), _const_spec(wg.shape), _const_spec(wu.shape),
                  _const_spec(wd.shape), _const_spec(nl.shape)],
        out_specs=rspec(D_MODEL),
        out_shape=jax.ShapeDtypeStruct((rows, D_MODEL), F32),
        compiler_params=_cparams(("arbitrary",)),
        name="ffn",
    )(x2d, oa, ob, sa, sb, wpa, wpb, wo, nf, wg, wu, wd, nl)


def _pack_w_in(w_in):
    sizes = (Q_DIM,) + (KV_DIM,) * 6 + (3 * N_HEADS,) + (HG_W,) * 4 + (D_MODEL,) * 2
    offs = np.concatenate([[0], np.cumsum(sizes)])
    q = w_in[:, offs[0]:offs[1]].reshape(D_MODEL, N_HEADS, HEAD_DIM)[:, _HEAD_PERM, :].reshape(D_MODEL, Q_DIM)
    g = jnp.pad(w_in[:, offs[7]:offs[8]], ((0, 0), (0, LANES - 3 * N_HEADS)))
    return jnp.concatenate([q, w_in[:, offs[1]:offs[7]], g, w_in[:, offs[8]:]], axis=1).astype(BF16)


def _strip(bvc, rel, lo=0, hi=None, masked=NEG):
    val = bvc[:, np.clip(rel, 0, 255)]
    ok = rel >= lo
    if hi is not None:
        ok = ok & (rel < hi)
    return jnp.where(jnp.asarray(ok)[None], val, masked)


def _toeplitz(bvc, a, n_rows, n_cols, lo=0, hi=None):
    n = n_rows + n_cols - 1
    u = _strip(bvc, a - (n_rows - 1) + np.arange(n), lo, hi)
    u = jnp.pad(u, ((0, 0), (0, 1)))
    circ = jnp.tile(u, (1, n_rows))[:, :n_rows * n].reshape(N_HEADS, n_rows, n)
    return circ[:, :, n_rows - 1:n_rows - 1 + n_cols]


def _bias_strips_prompt(bvc):
    gq = GROUP * Q_BLOCK

    def lanes(x):
        return x.reshape(N_KV, GROUP, x.shape[1], Q_BLOCK).transpose(0, 2, 1, 3).reshape(N_KV, x.shape[1], gq)

    cs = jnp.stack([lanes(_toeplitz(bvc, Q_BLOCK * d, Q_BLOCK, Q_BLOCK)) for d in (-1, 0, 1, 2)], axis=1)
    bw = lanes(jnp.concatenate(
        [_toeplitz(bvc, Q_BLOCK * d, Q_BLOCK, Q_BLOCK, 0, WINDOW) for d in range(WINDOW // Q_BLOCK, -1, -1)], axis=1))
    rr = np.arange(CMP_BAND)[:, None]
    rel_c = np.arange(Q_BLOCK)[None, :] - CMP_STRIDE * (rr - CMP_PAD) - (CMP_BLOCK - 1)
    cc = _strip(bvc, rel_c, masked=0.0).reshape(N_KV, GROUP, CMP_BAND, Q_BLOCK)
    cc = cc.transpose(0, 2, 1, 3).reshape(N_KV, CMP_BAND, gq)
    return cc, cs, bw


def _bias_strips_sample(bvc, past, t):
    def rows(a):
        a = a.reshape(N_HEADS * t, a.shape[-1])
        return jnp.pad(a, ((0, S_ROWS - N_HEADS * t), (0, 0)))
    tt = np.arange(t)[:, None]
    nrow = past // CMP_STRIDE
    n = (nrow - CMP_BAND + np.arange(CMP_BAND))[None, :]
    ccs = rows(_strip(bvc, past + tt - CMP_STRIDE * n - (CMP_BLOCK - 1), masked=0.0)).T
    i = np.arange(LANES)[None, :]
    css = rows(_strip(bvc, LANES + tt - i))
    cns = rows(_strip(bvc, np.where(i < t, tt - i, -1)))
    iw = np.arange(WINDOW)[None, :]
    cws = rows(_strip(bvc, WINDOW + tt - iw, 0, WINDOW))
    return ccs, css, cns, cws


def kernel(x_prompt, x_sample, cache_k_cmp, cache_v_cmp, cache_k_slc, cache_v_slc, state_k_win, state_v_win,
           state_hgrn, page_table, norm_mix, w_in, cmp_pe_k, cmp_w1_k, cmp_w2_k, cmp_pe_v, cmp_w1_v, cmp_w2_v,
           rel_bias, hg_lb_logits, hg_norm, w_proj_a, w_proj_b, w_out, norm_ffn, w_gate, w_up, w_down, norm_final):
    nbp, t_p, _ = x_prompt.shape
    nbs, t_s, _ = x_sample.shape
    assert nbp == 1 and norm_mix.shape[0] == 1
    n_pages = page_table.shape[1]
    past = n_pages * PAGE
    assert state_k_win.shape[2] == WINDOW and past % S_CHUNK == 0 and t_s <= SUBLANES

    lb = jnp.cumsum(jax.nn.softmax(hg_lb_logits.astype(F32), axis=0), axis=0)[0]
    lb3 = jnp.pad(jnp.stack([jnp.log(lb), jnp.log1p(-lb), 1.0 - lb]), ((0, SUBLANES - 3), (0, 0)))
    w_pack = _pack_w_in(w_in[0])
    g_mix = norm_mix[0][None, :]
    wpa = w_proj_a[0].reshape(N_HEADS, HEAD_DIM, D_MODEL)[_HEAD_PERM].reshape(Q_DIM, D_MODEL).astype(BF16)
    wpb = w_proj_b[0].astype(BF16)
    wo = w_out[0].astype(BF16)
    wg, wu, wd = w_gate[0].astype(BF16), w_up[0].astype(BF16), w_down[0].astype(BF16)
    nf, nl = norm_ffn[0][None, :], norm_final[None, :]
    gn = hg_norm[0][None, :]
    pe_k, w1_k, w2_k = _compress_weights(cmp_pe_k[0], cmp_w1_k[0], cmp_w2_k[0])
    pe_v, w1_v, w2_v = _compress_weights(cmp_pe_v[0], cmp_w1_v[0], cmp_w2_v[0])
    bvc = (rel_bias[_BUCKET] - rel_bias[N_BUCKETS - 1][None, :]).T * LOG2E
    cc, cs, bw = _bias_strips_prompt(bvc)
    ccs, css, cns, cws = _bias_strips_sample(bvc, past, t_s)

    xp2 = x_prompt.reshape(t_p, D_MODEL)
    xs2 = x_sample.reshape(nbs * t_s, D_MODEL)
    seg = lambda off, n: w_pack[:, off:off + n]
    w_t = jnp.concatenate([seg(_OFF_Q, Q_DIM), seg(_OFF_G, LANES), seg(_OFF_KV + 3 * KV_DIM, KV_DIM),
                           seg(_OFF_KV + 5 * KV_DIM, KV_DIM)], axis=1).T
    pp = _proj(xp2, g_mix, w_pack, w_t, lb3, 512, True)
    ps = _proj(xs2, g_mix, w_pack, w_t, lb3, nbs * t_s, False)
    (kc_p, vc_p, ks_p, vs_p, kw_p, vw_p, ksb_p, kwb_p,
     qh_p, lf_p, kh_p, vh_p, gs_p, sa_p, sb_p, qat_p, gat_p, vst_p, vwt_p) = pp
    (qa_s, kc_s, vc_s, ks_s, vs_s, kw_s, vw_s, ga_s,
     qh_s, lf_s, kh_s, vh_s, gs_s, sa_s, sb_s) = ps

    ob_p, s_p = _hgrn_prompt(qh_p, kh_p, vh_p, lf_p, gs_p, gn)
    r3 = lambda a: a.reshape(nbs, t_s, a.shape[-1])
    ob_s, s_s = _hgrn_sample(r3(qh_s), r3(kh_s), r3(vh_s), r3(lf_s), r3(gs_s), gn, state_hgrn[0])

    ident = jnp.arange(t_p // PAGE, dtype=jnp.int32)[None, :]
    pool_rows = lambda a: a.reshape(-1, PAGE, KV_DIM)
    kcb_p, _ = _compress(pool_rows(kc_p), ident, pe_k, w1_k, w2_k, False)
    _, vct_p = _compress(pool_rows(vc_p), ident, pe_v, w1_v, w2_v, False)
    back = -(CMP_PAD + kcb_p.shape[1]) % LANES
    kcb_p = jnp.pad(kcb_p[0], ((CMP_PAD, back), (0, 0)))
    vct_p = jnp.pad(vct_p[0], ((0, 0), (CMP_PAD, back)))
    nrow_p = kcb_p.shape[0]
    mt_p = jnp.asarray(_score_matrix(nrow_p, CMP_PAD, t_p // SEL_BLOCK), BF16)
    kwb_pad = jnp.pad(kwb_p, ((WINDOW, 0), (0, 0)))
    vst3 = vst_p.reshape(KV_DIM, t_p // SEL_TILE, SEL_TILE).transpose(1, 0, 2)
    ones_rows = jnp.ones((t_p // SEL_TILE, 2 * SUBLANES, SEL_TILE), BF16)
    vst3 = jnp.stack([jnp.concatenate([vst3[:, kv * HEAD_DIM:(kv + 1) * HEAD_DIM, :], ones_rows], axis=1)
                      for kv in range(N_KV)])
    vwt3 = jnp.pad(vwt_p, ((0, 0), (WINDOW, 0))).reshape(KV_DIM, (t_p + WINDOW) // Q_BLOCK, Q_BLOCK)
    vwt3 = vwt3.transpose(1, 0, 2)
    eb = (np.arange(LANES)[None, :] == np.arange(SEL_TILE)[:, None] // SEL_BLOCK).astype(np.float32)
    oa_p = _nsa_prompt(qat_p, gat_p, kcb_p, vct_p, ksb_p, vst3, kwb_pad, vwt3, mt_p, cc, cs, bw,
                       jnp.asarray(eb, BF16))

    pool_t = lambda a: a.transpose(0, 2, 3, 1).reshape(-1, KV_DIM, PAGE)
    kcb_s, _ = _compress(pool_t(cache_k_cmp[0]), page_table, pe_k, w1_k, w2_k, True)
    _, vct_s = _compress(pool_t(cache_v_cmp[0]), page_table, pe_v, w1_v, w2_v, True)
    nq = N_HEADS * t_s
    qs4 = qa_s.reshape(nbs, t_s, GROUP, N_KV, HEAD_DIM).astype(F32)
    qx = jnp.einsum('btjkd,kq->bkjtqd', qs4, jnp.eye(N_KV, dtype=F32)).reshape(nbs, nq, KV_DIM)
    qx = jnp.pad(qx, ((0, 0), (0, S_ROWS - nq), (0, 0))).astype(BF16)
    g4 = ga_s[:, :3 * N_HEADS].reshape(nbs, t_s, N_KV, GROUP, 3)
    gm = jnp.transpose(g4, (0, 4, 2, 3, 1)).reshape(nbs, 3, nq, 1)
    gm = jnp.broadcast_to(gm, (nbs, 3, nq, KV_DIM))
    new_tile = lambda a: jnp.pad(a.reshape(nbs, t_s, KV_DIM), ((0, 0), (0, LANES - t_s), (0, 0)))
    mt_s = jnp.asarray(_score_matrix(past // CMP_STRIDE, 0, past // SEL_BLOCK), BF16)
    gsum = np.zeros((S_ROWS, S_ROWS), np.float32)
    for kv in range(N_KV):
        for j in range(GROUP):
            for t in range(t_s):
                gsum[(kv * GROUP + j) * t_s + t, kv * t_s + t] = 1.0
    o_kv = _nsa_sample(page_table, qx, gm, kcb_s, vct_s,
                       pool_t(cache_k_slc[0]), pool_t(cache_v_slc[0]),
                       new_tile(ks_s), new_tile(vs_s), state_k_win[0].reshape(nbs, WINDOW, KV_DIM),
                       state_v_win[0].reshape(nbs, WINDOW, KV_DIM), new_tile(kw_s), new_tile(vw_s),
                       mt_s, jnp.asarray(gsum), ccs, css, cns, cws)
    o5 = o_kv.reshape(nbs, N_KV, GROUP, t_s, N_KV, HEAD_DIM)
    oa_s = jnp.einsum('bkjtqd,kq->btjkd', o5, jnp.eye(N_KV, dtype=F32)).reshape(nbs * t_s, Q_DIM).astype(BF16)

    y_p = _ffn(xp2, oa_p, ob_p, sa_p, sb_p, wpa, wpb, wo, nf, wg, wu, wd, nl, 256)
    y_s = _ffn(xs2, oa_s, ob_s.reshape(nbs * t_s, HG_W), sa_s, sb_s, wpa, wpb, wo, nf, wg, wu, wd, nl, nbs * t_s)

    kv5 = lambda a, nb_, tt: a.reshape(1, nb_, tt, N_KV, HEAD_DIM)
    wl = min(WINDOW, t_p)
    win = lambda st, new: jnp.concatenate(
        [st[0], new.reshape(nbs, t_s, N_KV, HEAD_DIM)], axis=1)[:, -WINDOW:][None]
    return (y_p.reshape(1, t_p, D_MODEL), y_s.reshape(nbs, t_s, D_MODEL),
            kv5(kc_p, 1, t_p), kv5(vc_p, 1, t_p), kv5(ks_p, 1, t_p), kv5(vs_p, 1, t_p),
            kv5(kw_p[-wl:], 1, wl), kv5(vw_p[-wl:], 1, wl), s_p[None, None],
            kv5(kc_s, nbs, t_s), kv5(vc_s, nbs, t_s), kv5(ks_s, nbs, t_s), kv5(vs_s, nbs, t_s),
            win(state_k_win, kw_s), win(state_v_win, vw_s), s_s[None])
```

```python
import functools
import math

import numpy as np
import jax
import jax.numpy as jnp
from jax import lax
from jax.experimental import pallas as pl
from jax.experimental.pallas import tpu as pltpu

F32 = jnp.float32
BF16 = jnp.bfloat16
HIGHEST = lax.Precision.HIGHEST

D_MODEL = 1024
N_HEADS = 8
N_KV = 2
GROUP = N_HEADS // N_KV
HEAD_DIM = 64
KV_DIM = N_KV * HEAD_DIM
Q_DIM = N_HEADS * HEAD_DIM
CMP_BLOCK = 32
CMP_STRIDE = 16
CMP_HIDDEN = 2 * HEAD_DIM
SEL_BLOCK = 64
N_SEL = 16
WINDOW = 512
Q_BLOCK = 128
FORCE_BONUS = 1e4
N_BUCKETS = 32
MAX_DISTANCE = 128
HG_HEADS = 4
HG_DIM = 128
HG_CHUNK = 64
HG_SUB = 16
HG_W = HG_HEADS * HG_DIM
D_FF = ((8 * D_MODEL // 3 + 255) // 256) * 256
EPS = 1e-6
PAGE = 128
NEG = -1e30
LOG2E = math.log2(math.e)
Q_SCALE = HEAD_DIM ** -0.5 * LOG2E

LANES = 128
SUBLANES = 8
VMEM_LIMIT = 56 * 1024 * 1024

_OFF_Q = 0
_OFF_KV = _OFF_Q + Q_DIM
_OFF_G = _OFF_KV + 6 * KV_DIM
_OFF_HG = _OFF_G + LANES
_OFF_GATE = _OFF_HG + 4 * HG_W
_PROJ_N = _OFF_GATE + 2 * D_MODEL

_HEAD_PERM = np.array([h for j in range(GROUP) for h in (j, GROUP + j)])

SEL_TILE = 512
BLK_PER_TILE = SEL_TILE // SEL_BLOCK
CMP_PAD = 16
CMP_BAND = 24


def _cparams(sem, vmem=VMEM_LIMIT):
    return pltpu.CompilerParams(dimension_semantics=sem, vmem_limit_bytes=vmem)


def _const_spec(shape):
    nd = len(shape)
    return pl.BlockSpec(shape, lambda *_: (0,) * nd, pipeline_mode=pl.Buffered(1))


def _bucket_table():
    n = np.arange(256)
    max_exact = N_BUCKETS // 2
    nf = np.maximum(n, 1).astype(np.float64)
    large = max_exact + (np.log(nf / max_exact) / math.log(MAX_DISTANCE / max_exact)
                         * (N_BUCKETS - max_exact)).astype(np.int64)
    large = np.minimum(large, N_BUCKETS - 1)
    return np.where(n < max_exact, n, large)


_BUCKET = _bucket_table()


def _proj_kernel(x_ref, g_ref, w_ref, wt_ref, lb_ref, *out_refs, prompt):
    if prompt:
        (kc_ref, vc_ref, ks_ref, vs_ref, kw_ref, vw_ref, ksb_ref, kwb_ref,
         qh_ref, lf_ref, kh_ref, vh_ref, gs_ref, sa_ref, sb_ref,
         qat_ref, gat_ref, vst_ref, vwt_ref) = out_refs
    else:
        (qa_ref, kc_ref, vc_ref, ks_ref, vs_ref, kw_ref, vw_ref, ga_ref,
         qh_ref, lf_ref, kh_ref, vh_ref, gs_ref, sa_ref, sb_ref) = out_refs
    x = x_ref[...]
    xn = x * lax.rsqrt(jnp.mean(x * x, axis=-1, keepdims=True) + EPS) * g_ref[...]
    xb = xn.astype(BF16)

    def seg(a, n):
        return jnp.dot(xb, w_ref[:, a:a + n], preferred_element_type=F32)

    f32_refs = (kc_ref, vc_ref, ks_ref, vs_ref, kw_ref, vw_ref)
    for i in range(0, 6, 2):
        u = seg(_OFF_KV + i * KV_DIM, 2 * KV_DIM)
        f32_refs[i][...] = u[:, :KV_DIM]
        f32_refs[i + 1][...] = u[:, KV_DIM:]
        if prompt and i > 0:
            (ksb_ref if i == 2 else kwb_ref)[...] = u[:, :KV_DIM].astype(BF16)

    if prompt:
        def seg_t(a, n):
            return lax.dot_general(wt_ref[a:a + n, :], xb, (((1,), (1,)), ((), ())),
                                   preferred_element_type=F32)

        qat_ref[...] = (seg_t(0, Q_DIM) * Q_SCALE).astype(BF16)
        gat_ref[...] = jax.nn.sigmoid(seg_t(Q_DIM, LANES))
        vt = seg_t(Q_DIM + LANES, 2 * KV_DIM)
        vst_ref[...] = vt[:KV_DIM, :].astype(BF16)
        vwt_ref[...] = vt[KV_DIM:, :].astype(BF16)
    else:
        qa_ref[...] = (seg(_OFF_Q, Q_DIM) * Q_SCALE).astype(BF16)
        ga_ref[...] = jax.nn.sigmoid(seg(_OFF_G, LANES))

    log_lb = lb_ref[0:1, :]
    log_1m = lb_ref[1:2, :]
    one_m = lb_ref[2:3, :]
    qh_ref[...] = jax.nn.silu(seg(_OFF_HG, HG_W))
    z = seg(_OFF_HG + HG_W, HG_W)
    b = log_1m + (jnp.minimum(z, 0.0) - jnp.log1p(jnp.exp(-jnp.abs(z))))
    hi = jnp.maximum(log_lb, b)
    lf_ref[...] = hi + jnp.log1p(jnp.exp(-jnp.abs(log_lb - b)))
    kh_ref[...] = one_m * jax.nn.sigmoid(-z)
    vh_ref[...] = seg(_OFF_HG + 2 * HG_W, HG_W)
    gs_ref[...] = jax.nn.silu(seg(_OFF_HG + 3 * HG_W, HG_W))
    sa_ref[...] = jax.nn.sigmoid(seg(_OFF_GATE, D_MODEL))
    sb_ref[...] = jax.nn.sigmoid(seg(_OFF_GATE + D_MODEL, D_MODEL))


_PROJ_T = Q_DIM + LANES + 2 * KV_DIM


def _proj(x2d, g, w, wt, lb3, tm, prompt):
    rows = x2d.shape[0]
    tail = [(HG_W, F32)] * 5 + [(D_MODEL, F32)] * 2
    if prompt:
        widths = [(KV_DIM, F32)] * 6 + [(KV_DIM, BF16)] * 2 + tail
        heights = [(Q_DIM, BF16), (LANES, F32), (KV_DIM, BF16), (KV_DIM, BF16)]
    else:
        widths = [(Q_DIM, BF16)] + [(KV_DIM, F32)] * 6 + [(LANES, F32)] + tail
        heights = []
    return pl.pallas_call(
        functools.partial(_proj_kernel, prompt=prompt),
        grid=(rows // tm,),
        in_specs=[pl.BlockSpec((tm, D_MODEL), lambda i: (i, 0)),
                  _const_spec((1, D_MODEL)),
                  _const_spec((D_MODEL, _PROJ_N)),
                  _const_spec((_PROJ_T, D_MODEL)),
                  _const_spec((SUBLANES, HG_W))],
        out_specs=([pl.BlockSpec((tm, n), lambda i: (i, 0)) for n, _ in widths]
                   + [pl.BlockSpec((n, tm), lambda i: (0, i)) for n, _ in heights]),
        out_shape=([jax.ShapeDtypeStruct((rows, n), dt) for n, dt in widths]
                   + [jax.ShapeDtypeStruct((n, rows), dt) for n, dt in heights]),
        compiler_params=_cparams(("arbitrary",)),
        name="proj",
    )(x2d, g, w, wt, lb3)


_CH_W = CMP_STRIDE * KV_DIM
_CH_PER_PAGE = PAGE // CMP_STRIDE
_CH_PITCH = CMP_STRIDE + SUBLANES


def _compress_kernel(pt_ref, pool_ref, pe_ref, w1_ref, w2_ref, out_ref, outt_ref, buf, rbuf, xa, hbuf, sem,
                     *, n_pages, pages_transposed):
    b = pl.program_id(0)
    nb = pl.num_programs(0)
    slot = b % 2
    n_ch = n_pages * _CH_PER_PAGE

    def page_copy(bb, p, s):
        return pltpu.make_async_copy(pool_ref.at[pt_ref[bb, p]], buf.at[s, p], sem.at[s])

    def start_all(bb, s):
        def body(p, c):
            page_copy(bb, p, s).start()
            return c
        lax.fori_loop(0, n_pages, body, 0)

    @pl.when(b == 0)
    def _():
        start_all(b, slot)

    @pl.when(b + 1 < nb)
    def _():
        start_all(b + 1, 1 - slot)

    pltpu.make_async_copy(buf.at[slot], buf.at[slot], sem.at[slot]).wait()

    rows = math.gcd(n_ch, 256)
    pages_per_group = rows // _CH_PER_PAGE

    def to_rows(g):
        for p in range(g * pages_per_group, (g + 1) * pages_per_group):
            page = buf[slot, p]
            page = page.T if pages_transposed else page
            for i in range(_CH_PER_PAGE):
                dst = (p * _CH_PER_PAGE + i) * _CH_PITCH
                rbuf[dst:dst + CMP_STRIDE, :] = page[i * CMP_STRIDE:(i + 1) * CMP_STRIDE, :]

    to_rows(0)
    for r in range(n_ch // rows):
        if r + 1 < n_ch // rows:
            to_rows(r + 1)
        for s in range(CMP_STRIDE):
            x = rbuf[pl.ds(r * rows * _CH_PITCH + s, rows, stride=_CH_PITCH), :]
            xa[:, s * KV_DIM:(s + 1) * KV_DIM] = x.astype(BF16)
        hbuf[r * rows:(r + 1) * rows, :] = jnp.dot(xa[...], w1_ref[...], preferred_element_type=F32)
    pw = _split_dot_rhs(pe_ref[...], w1_ref[...])
    nh = N_KV * CMP_HIDDEN
    bias = pw[0:1, 0:nh] + pw[1:2, nh:2 * nh]
    h = hbuf[:, 0:nh] + pltpu.roll(hbuf[:, nh:2 * nh], n_ch - 1, 0) + bias
    blocks = jnp.dot(jax.nn.gelu(h).astype(BF16), w2_ref[...], preferred_element_type=F32)
    row = lax.broadcasted_iota(jnp.int32, blocks.shape, 0)
    blocks = jnp.where(row < n_ch - 1, blocks, 0.0)
    out_ref[0] = blocks.astype(BF16)
    outt_ref[0] = blocks.T.astype(BF16)


def _compress(pool, page_table, pe, w1, w2, pages_transposed):
    nbatch, n_pages = page_table.shape
    n_ch = n_pages * _CH_PER_PAGE
    rows = math.gcd(n_ch, 256)
    grid_spec = pltpu.PrefetchScalarGridSpec(
        num_scalar_prefetch=1,
        grid=(nbatch,),
        in_specs=[pl.BlockSpec(memory_space=pl.ANY),
                  _const_spec((SUBLANES, _CH_W)),
                  _const_spec((_CH_W, 4 * CMP_HIDDEN)),
                  _const_spec((2 * CMP_HIDDEN, KV_DIM))],
        out_specs=[pl.BlockSpec((1, n_ch, KV_DIM), lambda b, pt: (b, 0, 0)),
                   pl.BlockSpec((1, KV_DIM, n_ch), lambda b, pt: (b, 0, 0))],
        scratch_shapes=[pltpu.VMEM((2, n_pages, PAGE, KV_DIM), F32),
                        pltpu.VMEM((n_ch * _CH_PITCH, KV_DIM), F32),
                        pltpu.VMEM((rows, _CH_W), BF16),
                        pltpu.VMEM((n_ch, 4 * CMP_HIDDEN), F32),
                        pltpu.SemaphoreType.DMA((2,))],
    )
    return pl.pallas_call(
        functools.partial(_compress_kernel, n_pages=n_pages, pages_transposed=pages_transposed),
        grid_spec=grid_spec,
        out_shape=[jax.ShapeDtypeStruct((nbatch, n_ch, KV_DIM), BF16),
                   jax.ShapeDtypeStruct((nbatch, KV_DIM, n_ch), BF16)],
        compiler_params=_cparams(("arbitrary",)),
        name="compress",
    )(page_table, pool, pe, w1, w2)


def _compress_weights(pe, w1, w2):
    c = CMP_BLOCK // CMP_STRIDE
    pe_r = pe.reshape(c, CMP_STRIDE, 1, HEAD_DIM)
    pe_x = jnp.broadcast_to(pe_r, (c, CMP_STRIDE, N_KV, HEAD_DIM)).reshape(c, _CH_W)
    pe_x = jnp.pad(pe_x, ((0, SUBLANES - c), (0, 0)))
    w1_r = w1.reshape(c, CMP_STRIDE, HEAD_DIM, CMP_HIDDEN)
    eye = jnp.eye(N_KV, dtype=w1.dtype)
    w1_x = jnp.einsum('jsde,kq->skdjqe', w1_r, eye).reshape(_CH_W, c * N_KV * CMP_HIDDEN)
    w2_x = jnp.einsum('ed,kq->keqd', w2, eye).reshape(N_KV * CMP_HIDDEN, KV_DIM)
    return pe_x, w1_x.astype(BF16), w2_x.astype(BF16)


def _hgrn_chunk(q, k, v, lf, st_ref, chunk, sub):
    if chunk > SUBLANES:
        r = lax.broadcasted_iota(jnp.int32, (chunk, chunk), 0)
        c = lax.broadcasted_iota(jnp.int32, (chunk, chunk), 1)
        tri = (r >= c).astype(F32)
        b = jnp.dot(tri, lf, preferred_element_type=F32, precision=HIGHEST)
    else:
        rows = [lf[0:1, :]]
        for t in range(1, chunk):
            rows.append(rows[-1] + lf[t:t + 1, :])
        b = jnp.concatenate(rows, axis=0)
    bl = b[chunk - 1:chunk, :]
    qe = q * jnp.exp(b)
    kd = k * jnp.exp(bl - b)
    ebl = jnp.exp(bl)
    n_sub = chunk // sub
    trow = lax.broadcasted_iota(jnp.int32, (sub, HG_W), 0)
    crow = lax.broadcasted_iota(jnp.int32, (chunk, HG_W), 0)

    diag = []
    for i in range(n_sub):
        qi = q[i * sub:(i + 1) * sub, :]
        bi = b[i * sub:(i + 1) * sub, :]
        acc = [jnp.zeros((sub, HG_DIM), F32) for _ in range(HG_HEADS)]
        for s in range(sub):
            row = i * sub + s
            dec = jnp.exp(jnp.where(trow >= s, bi - b[row:row + 1, :], -jnp.inf))
            prod = qi * k[row:row + 1, :] * dec
            for h in range(HG_HEADS):
                a = jnp.sum(prod[:, h * HG_DIM:(h + 1) * HG_DIM], axis=1, keepdims=True)
                acc[h] = acc[h] + a * v[row:row + 1, h * HG_DIM:(h + 1) * HG_DIM]
        diag.append(acc)

    off = []
    for i in range(n_sub):
        if i == 0:
            off.append(None)
            continue
        b0 = b[i * sub - 1:i * sub, :]
        qs = (q[i * sub:(i + 1) * sub, :] * jnp.exp(b[i * sub:(i + 1) * sub, :] - b0)).astype(BF16)
        ks = (k * jnp.exp(jnp.where(crow < i * sub, b0 - b, -jnp.inf))).astype(BF16)
        off.append((qs, ks))

    vb = v.astype(BF16)
    outs = []
    for h in range(HG_HEADS):
        sl = slice(h * HG_DIM, (h + 1) * HG_DIM)
        st = st_ref[h]
        o_h = lax.dot_general(qe[:, sl].astype(BF16), st.astype(BF16), (((1,), (1,)), ((), ())),
                              preferred_element_type=F32)
        parts = []
        for i in range(n_sub):
            d = diag[i][h]
            if off[i] is not None:
                qs, ks = off[i]
                a = lax.dot_general(qs[:, sl], ks[:, sl], (((1,), (1,)), ((), ())),
                                    preferred_element_type=F32)
                d = d + jnp.dot(a.astype(BF16), vb[:, sl], preferred_element_type=F32)
            parts.append(d)
        intra = parts[0] if n_sub == 1 else jnp.concatenate(parts, axis=0)
        outs.append(o_h + intra)
        st_ref[h] = st * ebl[:, sl] + lax.dot_general(
            vb[:, sl], kd[:, sl].astype(BF16), (((0,), (0,)), ((), ())), preferred_element_type=F32)
    return jnp.concatenate(outs, axis=1)


def _hgrn_finish(o, gs, gn):
    outs = []
    for h in range(HG_HEADS):
        oh = o[:, h * HG_DIM:(h + 1) * HG_DIM]
        y = oh * lax.rsqrt(jnp.mean(oh * oh, axis=-1, keepdims=True) + EPS) * gn
        outs.append(y)
    return (jnp.concatenate(outs, axis=1) * gs).astype(BF16)


def _hgrn_prompt_kernel(q_ref, k_ref, v_ref, lf_ref, gs_ref, gn_ref, o_ref, s_ref, st_ref, *, n_chunks):
    i = pl.program_id(0)

    @pl.when(i == 0)
    def _():
        st_ref[...] = jnp.zeros_like(st_ref)

    def body(c, carry):
        r = pl.ds(pl.multiple_of(c * HG_CHUNK, HG_CHUNK), HG_CHUNK)
        o = _hgrn_chunk(q_ref[r, :], k_ref[r, :], v_ref[r, :], lf_ref[r, :], st_ref, HG_CHUNK, HG_SUB)
        o_ref[r, :] = _hgrn_finish(o, gs_ref[r, :], gn_ref[...])
        return carry
    lax.fori_loop(0, n_chunks, body, 0, unroll=4)

    @pl.when(i == pl.num_programs(0) - 1)
    def _():
        for h in range(HG_HEADS):
            s_ref[h] = st_ref[h].T


def _hgrn_prompt(qh, kh, vh, lf, gs, gn, rows_per_step=512):
    t = qh.shape[0]
    spec = pl.BlockSpec((rows_per_step, HG_W), lambda i: (i, 0))
    return pl.pallas_call(
        functools.partial(_hgrn_prompt_kernel, n_chunks=rows_per_step // HG_CHUNK),
        grid=(t // rows_per_step,),
        in_specs=[spec] * 5 + [_const_spec((1, HG_DIM))],
        out_specs=[spec, pl.BlockSpec((HG_HEADS, HG_DIM, HG_DIM), lambda i: (0, 0, 0))],
        out_shape=[jax.ShapeDtypeStruct((t, HG_W), BF16),
                   jax.ShapeDtypeStruct((HG_HEADS, HG_DIM, HG_DIM), F32)],
        scratch_shapes=[pltpu.VMEM((HG_HEADS, HG_DIM, HG_DIM), F32)],
        compiler_params=_cparams(("arbitrary",)),
        name="hgrn_prompt",
    )(qh, kh, vh, lf, gs, gn)


def _hgrn_sample_kernel(q_ref, k_ref, v_ref, lf_ref, gs_ref, gn_ref, s0_ref, o_ref, s_ref, st_ref, *, t):
    for h in range(HG_HEADS):
        st_ref[h] = s0_ref[0, h].T
    o = _hgrn_chunk(q_ref[0], k_ref[0], v_ref[0], lf_ref[0], st_ref, t, t)
    o_ref[0] = _hgrn_finish(o, gs_ref[0], gn_ref[...])
    for h in range(HG_HEADS):
        s_ref[0, h] = st_ref[h].T


def _hgrn_sample(qh, kh, vh, lf, gs, gn, s0):
    nb, t, _ = qh.shape
    spec = pl.BlockSpec((1, t, HG_W), lambda b: (b, 0, 0))
    sspec = pl.BlockSpec((1, HG_HEADS, HG_DIM, HG_DIM), lambda b: (b, 0, 0, 0))
    return pl.pallas_call(
        functools.partial(_hgrn_sample_kernel, t=t),
        grid=(nb,),
        in_specs=[spec] * 5 + [_const_spec((1, HG_DIM)), sspec],
        out_specs=[spec, sspec],
        out_shape=[jax.ShapeDtypeStruct((nb, t, HG_W), BF16),
                   jax.ShapeDtypeStruct((nb, HG_HEADS, HG_DIM, HG_DIM), F32)],
        scratch_shapes=[pltpu.VMEM((HG_HEADS, HG_DIM, HG_DIM), F32)],
        compiler_params=_cparams(("arbitrary",)),
        name="hgrn_sample",
    )(qh, kh, vh, lf, gs, gn, s0)


def _select_topk(x, blk, n, always=None):
    nblk = x.shape[0]
    if always is None:
        sel = jnp.zeros_like(x)
    else:
        sel = jnp.where(always, 1.0, 0.0)
        x = jnp.where(always, -3e38, x)
    for _ in range(n):
        m = jnp.max(x, axis=0, keepdims=True)
        idx = jnp.min(jnp.where(x == m, blk, float(nblk)), axis=0, keepdims=True)
        pick = blk == idx
        sel = jnp.where(pick, 1.0, sel)
        x = jnp.where(pick, -3e38, x)
    return sel


def _softmax_cols(s, valid):
    m = jnp.max(jnp.where(valid, s, NEG), axis=0, keepdims=True)
    p = jnp.where(valid, jnp.exp2(s - m), 0.0)
    l = jnp.sum(p, axis=0, keepdims=True)
    return p * jnp.where(l > 0.0, 1.0 / l, 0.0)


def _split_dot(a_bf16, x):
    hi = x.astype(BF16)
    lo = (x - hi.astype(F32)).astype(BF16)
    return (jnp.dot(a_bf16, hi, preferred_element_type=F32)
            + jnp.dot(a_bf16, lo, preferred_element_type=F32))


def _split_dot_rhs(x, w_bf16):
    hi = x.astype(BF16)
    lo = (x - hi.astype(F32)).astype(BF16)
    return (jnp.dot(hi, w_bf16, preferred_element_type=F32)
            + jnp.dot(lo, w_bf16, preferred_element_type=F32))


def _nt(a, b):
    return lax.dot_general(a, b, (((1,), (1,)), ((), ())), preferred_element_type=F32)


def _online_update(s, v, m_ref, l_ref, acc_ref, v_transposed=False):
    m_old = m_ref[...]
    m_new = jnp.maximum(m_old, jnp.max(s, axis=1, keepdims=True))
    p = jnp.exp2(s - m_new)
    alpha = jnp.exp2(m_old - m_new)
    l_ref[...] = alpha * l_ref[...] + jnp.sum(p, axis=1, keepdims=True)
    pv = _nt(p.astype(BF16), v) if v_transposed else jnp.dot(p.astype(BF16), v, preferred_element_type=F32)
    acc_ref[...] = alpha * acc_ref[...] + pv
    m_ref[...] = m_new


def _score_matrix(n_rows, row_offset, n_blocks):
    c = CMP_BLOCK // CMP_STRIDE
    ratio = SEL_BLOCK // CMP_STRIDE
    n_ov = ratio + c - 1
    m = np.zeros((n_blocks, n_rows), np.float32)
    for j in range(n_blocks):
        for u in range(n_ov):
            start = CMP_STRIDE * (u - (c - 1))
            w_u = (min(start + CMP_BLOCK, SEL_BLOCK) - max(start, 0)) / CMP_STRIDE
            n = ratio * j + u - (c - 1)
            if 0 <= n and n + row_offset < n_rows:
                m[j, n + row_offset] = w_u
    return m


def _nsa_prompt_kernel(qt_ref, gt_ref, kc_ref, vct_ref, ks_ref, vst_ref, kw_ref, vwt_ref,
                       mt_ref, cc_ref, cs_ref, bw_ref, eb_ref, o_ref,
                       sc_ref, sw_ref, sel_ref, m_ref, acc_ref, oc_ref,
                       sa_ref, sb_ref, pa_ref, pb_ref, ala_ref, alb_ref, *, n_blocks, variants):
    qb = pl.program_id(0)
    nrow = kc_ref.shape[0]
    gq = GROUP * Q_BLOCK
    tiles_per_q = SEL_TILE // Q_BLOCK
    nband = WINDOW + Q_BLOCK
    max_tile = n_blocks // BLK_PER_TILE - 1
    frow = lax.broadcasted_iota(jnp.int32, (KV_DIM, Q_BLOCK), 0)

    def lanes4(x):
        return jnp.concatenate([x] * GROUP, axis=1)

    qx = []
    for kv in range(N_KV):
        keep = (frow >= HEAD_DIM) if kv else (frow < HEAD_DIM)
        qx.append(jnp.concatenate(
            [jnp.where(keep, qt_ref[j * KV_DIM:(j + 1) * KV_DIM, :], jnp.zeros((), BF16))
             for j in range(GROUP)], axis=1))

    def qk_stage(kt, s_ref, penalty=0.0):
        k = ks_ref[pl.ds(pl.multiple_of(kt * SEL_TILE, SEL_TILE), SEL_TILE), :]
        k_aug = jnp.concatenate([k, eb_ref[...]], axis=1)
        for kv in range(N_KV):
            srow = sel_ref[kv, pl.ds(pl.multiple_of(kt * BLK_PER_TILE, BLK_PER_TILE), BLK_PER_TILE), :]
            mrows = jnp.concatenate([lanes4(srow + penalty), jnp.zeros((KV_DIM - BLK_PER_TILE, gq), F32)], axis=0)
            q_aug = jnp.concatenate([qx[kv], mrows.astype(BF16)], axis=0)
            s_ref[kv] = jnp.dot(k_aug, q_aug, preferred_element_type=F32)

    def gate(kv, i):
        return jnp.concatenate(
            [gt_ref[3 * (kv * GROUP + j) + i:3 * (kv * GROUP + j) + i + 1, :] for j in range(GROUP)], axis=1)

    def compressed_and_select(n_r, n_b):
        for kv in range(N_KV):
            sc_ref[kv, 0:n_r, :] = jnp.dot(kc_ref[0:n_r, :], qx[kv], preferred_element_type=F32)
        r = lax.broadcasted_iota(jnp.int32, (n_r, Q_BLOCK), 0)
        qpos_c = qb * Q_BLOCK + lax.broadcasted_iota(jnp.int32, (n_r, Q_BLOCK), 1)
        end_pos = (r - CMP_PAD) * CMP_STRIDE + (CMP_BLOCK - 1)
        vis = (r >= CMP_PAD) & (r < CMP_PAD + n_blocks * (SEL_BLOCK // CMP_STRIDE) - 1) & (end_pos <= qpos_c)
        vis_add = lanes4(jnp.where(vis, 0.0, NEG))
        band = pl.ds(pl.multiple_of(qb * SUBLANES, SUBLANES), CMP_BAND)
        blk_i = lax.broadcasted_iota(jnp.int32, (n_b, Q_BLOCK), 0)
        cur = (qb * Q_BLOCK + lax.broadcasted_iota(jnp.int32, (n_b, Q_BLOCK), 1)) // SEL_BLOCK
        forced = (blk_i == 0) | (blk_i == cur) | (blk_i == cur - 1)
        scores = []
        for kv in range(N_KV):
            sc_ref[kv, band, :] = sc_ref[kv, band, :] + cc_ref[kv]
            s = sc_ref[kv, 0:n_r, :] + vis_add
            m = jnp.max(s, axis=0, keepdims=True)
            p = jnp.exp2(s - m)
            l = jnp.sum(p, axis=0, keepdims=True)
            pn = p * jnp.where(m > 0.5 * NEG, 1.0 / l, 0.0)
            oc_ref[kv] = gate(kv, 0) * jnp.dot(vct_ref[:, 0:n_r], pn.astype(BF16), preferred_element_type=F32)
            imp = pn[:, 0:Q_BLOCK]
            for j in range(1, GROUP):
                imp = imp + pn[:, j * Q_BLOCK:(j + 1) * Q_BLOCK]
            score = _split_dot(mt_ref[0:n_b, 0:n_r], imp)
            scores.append(jnp.where(blk_i <= cur, score, -FORCE_BONUS))
        kband = kw_ref[pl.ds(pl.multiple_of(qb * Q_BLOCK, Q_BLOCK), nband), :]
        for kv in range(N_KV):
            sw_ref[kv] = jnp.dot(kband, qx[kv], preferred_element_type=F32)
        blk2 = jnp.concatenate([blk_i.astype(F32)] * N_KV, axis=1)
        forced2 = jnp.concatenate([forced] * N_KV, axis=1)
        sel = _select_topk(jnp.concatenate(scores, axis=1), blk2, N_SEL - 3, always=forced2)
        for kv in range(N_KV):
            sel_ref[kv, 0:n_b, :] = (sel[:, kv * Q_BLOCK:(kv + 1) * Q_BLOCK] - 1.0) * (-NEG)

    sel_ref[...] = jnp.full_like(sel_ref, NEG)
    lo = 0
    for q_hi, n_r, n_b in variants:
        @pl.when((qb >= lo) & (qb <= q_hi))
        def _(n_r=n_r, n_b=n_b):
            compressed_and_select(n_r, n_b)
        lo = q_hi + 1

    m_ref[...] = jnp.full_like(m_ref, NEG)
    acc_ref[...] = jnp.zeros_like(acc_ref)

    def pv_stage(kt, p_ref, al_ref):
        for kv in range(N_KV):
            acc_ref[kv] = al_ref[kv] * acc_ref[kv] + jnp.dot(vst_ref[kv, kt], p_ref[kv],
                                                             preferred_element_type=F32)

    def sm_stage(kt, s_ref, p_ref, al_ref, near):
        d0 = qb - kt * tiles_per_q
        for kv in range(N_KV):
            s = s_ref[kv]
            if near:
                s = s + jnp.concatenate(
                    [cs_ref[kv, jnp.clip(d0 - i, -1, 2) + 1] for i in range(tiles_per_q)], axis=0)
            m_old = m_ref[kv]
            m_new = jnp.maximum(m_old, jnp.max(s, axis=0, keepdims=True))
            p_ref[kv] = jnp.exp2(s - m_new).astype(BF16)
            al_ref[kv] = jnp.exp2(m_old - m_new)
            m_ref[kv] = m_new

    n_far = jnp.maximum(qb - 1, 0) // tiles_per_q
    pb_ref[...] = jnp.zeros_like(pb_ref)
    alb_ref[...] = jnp.ones_like(alb_ref)
    qk_stage(0, sa_ref)

    def window(kv):
        vband = jnp.concatenate([vwt_ref[qb + i] for i in range(nband // Q_BLOCK)], axis=1)
        krow = lax.broadcasted_iota(jnp.int32, (nband, Q_BLOCK), 0)
        pos_add = lanes4(jnp.where(krow >= WINDOW - qb * Q_BLOCK, 0.0, NEG))
        s = sw_ref[kv] + bw_ref[kv] + pos_add
        m = jnp.max(s, axis=0, keepdims=True)
        p = jnp.exp2(s - m)
        o_w = (jnp.dot(vband, p.astype(BF16), preferred_element_type=F32)
               / jnp.sum(p, axis=0, keepdims=True))
        oc_ref[kv] = oc_ref[kv] + gate(kv, 2) * o_w

    window(0)

    def pair_body(u, c):
        t0 = 2 * u
        t1 = jnp.minimum(t0 + 1, max_tile)
        t2 = jnp.minimum(t0 + 2, max_tile)
        qk_stage(t1, sb_ref, jnp.where(t0 + 1 < n_far, 0.0, NEG))
        pv_stage(jnp.maximum(t0 - 1, 0), pb_ref, alb_ref)
        sm_stage(t0, sa_ref, pa_ref, ala_ref, False)
        qk_stage(t2, sa_ref)
        pv_stage(t0, pa_ref, ala_ref)
        sm_stage(t1, sb_ref, pb_ref, alb_ref, False)
        return c
    n_pairs = (n_far + 1) // 2
    lax.fori_loop(0, n_pairs, pair_body, 0)

    near_a = n_far
    near_b = jnp.minimum(n_far + 1, max_tile)
    qk_stage(near_a, sa_ref)
    pv_stage(jnp.clip(2 * n_pairs - 1, 0, max_tile), pb_ref, alb_ref)
    window(1)
    sm_stage(near_a, sa_ref, pa_ref, ala_ref, True)
    pv_stage(near_a, pa_ref, ala_ref)

    @pl.when(n_far + 1 <= qb // tiles_per_q)
    def _():
        qk_stage(near_b, sb_ref)
        sm_stage(near_b, sb_ref, pb_ref, alb_ref, True)
        pv_stage(near_b, pb_ref, alb_ref)

    o_kv = []
    for kv in range(N_KV):
        acc = acc_ref[kv]
        o_s = acc[0:HEAD_DIM, :] / acc[HEAD_DIM:HEAD_DIM + 1, :]
        o_kv.append(oc_ref[kv, kv * HEAD_DIM:(kv + 1) * HEAD_DIM, :] + gate(kv, 1) * o_s)
    o_t = jnp.concatenate(o_kv, axis=0)
    for j in range(GROUP):
        o_ref[:, j * LANES:(j + 1) * LANES] = o_t[:, j * Q_BLOCK:(j + 1) * Q_BLOCK].T.astype(BF16)


def _nsa_prompt(qat, gat, kc, vct, ksb, vst3, kwb, vwt3, mt, cc, cs, bw, eb):
    t = qat.shape[1]
    n_blocks = t // SEL_BLOCK
    nrow = kc.shape[0]
    gq = GROUP * Q_BLOCK
    n_q = t // Q_BLOCK
    variants = []
    for n_r in list(range(2 * LANES, nrow - LANES, 2 * LANES)) + [nrow]:
        q_hi = n_q - 1 if n_r == nrow else min((n_r - CMP_BAND) // SUBLANES, n_q - 1)
        n_b = min(-(-(2 * q_hi + 2) // (2 * SUBLANES)) * (2 * SUBLANES), n_blocks)
        variants.append((q_hi, n_r, n_b))
    return pl.pallas_call(
        functools.partial(_nsa_prompt_kernel, n_blocks=n_blocks, variants=tuple(variants)),
        grid=(t // Q_BLOCK,),
        in_specs=[pl.BlockSpec((Q_DIM, Q_BLOCK), lambda i: (0, i)),
                  pl.BlockSpec((LANES, Q_BLOCK), lambda i: (0, i)),
                  _const_spec(kc.shape), _const_spec(vct.shape),
                  _const_spec(ksb.shape), _const_spec(vst3.shape),
                  _const_spec(kwb.shape), _const_spec(vwt3.shape),
                  _const_spec(mt.shape), _const_spec(cc.shape),
                  _const_spec(cs.shape), _const_spec(bw.shape), _const_spec(eb.shape)],
        out_specs=pl.BlockSpec((Q_BLOCK, Q_DIM), lambda i: (i, 0)),
        out_shape=jax.ShapeDtypeStruct((t, Q_DIM), BF16),
        scratch_shapes=[pltpu.VMEM((N_KV, nrow, gq), F32),
                        pltpu.VMEM((N_KV, WINDOW + Q_BLOCK, gq), F32),
                        pltpu.VMEM((N_KV, n_blocks, Q_BLOCK), F32),
                        pltpu.VMEM((N_KV, 1, gq), F32),
                        pltpu.VMEM((N_KV, vst3.shape[2], gq), F32),
                        pltpu.VMEM((N_KV, KV_DIM, gq), F32),
                        pltpu.VMEM((N_KV, SEL_TILE, gq), F32), pltpu.VMEM((N_KV, SEL_TILE, gq), F32),
                        pltpu.VMEM((N_KV, SEL_TILE, gq), BF16), pltpu.VMEM((N_KV, SEL_TILE, gq), BF16),
                        pltpu.VMEM((N_KV, 1, gq), F32), pltpu.VMEM((N_KV, 1, gq), F32)],
        compiler_params=_cparams(("arbitrary",)),
        name="nsa_prompt",
    )(qat, gat, kc, vct, ksb, vst3, kwb, vwt3, mt, cc, cs, bw, eb)


S_ROWS = 128
S_CHUNK_PAGES = 128
S_CHUNK = S_CHUNK_PAGES * PAGE


def _nsa_sample_kernel(pt_ref, qx_ref, gm_ref, kc_ref, vct_ref, kpool_ref, vpool_ref,
                       knew_ref, vnew_ref, kwin_ref, vwin_ref, kwnew_ref, vwnew_ref,
                       mt_ref, gsum_ref, ccs_ref, css_ref, cns_ref, cws_ref, o_ref,
                       kbuf, vbuf, sem, mask_ref, m_ref, l_ref, acc_ref, oc_ref, ow_ref,
                       *, n_chunks, n_blocks):
    b = pl.program_id(0)
    c = pl.program_id(1)
    step = b * n_chunks + c
    total = pl.num_programs(0) * n_chunks
    slot = step % 2

    def copies(bb, cc, s, p):
        pg = cc * S_CHUNK_PAGES + p
        dst = pl.ds(pl.multiple_of(p * PAGE, PAGE), PAGE)
        return (pltpu.make_async_copy(kpool_ref.at[pt_ref[bb, pg]], kbuf.at[s, :, dst], sem.at[0, s]),
                pltpu.make_async_copy(vpool_ref.at[pt_ref[bb, pg]], vbuf.at[s, :, dst], sem.at[1, s]))

    def start_all(st, s):
        bb = st // n_chunks
        cc = st % n_chunks

        def body(p, carry):
            ck, cv = copies(bb, cc, s, p)
            ck.start()
            cv.start()
            return carry
        lax.fori_loop(0, S_CHUNK_PAGES, body, 0)

    @pl.when(step == 0)
    def _():
        start_all(step, slot)

    @pl.when(step + 1 < total)
    def _():
        start_all(step + 1, 1 - slot)

    qx = qx_ref[0]
    n_q = o_ref.shape[1]
    qq = qx[:n_q]

    @pl.when(c == 0)
    def _():
        nrow = kc_ref.shape[1]
        s = _nt(kc_ref[0], qx)
        r = lax.broadcasted_iota(jnp.int32, (nrow, S_ROWS), 0)
        band0 = nrow - CMP_BAND
        s = s + jnp.concatenate([jnp.zeros((band0, S_ROWS), F32), ccs_ref[...]], axis=0)
        pn = _softmax_cols(s, r < nrow - 1)
        oc_ref[...] = jnp.dot(vct_ref[0], pn.astype(BF16), preferred_element_type=F32).T[:n_q]
        imp = jnp.dot(pn, gsum_ref[...], preferred_element_type=F32, precision=HIGHEST)
        score = _split_dot(mt_ref[...], imp)
        blk_i = lax.broadcasted_iota(jnp.int32, (n_blocks, S_ROWS), 0)
        forced = (blk_i == 0) | (blk_i == n_blocks - 1)
        selt = _select_topk(score, blk_i.astype(F32), N_SEL - 3, always=forced)
        selt = lax.dot_general(selt.astype(BF16), gsum_ref[...].astype(BF16), (((1,), (1,)), ((), ())),
                               preferred_element_type=F32)
        selm1 = selt.T[:n_q] - 1.0
        e_r = lax.broadcasted_iota(jnp.int32, (BLK_PER_TILE, SEL_TILE), 0)
        e_c = lax.broadcasted_iota(jnp.int32, (BLK_PER_TILE, SEL_TILE), 1)
        expand = jnp.where(e_c // SEL_BLOCK == e_r, -NEG, 0.0)
        for kt in range(n_blocks // BLK_PER_TILE):
            mask_ref[:, kt * SEL_TILE:(kt + 1) * SEL_TILE] = jnp.dot(
                selm1[:, kt * BLK_PER_TILE:(kt + 1) * BLK_PER_TILE], expand, preferred_element_type=F32)

        sw = _nt(qq, kwin_ref[0].astype(BF16)) + cws_ref[:n_q, :]
        sn = _nt(qq, kwnew_ref[0].astype(BF16)) + cns_ref[:n_q, :]
        m = jnp.maximum(jnp.max(sw, axis=1, keepdims=True), jnp.max(sn, axis=1, keepdims=True))
        pw = jnp.exp2(sw - m)
        pn2 = jnp.exp2(sn - m)
        l = jnp.sum(pw, axis=1, keepdims=True) + jnp.sum(pn2, axis=1, keepdims=True)
        ow = (jnp.dot(pw.astype(BF16), vwin_ref[0].astype(BF16), preferred_element_type=F32)
              + jnp.dot(pn2.astype(BF16), vwnew_ref[0].astype(BF16), preferred_element_type=F32))
        ow_ref[...] = ow / l

        m_ref[...] = jnp.full_like(m_ref, NEG)
        l_ref[...] = jnp.zeros_like(l_ref)
        acc_ref[...] = jnp.zeros_like(acc_ref)

    pltpu.make_async_copy(kbuf.at[slot], kbuf.at[slot], sem.at[0, slot]).wait()
    pltpu.make_async_copy(vbuf.at[slot], vbuf.at[slot], sem.at[1, slot]).wait()

    col0 = pl.multiple_of(c * S_CHUNK, S_CHUNK)
    s = (jnp.dot(qq, kbuf[slot].astype(BF16), preferred_element_type=F32)
         + mask_ref[:, pl.ds(col0, S_CHUNK)])

    @pl.when(c < n_chunks - 1)
    def _():
        _online_update(s, vbuf[slot].astype(BF16), m_ref, l_ref, acc_ref, v_transposed=True)

    @pl.when(c == n_chunks - 1)
    def _():
        near = jnp.concatenate([jnp.zeros((n_q, S_CHUNK - LANES), F32), css_ref[:n_q, :]], axis=1)
        _online_update(s + near, vbuf[slot].astype(BF16), m_ref, l_ref, acc_ref, v_transposed=True)
        sn = _nt(qq, knew_ref[0].astype(BF16)) + cns_ref[:n_q, :]
        _online_update(sn, vnew_ref[0].astype(BF16), m_ref, l_ref, acc_ref)
        o_s = acc_ref[...] / l_ref[...]
        o_ref[0] = gm_ref[0, 0] * oc_ref[...] + gm_ref[0, 1] * o_s + gm_ref[0, 2] * ow_ref[...]


def _nsa_sample(page_table, qx, gm, kc, vct, kpool, vpool, knew, vnew, kwin, vwin, kwnew, vwnew,
                mt, gsum, ccs, css, cns, cws):
    nb = qx.shape[0]
    n_pages = page_table.shape[1]
    n_chunks = n_pages // S_CHUNK_PAGES
    past = n_pages * PAGE
    n_blocks = past // SEL_BLOCK
    n_q = gm.shape[2]

    def bspec(shape):
        nd = len(shape)
        return pl.BlockSpec((1,) + tuple(shape[1:]), lambda b, c, pt: (b,) + (0,) * (nd - 1))

    def cspec(shape):
        nd = len(shape)
        return pl.BlockSpec(tuple(shape), lambda b, c, pt: (0,) * nd, pipeline_mode=pl.Buffered(1))

    grid_spec = pltpu.PrefetchScalarGridSpec(
        num_scalar_prefetch=1,
        grid=(nb, n_chunks),
        in_specs=[bspec(qx.shape), bspec(gm.shape), bspec(kc.shape), bspec(vct.shape),
                  pl.BlockSpec(memory_space=pl.ANY), pl.BlockSpec(memory_space=pl.ANY),
                  bspec(knew.shape), bspec(vnew.shape), bspec(kwin.shape), bspec(vwin.shape),
                  bspec(kwnew.shape), bspec(vwnew.shape),
                  cspec(mt.shape), cspec(gsum.shape), cspec(ccs.shape), cspec(css.shape),
                  cspec(cns.shape), cspec(cws.shape)],
        out_specs=pl.BlockSpec((1, n_q, KV_DIM), lambda b, c, pt: (b, 0, 0)),
        scratch_shapes=[pltpu.VMEM((2, KV_DIM, S_CHUNK), F32),
                        pltpu.VMEM((2, KV_DIM, S_CHUNK), F32),
                        pltpu.SemaphoreType.DMA((2, 2)),
                        pltpu.VMEM((n_q, past), F32),
                        pltpu.VMEM((n_q, 1), F32), pltpu.VMEM((n_q, 1), F32),
                        pltpu.VMEM((n_q, KV_DIM), F32),
                        pltpu.VMEM((n_q, KV_DIM), F32), pltpu.VMEM((n_q, KV_DIM), F32)],
    )
    return pl.pallas_call(
        functools.partial(_nsa_sample_kernel, n_chunks=n_chunks, n_blocks=n_blocks),
        grid_spec=grid_spec,
        out_shape=jax.ShapeDtypeStruct((nb, n_q, KV_DIM), F32),
        compiler_params=_cparams(("arbitrary", "arbitrary")),
        name="nsa_sample",
    )(page_table, qx, gm, kc, vct, kpool, vpool, knew, vnew, kwin, vwin, kwnew, vwnew,
      mt, gsum, ccs, css, cns, cws)


def _ffn_kernel(x_ref, oa_ref, ob_ref, sa_ref, sb_ref, wpa_ref, wpb_ref, wo_ref, nf_ref,
                wg_ref, wu_ref, wd_ref, nl_ref, y_ref):
    pa = jnp.dot(oa_ref[...], wpa_ref[...], preferred_element_type=F32)
    pb = jnp.dot(ob_ref[...], wpb_ref[...], preferred_element_type=F32)
    merged = sa_ref[...] * pa + sb_ref[...] * pb
    x = x_ref[...] + jnp.dot(merged.astype(BF16), wo_ref[...], preferred_element_type=F32)
    hn = (x * lax.rsqrt(jnp.mean(x * x, axis=-1, keepdims=True) + EPS) * nf_ref[...]).astype(BF16)
    gate = jnp.dot(hn, wg_ref[...], preferred_element_type=F32)
    up = jnp.dot(hn, wu_ref[...], preferred_element_type=F32)
    ff = (jax.nn.silu(gate) * up).astype(BF16)
    x = x + jnp.dot(ff, wd_ref[...], preferred_element_type=F32)
    y_ref[...] = x * lax.rsqrt(jnp.mean(x * x, axis=-1, keepdims=True) + EPS) * nl_ref[...]


def _ffn(x2d, oa, ob, sa, sb, wpa, wpb, wo, nf, wg, wu, wd, nl, tm):
    rows = x2d.shape[0]

    def rspec(n):
        return pl.BlockSpec((tm, n), lambda i: (i, 0))

    return pl.pallas_call(
        _ffn_kernel,
        grid=(rows // tm,),
        in_specs=[rspec(D_MODEL), rspec(Q_DIM), rspec(HG_W), rspec(D_MODEL), rspec(D_MODEL),
                  _const_spec(wpa.shape), _const_spec(wpb.shape), _const_spec(wo.shape),
                  _const_spec(nf.shape), _const_spec(wg.shape), _const_spec(wu.shape),
                  _const_spec(wd.shape), _const_spec(nl.shape)],
        out_specs=rspec(D_MODEL),
        out_shape=jax.ShapeDtypeStruct((rows, D_MODEL), F32),
        compiler_params=_cparams(("arbitrary",)),
        name="ffn",
    )(x2d, oa, ob, sa, sb, wpa, wpb, wo, nf, wg, wu, wd, nl)


def _pack_w_in(w_in):
    sizes = (Q_DIM,) + (KV_DIM,) * 6 + (3 * N_HEADS,) + (HG_W,) * 4 + (D_MODEL,) * 2
    offs = np.concatenate([[0], np.cumsum(sizes)])
    q = w_in[:, offs[0]:offs[1]].reshape(D_MODEL, N_HEADS, HEAD_DIM)[:, _HEAD_PERM, :].reshape(D_MODEL, Q_DIM)
    g = jnp.pad(w_in[:, offs[7]:offs[8]], ((0, 0), (0, LANES - 3 * N_HEADS)))
    return jnp.concatenate([q, w_in[:, offs[1]:offs[7]], g, w_in[:, offs[8]:]], axis=1).astype(BF16)


def _strip(bvc, rel, lo=0, hi=None, masked=NEG):
    val = bvc[:, np.clip(rel, 0, 255)]
    ok = rel >= lo
    if hi is not None:
        ok = ok & (rel < hi)
    return jnp.where(jnp.asarray(ok)[None], val, masked)


def _toeplitz(bvc, a, n_rows, n_cols, lo=0, hi=None):
    n = n_rows + n_cols - 1
    u = _strip(bvc, a - (n_rows - 1) + np.arange(n), lo, hi)
    u = jnp.pad(u, ((0, 0), (0, 1)))
    circ = jnp.tile(u, (1, n_rows))[:, :n_rows * n].reshape(N_HEADS, n_rows, n)
    return circ[:, :, n_rows - 1:n_rows - 1 + n_cols]


def _bias_strips_prompt(bvc):
    gq = GROUP * Q_BLOCK

    def lanes(x):
        return x.reshape(N_KV, GROUP, x.shape[1], Q_BLOCK).transpose(0, 2, 1, 3).reshape(N_KV, x.shape[1], gq)

    cs = jnp.stack([lanes(_toeplitz(bvc, Q_BLOCK * d, Q_BLOCK, Q_BLOCK)) for d in (-1, 0, 1, 2)], axis=1)
    bw = lanes(jnp.concatenate(
        [_toeplitz(bvc, Q_BLOCK * d, Q_BLOCK, Q_BLOCK, 0, WINDOW) for d in range(WINDOW // Q_BLOCK, -1, -1)], axis=1))
    rr = np.arange(CMP_BAND)[:, None]
    rel_c = np.arange(Q_BLOCK)[None, :] - CMP_STRIDE * (rr - CMP_PAD) - (CMP_BLOCK - 1)
    cc = _strip(bvc, rel_c, masked=0.0).reshape(N_KV, GROUP, CMP_BAND, Q_BLOCK)
    cc = cc.transpose(0, 2, 1, 3).reshape(N_KV, CMP_BAND, gq)
    return cc, cs, bw


def _bias_strips_sample(bvc, past, t):
    def rows(a):
        a = a.reshape(N_HEADS * t, a.shape[-1])
        return jnp.pad(a, ((0, S_ROWS - N_HEADS * t), (0, 0)))
    tt = np.arange(t)[:, None]
    nrow = past // CMP_STRIDE
    n = (nrow - CMP_BAND + np.arange(CMP_BAND))[None, :]
    ccs = rows(_strip(bvc, past + tt - CMP_STRIDE * n - (CMP_BLOCK - 1), masked=0.0)).T
    i = np.arange(LANES)[None, :]
    css = rows(_strip(bvc, LANES + tt - i))
    cns = rows(_strip(bvc, np.where(i < t, tt - i, -1)))
    iw = np.arange(WINDOW)[None, :]
    cws = rows(_strip(bvc, WINDOW + tt - iw, 0, WINDOW))
    return ccs, css, cns, cws


def kernel(x_prompt, x_sample, cache_k_cmp, cache_v_cmp, cache_k_slc, cache_v_slc, state_k_win, state_v_win,
           state_hgrn, page_table, norm_mix, w_in, cmp_pe_k, cmp_w1_k, cmp_w2_k, cmp_pe_v, cmp_w1_v, cmp_w2_v,
           rel_bias, hg_lb_logits, hg_norm, w_proj_a, w_proj_b, w_out, norm_ffn, w_gate, w_up, w_down, norm_final):
    nbp, t_p, _ = x_prompt.shape
    nbs, t_s, _ = x_sample.shape
    assert nbp == 1 and norm_mix.shape[0] == 1
    n_pages = page_table.shape[1]
    past = n_pages * PAGE
    assert state_k_win.shape[2] == WINDOW and past % S_CHUNK == 0 and t_s <= SUBLANES

    lb = jnp.cumsum(jax.nn.softmax(hg_lb_logits.astype(F32), axis=0), axis=0)[0]
    lb3 = jnp.pad(jnp.stack([jnp.log(lb), jnp.log1p(-lb), 1.0 - lb]), ((0, SUBLANES - 3), (0, 0)))
    w_pack = _pack_w_in(w_in[0])
    g_mix = norm_mix[0][None, :]
    wpa = w_proj_a[0].reshape(N_HEADS, HEAD_DIM, D_MODEL)[_HEAD_PERM].reshape(Q_DIM, D_MODEL).astype(BF16)
    wpb = w_proj_b[0].astype(BF16)
    wo = w_out[0].astype(BF16)
    wg, wu, wd = w_gate[0].astype(BF16), w_up[0].astype(BF16), w_down[0].astype(BF16)
    nf, nl = norm_ffn[0][None, :], norm_final[None, :]
    gn = hg_norm[0][None, :]
    pe_k, w1_k, w2_k = _compress_weights(cmp_pe_k[0], cmp_w1_k[0], cmp_w2_k[0])
    pe_v, w1_v, w2_v = _compress_weights(cmp_pe_v[0], cmp_w1_v[0], cmp_w2_v[0])
    bvc = (rel_bias[_BUCKET] - rel_bias[N_BUCKETS - 1][None, :]).T * LOG2E
    cc, cs, bw = _bias_strips_prompt(bvc)
    ccs, css, cns, cws = _bias_strips_sample(bvc, past, t_s)

    xp2 = x_prompt.reshape(t_p, D_MODEL)
    xs2 = x_sample.reshape(nbs * t_s, D_MODEL)
    seg = lambda off, n: w_pack[:, off:off + n]
    w_t = jnp.concatenate([seg(_OFF_Q, Q_DIM), seg(_OFF_G, LANES), seg(_OFF_KV + 3 * KV_DIM, KV_DIM),
                           seg(_OFF_KV + 5 * KV_DIM, KV_DIM)], axis=1).T
    pp = _proj(xp2, g_mix, w_pack, w_t, lb3, 512, True)
    ps = _proj(xs2, g_mix, w_pack, w_t, lb3, nbs * t_s, False)
    (kc_p, vc_p, ks_p, vs_p, kw_p, vw_p, ksb_p, kwb_p,
     qh_p, lf_p, kh_p, vh_p, gs_p, sa_p, sb_p, qat_p, gat_p, vst_p, vwt_p) = pp
    (qa_s, kc_s, vc_s, ks_s, vs_s, kw_s, vw_s, ga_s,
     qh_s, lf_s, kh_s, vh_s, gs_s, sa_s, sb_s) = ps

    ob_p, s_p = _hgrn_prompt(qh_p, kh_p, vh_p, lf_p, gs_p, gn)
    r3 = lambda a: a.reshape(nbs, t_s, a.shape[-1])
    ob_s, s_s = _hgrn_sample(r3(qh_s), r3(kh_s), r3(vh_s), r3(lf_s), r3(gs_s), gn, state_hgrn[0])

    ident = jnp.arange(t_p // PAGE, dtype=jnp.int32)[None, :]
    pool_rows = lambda a: a.reshape(-1, PAGE, KV_DIM)
    kcb_p, _ = _compress(pool_rows(kc_p), ident, pe_k, w1_k, w2_k, False)
    _, vct_p = _compress(pool_rows(vc_p), ident, pe_v, w1_v, w2_v, False)
    back = -(CMP_PAD + kcb_p.shape[1]) % LANES
    kcb_p = jnp.pad(kcb_p[0], ((CMP_PAD, back), (0, 0)))
    vct_p = jnp.pad(vct_p[0], ((0, 0), (CMP_PAD, back)))
    nrow_p = kcb_p.shape[0]
    mt_p = jnp.asarray(_score_matrix(nrow_p, CMP_PAD, t_p // SEL_BLOCK), BF16)
    kwb_pad = jnp.pad(kwb_p, ((WINDOW, 0), (0, 0)))
    vst3 = vst_p.reshape(KV_DIM, t_p // SEL_TILE, SEL_TILE).transpose(1, 0, 2)
    ones_rows = jnp.ones((t_p // SEL_TILE, 2 * SUBLANES, SEL_TILE), BF16)
    vst3 = jnp.stack([jnp.concatenate([vst3[:, kv * HEAD_DIM:(kv + 1) * HEAD_DIM, :], ones_rows], axis=1)
                      for kv in range(N_KV)])
    vwt3 = jnp.pad(vwt_p, ((0, 0), (WINDOW, 0))).reshape(KV_DIM, (t_p + WINDOW) // Q_BLOCK, Q_BLOCK)
    vwt3 = vwt3.transpose(1, 0, 2)
    eb = (np.arange(LANES)[None, :] == np.arange(SEL_TILE)[:, None] // SEL_BLOCK).astype(np.float32)
    oa_p = _nsa_prompt(qat_p, gat_p, kcb_p, vct_p, ksb_p, vst3, kwb_pad, vwt3, mt_p, cc, cs, bw,
                       jnp.asarray(eb, BF16))

    pool_t = lambda a: a.transpose(0, 2, 3, 1).reshape(-1, KV_DIM, PAGE)
    kcb_s, _ = _compress(pool_t(cache_k_cmp[0]), page_table, pe_k, w1_k, w2_k, True)
    _, vct_s = _compress(pool_t(cache_v_cmp[0]), page_table, pe_v, w1_v, w2_v, True)
    nq = N_HEADS * t_s
    qs4 = qa_s.reshape(nbs, t_s, GROUP, N_KV, HEAD_DIM).astype(F32)
    qx = jnp.einsum('btjkd,kq->bkjtqd', qs4, jnp.eye(N_KV, dtype=F32)).reshape(nbs, nq, KV_DIM)
    qx = jnp.pad(qx, ((0, 0), (0, S_ROWS - nq), (0, 0))).astype(BF16)
    g4 = ga_s[:, :3 * N_HEADS].reshape(nbs, t_s, N_KV, GROUP, 3)
    gm = jnp.transpose(g4, (0, 4, 2, 3, 1)).reshape(nbs, 3, nq, 1)
    gm = jnp.broadcast_to(gm, (nbs, 3, nq, KV_DIM))
    new_tile = lambda a: jnp.pad(a.reshape(nbs, t_s, KV_DIM), ((0, 0), (0, LANES - t_s), (0, 0)))
    mt_s = jnp.asarray(_score_matrix(past // CMP_STRIDE, 0, past // SEL_BLOCK), BF16)
    gsum = np.zeros((S_ROWS, S_ROWS), np.float32)
    for kv in range(N_KV):
        for j in range(GROUP):
            for t in range(t_s):
                gsum[(kv * GROUP + j) * t_s + t, kv * t_s + t] = 1.0
    o_kv = _nsa_sample(page_table, qx, gm, kcb_s, vct_s,
                       pool_t(cache_k_slc[0]), pool_t(cache_v_slc[0]),
                       new_tile(ks_s), new_tile(vs_s), state_k_win[0].reshape(nbs, WINDOW, KV_DIM),
                       state_v_win[0].reshape(nbs, WINDOW, KV_DIM), new_tile(kw_s), new_tile(vw_s),
                       mt_s, jnp.asarray(gsum), ccs, css, cns, cws)
    o5 = o_kv.reshape(nbs, N_KV, GROUP, t_s, N_KV, HEAD_DIM)
    oa_s = jnp.einsum('bkjtqd,kq->btjkd', o5, jnp.eye(N_KV, dtype=F32)).reshape(nbs * t_s, Q_DIM).astype(BF16)

    y_p = _ffn(xp2, oa_p, ob_p, sa_p, sb_p, wpa, wpb, wo, nf, wg, wu, wd, nl, 256)
    y_s = _ffn(xs2, oa_s, ob_s.reshape(nbs * t_s, HG_W), sa_s, sb_s, wpa, wpb, wo, nf, wg, wu, wd, nl, nbs * t_s)

    kv5 = lambda a, nb_, tt: a.reshape(1, nb_, tt, N_KV, HEAD_DIM)
    wl = min(WINDOW, t_p)
    win = lambda st, new: jnp.concatenate(
        [st[0], new.reshape(nbs, t_s, N_KV, HEAD_DIM)], axis=1)[:, -WINDOW:][None]
    return (y_p.reshape(1, t_p, D_MODEL), y_s.reshape(nbs, t_s, D_MODEL),
            kv5(kc_p, 1, t_p), kv5(vc_p, 1, t_p), kv5(ks_p, 1, t_p), kv5(vs_p, 1, t_p),
            kv5(kw_p[-wl:], 1, wl), kv5(vw_p[-wl:], 1, wl), s_p[None, None],
            kv5(kc_s, nbs, t_s), kv5(vc_s, nbs, t_s), kv5(ks_s, nbs, t_s), kv5(vs_s, nbs, t_s),
            win(state_k_win, kw_s), win(state_v_win, vw_s), s_s[None])
```

```python
import functools
import math

import numpy as np
import jax
import jax.numpy as jnp
from jax import lax
from jax.experimental import pallas as pl
from jax.experimental.pallas import tpu as pltpu

F32 = jnp.float32
BF16 = jnp.bfloat16
HIGHEST = lax.Precision.HIGHEST

D_MODEL = 1024
N_HEADS = 8
N_KV = 2
GROUP = N_HEADS // N_KV
HEAD_DIM = 64
KV_DIM = N_KV * HEAD_DIM
Q_DIM = N_HEADS * HEAD_DIM
CMP_BLOCK = 32
CMP_STRIDE = 16
CMP_HIDDEN = 2 * HEAD_DIM
SEL_BLOCK = 64
N_SEL = 16
WINDOW = 512
Q_BLOCK = 128
FORCE_BONUS = 1e4
N_BUCKETS = 32
MAX_DISTANCE = 128
HG_HEADS = 4
HG_DIM = 128
HG_CHUNK = 64
HG_SUB = 16
HG_W = HG_HEADS * HG_DIM
D_FF = ((8 * D_MODEL // 3 + 255) // 256) * 256
EPS = 1e-6
PAGE = 128
NEG = -1e30
LOG2E = math.log2(math.e)
Q_SCALE = HEAD_DIM ** -0.5 * LOG2E

LANES = 128
SUBLANES = 8
MXU_DIM = 256
VMEM_LIMIT = 56 * 1024 * 1024

PROJ_ROWS = 512
FFN_ROWS = 256
HGRN_ROWS = 512
CMP_ROWS = MXU_DIM

_OFF_Q = 0
_OFF_KV = _OFF_Q + Q_DIM
_OFF_G = _OFF_KV + 6 * KV_DIM
_OFF_HG = _OFF_G + LANES
_OFF_GATE = _OFF_HG + 4 * HG_W
_PROJ_N = _OFF_GATE + 2 * D_MODEL

_HEAD_PERM = np.array([h for j in range(GROUP) for h in (j, GROUP + j)])

SEL_TILE = 512
BLK_PER_TILE = SEL_TILE // SEL_BLOCK
CMP_PAD = 16
CMP_BAND = 24


def _cparams(sem, vmem=VMEM_LIMIT):
    return pltpu.CompilerParams(dimension_semantics=sem, vmem_limit_bytes=vmem)


def _const_spec(shape):
    nd = len(shape)
    return pl.BlockSpec(shape, lambda *_: (0,) * nd, pipeline_mode=pl.Buffered(1))


def _bucket_table():
    n = np.arange(256)
    max_exact = N_BUCKETS // 2
    nf = np.maximum(n, 1).astype(np.float64)
    large = max_exact + (np.log(nf / max_exact) / math.log(MAX_DISTANCE / max_exact)
                         * (N_BUCKETS - max_exact)).astype(np.int64)
    large = np.minimum(large, N_BUCKETS - 1)
    return np.where(n < max_exact, n, large)


_BUCKET = _bucket_table()


def _proj_kernel(x_ref, g_ref, w_ref, wt_ref, lb_ref, *out_refs, prompt):
    if prompt:
        (kc_ref, vc_ref, ks_ref, vs_ref, kw_ref, vw_ref, ksb_ref, kwb_ref,
         qh_ref, lf_ref, kh_ref, vh_ref, gs_ref, sa_ref, sb_ref,
         qat_ref, gat_ref, vst_ref, vwt_ref) = out_refs
    else:
        (qa_ref, kc_ref, vc_ref, ks_ref, vs_ref, kw_ref, vw_ref, ga_ref,
         qh_ref, lf_ref, kh_ref, vh_ref, gs_ref, sa_ref, sb_ref) = out_refs
    x = x_ref[...]
    xn = x * lax.rsqrt(jnp.mean(x * x, axis=-1, keepdims=True) + EPS) * g_ref[...]
    xb = xn.astype(BF16)

    def seg(a, n):
        return jnp.dot(xb, w_ref[:, a:a + n], preferred_element_type=F32)

    f32_refs = (kc_ref, vc_ref, ks_ref, vs_ref, kw_ref, vw_ref)
    for i in range(0, 6, 2):
        u = seg(_OFF_KV + i * KV_DIM, 2 * KV_DIM)
        f32_refs[i][...] = u[:, :KV_DIM]
        f32_refs[i + 1][...] = u[:, KV_DIM:]
        if prompt and i > 0:
            (ksb_ref if i == 2 else kwb_ref)[...] = u[:, :KV_DIM].astype(BF16)

    if prompt:
        def seg_t(a, n):
            return lax.dot_general(wt_ref[a:a + n, :], xb, (((1,), (1,)), ((), ())),
                                   preferred_element_type=F32)

        qat_ref[...] = (seg_t(0, Q_DIM) * Q_SCALE).astype(BF16)
        gat_ref[...] = jax.nn.sigmoid(seg_t(Q_DIM, LANES))
        vt = seg_t(Q_DIM + LANES, 2 * KV_DIM)
        vst_ref[...] = vt[:KV_DIM, :].astype(BF16)
        vwt_ref[...] = vt[KV_DIM:, :].astype(BF16)
    else:
        qa_ref[...] = (seg(_OFF_Q, Q_DIM) * Q_SCALE).astype(BF16)
        ga_ref[...] = jax.nn.sigmoid(seg(_OFF_G, LANES))

    log_lb = lb_ref[0:1, :]
    log_1m = lb_ref[1:2, :]
    one_m = lb_ref[2:3, :]
    qh_ref[...] = jax.nn.silu(seg(_OFF_HG, HG_W))
    z = seg(_OFF_HG + HG_W, HG_W)
    b = log_1m + (jnp.minimum(z, 0.0) - jnp.log1p(jnp.exp(-jnp.abs(z))))
    hi = jnp.maximum(log_lb, b)
    lf_ref[...] = hi + jnp.log1p(jnp.exp(-jnp.abs(log_lb - b)))
    kh_ref[...] = one_m * jax.nn.sigmoid(-z)
    vh_ref[...] = seg(_OFF_HG + 2 * HG_W, HG_W)
    gs_ref[...] = jax.nn.silu(seg(_OFF_HG + 3 * HG_W, HG_W))
    sa_ref[...] = jax.nn.sigmoid(seg(_OFF_GATE, D_MODEL))
    sb_ref[...] = jax.nn.sigmoid(seg(_OFF_GATE + D_MODEL, D_MODEL))


_PROJ_T = Q_DIM + LANES + 2 * KV_DIM


def _proj(x2d, g, w, wt, lb3, tm, prompt):
    rows = x2d.shape[0]
    tail = [(HG_W, F32)] * 5 + [(D_MODEL, F32)] * 2
    if prompt:
        widths = [(KV_DIM, F32)] * 6 + [(KV_DIM, BF16)] * 2 + tail
        heights = [(Q_DIM, BF16), (LANES, F32), (KV_DIM, BF16), (KV_DIM, BF16)]
    else:
        widths = [(Q_DIM, BF16)] + [(KV_DIM, F32)] * 6 + [(LANES, F32)] + tail
        heights = []
    return pl.pallas_call(
        functools.partial(_proj_kernel, prompt=prompt),
        grid=(rows // tm,),
        in_specs=[pl.BlockSpec((tm, D_MODEL), lambda i: (i, 0)),
                  _const_spec((1, D_MODEL)),
                  _const_spec((D_MODEL, _PROJ_N)),
                  _const_spec((_PROJ_T, D_MODEL)),
                  _const_spec((SUBLANES, HG_W))],
        out_specs=([pl.BlockSpec((tm, n), lambda i: (i, 0)) for n, _ in widths]
                   + [pl.BlockSpec((n, tm), lambda i: (0, i)) for n, _ in heights]),
        out_shape=([jax.ShapeDtypeStruct((rows, n), dt) for n, dt in widths]
                   + [jax.ShapeDtypeStruct((n, rows), dt) for n, dt in heights]),
        compiler_params=_cparams(("arbitrary",)),
        name="proj",
    )(x2d, g, w, wt, lb3)


_CH_W = CMP_STRIDE * KV_DIM
_CH_PER_PAGE = PAGE // CMP_STRIDE
_CH_PITCH = CMP_STRIDE + SUBLANES


def _compress_kernel(pt_ref, pool_ref, pe_ref, w1_ref, w2_ref, out_ref, outt_ref, buf, rbuf, xa, hbuf, sem,
                     *, n_pages, pages_transposed):
    b = pl.program_id(0)
    nb = pl.num_programs(0)
    slot = b % 2
    n_ch = n_pages * _CH_PER_PAGE

    def page_copy(bb, p, s):
        return pltpu.make_async_copy(pool_ref.at[pt_ref[bb, p]], buf.at[s, p], sem.at[s])

    def start_all(bb, s):
        def body(p, c):
            page_copy(bb, p, s).start()
            return c
        lax.fori_loop(0, n_pages, body, 0)

    @pl.when(b == 0)
    def _():
        start_all(b, slot)

    @pl.when(b + 1 < nb)
    def _():
        start_all(b + 1, 1 - slot)

    pltpu.make_async_copy(buf.at[slot], buf.at[slot], sem.at[slot]).wait()

    rows = math.gcd(n_ch, CMP_ROWS)
    pages_per_group = rows // _CH_PER_PAGE

    def to_rows(g):
        for p in range(g * pages_per_group, (g + 1) * pages_per_group):
            page = buf[slot, p]
            page = page.T if pages_transposed else page
            for i in range(_CH_PER_PAGE):
                dst = (p * _CH_PER_PAGE + i) * _CH_PITCH
                rbuf[dst:dst + CMP_STRIDE, :] = page[i * CMP_STRIDE:(i + 1) * CMP_STRIDE, :]

    to_rows(0)
    for r in range(n_ch // rows):
        if r + 1 < n_ch // rows:
            to_rows(r + 1)
        for s in range(CMP_STRIDE):
            x = rbuf[pl.ds(r * rows * _CH_PITCH + s, rows, stride=_CH_PITCH), :]
            xa[:, s * KV_DIM:(s + 1) * KV_DIM] = x.astype(BF16)
        hbuf[r * rows:(r + 1) * rows, :] = jnp.dot(xa[...], w1_ref[...], preferred_element_type=F32)
    pw = _split_dot_rhs(pe_ref[...], w1_ref[...])
    nh = N_KV * CMP_HIDDEN
    bias = pw[0:1, 0:nh] + pw[1:2, nh:2 * nh]
    h = hbuf[:, 0:nh] + pltpu.roll(hbuf[:, nh:2 * nh], n_ch - 1, 0) + bias
    blocks = jnp.dot(jax.nn.gelu(h).astype(BF16), w2_ref[...], preferred_element_type=F32)
    row = lax.broadcasted_iota(jnp.int32, blocks.shape, 0)
    blocks = jnp.where(row < n_ch - 1, blocks, 0.0)
    out_ref[0] = blocks.astype(BF16)
    outt_ref[0] = blocks.T.astype(BF16)


def _compress(pool, page_table, pe, w1, w2, pages_transposed):
    nbatch, n_pages = page_table.shape
    n_ch = n_pages * _CH_PER_PAGE
    rows = math.gcd(n_ch, CMP_ROWS)
    grid_spec = pltpu.PrefetchScalarGridSpec(
        num_scalar_prefetch=1,
        grid=(nbatch,),
        in_specs=[pl.BlockSpec(memory_space=pl.ANY),
                  _const_spec((SUBLANES, _CH_W)),
                  _const_spec((_CH_W, 4 * CMP_HIDDEN)),
                  _const_spec((2 * CMP_HIDDEN, KV_DIM))],
        out_specs=[pl.BlockSpec((1, n_ch, KV_DIM), lambda b, pt: (b, 0, 0)),
                   pl.BlockSpec((1, KV_DIM, n_ch), lambda b, pt: (b, 0, 0))],
        scratch_shapes=[pltpu.VMEM((2, n_pages, PAGE, KV_DIM), F32),
                        pltpu.VMEM((n_ch * _CH_PITCH, KV_DIM), F32),
                        pltpu.VMEM((rows, _CH_W), BF16),
                        pltpu.VMEM((n_ch, 4 * CMP_HIDDEN), F32),
                        pltpu.SemaphoreType.DMA((2,))],
    )
    return pl.pallas_call(
        functools.partial(_compress_kernel, n_pages=n_pages, pages_transposed=pages_transposed),
        grid_spec=grid_spec,
        out_shape=[jax.ShapeDtypeStruct((nbatch, n_ch, KV_DIM), BF16),
                   jax.ShapeDtypeStruct((nbatch, KV_DIM, n_ch), BF16)],
        compiler_params=_cparams(("arbitrary",)),
        name="compress",
    )(page_table, pool, pe, w1, w2)


def _compress_weights(pe, w1, w2):
    c = CMP_BLOCK // CMP_STRIDE
    pe_r = pe.reshape(c, CMP_STRIDE, 1, HEAD_DIM)
    pe_x = jnp.broadcast_to(pe_r, (c, CMP_STRIDE, N_KV, HEAD_DIM)).reshape(c, _CH_W)
    pe_x = jnp.pad(pe_x, ((0, SUBLANES - c), (0, 0)))
    w1_r = w1.reshape(c, CMP_STRIDE, HEAD_DIM, CMP_HIDDEN)
    eye = jnp.eye(N_KV, dtype=w1.dtype)
    w1_x = jnp.einsum('jsde,kq->skdjqe', w1_r, eye).reshape(_CH_W, c * N_KV * CMP_HIDDEN)
    w2_x = jnp.einsum('ed,kq->keqd', w2, eye).reshape(N_KV * CMP_HIDDEN, KV_DIM)
    return pe_x, w1_x.astype(BF16), w2_x.astype(BF16)


def _hgrn_chunk(q, k, v, lf, st_ref, chunk, sub):
    if chunk > SUBLANES:
        r = lax.broadcasted_iota(jnp.int32, (chunk, chunk), 0)
        c = lax.broadcasted_iota(jnp.int32, (chunk, chunk), 1)
        tri = (r >= c).astype(F32)
        b = jnp.dot(tri, lf, preferred_element_type=F32, precision=HIGHEST)
    else:
        rows = [lf[0:1, :]]
        for t in range(1, chunk):
            rows.append(rows[-1] + lf[t:t + 1, :])
        b = jnp.concatenate(rows, axis=0)
    bl = b[chunk - 1:chunk, :]
    qe = q * jnp.exp(b)
    kd = k * jnp.exp(bl - b)
    ebl = jnp.exp(bl)
    n_sub = chunk // sub
    trow = lax.broadcasted_iota(jnp.int32, (sub, HG_W), 0)
    crow = lax.broadcasted_iota(jnp.int32, (chunk, HG_W), 0)

    diag = []
    for i in range(n_sub):
        qi = q[i * sub:(i + 1) * sub, :]
        bi = b[i * sub:(i + 1) * sub, :]
        acc = [jnp.zeros((sub, HG_DIM), F32) for _ in range(HG_HEADS)]
        for s in range(sub):
            row = i * sub + s
            dec = jnp.exp(jnp.where(trow >= s, bi - b[row:row + 1, :], -jnp.inf))
            prod = qi * k[row:row + 1, :] * dec
            for h in range(HG_HEADS):
                a = jnp.sum(prod[:, h * HG_DIM:(h + 1) * HG_DIM], axis=1, keepdims=True)
                acc[h] = acc[h] + a * v[row:row + 1, h * HG_DIM:(h + 1) * HG_DIM]
        diag.append(acc)

    off = []
    for i in range(n_sub):
        if i == 0:
            off.append(None)
            continue
        b0 = b[i * sub - 1:i * sub, :]
        qs = (q[i * sub:(i + 1) * sub, :] * jnp.exp(b[i * sub:(i + 1) * sub, :] - b0)).astype(BF16)
        ks = (k * jnp.exp(jnp.where(crow < i * sub, b0 - b, -jnp.inf))).astype(BF16)
        off.append((qs, ks))

    vb = v.astype(BF16)
    outs = []
    for h in range(HG_HEADS):
        sl = slice(h * HG_DIM, (h + 1) * HG_DIM)
        st = st_ref[h]
        o_h = lax.dot_general(qe[:, sl].astype(BF16), st.astype(BF16), (((1,), (1,)), ((), ())),
                              preferred_element_type=F32)
        parts = []
        for i in range(n_sub):
            d = diag[i][h]
            if off[i] is not None:
                qs, ks = off[i]
                a = lax.dot_general(qs[:, sl], ks[:, sl], (((1,), (1,)), ((), ())),
                                    preferred_element_type=F32)
                d = d + jnp.dot(a.astype(BF16), vb[:, sl], preferred_element_type=F32)
            parts.append(d)
        intra = parts[0] if n_sub == 1 else jnp.concatenate(parts, axis=0)
        outs.append(o_h + intra)
        st_ref[h] = st * ebl[:, sl] + lax.dot_general(
            vb[:, sl], kd[:, sl].astype(BF16), (((0,), (0,)), ((), ())), preferred_element_type=F32)
    return jnp.concatenate(outs, axis=1)


def _hgrn_finish(o, gs, gn):
    outs = []
    for h in range(HG_HEADS):
        oh = o[:, h * HG_DIM:(h + 1) * HG_DIM]
        y = oh * lax.rsqrt(jnp.mean(oh * oh, axis=-1, keepdims=True) + EPS) * gn
        outs.append(y)
    return (jnp.concatenate(outs, axis=1) * gs).astype(BF16)


def _hgrn_prompt_kernel(q_ref, k_ref, v_ref, lf_ref, gs_ref, gn_ref, o_ref, s_ref, st_ref, *, n_chunks):
    i = pl.program_id(0)

    @pl.when(i == 0)
    def _():
        st_ref[...] = jnp.zeros_like(st_ref)

    def body(c, carry):
        r = pl.ds(pl.multiple_of(c * HG_CHUNK, HG_CHUNK), HG_CHUNK)
        o = _hgrn_chunk(q_ref[r, :], k_ref[r, :], v_ref[r, :], lf_ref[r, :], st_ref, HG_CHUNK, HG_SUB)
        o_ref[r, :] = _hgrn_finish(o, gs_ref[r, :], gn_ref[...])
        return carry
    lax.fori_loop(0, n_chunks, body, 0, unroll=8)

    @pl.when(i == pl.num_programs(0) - 1)
    def _():
        for h in range(HG_HEADS):
            s_ref[h] = st_ref[h].T


def _hgrn_prompt(qh, kh, vh, lf, gs, gn, rows_per_step=HGRN_ROWS):
    t = qh.shape[0]
    spec = pl.BlockSpec((rows_per_step, HG_W), lambda i: (i, 0))
    return pl.pallas_call(
        functools.partial(_hgrn_prompt_kernel, n_chunks=rows_per_step // HG_CHUNK),
        grid=(t // rows_per_step,),
        in_specs=[spec] * 5 + [_const_spec((1, HG_DIM))],
        out_specs=[spec, pl.BlockSpec((HG_HEADS, HG_DIM, HG_DIM), lambda i: (0, 0, 0))],
        out_shape=[jax.ShapeDtypeStruct((t, HG_W), BF16),
                   jax.ShapeDtypeStruct((HG_HEADS, HG_DIM, HG_DIM), F32)],
        scratch_shapes=[pltpu.VMEM((HG_HEADS, HG_DIM, HG_DIM), F32)],
        compiler_params=_cparams(("arbitrary",)),
        name="hgrn_prompt",
    )(qh, kh, vh, lf, gs, gn)


def _hgrn_sample_kernel(q_ref, k_ref, v_ref, lf_ref, gs_ref, gn_ref, s0_ref, o_ref, s_ref, st_ref, *, t):
    for h in range(HG_HEADS):
        st_ref[h] = s0_ref[0, h].T
    o = _hgrn_chunk(q_ref[0], k_ref[0], v_ref[0], lf_ref[0], st_ref, t, t)
    o_ref[0] = _hgrn_finish(o, gs_ref[0], gn_ref[...])
    for h in range(HG_HEADS):
        s_ref[0, h] = st_ref[h].T


def _hgrn_sample(qh, kh, vh, lf, gs, gn, s0):
    nb, t, _ = qh.shape
    spec = pl.BlockSpec((1, t, HG_W), lambda b: (b, 0, 0))
    sspec = pl.BlockSpec((1, HG_HEADS, HG_DIM, HG_DIM), lambda b: (b, 0, 0, 0))
    return pl.pallas_call(
        functools.partial(_hgrn_sample_kernel, t=t),
        grid=(nb,),
        in_specs=[spec] * 5 + [_const_spec((1, HG_DIM)), sspec],
        out_specs=[spec, sspec],
        out_shape=[jax.ShapeDtypeStruct((nb, t, HG_W), BF16),
                   jax.ShapeDtypeStruct((nb, HG_HEADS, HG_DIM, HG_DIM), F32)],
        scratch_shapes=[pltpu.VMEM((HG_HEADS, HG_DIM, HG_DIM), F32)],
        compiler_params=_cparams(("arbitrary",)),
        name="hgrn_sample",
    )(qh, kh, vh, lf, gs, gn, s0)


def _select_topk(x, blk, n, always=None):
    nblk = x.shape[0]
    if always is None:
        sel = jnp.zeros_like(x)
    else:
        sel = jnp.where(always, 1.0, 0.0)
        x = jnp.where(always, -3e38, x)
    for _ in range(n):
        m = jnp.max(x, axis=0, keepdims=True)
        idx = jnp.min(jnp.where(x == m, blk, float(nblk)), axis=0, keepdims=True)
        pick = blk == idx
        sel = jnp.where(pick, 1.0, sel)
        x = jnp.where(pick, -3e38, x)
    return sel


def _softmax_cols(s, valid):
    m = jnp.max(jnp.where(valid, s, NEG), axis=0, keepdims=True)
    p = jnp.where(valid, jnp.exp2(s - m), 0.0)
    l = jnp.sum(p, axis=0, keepdims=True)
    return p * jnp.where(l > 0.0, 1.0 / l, 0.0)


def _split_dot(a_bf16, x):
    hi = x.astype(BF16)
    lo = (x - hi.astype(F32)).astype(BF16)
    return (jnp.dot(a_bf16, hi, preferred_element_type=F32)
            + jnp.dot(a_bf16, lo, preferred_element_type=F32))


def _split_dot_rhs(x, w_bf16):
    hi = x.astype(BF16)
    lo = (x - hi.astype(F32)).astype(BF16)
    return (jnp.dot(hi, w_bf16, preferred_element_type=F32)
            + jnp.dot(lo, w_bf16, preferred_element_type=F32))


def _nt(a, b):
    return lax.dot_general(a, b, (((1,), (1,)), ((), ())), preferred_element_type=F32)


def _online_update(s, v, m_ref, l_ref, acc_ref, v_transposed=False):
    m_old = m_ref[...]
    m_new = jnp.maximum(m_old, jnp.max(s, axis=1, keepdims=True))
    p = jnp.exp2(s - m_new)
    alpha = jnp.exp2(m_old - m_new)
    l_ref[...] = alpha * l_ref[...] + jnp.sum(p, axis=1, keepdims=True)
    pv = _nt(p.astype(BF16), v) if v_transposed else jnp.dot(p.astype(BF16), v, preferred_element_type=F32)
    acc_ref[...] = alpha * acc_ref[...] + pv
    m_ref[...] = m_new


def _score_matrix(n_rows, row_offset, n_blocks):
    c = CMP_BLOCK // CMP_STRIDE
    ratio = SEL_BLOCK // CMP_STRIDE
    n_ov = ratio + c - 1
    m = np.zeros((n_blocks, n_rows), np.float32)
    for j in range(n_blocks):
        for u in range(n_ov):
            start = CMP_STRIDE * (u - (c - 1))
            w_u = (min(start + CMP_BLOCK, SEL_BLOCK) - max(start, 0)) / CMP_STRIDE
            n = ratio * j + u - (c - 1)
            if 0 <= n and n + row_offset < n_rows:
                m[j, n + row_offset] = w_u
    return m


def _nsa_prompt_kernel(qt_ref, gt_ref, kc_ref, vct_ref, ks_ref, vst_ref, kw_ref, vwt_ref,
                       mt_ref, cc_ref, cs_ref, bw_ref, eb_ref, o_ref,
                       sc_ref, sw_ref, sel_ref, m_ref, acc_ref, oc_ref,
                       sa_ref, sb_ref, pa_ref, pb_ref, ala_ref, alb_ref, *, n_blocks, variants):
    qb = pl.program_id(0)
    nrow = kc_ref.shape[0]
    gq = GROUP * Q_BLOCK
    tiles_per_q = SEL_TILE // Q_BLOCK
    nband = WINDOW + Q_BLOCK
    max_tile = n_blocks // BLK_PER_TILE - 1
    frow = lax.broadcasted_iota(jnp.int32, (KV_DIM, Q_BLOCK), 0)

    def lanes4(x):
        return jnp.concatenate([x] * GROUP, axis=1)

    qx = []
    for kv in range(N_KV):
        keep = (frow >= HEAD_DIM) if kv else (frow < HEAD_DIM)
        qx.append(jnp.concatenate(
            [jnp.where(keep, qt_ref[j * KV_DIM:(j + 1) * KV_DIM, :], jnp.zeros((), BF16))
             for j in range(GROUP)], axis=1))

    def qk_stage(kt, s_ref, penalty=0.0):
        k = ks_ref[pl.ds(pl.multiple_of(kt * SEL_TILE, SEL_TILE), SEL_TILE), :]
        k_aug = jnp.concatenate([k, eb_ref[...]], axis=1)
        for kv in range(N_KV):
            srow = sel_ref[kv, pl.ds(pl.multiple_of(kt * BLK_PER_TILE, BLK_PER_TILE), BLK_PER_TILE), :]
            mrows = jnp.concatenate([lanes4(srow + penalty), jnp.zeros((KV_DIM - BLK_PER_TILE, gq), F32)], axis=0)
            q_aug = jnp.concatenate([qx[kv], mrows.astype(BF16)], axis=0)
            s_ref[kv] = jnp.dot(k_aug, q_aug, preferred_element_type=F32)

    def gate(kv, i):
        return jnp.concatenate(
            [gt_ref[3 * (kv * GROUP + j) + i:3 * (kv * GROUP + j) + i + 1, :] for j in range(GROUP)], axis=1)

    def compressed_and_select(n_r, n_b):
        for kv in range(N_KV):
            sc_ref[kv, 0:n_r, :] = jnp.dot(kc_ref[0:n_r, :], qx[kv], preferred_element_type=F32)
        r = lax.broadcasted_iota(jnp.int32, (n_r, Q_BLOCK), 0)
        qpos_c = qb * Q_BLOCK + lax.broadcasted_iota(jnp.int32, (n_r, Q_BLOCK), 1)
        end_pos = (r - CMP_PAD) * CMP_STRIDE + (CMP_BLOCK - 1)
        vis = (r >= CMP_PAD) & (r < CMP_PAD + n_blocks * (SEL_BLOCK // CMP_STRIDE) - 1) & (end_pos <= qpos_c)
        vis_add = lanes4(jnp.where(vis, 0.0, NEG))
        band = pl.ds(pl.multiple_of(qb * SUBLANES, SUBLANES), CMP_BAND)
        blk_i = lax.broadcasted_iota(jnp.int32, (n_b, Q_BLOCK), 0)
        cur = (qb * Q_BLOCK + lax.broadcasted_iota(jnp.int32, (n_b, Q_BLOCK), 1)) // SEL_BLOCK
        forced = (blk_i == 0) | (blk_i == cur) | (blk_i == cur - 1)
        scores = []
        for kv in range(N_KV):
            sc_ref[kv, band, :] = sc_ref[kv, band, :] + cc_ref[kv]
            s = sc_ref[kv, 0:n_r, :] + vis_add
            m = jnp.max(s, axis=0, keepdims=True)
            p = jnp.exp2(s - m)
            l = jnp.sum(p, axis=0, keepdims=True)
            pn = p * jnp.where(m > 0.5 * NEG, 1.0 / l, 0.0)
            oc_ref[kv] = gate(kv, 0) * jnp.dot(vct_ref[:, 0:n_r], pn.astype(BF16), preferred_element_type=F32)
            imp = pn[:, 0:Q_BLOCK]
            for j in range(1, GROUP):
                imp = imp + pn[:, j * Q_BLOCK:(j + 1) * Q_BLOCK]
            score = _split_dot(mt_ref[0:n_b, 0:n_r], imp)
            scores.append(jnp.where(blk_i <= cur, score, -FORCE_BONUS))
        kband = kw_ref[pl.ds(pl.multiple_of(qb * Q_BLOCK, Q_BLOCK), nband), :]
        for kv in range(N_KV):
            sw_ref[kv] = jnp.dot(kband, qx[kv], preferred_element_type=F32)
        blk2 = jnp.concatenate([blk_i.astype(F32)] * N_KV, axis=1)
        forced2 = jnp.concatenate([forced] * N_KV, axis=1)
        sel = _select_topk(jnp.concatenate(scores, axis=1), blk2, N_SEL - 3, always=forced2)
        for kv in range(N_KV):
            sel_ref[kv, 0:n_b, :] = (sel[:, kv * Q_BLOCK:(kv + 1) * Q_BLOCK] - 1.0) * (-NEG)

    sel_ref[...] = jnp.full_like(sel_ref, NEG)
    lo = 0
    for q_hi, n_r, n_b in variants:
        @pl.when((qb >= lo) & (qb <= q_hi))
        def _(n_r=n_r, n_b=n_b):
            compressed_and_select(n_r, n_b)
        lo = q_hi + 1

    m_ref[...] = jnp.full_like(m_ref, NEG)
    acc_ref[...] = jnp.zeros_like(acc_ref)

    def pv_stage(kt, p_ref, al_ref):
        for kv in range(N_KV):
            acc_ref[kv] = al_ref[kv] * acc_ref[kv] + jnp.dot(vst_ref[kv, kt], p_ref[kv],
                                                             preferred_element_type=F32)

    def sm_stage(kt, s_ref, p_ref, al_ref, near):
        d0 = qb - kt * tiles_per_q
        for kv in range(N_KV):
            s = s_ref[kv]
            if near:
                s = s + jnp.concatenate(
                    [cs_ref[kv, jnp.clip(d0 - i, -1, 2) + 1] for i in range(tiles_per_q)], axis=0)
            m_old = m_ref[kv]
            m_new = jnp.maximum(m_old, jnp.max(s, axis=0, keepdims=True))
            p_ref[kv] = jnp.exp2(s - m_new).astype(BF16)
            al_ref[kv] = jnp.exp2(m_old - m_new)
            m_ref[kv] = m_new

    n_far = jnp.maximum(qb - 1, 0) // tiles_per_q
    pb_ref[...] = jnp.zeros_like(pb_ref)
    alb_ref[...] = jnp.ones_like(alb_ref)
    qk_stage(0, sa_ref)

    def window(kv):
        vband = jnp.concatenate([vwt_ref[qb + i] for i in range(nband // Q_BLOCK)], axis=1)
        krow = lax.broadcasted_iota(jnp.int32, (nband, Q_BLOCK), 0)
        pos_add = lanes4(jnp.where(krow >= WINDOW - qb * Q_BLOCK, 0.0, NEG))
        s = sw_ref[kv] + bw_ref[kv] + pos_add
        m = jnp.max(s, axis=0, keepdims=True)
        p = jnp.exp2(s - m)
        o_w = (jnp.dot(vband, p.astype(BF16), preferred_element_type=F32)
               / jnp.sum(p, axis=0, keepdims=True))
        oc_ref[kv] = oc_ref[kv] + gate(kv, 2) * o_w

    window(0)

    def pair_body(u, c):
        t0 = 2 * u
        t1 = jnp.minimum(t0 + 1, max_tile)
        t2 = jnp.minimum(t0 + 2, max_tile)
        qk_stage(t1, sb_ref, jnp.where(t0 + 1 < n_far, 0.0, NEG))
        pv_stage(jnp.maximum(t0 - 1, 0), pb_ref, alb_ref)
        sm_stage(t0, sa_ref, pa_ref, ala_ref, False)
        qk_stage(t2, sa_ref)
        pv_stage(t0, pa_ref, ala_ref)
        sm_stage(t1, sb_ref, pb_ref, alb_ref, False)
        return c
    n_pairs = (n_far + 1) // 2
    lax.fori_loop(0, n_pairs, pair_body, 0)

    near_a = n_far
    near_b = jnp.minimum(n_far + 1, max_tile)
    qk_stage(near_a, sa_ref)
    pv_stage(jnp.clip(2 * n_pairs - 1, 0, max_tile), pb_ref, alb_ref)
    window(1)
    sm_stage(near_a, sa_ref, pa_ref, ala_ref, True)
    pv_stage(near_a, pa_ref, ala_ref)

    @pl.when(n_far + 1 <= qb // tiles_per_q)
    def _():
        qk_stage(near_b, sb_ref)
        sm_stage(near_b, sb_ref, pb_ref, alb_ref, True)
        pv_stage(near_b, pb_ref, alb_ref)

    o_kv = []
    for kv in range(N_KV):
        acc = acc_ref[kv]
        o_s = acc[0:HEAD_DIM, :] / acc[HEAD_DIM:HEAD_DIM + 1, :]
        o_kv.append(oc_ref[kv, kv * HEAD_DIM:(kv + 1) * HEAD_DIM, :] + gate(kv, 1) * o_s)
    o_t = jnp.concatenate(o_kv, axis=0)
    for j in range(GROUP):
        o_ref[:, j * LANES:(j + 1) * LANES] = o_t[:, j * Q_BLOCK:(j + 1) * Q_BLOCK].T.astype(BF16)


def _nsa_prompt(qat, gat, kc, vct, ksb, vst3, kwb, vwt3, mt, cc, cs, bw, eb):
    t = qat.shape[1]
    n_blocks = t // SEL_BLOCK
    nrow = kc.shape[0]
    gq = GROUP * Q_BLOCK
    n_q = t // Q_BLOCK
    variants = []
    for n_r in list(range(2 * LANES, nrow - LANES, 2 * LANES)) + [nrow]:
        q_hi = n_q - 1 if n_r == nrow else min((n_r - CMP_BAND) // SUBLANES, n_q - 1)
        n_b = min(-(-(2 * q_hi + 2) // (2 * SUBLANES)) * (2 * SUBLANES), n_blocks)
        variants.append((q_hi, n_r, n_b))
    return pl.pallas_call(
        functools.partial(_nsa_prompt_kernel, n_blocks=n_blocks, variants=tuple(variants)),
        grid=(t // Q_BLOCK,),
        in_specs=[pl.BlockSpec((Q_DIM, Q_BLOCK), lambda i: (0, i)),
                  pl.BlockSpec((LANES, Q_BLOCK), lambda i: (0, i)),
                  _const_spec(kc.shape), _const_spec(vct.shape),
                  _const_spec(ksb.shape), _const_spec(vst3.shape),
                  _const_spec(kwb.shape), _const_spec(vwt3.shape),
                  _const_spec(mt.shape), _const_spec(cc.shape),
                  _const_spec(cs.shape), _const_spec(bw.shape), _const_spec(eb.shape)],
        out_specs=pl.BlockSpec((Q_BLOCK, Q_DIM), lambda i: (i, 0)),
        out_shape=jax.ShapeDtypeStruct((t, Q_DIM), BF16),
        scratch_shapes=[pltpu.VMEM((N_KV, nrow, gq), F32),
                        pltpu.VMEM((N_KV, WINDOW + Q_BLOCK, gq), F32),
                        pltpu.VMEM((N_KV, n_blocks, Q_BLOCK), F32),
                        pltpu.VMEM((N_KV, 1, gq), F32),
                        pltpu.VMEM((N_KV, vst3.shape[2], gq), F32),
                        pltpu.VMEM((N_KV, KV_DIM, gq), F32),
                        pltpu.VMEM((N_KV, SEL_TILE, gq), F32), pltpu.VMEM((N_KV, SEL_TILE, gq), F32),
                        pltpu.VMEM((N_KV, SEL_TILE, gq), BF16), pltpu.VMEM((N_KV, SEL_TILE, gq), BF16),
                        pltpu.VMEM((N_KV, 1, gq), F32), pltpu.VMEM((N_KV, 1, gq), F32)],
        compiler_params=_cparams(("arbitrary",)),
        name="nsa_prompt",
    )(qat, gat, kc, vct, ksb, vst3, kwb, vwt3, mt, cc, cs, bw, eb)


S_ROWS = 128
S_CHUNK_PAGES = 128
S_CHUNK = S_CHUNK_PAGES * PAGE


def _nsa_sample_kernel(pt_ref, qx_ref, gm_ref, kc_ref, vct_ref, kpool_ref, vpool_ref,
                       knew_ref, vnew_ref, kwin_ref, vwin_ref, kwnew_ref, vwnew_ref,
                       mt_ref, gsum_ref, ccs_ref, css_ref, cns_ref, cws_ref, o_ref,
                       kbuf, vbuf, sem, mask_ref, m_ref, l_ref, acc_ref, oc_ref, ow_ref,
                       *, n_chunks, n_blocks):
    b = pl.program_id(0)
    c = pl.program_id(1)
    step = b * n_chunks + c
    total = pl.num_programs(0) * n_chunks
    slot = step % 2

    def copies(bb, cc, s, p):
        pg = cc * S_CHUNK_PAGES + p
        dst = pl.ds(pl.multiple_of(p * PAGE, PAGE), PAGE)
        return (pltpu.make_async_copy(kpool_ref.at[pt_ref[bb, pg]], kbuf.at[s, :, dst], sem.at[0, s]),
                pltpu.make_async_copy(vpool_ref.at[pt_ref[bb, pg]], vbuf.at[s, :, dst], sem.at[1, s]))

    def start_all(st, s):
        bb = st // n_chunks
        cc = st % n_chunks

        def body(p, carry):
            ck, cv = copies(bb, cc, s, p)
            ck.start()
            cv.start()
            return carry
        lax.fori_loop(0, S_CHUNK_PAGES, body, 0)

    @pl.when(step == 0)
    def _():
        start_all(step, slot)

    @pl.when(step + 1 < total)
    def _():
        start_all(step + 1, 1 - slot)

    qx = qx_ref[0]
    n_q = o_ref.shape[1]
    qq = qx[:n_q]

    @pl.when(c == 0)
    def _():
        nrow = kc_ref.shape[1]
        s = _nt(kc_ref[0], qx)
        r = lax.broadcasted_iota(jnp.int32, (nrow, S_ROWS), 0)
        band0 = nrow - CMP_BAND
        s = s + jnp.concatenate([jnp.zeros((band0, S_ROWS), F32), ccs_ref[...]], axis=0)
        pn = _softmax_cols(s, r < nrow - 1)
        oc_ref[...] = jnp.dot(vct_ref[0], pn.astype(BF16), preferred_element_type=F32).T[:n_q]
        imp = jnp.dot(pn, gsum_ref[...], preferred_element_type=F32, precision=HIGHEST)
        score = _split_dot(mt_ref[...], imp)
        blk_i = lax.broadcasted_iota(jnp.int32, (n_blocks, S_ROWS), 0)
        forced = (blk_i == 0) | (blk_i == n_blocks - 1)
        selt = _select_topk(score, blk_i.astype(F32), N_SEL - 3, always=forced)
        selt = lax.dot_general(selt.astype(BF16), gsum_ref[...].astype(BF16), (((1,), (1,)), ((), ())),
                               preferred_element_type=F32)
        selm1 = selt.T[:n_q] - 1.0
        e_r = lax.broadcasted_iota(jnp.int32, (BLK_PER_TILE, SEL_TILE), 0)
        e_c = lax.broadcasted_iota(jnp.int32, (BLK_PER_TILE, SEL_TILE), 1)
        expand = jnp.where(e_c // SEL_BLOCK == e_r, -NEG, 0.0)
        for kt in range(n_blocks // BLK_PER_TILE):
            mask_ref[:, kt * SEL_TILE:(kt + 1) * SEL_TILE] = jnp.dot(
                selm1[:, kt * BLK_PER_TILE:(kt + 1) * BLK_PER_TILE], expand, preferred_element_type=F32)

        sw = _nt(qq, kwin_ref[0].astype(BF16)) + cws_ref[:n_q, :]
        sn = _nt(qq, kwnew_ref[0].astype(BF16)) + cns_ref[:n_q, :]
        m = jnp.maximum(jnp.max(sw, axis=1, keepdims=True), jnp.max(sn, axis=1, keepdims=True))
        pw = jnp.exp2(sw - m)
        pn2 = jnp.exp2(sn - m)
        l = jnp.sum(pw, axis=1, keepdims=True) + jnp.sum(pn2, axis=1, keepdims=True)
        ow = (jnp.dot(pw.astype(BF16), vwin_ref[0].astype(BF16), preferred_element_type=F32)
              + jnp.dot(pn2.astype(BF16), vwnew_ref[0].astype(BF16), preferred_element_type=F32))
        ow_ref[...] = ow / l

        m_ref[...] = jnp.full_like(m_ref, NEG)
        l_ref[...] = jnp.zeros_like(l_ref)
        acc_ref[...] = jnp.zeros_like(acc_ref)

    pltpu.make_async_copy(kbuf.at[slot], kbuf.at[slot], sem.at[0, slot]).wait()
    pltpu.make_async_copy(vbuf.at[slot], vbuf.at[slot], sem.at[1, slot]).wait()

    col0 = pl.multiple_of(c * S_CHUNK, S_CHUNK)
    s = (jnp.dot(qq, kbuf[slot].astype(BF16), preferred_element_type=F32)
         + mask_ref[:, pl.ds(col0, S_CHUNK)])

    @pl.when(c < n_chunks - 1)
    def _():
        _online_update(s, vbuf[slot].astype(BF16), m_ref, l_ref, acc_ref, v_transposed=True)

    @pl.when(c == n_chunks - 1)
    def _():
        near = jnp.concatenate([jnp.zeros((n_q, S_CHUNK - LANES), F32), css_ref[:n_q, :]], axis=1)
        _online_update(s + near, vbuf[slot].astype(BF16), m_ref, l_ref, acc_ref, v_transposed=True)
        sn = _nt(qq, knew_ref[0].astype(BF16)) + cns_ref[:n_q, :]
        _online_update(sn, vnew_ref[0].astype(BF16), m_ref, l_ref, acc_ref)
        o_s = acc_ref[...] / l_ref[...]
        o_ref[0] = gm_ref[0, 0] * oc_ref[...] + gm_ref[0, 1] * o_s + gm_ref[0, 2] * ow_ref[...]


def _nsa_sample(page_table, qx, gm, kc, vct, kpool, vpool, knew, vnew, kwin, vwin, kwnew, vwnew,
                mt, gsum, ccs, css, cns, cws):
    nb = qx.shape[0]
    n_pages = page_table.shape[1]
    n_chunks = n_pages // S_CHUNK_PAGES
    past = n_pages * PAGE
    n_blocks = past // SEL_BLOCK
    n_q = gm.shape[2]

    def bspec(shape):
        nd = len(shape)
        return pl.BlockSpec((1,) + tuple(shape[1:]), lambda b, c, pt: (b,) + (0,) * (nd - 1))

    def cspec(shape):
        nd = len(shape)
        return pl.BlockSpec(tuple(shape), lambda b, c, pt: (0,) * nd, pipeline_mode=pl.Buffered(1))

    grid_spec = pltpu.PrefetchScalarGridSpec(
        num_scalar_prefetch=1,
        grid=(nb, n_chunks),
        in_specs=[bspec(qx.shape), bspec(gm.shape), bspec(kc.shape), bspec(vct.shape),
                  pl.BlockSpec(memory_space=pl.ANY), pl.BlockSpec(memory_space=pl.ANY),
                  bspec(knew.shape), bspec(vnew.shape), bspec(kwin.shape), bspec(vwin.shape),
                  bspec(kwnew.shape), bspec(vwnew.shape),
                  cspec(mt.shape), cspec(gsum.shape), cspec(ccs.shape), cspec(css.shape),
                  cspec(cns.shape), cspec(cws.shape)],
        out_specs=pl.BlockSpec((1, n_q, KV_DIM), lambda b, c, pt: (b, 0, 0)),
        scratch_shapes=[pltpu.VMEM((2, KV_DIM, S_CHUNK), F32),
                        pltpu.VMEM((2, KV_DIM, S_CHUNK), F32),
                        pltpu.SemaphoreType.DMA((2, 2)),
                        pltpu.VMEM((n_q, past), F32),
                        pltpu.VMEM((n_q, 1), F32), pltpu.VMEM((n_q, 1), F32),
                        pltpu.VMEM((n_q, KV_DIM), F32),
                        pltpu.VMEM((n_q, KV_DIM), F32), pltpu.VMEM((n_q, KV_DIM), F32)],
    )
    return pl.pallas_call(
        functools.partial(_nsa_sample_kernel, n_chunks=n_chunks, n_blocks=n_blocks),
        grid_spec=grid_spec,
        out_shape=jax.ShapeDtypeStruct((nb, n_q, KV_DIM), F32),
        compiler_params=_cparams(("arbitrary", "arbitrary")),
        name="nsa_sample",
    )(page_table, qx, gm, kc, vct, kpool, vpool, knew, vnew, kwin, vwin, kwnew, vwnew,
      mt, gsum, ccs, css, cns, cws)


def _ffn_kernel(x_ref, oa_ref, ob_ref, sa_ref, sb_ref, wpa_ref, wpb_ref, wo_ref, nf_ref,
                wg_ref, wu_ref, wd_ref, nl_ref, y_ref):
    pa = jnp.dot(oa_ref[...], wpa_ref[...], preferred_element_type=F32)
    pb = jnp.dot(ob_ref[...], wpb_ref[...], preferred_element_type=F32)
    merged = sa_ref[...] * pa + sb_ref[...] * pb
    x = x_ref[...] + jnp.dot(merged.astype(BF16), wo_ref[...], preferred_element_type=F32)
    hn = (x * lax.rsqrt(jnp.mean(x * x, axis=-1, keepdims=True) + EPS) * nf_ref[...]).astype(BF16)
    gate = jnp.dot(hn, wg_ref[...], preferred_element_type=F32)
    up = jnp.dot(hn, wu_ref[...], preferred_element_type=F32)
    ff = (jax.nn.silu(gate) * up).astype(BF16)
    x = x + jnp.dot(ff, wd_ref[...], preferred_element_type=F32)
    y_ref[...] = x * lax.rsqrt(jnp.mean(x * x, axis=-1, keepdims=True) + EPS) * nl_ref[...]


def _ffn(x2d, oa, ob, sa, sb, wpa, wpb, wo, nf, wg, wu, wd, nl, tm):
    rows = x2d.shape[0]

    def rspec(n):
        return pl.BlockSpec((tm, n), lambda i: (i, 0))

    return pl.pallas_call(
        _ffn_kernel,
        grid=(rows // tm,),
        in_specs=[rspec(D_MODEL), rspec(Q_DIM), rspec(HG_W), rspec(D_MODEL), rspec(D_MODEL),
                  _const_spec(wpa.shape), _const_spec(wpb.shape), _const_spec(wo.shape),
                  _const_spec(nf.shape), _const_spec(wg.shape), _const_spec(wu.shape),
                  _const_spec(wd.shape), _const_spec(nl.shape)],
        out_specs=rspec(D_MODEL),
        out_shape=jax.ShapeDtypeStruct((rows, D_MODEL), F32),
        compiler_params=_cparams(("arbitrary",)),
        name="ffn",
    )(x2d, oa, ob, sa, sb, wpa, wpb, wo, nf, wg, wu, wd, nl)


def _pack_w_in(w_in):
    sizes = (Q_DIM,) + (KV_DIM,) * 6 + (3 * N_HEADS,) + (HG_W,) * 4 + (D_MODEL,) * 2
    offs = np.concatenate([[0], np.cumsum(sizes)])
    q = w_in[:, offs[0]:offs[1]].reshape(D_MODEL, N_HEADS, HEAD_DIM)[:, _HEAD_PERM, :].reshape(D_MODEL, Q_DIM)
    g = jnp.pad(w_in[:, offs[7]:offs[8]], ((0, 0), (0, LANES - 3 * N_HEADS)))
    return jnp.concatenate([q, w_in[:, offs[1]:offs[7]], g, w_in[:, offs[8]:]], axis=1).astype(BF16)


def _strip(bvc, rel, lo=0, hi=None, masked=NEG):
    val = bvc[:, np.clip(rel, 0, 255)]
    ok = rel >= lo
    if hi is not None:
        ok = ok & (rel < hi)
    return jnp.where(jnp.asarray(ok)[None], val, masked)


def _toeplitz(bvc, a, n_rows, n_cols, lo=0, hi=None):
    n = n_rows + n_cols - 1
    u = _strip(bvc, a - (n_rows - 1) + np.arange(n), lo, hi)
    u = jnp.pad(u, ((0, 0), (0, 1)))
    circ = jnp.tile(u, (1, n_rows))[:, :n_rows * n].reshape(N_HEADS, n_rows, n)
    return circ[:, :, n_rows - 1:n_rows - 1 + n_cols]


def _bias_strips_prompt(bvc):
    gq = GROUP * Q_BLOCK

    def lanes(x):
        return x.reshape(N_KV, GROUP, x.shape[1], Q_BLOCK).transpose(0, 2, 1, 3).reshape(N_KV, x.shape[1], gq)

    cs = jnp.stack([lanes(_toeplitz(bvc, Q_BLOCK * d, Q_BLOCK, Q_BLOCK)) for d in (-1, 0, 1, 2)], axis=1)
    bw = lanes(jnp.concatenate(
        [_toeplitz(bvc, Q_BLOCK * d, Q_BLOCK, Q_BLOCK, 0, WINDOW) for d in range(WINDOW // Q_BLOCK, -1, -1)], axis=1))
    rr = np.arange(CMP_BAND)[:, None]
    rel_c = np.arange(Q_BLOCK)[None, :] - CMP_STRIDE * (rr - CMP_PAD) - (CMP_BLOCK - 1)
    cc = _strip(bvc, rel_c, masked=0.0).reshape(N_KV, GROUP, CMP_BAND, Q_BLOCK)
    cc = cc.transpose(0, 2, 1, 3).reshape(N_KV, CMP_BAND, gq)
    return cc, cs, bw


def _bias_strips_sample(bvc, past, t):
    def rows(a):
        a = a.reshape(N_HEADS * t, a.shape[-1])
        return jnp.pad(a, ((0, S_ROWS - N_HEADS * t), (0, 0)))
    tt = np.arange(t)[:, None]
    nrow = past // CMP_STRIDE
    n = (nrow - CMP_BAND + np.arange(CMP_BAND))[None, :]
    ccs = rows(_strip(bvc, past + tt - CMP_STRIDE * n - (CMP_BLOCK - 1), masked=0.0)).T
    i = np.arange(LANES)[None, :]
    css = rows(_strip(bvc, LANES + tt - i))
    cns = rows(_strip(bvc, np.where(i < t, tt - i, -1)))
    iw = np.arange(WINDOW)[None, :]
    cws = rows(_strip(bvc, WINDOW + tt - iw, 0, WINDOW))
    return ccs, css, cns, cws


def kernel(x_prompt, x_sample, cache_k_cmp, cache_v_cmp, cache_k_slc, cache_v_slc, state_k_win, state_v_win,
           state_hgrn, page_table, norm_mix, w_in, cmp_pe_k, cmp_w1_k, cmp_w2_k, cmp_pe_v, cmp_w1_v, cmp_w2_v,
           rel_bias, hg_lb_logits, hg_norm, w_proj_a, w_proj_b, w_out, norm_ffn, w_gate, w_up, w_down, norm_final):
    nbp, t_p, _ = x_prompt.shape
    nbs, t_s, _ = x_sample.shape
    assert nbp == 1 and norm_mix.shape[0] == 1
    n_pages = page_table.shape[1]
    past = n_pages * PAGE
    assert state_k_win.shape[2] == WINDOW and past % S_CHUNK == 0 and t_s <= SUBLANES

    lb = jnp.cumsum(jax.nn.softmax(hg_lb_logits.astype(F32), axis=0), axis=0)[0]
    lb3 = jnp.pad(jnp.stack([jnp.log(lb), jnp.log1p(-lb), 1.0 - lb]), ((0, SUBLANES - 3), (0, 0)))
    w_pack = _pack_w_in(w_in[0])
    g_mix = norm_mix[0][None, :]
    wpa = w_proj_a[0].reshape(N_HEADS, HEAD_DIM, D_MODEL)[_HEAD_PERM].reshape(Q_DIM, D_MODEL).astype(BF16)
    wpb = w_proj_b[0].astype(BF16)
    wo = w_out[0].astype(BF16)
    wg, wu, wd = w_gate[0].astype(BF16), w_up[0].astype(BF16), w_down[0].astype(BF16)
    nf, nl = norm_ffn[0][None, :], norm_final[None, :]
    gn = hg_norm[0][None, :]
    pe_k, w1_k, w2_k = _compress_weights(cmp_pe_k[0], cmp_w1_k[0], cmp_w2_k[0])
    pe_v, w1_v, w2_v = _compress_weights(cmp_pe_v[0], cmp_w1_v[0], cmp_w2_v[0])
    bvc = (rel_bias[_BUCKET] - rel_bias[N_BUCKETS - 1][None, :]).T * LOG2E
    cc, cs, bw = _bias_strips_prompt(bvc)
    ccs, css, cns, cws = _bias_strips_sample(bvc, past, t_s)

    xp2 = x_prompt.reshape(t_p, D_MODEL)
    xs2 = x_sample.reshape(nbs * t_s, D_MODEL)
    seg = lambda off, n: w_pack[:, off:off + n]
    w_t = jnp.concatenate([seg(_OFF_Q, Q_DIM), seg(_OFF_G, LANES), seg(_OFF_KV + 3 * KV_DIM, KV_DIM),
                           seg(_OFF_KV + 5 * KV_DIM, KV_DIM)], axis=1).T
    pp = _proj(xp2, g_mix, w_pack, w_t, lb3, PROJ_ROWS, True)
    ps = _proj(xs2, g_mix, w_pack, w_t, lb3, nbs * t_s, False)
    (kc_p, vc_p, ks_p, vs_p, kw_p, vw_p, ksb_p, kwb_p,
     qh_p, lf_p, kh_p, vh_p, gs_p, sa_p, sb_p, qat_p, gat_p, vst_p, vwt_p) = pp
    (qa_s, kc_s, vc_s, ks_s, vs_s, kw_s, vw_s, ga_s,
     qh_s, lf_s, kh_s, vh_s, gs_s, sa_s, sb_s) = ps

    ob_p, s_p = _hgrn_prompt(qh_p, kh_p, vh_p, lf_p, gs_p, gn)
    r3 = lambda a: a.reshape(nbs, t_s, a.shape[-1])
    ob_s, s_s = _hgrn_sample(r3(qh_s), r3(kh_s), r3(vh_s), r3(lf_s), r3(gs_s), gn, state_hgrn[0])

    ident = jnp.arange(t_p // PAGE, dtype=jnp.int32)[None, :]
    pool_rows = lambda a: a.reshape(-1, PAGE, KV_DIM)
    kcb_p, _ = _compress(pool_rows(kc_p), ident, pe_k, w1_k, w2_k, False)
    _, vct_p = _compress(pool_rows(vc_p), ident, pe_v, w1_v, w2_v, False)
    back = -(CMP_PAD + kcb_p.shape[1]) % LANES
    kcb_p = jnp.pad(kcb_p[0], ((CMP_PAD, back), (0, 0)))
    vct_p = jnp.pad(vct_p[0], ((0, 0), (CMP_PAD, back)))
    nrow_p = kcb_p.shape[0]
    mt_p = jnp.asarray(_score_matrix(nrow_p, CMP_PAD, t_p // SEL_BLOCK), BF16)
    kwb_pad = jnp.pad(kwb_p, ((WINDOW, 0), (0, 0)))
    vst3 = vst_p.reshape(KV_DIM, t_p // SEL_TILE, SEL_TILE).transpose(1, 0, 2)
    ones_rows = jnp.ones((t_p // SEL_TILE, 2 * SUBLANES, SEL_TILE), BF16)
    vst3 = jnp.stack([jnp.concatenate([vst3[:, kv * HEAD_DIM:(kv + 1) * HEAD_DIM, :], ones_rows], axis=1)
                      for kv in range(N_KV)])
    vwt3 = jnp.pad(vwt_p, ((0, 0), (WINDOW, 0))).reshape(KV_DIM, (t_p + WINDOW) // Q_BLOCK, Q_BLOCK)
    vwt3 = vwt3.transpose(1, 0, 2)
    eb = (np.arange(LANES)[None, :] == np.arange(SEL_TILE)[:, None] // SEL_BLOCK).astype(np.float32)
    oa_p = _nsa_prompt(qat_p, gat_p, kcb_p, vct_p, ksb_p, vst3, kwb_pad, vwt3, mt_p, cc, cs, bw,
                       jnp.asarray(eb, BF16))

    pool_t = lambda a: a.transpose(0, 2, 3, 1).reshape(-1, KV_DIM, PAGE)
    kcb_s, _ = _compress(pool_t(cache_k_cmp[0]), page_table, pe_k, w1_k, w2_k, True)
    _, vct_s = _compress(pool_t(cache_v_cmp[0]), page_table, pe_v, w1_v, w2_v, True)
    nq = N_HEADS * t_s
    qs4 = qa_s.reshape(nbs, t_s, GROUP, N_KV, HEAD_DIM).astype(F32)
    qx = jnp.einsum('btjkd,kq->bkjtqd', qs4, jnp.eye(N_KV, dtype=F32)).reshape(nbs, nq, KV_DIM)
    qx = jnp.pad(qx, ((0, 0), (0, S_ROWS - nq), (0, 0))).astype(BF16)
    g4 = ga_s[:, :3 * N_HEADS].reshape(nbs, t_s, N_KV, GROUP, 3)
    gm = jnp.transpose(g4, (0, 4, 2, 3, 1)).reshape(nbs, 3, nq, 1)
    gm = jnp.broadcast_to(gm, (nbs, 3, nq, KV_DIM))
    new_tile = lambda a: jnp.pad(a.reshape(nbs, t_s, KV_DIM), ((0, 0), (0, LANES - t_s), (0, 0)))
    mt_s = jnp.asarray(_score_matrix(past // CMP_STRIDE, 0, past // SEL_BLOCK), BF16)
    gsum = np.zeros((S_ROWS, S_ROWS), np.float32)
    for kv in range(N_KV):
        for j in range(GROUP):
            for t in range(t_s):
                gsum[(kv * GROUP + j) * t_s + t, kv * t_s + t] = 1.0
    o_kv = _nsa_sample(page_table, qx, gm, kcb_s, vct_s,
                       pool_t(cache_k_slc[0]), pool_t(cache_v_slc[0]),
                       new_tile(ks_s), new_tile(vs_s), state_k_win[0].reshape(nbs, WINDOW, KV_DIM),
                       state_v_win[0].reshape(nbs, WINDOW, KV_DIM), new_tile(kw_s), new_tile(vw_s),
                       mt_s, jnp.asarray(gsum), ccs, css, cns, cws)
    o5 = o_kv.reshape(nbs, N_KV, GROUP, t_s, N_KV, HEAD_DIM)
    oa_s = jnp.einsum('bkjtqd,kq->btjkd', o5, jnp.eye(N_KV, dtype=F32)).reshape(nbs * t_s, Q_DIM).astype(BF16)

    y_p = _ffn(xp2, oa_p, ob_p, sa_p, sb_p, wpa, wpb, wo, nf, wg, wu, wd, nl, FFN_ROWS)
    y_s = _ffn(xs2, oa_s, ob_s.reshape(nbs * t_s, HG_W), sa_s, sb_s, wpa, wpb, wo, nf, wg, wu, wd, nl, nbs * t_s)

    kv5 = lambda a, nb_, tt: a.reshape(1, nb_, tt, N_KV, HEAD_DIM)
    wl = min(WINDOW, t_p)
    win = lambda st, new: jnp.concatenate(
        [st[0], new.reshape(nbs, t_s, N_KV, HEAD_DIM)], axis=1)[:, -WINDOW:][None]
    return (y_p.reshape(1, t_p, D_MODEL), y_s.reshape(nbs, t_s, D_MODEL),
            kv5(kc_p, 1, t_p), kv5(vc_p, 1, t_p), kv5(ks_p, 1, t_p), kv5(vs_p, 1, t_p),
            kv5(kw_p[-wl:], 1, wl), kv5(vw_p[-wl:], 1, wl), s_p[None, None],
            kv5(kc_s, nbs, t_s), kv5(vc_s, nbs, t_s), kv5(ks_s, nbs, t_s), kv5(vs_s, nbs, t_s),
            win(state_k_win, kw_s), win(state_v_win, vw_s), s_s[None])
```

```python
import functools
import math

import numpy as np
import jax
import jax.numpy as jnp
from jax import lax
from jax.experimental import pallas as pl
from jax.experimental.pallas import tpu as pltpu

F32 = jnp.float32
BF16 = jnp.bfloat16
HIGHEST = lax.Precision.HIGHEST

D_MODEL = 1024
N_HEADS = 8
N_KV = 2
GROUP = N_HEADS // N_KV
HEAD_DIM = 64
KV_DIM = N_KV * HEAD_DIM
Q_DIM = N_HEADS * HEAD_DIM
CMP_BLOCK = 32
CMP_STRIDE = 16
CMP_HIDDEN = 2 * HEAD_DIM
SEL_BLOCK = 64
N_SEL = 16
WINDOW = 512
Q_BLOCK = 128
FORCE_BONUS = 1e4
N_BUCKETS = 32
MAX_DISTANCE = 128
HG_HEADS = 4
HG_DIM = 128
HG_CHUNK = 32
HG_SUB = 16
HG_W = HG_HEADS * HG_DIM
D_FF = ((8 * D_MODEL // 3 + 255) // 256) * 256
EPS = 1e-6
PAGE = 128
NEG = -1e30
LOG2E = math.log2(math.e)
Q_SCALE = HEAD_DIM ** -0.5 * LOG2E

LANES = 128
SUBLANES = 8
MXU_DIM = 256
VMEM_LIMIT = 56 * 1024 * 1024

PROJ_ROWS = 512
FFN_ROWS = 256
HGRN_ROWS = 512
CMP_ROWS = MXU_DIM

_OFF_Q = 0
_OFF_KV = _OFF_Q + Q_DIM
_OFF_G = _OFF_KV + 6 * KV_DIM
_OFF_HG = _OFF_G + LANES
_OFF_GATE = _OFF_HG + 4 * HG_W
_PROJ_N = _OFF_GATE + 2 * D_MODEL

_HEAD_PERM = np.array([h for j in range(GROUP) for h in (j, GROUP + j)])

SEL_TILE = 512
BLK_PER_TILE = SEL_TILE // SEL_BLOCK
CMP_PAD = 16
CMP_BAND = 24


def _cparams(sem, vmem=VMEM_LIMIT):
    return pltpu.CompilerParams(dimension_semantics=sem, vmem_limit_bytes=vmem)


def _const_spec(shape):
    nd = len(shape)
    return pl.BlockSpec(shape, lambda *_: (0,) * nd, pipeline_mode=pl.Buffered(1))


def _bucket_table():
    n = np.arange(256)
    max_exact = N_BUCKETS // 2
    nf = np.maximum(n, 1).astype(np.float64)
    large = max_exact + (np.log(nf / max_exact) / math.log(MAX_DISTANCE / max_exact)
                         * (N_BUCKETS - max_exact)).astype(np.int64)
    large = np.minimum(large, N_BUCKETS - 1)
    return np.where(n < max_exact, n, large)


_BUCKET = _bucket_table()


def _proj_kernel(x_ref, g_ref, w_ref, wt_ref, lb_ref, *out_refs, prompt):
    if prompt:
        (kc_ref, vc_ref, ks_ref, vs_ref, kw_ref, vw_ref, ksb_ref, kwb_ref,
         qh_ref, lf_ref, kh_ref, vh_ref, gs_ref, sa_ref, sb_ref,
         qat_ref, gat_ref, vst_ref, vwt_ref) = out_refs
    else:
        (qa_ref, kc_ref, vc_ref, ks_ref, vs_ref, kw_ref, vw_ref, ga_ref,
         qh_ref, lf_ref, kh_ref, vh_ref, gs_ref, sa_ref, sb_ref) = out_refs
    x = x_ref[...]
    xn = x * lax.rsqrt(jnp.mean(x * x, axis=-1, keepdims=True) + EPS) * g_ref[...]
    xb = xn.astype(BF16)

    def seg(a, n):
        return jnp.dot(xb, w_ref[:, a:a + n], preferred_element_type=F32)

    f32_refs = (kc_ref, vc_ref, ks_ref, vs_ref, kw_ref, vw_ref)
    for i in range(0, 6, 2):
        u = seg(_OFF_KV + i * KV_DIM, 2 * KV_DIM)
        f32_refs[i][...] = u[:, :KV_DIM]
        f32_refs[i + 1][...] = u[:, KV_DIM:]
        if prompt and i > 0:
            (ksb_ref if i == 2 else kwb_ref)[...] = u[:, :KV_DIM].astype(BF16)

    if prompt:
        def seg_t(a, n):
            return lax.dot_general(wt_ref[a:a + n, :], xb, (((1,), (1,)), ((), ())),
                                   preferred_element_type=F32)

        qat_ref[...] = (seg_t(0, Q_DIM) * Q_SCALE).astype(BF16)
        gat_ref[...] = jax.nn.sigmoid(seg_t(Q_DIM, LANES))
        vt = seg_t(Q_DIM + LANES, 2 * KV_DIM)
        vst_ref[...] = vt[:KV_DIM, :].astype(BF16)
        vwt_ref[...] = vt[KV_DIM:, :].astype(BF16)
    else:
        qa_ref[...] = (seg(_OFF_Q, Q_DIM) * Q_SCALE).astype(BF16)
        ga_ref[...] = jax.nn.sigmoid(seg(_OFF_G, LANES))

    log_lb = lb_ref[0:1, :]
    log_1m = lb_ref[1:2, :]
    one_m = lb_ref[2:3, :]
    qh_ref[...] = jax.nn.silu(seg(_OFF_HG, HG_W))
    z = seg(_OFF_HG + HG_W, HG_W)
    b = log_1m + (jnp.minimum(z, 0.0) - jnp.log1p(jnp.exp(-jnp.abs(z))))
    hi = jnp.maximum(log_lb, b)
    lf_ref[...] = hi + jnp.log1p(jnp.exp(-jnp.abs(log_lb - b)))
    kh_ref[...] = one_m * jax.nn.sigmoid(-z)
    vh_ref[...] = seg(_OFF_HG + 2 * HG_W, HG_W)
    gs_ref[...] = jax.nn.silu(seg(_OFF_HG + 3 * HG_W, HG_W))
    sa_ref[...] = jax.nn.sigmoid(seg(_OFF_GATE, D_MODEL))
    sb_ref[...] = jax.nn.sigmoid(seg(_OFF_GATE + D_MODEL, D_MODEL))


_PROJ_T = Q_DIM + LANES + 2 * KV_DIM


def _proj(x2d, g, w, wt, lb3, tm, prompt):
    rows = x2d.shape[0]
    tail = [(HG_W, F32)] * 5 + [(D_MODEL, F32)] * 2
    if prompt:
        widths = [(KV_DIM, F32)] * 6 + [(KV_DIM, BF16)] * 2 + tail
        heights = [(Q_DIM, BF16), (LANES, F32), (KV_DIM, BF16), (KV_DIM, BF16)]
    else:
        widths = [(Q_DIM, BF16)] + [(KV_DIM, F32)] * 6 + [(LANES, F32)] + tail
        heights = []
    return pl.pallas_call(
        functools.partial(_proj_kernel, prompt=prompt),
        grid=(rows // tm,),
        in_specs=[pl.BlockSpec((tm, D_MODEL), lambda i: (i, 0)),
                  _const_spec((1, D_MODEL)),
                  _const_spec((D_MODEL, _PROJ_N)),
                  _const_spec((_PROJ_T, D_MODEL)),
                  _const_spec((SUBLANES, HG_W))],
        out_specs=([pl.BlockSpec((tm, n), lambda i: (i, 0)) for n, _ in widths]
                   + [pl.BlockSpec((n, tm), lambda i: (0, i)) for n, _ in heights]),
        out_shape=([jax.ShapeDtypeStruct((rows, n), dt) for n, dt in widths]
                   + [jax.ShapeDtypeStruct((n, rows), dt) for n, dt in heights]),
        compiler_params=_cparams(("arbitrary",)),
        name="proj",
    )(x2d, g, w, wt, lb3)


_CH_W = CMP_STRIDE * KV_DIM
_CH_PER_PAGE = PAGE // CMP_STRIDE
_CH_PITCH = CMP_STRIDE + SUBLANES


def _compress_kernel(pt_ref, pool_ref, pe_ref, w1_ref, w2_ref, out_ref, outt_ref, buf, rbuf, xa, hbuf, sem,
                     *, n_pages, pages_transposed):
    b = pl.program_id(0)
    nb = pl.num_programs(0)
    slot = b % 2
    n_ch = n_pages * _CH_PER_PAGE

    def page_copy(bb, p, s):
        return pltpu.make_async_copy(pool_ref.at[pt_ref[bb, p]], buf.at[s, p], sem.at[s])

    def start_all(bb, s):
        def body(p, c):
            page_copy(bb, p, s).start()
            return c
        lax.fori_loop(0, n_pages, body, 0)

    @pl.when(b == 0)
    def _():
        start_all(b, slot)

    @pl.when(b + 1 < nb)
    def _():
        start_all(b + 1, 1 - slot)

    pltpu.make_async_copy(buf.at[slot], buf.at[slot], sem.at[slot]).wait()

    rows = math.gcd(n_ch, CMP_ROWS)
    pages_per_group = rows // _CH_PER_PAGE

    def to_rows(g):
        for p in range(g * pages_per_group, (g + 1) * pages_per_group):
            page = buf[slot, p]
            page = page.T if pages_transposed else page
            for i in range(_CH_PER_PAGE):
                dst = (p * _CH_PER_PAGE + i) * _CH_PITCH
                rbuf[dst:dst + CMP_STRIDE, :] = page[i * CMP_STRIDE:(i + 1) * CMP_STRIDE, :]

    to_rows(0)
    for r in range(n_ch // rows):
        if r + 1 < n_ch // rows:
            to_rows(r + 1)
        for s in range(CMP_STRIDE):
            x = rbuf[pl.ds(r * rows * _CH_PITCH + s, rows, stride=_CH_PITCH), :]
            xa[:, s * KV_DIM:(s + 1) * KV_DIM] = x.astype(BF16)
        hbuf[r * rows:(r + 1) * rows, :] = jnp.dot(xa[...], w1_ref[...], preferred_element_type=F32)
    pw = _split_dot_rhs(pe_ref[...], w1_ref[...])
    nh = N_KV * CMP_HIDDEN
    bias = pw[0:1, 0:nh] + pw[1:2, nh:2 * nh]
    h = hbuf[:, 0:nh] + pltpu.roll(hbuf[:, nh:2 * nh], n_ch - 1, 0) + bias
    blocks = jnp.dot(jax.nn.gelu(h).astype(BF16), w2_ref[...], preferred_element_type=F32)
    row = lax.broadcasted_iota(jnp.int32, blocks.shape, 0)
    blocks = jnp.where(row < n_ch - 1, blocks, 0.0)
    out_ref[0] = blocks.astype(BF16)
    outt_ref[0] = blocks.T.astype(BF16)


def _compress(pool, page_table, pe, w1, w2, pages_transposed):
    nbatch, n_pages = page_table.shape
    n_ch = n_pages * _CH_PER_PAGE
    rows = math.gcd(n_ch, CMP_ROWS)
    grid_spec = pltpu.PrefetchScalarGridSpec(
        num_scalar_prefetch=1,
        grid=(nbatch,),
        in_specs=[pl.BlockSpec(memory_space=pl.ANY),
                  _const_spec((SUBLANES, _CH_W)),
                  _const_spec((_CH_W, 4 * CMP_HIDDEN)),
                  _const_spec((2 * CMP_HIDDEN, KV_DIM))],
        out_specs=[pl.BlockSpec((1, n_ch, KV_DIM), lambda b, pt: (b, 0, 0)),
                   pl.BlockSpec((1, KV_DIM, n_ch), lambda b, pt: (b, 0, 0))],
        scratch_shapes=[pltpu.VMEM((2, n_pages, PAGE, KV_DIM), F32),
                        pltpu.VMEM((n_ch * _CH_PITCH, KV_DIM), F32),
                        pltpu.VMEM((rows, _CH_W), BF16),
                        pltpu.VMEM((n_ch, 4 * CMP_HIDDEN), F32),
                        pltpu.SemaphoreType.DMA((2,))],
    )
    return pl.pallas_call(
        functools.partial(_compress_kernel, n_pages=n_pages, pages_transposed=pages_transposed),
        grid_spec=grid_spec,
        out_shape=[jax.ShapeDtypeStruct((nbatch, n_ch, KV_DIM), BF16),
                   jax.ShapeDtypeStruct((nbatch, KV_DIM, n_ch), BF16)],
        compiler_params=_cparams(("arbitrary",)),
        name="compress",
    )(page_table, pool, pe, w1, w2)


def _compress_weights(pe, w1, w2):
    c = CMP_BLOCK // CMP_STRIDE
    pe_r = pe.reshape(c, CMP_STRIDE, 1, HEAD_DIM)
    pe_x = jnp.broadcast_to(pe_r, (c, CMP_STRIDE, N_KV, HEAD_DIM)).reshape(c, _CH_W)
    pe_x = jnp.pad(pe_x, ((0, SUBLANES - c), (0, 0)))
    w1_r = w1.reshape(c, CMP_STRIDE, HEAD_DIM, CMP_HIDDEN)
    eye = jnp.eye(N_KV, dtype=w1.dtype)
    w1_x = jnp.einsum('jsde,kq->skdjqe', w1_r, eye).reshape(_CH_W, c * N_KV * CMP_HIDDEN)
    w2_x = jnp.einsum('ed,kq->keqd', w2, eye).reshape(N_KV * CMP_HIDDEN, KV_DIM)
    return pe_x, w1_x.astype(BF16), w2_x.astype(BF16)


def _hgrn_chunk(q, k, v, lf, st_ref, chunk, sub):
    if chunk > SUBLANES:
        r = lax.broadcasted_iota(jnp.int32, (chunk, chunk), 0)
        c = lax.broadcasted_iota(jnp.int32, (chunk, chunk), 1)
        tri = (r >= c).astype(F32)
        b = jnp.dot(tri, lf, preferred_element_type=F32, precision=HIGHEST)
    else:
        rows = [lf[0:1, :]]
        for t in range(1, chunk):
            rows.append(rows[-1] + lf[t:t + 1, :])
        b = jnp.concatenate(rows, axis=0)
    bl = b[chunk - 1:chunk, :]
    qe = q * jnp.exp(b)
    kd = k * jnp.exp(bl - b)
    ebl = jnp.exp(bl)
    n_sub = chunk // sub
    trow = lax.broadcasted_iota(jnp.int32, (sub, HG_W), 0)
    crow = lax.broadcasted_iota(jnp.int32, (chunk, HG_W), 0)

    diag = []
    for i in range(n_sub):
        qi = q[i * sub:(i + 1) * sub, :]
        bi = b[i * sub:(i + 1) * sub, :]
        acc = [jnp.zeros((sub, HG_DIM), F32) for _ in range(HG_HEADS)]
        for s in range(sub):
            row = i * sub + s
            dec = jnp.exp(jnp.where(trow >= s, bi - b[row:row + 1, :], -jnp.inf))
            prod = qi * k[row:row + 1, :] * dec
            for h in range(HG_HEADS):
                a = jnp.sum(prod[:, h * HG_DIM:(h + 1) * HG_DIM], axis=1, keepdims=True)
                acc[h] = acc[h] + a * v[row:row + 1, h * HG_DIM:(h + 1) * HG_DIM]
        diag.append(acc)

    off = []
    for i in range(n_sub):
        if i == 0:
            off.append(None)
            continue
        b0 = b[i * sub - 1:i * sub, :]
        qs = (q[i * sub:(i + 1) * sub, :] * jnp.exp(b[i * sub:(i + 1) * sub, :] - b0)).astype(BF16)
        ks = (k * jnp.exp(jnp.where(crow < i * sub, b0 - b, -jnp.inf))).astype(BF16)
        off.append((qs, ks))

    vb = v.astype(BF16)
    outs = []
    for h in range(HG_HEADS):
        sl = slice(h * HG_DIM, (h + 1) * HG_DIM)
        st = st_ref[h]
        o_h = lax.dot_general(qe[:, sl].astype(BF16), st.astype(BF16), (((1,), (1,)), ((), ())),
                              preferred_element_type=F32)
        parts = []
        for i in range(n_sub):
            d = diag[i][h]
            if off[i] is not None:
                qs, ks = off[i]
                a = lax.dot_general(qs[:, sl], ks[:, sl], (((1,), (1,)), ((), ())),
                                    preferred_element_type=F32)
                d = d + jnp.dot(a.astype(BF16), vb[:, sl], preferred_element_type=F32)
            parts.append(d)
        intra = parts[0] if n_sub == 1 else jnp.concatenate(parts, axis=0)
        outs.append(o_h + intra)
        st_ref[h] = st * ebl[:, sl] + lax.dot_general(
            vb[:, sl], kd[:, sl].astype(BF16), (((0,), (0,)), ((), ())), preferred_element_type=F32)
    return jnp.concatenate(outs, axis=1)


def _hgrn_finish(o, gs, gn):
    outs = []
    for h in range(HG_HEADS):
        oh = o[:, h * HG_DIM:(h + 1) * HG_DIM]
        y = oh * lax.rsqrt(jnp.mean(oh * oh, axis=-1, keepdims=True) + EPS) * gn
        outs.append(y)
    return (jnp.concatenate(outs, axis=1) * gs).astype(BF16)


def _hgrn_prompt_kernel(q_ref, k_ref, v_ref, lf_ref, gs_ref, gn_ref, o_ref, s_ref, st_ref, *, n_chunks):
    i = pl.program_id(0)

    @pl.when(i == 0)
    def _():
        st_ref[...] = jnp.zeros_like(st_ref)

    def body(c, carry):
        r = pl.ds(pl.multiple_of(c * HG_CHUNK, HG_CHUNK), HG_CHUNK)
        o = _hgrn_chunk(q_ref[r, :], k_ref[r, :], v_ref[r, :], lf_ref[r, :], st_ref, HG_CHUNK, HG_SUB)
        o_ref[r, :] = _hgrn_finish(o, gs_ref[r, :], gn_ref[...])
        return carry
    lax.fori_loop(0, n_chunks, body, 0, unroll=8)

    @pl.when(i == pl.num_programs(0) - 1)
    def _():
        for h in range(HG_HEADS):
            s_ref[h] = st_ref[h].T


def _hgrn_prompt(qh, kh, vh, lf, gs, gn, rows_per_step=HGRN_ROWS):
    t = qh.shape[0]
    spec = pl.BlockSpec((rows_per_step, HG_W), lambda i: (i, 0))
    return pl.pallas_call(
        functools.partial(_hgrn_prompt_kernel, n_chunks=rows_per_step // HG_CHUNK),
        grid=(t // rows_per_step,),
        in_specs=[spec] * 5 + [_const_spec((1, HG_DIM))],
        out_specs=[spec, pl.BlockSpec((HG_HEADS, HG_DIM, HG_DIM), lambda i: (0, 0, 0))],
        out_shape=[jax.ShapeDtypeStruct((t, HG_W), BF16),
                   jax.ShapeDtypeStruct((HG_HEADS, HG_DIM, HG_DIM), F32)],
        scratch_shapes=[pltpu.VMEM((HG_HEADS, HG_DIM, HG_DIM), F32)],
        compiler_params=_cparams(("arbitrary",)),
        name="hgrn_prompt",
    )(qh, kh, vh, lf, gs, gn)


def _hgrn_sample_kernel(q_ref, k_ref, v_ref, lf_ref, gs_ref, gn_ref, s0_ref, o_ref, s_ref, st_ref, *, t):
    for h in range(HG_HEADS):
        st_ref[h] = s0_ref[0, h].T
    o = _hgrn_chunk(q_ref[0], k_ref[0], v_ref[0], lf_ref[0], st_ref, t, t)
    o_ref[0] = _hgrn_finish(o, gs_ref[0], gn_ref[...])
    for h in range(HG_HEADS):
        s_ref[0, h] = st_ref[h].T


def _hgrn_sample(qh, kh, vh, lf, gs, gn, s0):
    nb, t, _ = qh.shape
    spec = pl.BlockSpec((1, t, HG_W), lambda b: (b, 0, 0))
    sspec = pl.BlockSpec((1, HG_HEADS, HG_DIM, HG_DIM), lambda b: (b, 0, 0, 0))
    return pl.pallas_call(
        functools.partial(_hgrn_sample_kernel, t=t),
        grid=(nb,),
        in_specs=[spec] * 5 + [_const_spec((1, HG_DIM)), sspec],
        out_specs=[spec, sspec],
        out_shape=[jax.ShapeDtypeStruct((nb, t, HG_W), BF16),
                   jax.ShapeDtypeStruct((nb, HG_HEADS, HG_DIM, HG_DIM), F32)],
        scratch_shapes=[pltpu.VMEM((HG_HEADS, HG_DIM, HG_DIM), F32)],
        compiler_params=_cparams(("arbitrary",)),
        name="hgrn_sample",
    )(qh, kh, vh, lf, gs, gn, s0)


def _select_topk(x, blk, n, always=None):
    nblk = x.shape[0]
    if always is None:
        sel = jnp.zeros_like(x)
    else:
        sel = jnp.where(always, 1.0, 0.0)
        x = jnp.where(always, -3e38, x)
    for _ in range(n):
        m = jnp.max(x, axis=0, keepdims=True)
        idx = jnp.min(jnp.where(x == m, blk, float(nblk)), axis=0, keepdims=True)
        pick = blk == idx
        sel = jnp.where(pick, 1.0, sel)
        x = jnp.where(pick, -3e38, x)
    return sel


def _softmax_cols(s, valid):
    m = jnp.max(jnp.where(valid, s, NEG), axis=0, keepdims=True)
    p = jnp.where(valid, jnp.exp2(s - m), 0.0)
    l = jnp.sum(p, axis=0, keepdims=True)
    return p * jnp.where(l > 0.0, 1.0 / l, 0.0)


def _split_dot(a_bf16, x):
    hi = x.astype(BF16)
    lo = (x - hi.astype(F32)).astype(BF16)
    return (jnp.dot(a_bf16, hi, preferred_element_type=F32)
            + jnp.dot(a_bf16, lo, preferred_element_type=F32))


def _split_dot_rhs(x, w_bf16):
    hi = x.astype(BF16)
    lo = (x - hi.astype(F32)).astype(BF16)
    return (jnp.dot(hi, w_bf16, preferred_element_type=F32)
            + jnp.dot(lo, w_bf16, preferred_element_type=F32))


def _nt(a, b):
    return lax.dot_general(a, b, (((1,), (1,)), ((), ())), preferred_element_type=F32)


def _online_update(s, v, m_ref, l_ref, acc_ref, v_transposed=False):
    m_old = m_ref[...]
    m_new = jnp.maximum(m_old, jnp.max(s, axis=1, keepdims=True))
    p = jnp.exp2(s - m_new)
    alpha = jnp.exp2(m_old - m_new)
    l_ref[...] = alpha * l_ref[...] + jnp.sum(p, axis=1, keepdims=True)
    pv = _nt(p.astype(BF16), v) if v_transposed else jnp.dot(p.astype(BF16), v, preferred_element_type=F32)
    acc_ref[...] = alpha * acc_ref[...] + pv
    m_ref[...] = m_new


def _score_matrix(n_rows, row_offset, n_blocks):
    c = CMP_BLOCK // CMP_STRIDE
    ratio = SEL_BLOCK // CMP_STRIDE
    n_ov = ratio + c - 1
    m = np.zeros((n_blocks, n_rows), np.float32)
    for j in range(n_blocks):
        for u in range(n_ov):
            start = CMP_STRIDE * (u - (c - 1))
            w_u = (min(start + CMP_BLOCK, SEL_BLOCK) - max(start, 0)) / CMP_STRIDE
            n = ratio * j + u - (c - 1)
            if 0 <= n and n + row_offset < n_rows:
                m[j, n + row_offset] = w_u
    return m


def _nsa_prompt_kernel(qt_ref, gt_ref, kc_ref, vct_ref, ks_ref, vst_ref, kw_ref, vwt_ref,
                       mt_ref, cc_ref, cs_ref, bw_ref, eb_ref, o_ref,
                       sc_ref, sw_ref, sel_ref, m_ref, acc_ref, oc_ref,
                       sa_ref, sb_ref, pa_ref, pb_ref, ala_ref, alb_ref, *, n_blocks, variants):
    qb = pl.program_id(0)
    nrow = kc_ref.shape[0]
    gq = GROUP * Q_BLOCK
    tiles_per_q = SEL_TILE // Q_BLOCK
    nband = WINDOW + Q_BLOCK
    max_tile = n_blocks // BLK_PER_TILE - 1
    frow = lax.broadcasted_iota(jnp.int32, (KV_DIM, Q_BLOCK), 0)

    def lanes4(x):
        return jnp.concatenate([x] * GROUP, axis=1)

    qx = []
    for kv in range(N_KV):
        keep = (frow >= HEAD_DIM) if kv else (frow < HEAD_DIM)
        qx.append(jnp.concatenate(
            [jnp.where(keep, qt_ref[j * KV_DIM:(j + 1) * KV_DIM, :], jnp.zeros((), BF16))
             for j in range(GROUP)], axis=1))

    def qk_stage(kt, s_ref, penalty=0.0):
        k = ks_ref[pl.ds(pl.multiple_of(kt * SEL_TILE, SEL_TILE), SEL_TILE), :]
        k_aug = jnp.concatenate([k, eb_ref[...]], axis=1)
        for kv in range(N_KV):
            srow = sel_ref[kv, pl.ds(pl.multiple_of(kt * BLK_PER_TILE, BLK_PER_TILE), BLK_PER_TILE), :]
            mrows = jnp.concatenate([lanes4(srow + penalty), jnp.zeros((KV_DIM - BLK_PER_TILE, gq), F32)], axis=0)
            q_aug = jnp.concatenate([qx[kv], mrows.astype(BF16)], axis=0)
            s_ref[kv] = jnp.dot(k_aug, q_aug, preferred_element_type=F32)

    def gate(kv, i):
        return jnp.concatenate(
            [gt_ref[3 * (kv * GROUP + j) + i:3 * (kv * GROUP + j) + i + 1, :] for j in range(GROUP)], axis=1)

    def compressed_and_select(n_r, n_b):
        for kv in range(N_KV):
            sc_ref[kv, 0:n_r, :] = jnp.dot(kc_ref[0:n_r, :], qx[kv], preferred_element_type=F32)
        r = lax.broadcasted_iota(jnp.int32, (n_r, Q_BLOCK), 0)
        qpos_c = qb * Q_BLOCK + lax.broadcasted_iota(jnp.int32, (n_r, Q_BLOCK), 1)
        end_pos = (r - CMP_PAD) * CMP_STRIDE + (CMP_BLOCK - 1)
        vis = (r >= CMP_PAD) & (r < CMP_PAD + n_blocks * (SEL_BLOCK // CMP_STRIDE) - 1) & (end_pos <= qpos_c)
        vis_add = lanes4(jnp.where(vis, 0.0, NEG))
        band = pl.ds(pl.multiple_of(qb * SUBLANES, SUBLANES), CMP_BAND)
        blk_i = lax.broadcasted_iota(jnp.int32, (n_b, Q_BLOCK), 0)
        cur = (qb * Q_BLOCK + lax.broadcasted_iota(jnp.int32, (n_b, Q_BLOCK), 1)) // SEL_BLOCK
        forced = (blk_i == 0) | (blk_i == cur) | (blk_i == cur - 1)
        scores = []
        for kv in range(N_KV):
            sc_ref[kv, band, :] = sc_ref[kv, band, :] + cc_ref[kv]
            s = sc_ref[kv, 0:n_r, :] + vis_add
            m = jnp.max(s, axis=0, keepdims=True)
            p = jnp.exp2(s - m)
            l = jnp.sum(p, axis=0, keepdims=True)
            pn = p * jnp.where(m > 0.5 * NEG, 1.0 / l, 0.0)
            oc_ref[kv] = gate(kv, 0) * jnp.dot(vct_ref[:, 0:n_r], pn.astype(BF16), preferred_element_type=F32)
            imp = pn[:, 0:Q_BLOCK]
            for j in range(1, GROUP):
                imp = imp + pn[:, j * Q_BLOCK:(j + 1) * Q_BLOCK]
            score = _split_dot(mt_ref[0:n_b, 0:n_r], imp)
            scores.append(jnp.where(blk_i <= cur, score, -FORCE_BONUS))
        kband = kw_ref[pl.ds(pl.multiple_of(qb * Q_BLOCK, Q_BLOCK), nband), :]
        for kv in range(N_KV):
            sw_ref[kv] = jnp.dot(kband, qx[kv], preferred_element_type=F32)
        blk2 = jnp.concatenate([blk_i.astype(F32)] * N_KV, axis=1)
        forced2 = jnp.concatenate([forced] * N_KV, axis=1)
        sel = _select_topk(jnp.concatenate(scores, axis=1), blk2, N_SEL - 3, always=forced2)
        for kv in range(N_KV):
            sel_ref[kv, 0:n_b, :] = (sel[:, kv * Q_BLOCK:(kv + 1) * Q_BLOCK] - 1.0) * (-NEG)

    sel_ref[...] = jnp.full_like(sel_ref, NEG)
    lo = 0
    for q_hi, n_r, n_b in variants:
        @pl.when((qb >= lo) & (qb <= q_hi))
        def _(n_r=n_r, n_b=n_b):
            compressed_and_select(n_r, n_b)
        lo = q_hi + 1

    m_ref[...] = jnp.full_like(m_ref, NEG)
    acc_ref[...] = jnp.zeros_like(acc_ref)

    def pv_stage(kt, p_ref, al_ref):
        for kv in range(N_KV):
            acc_ref[kv] = al_ref[kv] * acc_ref[kv] + jnp.dot(vst_ref[kv, kt], p_ref[kv],
                                                             preferred_element_type=F32)

    def sm_stage(kt, s_ref, p_ref, al_ref, near):
        d0 = qb - kt * tiles_per_q
        for kv in range(N_KV):
            s = s_ref[kv]
            if near:
                s = s + jnp.concatenate(
                    [cs_ref[kv, jnp.clip(d0 - i, -1, 2) + 1] for i in range(tiles_per_q)], axis=0)
            m_old = m_ref[kv]
            m_new = jnp.maximum(m_old, jnp.max(s, axis=0, keepdims=True))
            p_ref[kv] = jnp.exp2(s - m_new).astype(BF16)
            al_ref[kv] = jnp.exp2(m_old - m_new)
            m_ref[kv] = m_new

    n_far = jnp.maximum(qb - 1, 0) // tiles_per_q
    pb_ref[...] = jnp.zeros_like(pb_ref)
    alb_ref[...] = jnp.ones_like(alb_ref)
    qk_stage(0, sa_ref)

    def window(kv):
        vband = jnp.concatenate([vwt_ref[qb + i] for i in range(nband // Q_BLOCK)], axis=1)
        krow = lax.broadcasted_iota(jnp.int32, (nband, Q_BLOCK), 0)
        pos_add = lanes4(jnp.where(krow >= WINDOW - qb * Q_BLOCK, 0.0, NEG))
        s = sw_ref[kv] + bw_ref[kv] + pos_add
        m = jnp.max(s, axis=0, keepdims=True)
        p = jnp.exp2(s - m)
        o_w = (jnp.dot(vband, p.astype(BF16), preferred_element_type=F32)
               / jnp.sum(p, axis=0, keepdims=True))
        oc_ref[kv] = oc_ref[kv] + gate(kv, 2) * o_w

    window(0)

    def pair_body(u, c):
        t0 = 2 * u
        t1 = jnp.minimum(t0 + 1, max_tile)
        t2 = jnp.minimum(t0 + 2, max_tile)
        qk_stage(t1, sb_ref, jnp.where(t0 + 1 < n_far, 0.0, NEG))
        pv_stage(jnp.maximum(t0 - 1, 0), pb_ref, alb_ref)
        sm_stage(t0, sa_ref, pa_ref, ala_ref, False)
        qk_stage(t2, sa_ref)
        pv_stage(t0, pa_ref, ala_ref)
        sm_stage(t1, sb_ref, pb_ref, alb_ref, False)
        return c
    n_pairs = (n_far + 1) // 2
    lax.fori_loop(0, n_pairs, pair_body, 0)

    near_a = n_far
    near_b = jnp.minimum(n_far + 1, max_tile)
    qk_stage(near_a, sa_ref)
    pv_stage(jnp.clip(2 * n_pairs - 1, 0, max_tile), pb_ref, alb_ref)
    window(1)
    sm_stage(near_a, sa_ref, pa_ref, ala_ref, True)
    pv_stage(near_a, pa_ref, ala_ref)

    @pl.when(n_far + 1 <= qb // tiles_per_q)
    def _():
        qk_stage(near_b, sb_ref)
        sm_stage(near_b, sb_ref, pb_ref, alb_ref, True)
        pv_stage(near_b, pb_ref, alb_ref)

    o_kv = []
    for kv in range(N_KV):
        acc = acc_ref[kv]
        o_s = acc[0:HEAD_DIM, :] / acc[HEAD_DIM:HEAD_DIM + 1, :]
        o_kv.append(oc_ref[kv, kv * HEAD_DIM:(kv + 1) * HEAD_DIM, :] + gate(kv, 1) * o_s)
    o_t = jnp.concatenate(o_kv, axis=0)
    for j in range(GROUP):
        o_ref[:, j * LANES:(j + 1) * LANES] = o_t[:, j * Q_BLOCK:(j + 1) * Q_BLOCK].T.astype(BF16)


def _nsa_prompt(qat, gat, kc, vct, ksb, vst3, kwb, vwt3, mt, cc, cs, bw, eb):
    t = qat.shape[1]
    n_blocks = t // SEL_BLOCK
    nrow = kc.shape[0]
    gq = GROUP * Q_BLOCK
    n_q = t // Q_BLOCK
    variants = []
    for n_r in list(range(2 * LANES, nrow - LANES, 2 * LANES)) + [nrow]:
        q_hi = n_q - 1 if n_r == nrow else min((n_r - CMP_BAND) // SUBLANES, n_q - 1)
        n_b = min(-(-(2 * q_hi + 2) // (2 * SUBLANES)) * (2 * SUBLANES), n_blocks)
        variants.append((q_hi, n_r, n_b))
    return pl.pallas_call(
        functools.partial(_nsa_prompt_kernel, n_blocks=n_blocks, variants=tuple(variants)),
        grid=(t // Q_BLOCK,),
        in_specs=[pl.BlockSpec((Q_DIM, Q_BLOCK), lambda i: (0, i)),
                  pl.BlockSpec((LANES, Q_BLOCK), lambda i: (0, i)),
                  _const_spec(kc.shape), _const_spec(vct.shape),
                  _const_spec(ksb.shape), _const_spec(vst3.shape),
                  _const_spec(kwb.shape), _const_spec(vwt3.shape),
                  _const_spec(mt.shape), _const_spec(cc.shape),
                  _const_spec(cs.shape), _const_spec(bw.shape), _const_spec(eb.shape)],
        out_specs=pl.BlockSpec((Q_BLOCK, Q_DIM), lambda i: (i, 0)),
        out_shape=jax.ShapeDtypeStruct((t, Q_DIM), BF16),
        scratch_shapes=[pltpu.VMEM((N_KV, nrow, gq), F32),
                        pltpu.VMEM((N_KV, WINDOW + Q_BLOCK, gq), F32),
                        pltpu.VMEM((N_KV, n_blocks, Q_BLOCK), F32),
                        pltpu.VMEM((N_KV, 1, gq), F32),
                        pltpu.VMEM((N_KV, vst3.shape[2], gq), F32),
                        pltpu.VMEM((N_KV, KV_DIM, gq), F32),
                        pltpu.VMEM((N_KV, SEL_TILE, gq), F32), pltpu.VMEM((N_KV, SEL_TILE, gq), F32),
                        pltpu.VMEM((N_KV, SEL_TILE, gq), BF16), pltpu.VMEM((N_KV, SEL_TILE, gq), BF16),
                        pltpu.VMEM((N_KV, 1, gq), F32), pltpu.VMEM((N_KV, 1, gq), F32)],
        compiler_params=_cparams(("arbitrary",)),
        name="nsa_prompt",
    )(qat, gat, kc, vct, ksb, vst3, kwb, vwt3, mt, cc, cs, bw, eb)


S_ROWS = 128
S_CHUNK_PAGES = 128
S_CHUNK = S_CHUNK_PAGES * PAGE


def _nsa_sample_kernel(pt_ref, qx_ref, gm_ref, kc_ref, vct_ref, kpool_ref, vpool_ref,
                       knew_ref, vnew_ref, kwin_ref, vwin_ref, kwnew_ref, vwnew_ref,
                       mt_ref, gsum_ref, ccs_ref, css_ref, cns_ref, cws_ref, o_ref,
                       kbuf, vbuf, sem, mask_ref, m_ref, l_ref, acc_ref, oc_ref, ow_ref,
                       *, n_chunks, n_blocks):
    b = pl.program_id(0)
    c = pl.program_id(1)
    step = b * n_chunks + c
    total = pl.num_programs(0) * n_chunks
    slot = step % 2

    def copies(bb, cc, s, p):
        pg = cc * S_CHUNK_PAGES + p
        dst = pl.ds(pl.multiple_of(p * PAGE, PAGE), PAGE)
        return (pltpu.make_async_copy(kpool_ref.at[pt_ref[bb, pg]], kbuf.at[s, :, dst], sem.at[0, s]),
                pltpu.make_async_copy(vpool_ref.at[pt_ref[bb, pg]], vbuf.at[s, :, dst], sem.at[1, s]))

    def start_all(st, s):
        bb = st // n_chunks
        cc = st % n_chunks

        def body(p, carry):
            ck, cv = copies(bb, cc, s, p)
            ck.start()
            cv.start()
            return carry
        lax.fori_loop(0, S_CHUNK_PAGES, body, 0)

    @pl.when(step == 0)
    def _():
        start_all(step, slot)

    @pl.when(step + 1 < total)
    def _():
        start_all(step + 1, 1 - slot)

    qx = qx_ref[0]
    n_q = o_ref.shape[1]
    qq = qx[:n_q]

    @pl.when(c == 0)
    def _():
        nrow = kc_ref.shape[1]
        s = _nt(kc_ref[0], qx)
        r = lax.broadcasted_iota(jnp.int32, (nrow, S_ROWS), 0)
        band0 = nrow - CMP_BAND
        s = s + jnp.concatenate([jnp.zeros((band0, S_ROWS), F32), ccs_ref[...]], axis=0)
        pn = _softmax_cols(s, r < nrow - 1)
        oc_ref[...] = jnp.dot(vct_ref[0], pn.astype(BF16), preferred_element_type=F32).T[:n_q]
        imp = jnp.dot(pn, gsum_ref[...], preferred_element_type=F32, precision=HIGHEST)
        score = _split_dot(mt_ref[...], imp)
        blk_i = lax.broadcasted_iota(jnp.int32, (n_blocks, S_ROWS), 0)
        forced = (blk_i == 0) | (blk_i == n_blocks - 1)
        selt = _select_topk(score, blk_i.astype(F32), N_SEL - 3, always=forced)
        selt = lax.dot_general(selt.astype(BF16), gsum_ref[...].astype(BF16), (((1,), (1,)), ((), ())),
                               preferred_element_type=F32)
        selm1 = selt.T[:n_q] - 1.0
        e_r = lax.broadcasted_iota(jnp.int32, (BLK_PER_TILE, SEL_TILE), 0)
        e_c = lax.broadcasted_iota(jnp.int32, (BLK_PER_TILE, SEL_TILE), 1)
        expand = jnp.where(e_c // SEL_BLOCK == e_r, -NEG, 0.0)
        for kt in range(n_blocks // BLK_PER_TILE):
            mask_ref[:, kt * SEL_TILE:(kt + 1) * SEL_TILE] = jnp.dot(
                selm1[:, kt * BLK_PER_TILE:(kt + 1) * BLK_PER_TILE], expand, preferred_element_type=F32)

        sw = _nt(qq, kwin_ref[0].astype(BF16)) + cws_ref[:n_q, :]
        sn = _nt(qq, kwnew_ref[0].astype(BF16)) + cns_ref[:n_q, :]
        m = jnp.maximum(jnp.max(sw, axis=1, keepdims=True), jnp.max(sn, axis=1, keepdims=True))
        pw = jnp.exp2(sw - m)
        pn2 = jnp.exp2(sn - m)
        l = jnp.sum(pw, axis=1, keepdims=True) + jnp.sum(pn2, axis=1, keepdims=True)
        ow = (jnp.dot(pw.astype(BF16), vwin_ref[0].astype(BF16), preferred_element_type=F32)
              + jnp.dot(pn2.astype(BF16), vwnew_ref[0].astype(BF16), preferred_element_type=F32))
        ow_ref[...] = ow / l

        m_ref[...] = jnp.full_like(m_ref, NEG)
        l_ref[...] = jnp.zeros_like(l_ref)
        acc_ref[...] = jnp.zeros_like(acc_ref)

    pltpu.make_async_copy(kbuf.at[slot], kbuf.at[slot], sem.at[0, slot]).wait()
    pltpu.make_async_copy(vbuf.at[slot], vbuf.at[slot], sem.at[1, slot]).wait()

    col0 = pl.multiple_of(c * S_CHUNK, S_CHUNK)
    s = (jnp.dot(qq, kbuf[slot].astype(BF16), preferred_element_type=F32)
         + mask_ref[:, pl.ds(col0, S_CHUNK)])

    @pl.when(c < n_chunks - 1)
    def _():
        _online_update(s, vbuf[slot].astype(BF16), m_ref, l_ref, acc_ref, v_transposed=True)

    @pl.when(c == n_chunks - 1)
    def _():
        near = jnp.concatenate([jnp.zeros((n_q, S_CHUNK - LANES), F32), css_ref[:n_q, :]], axis=1)
        _online_update(s + near, vbuf[slot].astype(BF16), m_ref, l_ref, acc_ref, v_transposed=True)
        sn = _nt(qq, knew_ref[0].astype(BF16)) + cns_ref[:n_q, :]
        _online_update(sn, vnew_ref[0].astype(BF16), m_ref, l_ref, acc_ref)
        o_s = acc_ref[...] / l_ref[...]
        o_ref[0] = gm_ref[0, 0] * oc_ref[...] + gm_ref[0, 1] * o_s + gm_ref[0, 2] * ow_ref[...]


def _nsa_sample(page_table, qx, gm, kc, vct, kpool, vpool, knew, vnew, kwin, vwin, kwnew, vwnew,
                mt, gsum, ccs, css, cns, cws):
    nb = qx.shape[0]
    n_pages = page_table.shape[1]
    n_chunks = n_pages // S_CHUNK_PAGES
    past = n_pages * PAGE
    n_blocks = past // SEL_BLOCK
    n_q = gm.shape[2]

    def bspec(shape):
        nd = len(shape)
        return pl.BlockSpec((1,) + tuple(shape[1:]), lambda b, c, pt: (b,) + (0,) * (nd - 1))

    def cspec(shape):
        nd = len(shape)
        return pl.BlockSpec(tuple(shape), lambda b, c, pt: (0,) * nd, pipeline_mode=pl.Buffered(1))

    grid_spec = pltpu.PrefetchScalarGridSpec(
        num_scalar_prefetch=1,
        grid=(nb, n_chunks),
        in_specs=[bspec(qx.shape), bspec(gm.shape), bspec(kc.shape), bspec(vct.shape),
                  pl.BlockSpec(memory_space=pl.ANY), pl.BlockSpec(memory_space=pl.ANY),
                  bspec(knew.shape), bspec(vnew.shape), bspec(kwin.shape), bspec(vwin.shape),
                  bspec(kwnew.shape), bspec(vwnew.shape),
                  cspec(mt.shape), cspec(gsum.shape), cspec(ccs.shape), cspec(css.shape),
                  cspec(cns.shape), cspec(cws.shape)],
        out_specs=pl.BlockSpec((1, n_q, KV_DIM), lambda b, c, pt: (b, 0, 0)),
        scratch_shapes=[pltpu.VMEM((2, KV_DIM, S_CHUNK), F32),
                        pltpu.VMEM((2, KV_DIM, S_CHUNK), F32),
                        pltpu.SemaphoreType.DMA((2, 2)),
                        pltpu.VMEM((n_q, past), F32),
                        pltpu.VMEM((n_q, 1), F32), pltpu.VMEM((n_q, 1), F32),
                        pltpu.VMEM((n_q, KV_DIM), F32),
                        pltpu.VMEM((n_q, KV_DIM), F32), pltpu.VMEM((n_q, KV_DIM), F32)],
    )
    return pl.pallas_call(
        functools.partial(_nsa_sample_kernel, n_chunks=n_chunks, n_blocks=n_blocks),
        grid_spec=grid_spec,
        out_shape=jax.ShapeDtypeStruct((nb, n_q, KV_DIM), F32),
        compiler_params=_cparams(("arbitrary", "arbitrary")),
        name="nsa_sample",
    )(page_table, qx, gm, kc, vct, kpool, vpool, knew, vnew, kwin, vwin, kwnew, vwnew,
      mt, gsum, ccs, css, cns, cws)


def _ffn_kernel(x_ref, oa_ref, ob_ref, sa_ref, sb_ref, wpa_ref, wpb_ref, wo_ref, nf_ref,
                wg_ref, wu_ref, wd_ref, nl_ref, y_ref):
    pa = jnp.dot(oa_ref[...], wpa_ref[...], preferred_element_type=F32)
    pb = jnp.dot(ob_ref[...], wpb_ref[...], preferred_element_type=F32)
    merged = sa_ref[...] * pa + sb_ref[...] * pb
    x = x_ref[...] + jnp.dot(merged.astype(BF16), wo_ref[...], preferred_element_type=F32)
    hn = (x * lax.rsqrt(jnp.mean(x * x, axis=-1, keepdims=True) + EPS) * nf_ref[...]).astype(BF16)
    gate = jnp.dot(hn, wg_ref[...], preferred_element_type=F32)
    up = jnp.dot(hn, wu_ref[...], preferred_element_type=F32)
    ff = (jax.nn.silu(gate) * up).astype(BF16)
    x = x + jnp.dot(ff, wd_ref[...], preferred_element_type=F32)
    y_ref[...] = x * lax.rsqrt(jnp.mean(x * x, axis=-1, keepdims=True) + EPS) * nl_ref[...]


def _ffn(x2d, oa, ob, sa, sb, wpa, wpb, wo, nf, wg, wu, wd, nl, tm):
    rows = x2d.shape[0]

    def rspec(n):
        return pl.BlockSpec((tm, n), lambda i: (i, 0))

    return pl.pallas_call(
        _ffn_kernel,
        grid=(rows // tm,),
        in_specs=[rspec(D_MODEL), rspec(Q_DIM), rspec(HG_W), rspec(D_MODEL), rspec(D_MODEL),
                  _const_spec(wpa.shape), _const_spec(wpb.shape), _const_spec(wo.shape),
                  _const_spec(nf.shape), _const_spec(wg.shape), _const_spec(wu.shape),
                  _const_spec(wd.shape), _const_spec(nl.shape)],
        out_specs=rspec(D_MODEL),
        out_shape=jax.ShapeDtypeStruct((rows, D_MODEL), F32),
        compiler_params=_cparams(("arbitrary",)),
        name="ffn",
    )(x2d, oa, ob, sa, sb, wpa, wpb, wo, nf, wg, wu, wd, nl)


def _pack_w_in(w_in):
    sizes = (Q_DIM,) + (KV_DIM,) * 6 + (3 * N_HEADS,) + (HG_W,) * 4 + (D_MODEL,) * 2
    offs = np.concatenate([[0], np.cumsum(sizes)])
    q = w_in[:, offs[0]:offs[1]].reshape(D_MODEL, N_HEADS, HEAD_DIM)[:, _HEAD_PERM, :].reshape(D_MODEL, Q_DIM)
    g = jnp.pad(w_in[:, offs[7]:offs[8]], ((0, 0), (0, LANES - 3 * N_HEADS)))
    return jnp.concatenate([q, w_in[:, offs[1]:offs[7]], g, w_in[:, offs[8]:]], axis=1).astype(BF16)


def _strip(bvc, rel, lo=0, hi=None, masked=NEG):
    val = bvc[:, np.clip(rel, 0, 255)]
    ok = rel >= lo
    if hi is not None:
        ok = ok & (rel < hi)
    return jnp.where(jnp.asarray(ok)[None], val, masked)


def _toeplitz(bvc, a, n_rows, n_cols, lo=0, hi=None):
    n = n_rows + n_cols - 1
    u = _strip(bvc, a - (n_rows - 1) + np.arange(n), lo, hi)
    u = jnp.pad(u, ((0, 0), (0, 1)))
    circ = jnp.tile(u, (1, n_rows))[:, :n_rows * n].reshape(N_HEADS, n_rows, n)
    return circ[:, :, n_rows - 1:n_rows - 1 + n_cols]


def _bias_strips_prompt(bvc):
    gq = GROUP * Q_BLOCK

    def lanes(x):
        return x.reshape(N_KV, GROUP, x.shape[1], Q_BLOCK).transpose(0, 2, 1, 3).reshape(N_KV, x.shape[1], gq)

    cs = jnp.stack([lanes(_toeplitz(bvc, Q_BLOCK * d, Q_BLOCK, Q_BLOCK)) for d in (-1, 0, 1, 2)], axis=1)
    bw = lanes(jnp.concatenate(
        [_toeplitz(bvc, Q_BLOCK * d, Q_BLOCK, Q_BLOCK, 0, WINDOW) for d in range(WINDOW // Q_BLOCK, -1, -1)], axis=1))
    rr = np.arange(CMP_BAND)[:, None]
    rel_c = np.arange(Q_BLOCK)[None, :] - CMP_STRIDE * (rr - CMP_PAD) - (CMP_BLOCK - 1)
    cc = _strip(bvc, rel_c, masked=0.0).reshape(N_KV, GROUP, CMP_BAND, Q_BLOCK)
    cc = cc.transpose(0, 2, 1, 3).reshape(N_KV, CMP_BAND, gq)
    return cc, cs, bw


def _bias_strips_sample(bvc, past, t):
    def rows(a):
        a = a.reshape(N_HEADS * t, a.shape[-1])
        return jnp.pad(a, ((0, S_ROWS - N_HEADS * t), (0, 0)))
    tt = np.arange(t)[:, None]
    nrow = past // CMP_STRIDE
    n = (nrow - CMP_BAND + np.arange(CMP_BAND))[None, :]
    ccs = rows(_strip(bvc, past + tt - CMP_STRIDE * n - (CMP_BLOCK - 1), masked=0.0)).T
    i = np.arange(LANES)[None, :]
    css = rows(_strip(bvc, LANES + tt - i))
    cns = rows(_strip(bvc, np.where(i < t, tt - i, -1)))
    iw = np.arange(WINDOW)[None, :]
    cws = rows(_strip(bvc, WINDOW + tt - iw, 0, WINDOW))
    return ccs, css, cns, cws


def kernel(x_prompt, x_sample, cache_k_cmp, cache_v_cmp, cache_k_slc, cache_v_slc, state_k_win, state_v_win,
           state_hgrn, page_table, norm_mix, w_in, cmp_pe_k, cmp_w1_k, cmp_w2_k, cmp_pe_v, cmp_w1_v, cmp_w2_v,
           rel_bias, hg_lb_logits, hg_norm, w_proj_a, w_proj_b, w_out, norm_ffn, w_gate, w_up, w_down, norm_final):
    nbp, t_p, _ = x_prompt.shape
    nbs, t_s, _ = x_sample.shape
    assert nbp == 1 and norm_mix.shape[0] == 1
    n_pages = page_table.shape[1]
    past = n_pages * PAGE
    assert state_k_win.shape[2] == WINDOW and past % S_CHUNK == 0 and t_s <= SUBLANES

    lb = jnp.cumsum(jax.nn.softmax(hg_lb_logits.astype(F32), axis=0), axis=0)[0]
    lb3 = jnp.pad(jnp.stack([jnp.log(lb), jnp.log1p(-lb), 1.0 - lb]), ((0, SUBLANES - 3), (0, 0)))
    w_pack = _pack_w_in(w_in[0])
    g_mix = norm_mix[0][None, :]
    wpa = w_proj_a[0].reshape(N_HEADS, HEAD_DIM, D_MODEL)[_HEAD_PERM].reshape(Q_DIM, D_MODEL).astype(BF16)
    wpb = w_proj_b[0].astype(BF16)
    wo = w_out[0].astype(BF16)
    wg, wu, wd = w_gate[0].astype(BF16), w_up[0].astype(BF16), w_down[0].astype(BF16)
    nf, nl = norm_ffn[0][None, :], norm_final[None, :]
    gn = hg_norm[0][None, :]
    pe_k, w1_k, w2_k = _compress_weights(cmp_pe_k[0], cmp_w1_k[0], cmp_w2_k[0])
    pe_v, w1_v, w2_v = _compress_weights(cmp_pe_v[0], cmp_w1_v[0], cmp_w2_v[0])
    bvc = (rel_bias[_BUCKET] - rel_bias[N_BUCKETS - 1][None, :]).T * LOG2E
    cc, cs, bw = _bias_strips_prompt(bvc)
    ccs, css, cns, cws = _bias_strips_sample(bvc, past, t_s)

    xp2 = x_prompt.reshape(t_p, D_MODEL)
    xs2 = x_sample.reshape(nbs * t_s, D_MODEL)
    seg = lambda off, n: w_pack[:, off:off + n]
    w_t = jnp.concatenate([seg(_OFF_Q, Q_DIM), seg(_OFF_G, LANES), seg(_OFF_KV + 3 * KV_DIM, KV_DIM),
                           seg(_OFF_KV + 5 * KV_DIM, KV_DIM)], axis=1).T
    pp = _proj(xp2, g_mix, w_pack, w_t, lb3, PROJ_ROWS, True)
    ps = _proj(xs2, g_mix, w_pack, w_t, lb3, nbs * t_s, False)
    (kc_p, vc_p, ks_p, vs_p, kw_p, vw_p, ksb_p, kwb_p,
     qh_p, lf_p, kh_p, vh_p, gs_p, sa_p, sb_p, qat_p, gat_p, vst_p, vwt_p) = pp
    (qa_s, kc_s, vc_s, ks_s, vs_s, kw_s, vw_s, ga_s,
     qh_s, lf_s, kh_s, vh_s, gs_s, sa_s, sb_s) = ps

    ob_p, s_p = _hgrn_prompt(qh_p, kh_p, vh_p, lf_p, gs_p, gn)
    r3 = lambda a: a.reshape(nbs, t_s, a.shape[-1])
    ob_s, s_s = _hgrn_sample(r3(qh_s), r3(kh_s), r3(vh_s), r3(lf_s), r3(gs_s), gn, state_hgrn[0])

    ident = jnp.arange(t_p // PAGE, dtype=jnp.int32)[None, :]
    pool_rows = lambda a: a.reshape(-1, PAGE, KV_DIM)
    kcb_p, _ = _compress(pool_rows(kc_p), ident, pe_k, w1_k, w2_k, False)
    _, vct_p = _compress(pool_rows(vc_p), ident, pe_v, w1_v, w2_v, False)
    back = -(CMP_PAD + kcb_p.shape[1]) % LANES
    kcb_p = jnp.pad(kcb_p[0], ((CMP_PAD, back), (0, 0)))
    vct_p = jnp.pad(vct_p[0], ((0, 0), (CMP_PAD, back)))
    nrow_p = kcb_p.shape[0]
    mt_p = jnp.asarray(_score_matrix(nrow_p, CMP_PAD, t_p // SEL_BLOCK), BF16)
    kwb_pad = jnp.pad(kwb_p, ((WINDOW, 0), (0, 0)))
    vst3 = vst_p.reshape(KV_DIM, t_p // SEL_TILE, SEL_TILE).transpose(1, 0, 2)
    ones_rows = jnp.ones((t_p // SEL_TILE, 2 * SUBLANES, SEL_TILE), BF16)
    vst3 = jnp.stack([jnp.concatenate([vst3[:, kv * HEAD_DIM:(kv + 1) * HEAD_DIM, :], ones_rows], axis=1)
                      for kv in range(N_KV)])
    vwt3 = jnp.pad(vwt_p, ((0, 0), (WINDOW, 0))).reshape(KV_DIM, (t_p + WINDOW) // Q_BLOCK, Q_BLOCK)
    vwt3 = vwt3.transpose(1, 0, 2)
    eb = (np.arange(LANES)[None, :] == np.arange(SEL_TILE)[:, None] // SEL_BLOCK).astype(np.float32)
    oa_p = _nsa_prompt(qat_p, gat_p, kcb_p, vct_p, ksb_p, vst3, kwb_pad, vwt3, mt_p, cc, cs, bw,
                       jnp.asarray(eb, BF16))

    pool_t = lambda a: a.transpose(0, 2, 3, 1).reshape(-1, KV_DIM, PAGE)
    kcb_s, _ = _compress(pool_t(cache_k_cmp[0]), page_table, pe_k, w1_k, w2_k, True)
    _, vct_s = _compress(pool_t(cache_v_cmp[0]), page_table, pe_v, w1_v, w2_v, True)
    nq = N_HEADS * t_s
    qs4 = qa_s.reshape(nbs, t_s, GROUP, N_KV, HEAD_DIM).astype(F32)
    qx = jnp.einsum('btjkd,kq->bkjtqd', qs4, jnp.eye(N_KV, dtype=F32)).reshape(nbs, nq, KV_DIM)
    qx = jnp.pad(qx, ((0, 0), (0, S_ROWS - nq), (0, 0))).astype(BF16)
    g4 = ga_s[:, :3 * N_HEADS].reshape(nbs, t_s, N_KV, GROUP, 3)
    gm = jnp.transpose(g4, (0, 4, 2, 3, 1)).reshape(nbs, 3, nq, 1)
    gm = jnp.broadcast_to(gm, (nbs, 3, nq, KV_DIM))
    new_tile = lambda a: jnp.pad(a.reshape(nbs, t_s, KV_DIM), ((0, 0), (0, LANES - t_s), (0, 0)))
    mt_s = jnp.asarray(_score_matrix(past // CMP_STRIDE, 0, past // SEL_BLOCK), BF16)
    gsum = np.zeros((S_ROWS, S_ROWS), np.float32)
    for kv in range(N_KV):
        for j in range(GROUP):
            for t in range(t_s):
                gsum[(kv * GROUP + j) * t_s + t, kv * t_s + t] = 1.0
    o_kv = _nsa_sample(page_table, qx, gm, kcb_s, vct_s,
                       pool_t(cache_k_slc[0]), pool_t(cache_v_slc[0]),
                       new_tile(ks_s), new_tile(vs_s), state_k_win[0].reshape(nbs, WINDOW, KV_DIM),
                       state_v_win[0].reshape(nbs, WINDOW, KV_DIM), new_tile(kw_s), new_tile(vw_s),
                       mt_s, jnp.asarray(gsum), ccs, css, cns, cws)
    o5 = o_kv.reshape(nbs, N_KV, GROUP, t_s, N_KV, HEAD_DIM)
    oa_s = jnp.einsum('bkjtqd,kq->btjkd', o5, jnp.eye(N_KV, dtype=F32)).reshape(nbs * t_s, Q_DIM).astype(BF16)

    y_p = _ffn(xp2, oa_p, ob_p, sa_p, sb_p, wpa, wpb, wo, nf, wg, wu, wd, nl, FFN_ROWS)
    y_s = _ffn(xs2, oa_s, ob_s.reshape(nbs * t_s, HG_W), sa_s, sb_s, wpa, wpb, wo, nf, wg, wu, wd, nl, nbs * t_s)

    kv5 = lambda a, nb_, tt: a.reshape(1, nb_, tt, N_KV, HEAD_DIM)
    wl = min(WINDOW, t_p)
    win = lambda st, new: jnp.concatenate(
        [st[0], new.reshape(nbs, t_s, N_KV, HEAD_DIM)], axis=1)[:, -WINDOW:][None]
    return (y_p.reshape(1, t_p, D_MODEL), y_s.reshape(nbs, t_s, D_MODEL),
            kv5(kc_p, 1, t_p), kv5(vc_p, 1, t_p), kv5(ks_p, 1, t_p), kv5(vs_p, 1, t_p),
            kv5(kw_p[-wl:], 1, wl), kv5(vw_p[-wl:], 1, wl), s_p[None, None],
            kv5(kc_s, nbs, t_s), kv5(vc_s, nbs, t_s), kv5(ks_s, nbs, t_s), kv5(vs_s, nbs, t_s),
            win(state_k_win, kw_s), win(state_v_win, vw_s), s_s[None])
```

```python
import functools
import math

import numpy as np
import jax
import jax.numpy as jnp
from jax import lax
from jax.experimental import pallas as pl
from jax.experimental.pallas import tpu as pltpu

F32 = jnp.float32
BF16 = jnp.bfloat16
HIGHEST = lax.Precision.HIGHEST

D_MODEL = 1024
N_HEADS = 8
N_KV = 2
GROUP = N_HEADS // N_KV
HEAD_DIM = 64
KV_DIM = N_KV * HEAD_DIM
Q_DIM = N_HEADS * HEAD_DIM
CMP_BLOCK = 32
CMP_STRIDE = 16
CMP_HIDDEN = 2 * HEAD_DIM
SEL_BLOCK = 64
N_SEL = 16
WINDOW = 512
Q_BLOCK = 128
FORCE_BONUS = 1e4
N_BUCKETS = 32
MAX_DISTANCE = 128
HG_HEADS = 4
HG_DIM = 128
HG_CHUNK = 32
HG_SUB = 16
HG_W = HG_HEADS * HG_DIM
D_FF = ((8 * D_MODEL // 3 + 255) // 256) * 256
EPS = 1e-6
PAGE = 128
NEG = -1e30
LOG2E = math.log2(math.e)
Q_SCALE = HEAD_DIM ** -0.5 * LOG2E

LANES = 128
SUBLANES = 8
MXU_DIM = 256
VMEM_LIMIT = 56 * 1024 * 1024

PROJ_ROWS = 512
FFN_ROWS = 256
HGRN_ROWS = 512
CMP_ROWS = MXU_DIM

_OFF_Q = 0
_OFF_KV = _OFF_Q + Q_DIM
_OFF_G = _OFF_KV + 6 * KV_DIM
_OFF_HG = _OFF_G + LANES
_OFF_GATE = _OFF_HG + 4 * HG_W
_PROJ_N = _OFF_GATE + 2 * D_MODEL

_HEAD_PERM = np.array([h for j in range(GROUP) for h in (j, GROUP + j)])

SEL_TILE = 512
BLK_PER_TILE = SEL_TILE // SEL_BLOCK
CMP_PAD = 16
CMP_BAND = 24


def _cparams(sem, vmem=VMEM_LIMIT):
    return pltpu.CompilerParams(dimension_semantics=sem, vmem_limit_bytes=vmem)


def _const_spec(shape):
    nd = len(shape)
    return pl.BlockSpec(shape, lambda *_: (0,) * nd, pipeline_mode=pl.Buffered(1))


def _bucket_table():
    n = np.arange(256)
    max_exact = N_BUCKETS // 2
    nf = np.maximum(n, 1).astype(np.float64)
    large = max_exact + (np.log(nf / max_exact) / math.log(MAX_DISTANCE / max_exact)
                         * (N_BUCKETS - max_exact)).astype(np.int64)
    large = np.minimum(large, N_BUCKETS - 1)
    return np.where(n < max_exact, n, large)


_BUCKET = _bucket_table()


def _proj_kernel(x_ref, g_ref, w_ref, wt_ref, lb_ref, *out_refs, prompt):
    if prompt:
        (kc_ref, vc_ref, ks_ref, vs_ref, kw_ref, vw_ref, ksb_ref, kwb_ref,
         qh_ref, lf_ref, kh_ref, vh_ref, gs_ref, sa_ref, sb_ref,
         qat_ref, gat_ref, vst_ref, vwt_ref) = out_refs
    else:
        (qa_ref, kc_ref, vc_ref, ks_ref, vs_ref, kw_ref, vw_ref, ga_ref,
         qh_ref, lf_ref, kh_ref, vh_ref, gs_ref, sa_ref, sb_ref) = out_refs
    x = x_ref[...]
    xn = x * lax.rsqrt(jnp.mean(x * x, axis=-1, keepdims=True) + EPS) * g_ref[...]
    xb = xn.astype(BF16)

    def seg(a, n):
        return jnp.dot(xb, w_ref[:, a:a + n], preferred_element_type=F32)

    f32_refs = (kc_ref, vc_ref, ks_ref, vs_ref, kw_ref, vw_ref)
    for i in range(0, 6, 2):
        u = seg(_OFF_KV + i * KV_DIM, 2 * KV_DIM)
        f32_refs[i][...] = u[:, :KV_DIM]
        f32_refs[i + 1][...] = u[:, KV_DIM:]
        if prompt and i > 0:
            (ksb_ref if i == 2 else kwb_ref)[...] = u[:, :KV_DIM].astype(BF16)

    if prompt:
        def seg_t(a, n):
            return lax.dot_general(wt_ref[a:a + n, :], xb, (((1,), (1,)), ((), ())),
                                   preferred_element_type=F32)

        qat_ref[...] = (seg_t(0, Q_DIM) * Q_SCALE).astype(BF16)
        gat_ref[...] = jax.nn.sigmoid(seg_t(Q_DIM, LANES))
        vt = seg_t(Q_DIM + LANES, 2 * KV_DIM)
        vst_ref[...] = vt[:KV_DIM, :].astype(BF16)
        vwt_ref[...] = vt[KV_DIM:, :].astype(BF16)
    else:
        qa_ref[...] = (seg(_OFF_Q, Q_DIM) * Q_SCALE).astype(BF16)
        ga_ref[...] = jax.nn.sigmoid(seg(_OFF_G, LANES))

    log_lb = lb_ref[0:1, :]
    log_1m = lb_ref[1:2, :]
    one_m = lb_ref[2:3, :]
    qh_ref[...] = jax.nn.silu(seg(_OFF_HG, HG_W))
    z = seg(_OFF_HG + HG_W, HG_W)
    b = log_1m + (jnp.minimum(z, 0.0) - jnp.log1p(jnp.exp(-jnp.abs(z))))
    hi = jnp.maximum(log_lb, b)
    lf_ref[...] = hi + jnp.log1p(jnp.exp(-jnp.abs(log_lb - b)))
    kh_ref[...] = one_m * jax.nn.sigmoid(-z)
    vh_ref[...] = seg(_OFF_HG + 2 * HG_W, HG_W)
    gs_ref[...] = jax.nn.silu(seg(_OFF_HG + 3 * HG_W, HG_W))
    sa_ref[...] = jax.nn.sigmoid(seg(_OFF_GATE, D_MODEL))
    sb_ref[...] = jax.nn.sigmoid(seg(_OFF_GATE + D_MODEL, D_MODEL))


_PROJ_T = Q_DIM + LANES + 2 * KV_DIM


def _proj(x2d, g, w, wt, lb3, tm, prompt):
    rows = x2d.shape[0]
    tail = [(HG_W, F32)] * 5 + [(D_MODEL, F32)] * 2
    if prompt:
        widths = [(KV_DIM, F32)] * 6 + [(KV_DIM, BF16)] * 2 + tail
        heights = [(Q_DIM, BF16), (LANES, F32), (KV_DIM, BF16), (KV_DIM, BF16)]
    else:
        widths = [(Q_DIM, BF16)] + [(KV_DIM, F32)] * 6 + [(LANES, F32)] + tail
        heights = []
    return pl.pallas_call(
        functools.partial(_proj_kernel, prompt=prompt),
        grid=(rows // tm,),
        in_specs=[pl.BlockSpec((tm, D_MODEL), lambda i: (i, 0)),
                  _const_spec((1, D_MODEL)),
                  _const_spec((D_MODEL, _PROJ_N)),
                  _const_spec((_PROJ_T, D_MODEL)),
                  _const_spec((SUBLANES, HG_W))],
        out_specs=([pl.BlockSpec((tm, n), lambda i: (i, 0)) for n, _ in widths]
                   + [pl.BlockSpec((n, tm), lambda i: (0, i)) for n, _ in heights]),
        out_shape=([jax.ShapeDtypeStruct((rows, n), dt) for n, dt in widths]
                   + [jax.ShapeDtypeStruct((n, rows), dt) for n, dt in heights]),
        compiler_params=_cparams(("arbitrary",)),
        name="proj",
    )(x2d, g, w, wt, lb3)


_CH_W = CMP_STRIDE * KV_DIM
_CH_PER_PAGE = PAGE // CMP_STRIDE
_CH_PITCH = CMP_STRIDE + SUBLANES


def _compress_kernel(pt_ref, pool_ref, pe_ref, w1_ref, w2_ref, out_ref, outt_ref, buf, rbuf, xa, hbuf, sem,
                     *, n_pages, pages_transposed):
    b = pl.program_id(0)
    nb = pl.num_programs(0)
    slot = b % 2
    n_ch = n_pages * _CH_PER_PAGE

    def page_copy(bb, p, s):
        return pltpu.make_async_copy(pool_ref.at[pt_ref[bb, p]], buf.at[s, p], sem.at[s])

    def start_all(bb, s):
        def body(p, c):
            page_copy(bb, 2 * p, s).start(priority=0)
            page_copy(bb, 2 * p + 1, s).start(priority=1)
            return c
        lax.fori_loop(0, n_pages // 2, body, 0)

    @pl.when(b == 0)
    def _():
        start_all(b, slot)

    @pl.when(b + 1 < nb)
    def _():
        start_all(b + 1, 1 - slot)

    pltpu.make_async_copy(buf.at[slot], buf.at[slot], sem.at[slot]).wait()

    rows = math.gcd(n_ch, CMP_ROWS)
    pages_per_group = rows // _CH_PER_PAGE

    def to_rows(g):
        for p in range(g * pages_per_group, (g + 1) * pages_per_group):
            page = buf[slot, p]
            page = page.T if pages_transposed else page
            for i in range(_CH_PER_PAGE):
                dst = (p * _CH_PER_PAGE + i) * _CH_PITCH
                rbuf[dst:dst + CMP_STRIDE, :] = page[i * CMP_STRIDE:(i + 1) * CMP_STRIDE, :]

    to_rows(0)
    for r in range(n_ch // rows):
        if r + 1 < n_ch // rows:
            to_rows(r + 1)
        for s in range(CMP_STRIDE):
            x = rbuf[pl.ds(r * rows * _CH_PITCH + s, rows, stride=_CH_PITCH), :]
            xa[:, s * KV_DIM:(s + 1) * KV_DIM] = x.astype(BF16)
        hbuf[r * rows:(r + 1) * rows, :] = jnp.dot(xa[...], w1_ref[...], preferred_element_type=F32)
    pw = _split_dot_rhs(pe_ref[...], w1_ref[...])
    nh = N_KV * CMP_HIDDEN
    bias = pw[0:1, 0:nh] + pw[1:2, nh:2 * nh]
    h = hbuf[:, 0:nh] + pltpu.roll(hbuf[:, nh:2 * nh], n_ch - 1, 0) + bias
    blocks = jnp.dot(jax.nn.gelu(h).astype(BF16), w2_ref[...], preferred_element_type=F32)
    row = lax.broadcasted_iota(jnp.int32, blocks.shape, 0)
    blocks = jnp.where(row < n_ch - 1, blocks, 0.0)
    out_ref[0] = blocks.astype(BF16)
    outt_ref[0] = blocks.T.astype(BF16)


def _compress(pool, page_table, pe, w1, w2, pages_transposed):
    nbatch, n_pages = page_table.shape
    assert n_pages % 2 == 0
    n_ch = n_pages * _CH_PER_PAGE
    rows = math.gcd(n_ch, CMP_ROWS)
    grid_spec = pltpu.PrefetchScalarGridSpec(
        num_scalar_prefetch=1,
        grid=(nbatch,),
        in_specs=[pl.BlockSpec(memory_space=pl.ANY),
                  _const_spec((SUBLANES, _CH_W)),
                  _const_spec((_CH_W, 4 * CMP_HIDDEN)),
                  _const_spec((2 * CMP_HIDDEN, KV_DIM))],
        out_specs=[pl.BlockSpec((1, n_ch, KV_DIM), lambda b, pt: (b, 0, 0)),
                   pl.BlockSpec((1, KV_DIM, n_ch), lambda b, pt: (b, 0, 0))],
        scratch_shapes=[pltpu.VMEM((2, n_pages, PAGE, KV_DIM), F32),
                        pltpu.VMEM((n_ch * _CH_PITCH, KV_DIM), F32),
                        pltpu.VMEM((rows, _CH_W), BF16),
                        pltpu.VMEM((n_ch, 4 * CMP_HIDDEN), F32),
                        pltpu.SemaphoreType.DMA((2,))],
    )
    return pl.pallas_call(
        functools.partial(_compress_kernel, n_pages=n_pages, pages_transposed=pages_transposed),
        grid_spec=grid_spec,
        out_shape=[jax.ShapeDtypeStruct((nbatch, n_ch, KV_DIM), BF16),
                   jax.ShapeDtypeStruct((nbatch, KV_DIM, n_ch), BF16)],
        compiler_params=_cparams(("arbitrary",)),
        name="compress",
    )(page_table, pool, pe, w1, w2)


def _compress_weights(pe, w1, w2):
    c = CMP_BLOCK // CMP_STRIDE
    pe_r = pe.reshape(c, CMP_STRIDE, 1, HEAD_DIM)
    pe_x = jnp.broadcast_to(pe_r, (c, CMP_STRIDE, N_KV, HEAD_DIM)).reshape(c, _CH_W)
    pe_x = jnp.pad(pe_x, ((0, SUBLANES - c), (0, 0)))
    w1_r = w1.reshape(c, CMP_STRIDE, HEAD_DIM, CMP_HIDDEN)
    eye = jnp.eye(N_KV, dtype=w1.dtype)
    w1_x = jnp.einsum('jsde,kq->skdjqe', w1_r, eye).reshape(_CH_W, c * N_KV * CMP_HIDDEN)
    w2_x = jnp.einsum('ed,kq->keqd', w2, eye).reshape(N_KV * CMP_HIDDEN, KV_DIM)
    return pe_x, w1_x.astype(BF16), w2_x.astype(BF16)


def _hgrn_chunk(q, k, v, lf, st_ref, chunk, sub):
    if chunk > SUBLANES:
        r = lax.broadcasted_iota(jnp.int32, (chunk, chunk), 0)
        c = lax.broadcasted_iota(jnp.int32, (chunk, chunk), 1)
        tri = (r >= c).astype(F32)
        b = jnp.dot(tri, lf, preferred_element_type=F32, precision=HIGHEST)
    else:
        rows = [lf[0:1, :]]
        for t in range(1, chunk):
            rows.append(rows[-1] + lf[t:t + 1, :])
        b = jnp.concatenate(rows, axis=0)
    bl = b[chunk - 1:chunk, :]
    qe = q * jnp.exp(b)
    kd = k * jnp.exp(bl - b)
    ebl = jnp.exp(bl)
    n_sub = chunk // sub
    trow = lax.broadcasted_iota(jnp.int32, (sub, HG_W), 0)
    crow = lax.broadcasted_iota(jnp.int32, (chunk, HG_W), 0)

    diag = []
    for i in range(n_sub):
        qi = q[i * sub:(i + 1) * sub, :]
        bi = b[i * sub:(i + 1) * sub, :]
        acc = [jnp.zeros((sub, HG_DIM), F32) for _ in range(HG_HEADS)]
        for s in range(sub):
            row = i * sub + s
            dec = jnp.exp(jnp.where(trow >= s, bi - b[row:row + 1, :], -jnp.inf))
            prod = qi * k[row:row + 1, :] * dec
            for h in range(HG_HEADS):
                a = jnp.sum(prod[:, h * HG_DIM:(h + 1) * HG_DIM], axis=1, keepdims=True)
                acc[h] = acc[h] + a * v[row:row + 1, h * HG_DIM:(h + 1) * HG_DIM]
        diag.append(acc)

    off = []
    for i in range(n_sub):
        if i == 0:
            off.append(None)
            continue
        b0 = b[i * sub - 1:i * sub, :]
        qs = (q[i * sub:(i + 1) * sub, :] * jnp.exp(b[i * sub:(i + 1) * sub, :] - b0)).astype(BF16)
        ks = (k * jnp.exp(jnp.where(crow < i * sub, b0 - b, -jnp.inf))).astype(BF16)
        off.append((qs, ks))

    vb = v.astype(BF16)
    outs = []
    for h in range(HG_HEADS):
        sl = slice(h * HG_DIM, (h + 1) * HG_DIM)
        st = st_ref[h]
        o_h = lax.dot_general(qe[:, sl].astype(BF16), st.astype(BF16), (((1,), (1,)), ((), ())),
                              preferred_element_type=F32)
        parts = []
        for i in range(n_sub):
            d = diag[i][h]
            if off[i] is not None:
                qs, ks = off[i]
                a = lax.dot_general(qs[:, sl], ks[:, sl], (((1,), (1,)), ((), ())),
                                    preferred_element_type=F32)
                d = d + jnp.dot(a.astype(BF16), vb[:, sl], preferred_element_type=F32)
            parts.append(d)
        intra = parts[0] if n_sub == 1 else jnp.concatenate(parts, axis=0)
        outs.append(o_h + intra)
        st_ref[h] = st * ebl[:, sl] + lax.dot_general(
            vb[:, sl], kd[:, sl].astype(BF16), (((0,), (0,)), ((), ())), preferred_element_type=F32)
    return jnp.concatenate(outs, axis=1)


def _hgrn_finish(o, gs, gn):
    outs = []
    for h in range(HG_HEADS):
        oh = o[:, h * HG_DIM:(h + 1) * HG_DIM]
        y = oh * lax.rsqrt(jnp.mean(oh * oh, axis=-1, keepdims=True) + EPS) * gn
        outs.append(y)
    return (jnp.concatenate(outs, axis=1) * gs).astype(BF16)


def _hgrn_prompt_kernel(q_ref, k_ref, v_ref, lf_ref, gs_ref, gn_ref, o_ref, s_ref, st_ref, *, n_chunks):
    i = pl.program_id(0)

    @pl.when(i == 0)
    def _():
        st_ref[...] = jnp.zeros_like(st_ref)

    def body(c, carry):
        r = pl.ds(pl.multiple_of(c * HG_CHUNK, HG_CHUNK), HG_CHUNK)
        o = _hgrn_chunk(q_ref[r, :], k_ref[r, :], v_ref[r, :], lf_ref[r, :], st_ref, HG_CHUNK, HG_SUB)
        o_ref[r, :] = _hgrn_finish(o, gs_ref[r, :], gn_ref[...])
        return carry
    lax.fori_loop(0, n_chunks, body, 0, unroll=8)

    @pl.when(i == pl.num_programs(0) - 1)
    def _():
        for h in range(HG_HEADS):
            s_ref[h] = st_ref[h].T


def _hgrn_prompt(qh, kh, vh, lf, gs, gn, rows_per_step=HGRN_ROWS):
    t = qh.shape[0]
    spec = pl.BlockSpec((rows_per_step, HG_W), lambda i: (i, 0))
    return pl.pallas_call(
        functools.partial(_hgrn_prompt_kernel, n_chunks=rows_per_step // HG_CHUNK),
        grid=(t // rows_per_step,),
        in_specs=[spec] * 5 + [_const_spec((1, HG_DIM))],
        out_specs=[spec, pl.BlockSpec((HG_HEADS, HG_DIM, HG_DIM), lambda i: (0, 0, 0))],
        out_shape=[jax.ShapeDtypeStruct((t, HG_W), BF16),
                   jax.ShapeDtypeStruct((HG_HEADS, HG_DIM, HG_DIM), F32)],
        scratch_shapes=[pltpu.VMEM((HG_HEADS, HG_DIM, HG_DIM), F32)],
        compiler_params=_cparams(("arbitrary",)),
        name="hgrn_prompt",
    )(qh, kh, vh, lf, gs, gn)


def _hgrn_sample_kernel(q_ref, k_ref, v_ref, lf_ref, gs_ref, gn_ref, s0_ref, o_ref, s_ref, st_ref, *, t):
    for h in range(HG_HEADS):
        st_ref[h] = s0_ref[0, h].T
    o = _hgrn_chunk(q_ref[0], k_ref[0], v_ref[0], lf_ref[0], st_ref, t, t)
    o_ref[0] = _hgrn_finish(o, gs_ref[0], gn_ref[...])
    for h in range(HG_HEADS):
        s_ref[0, h] = st_ref[h].T


def _hgrn_sample(qh, kh, vh, lf, gs, gn, s0):
    nb, t, _ = qh.shape
    spec = pl.BlockSpec((1, t, HG_W), lambda b: (b, 0, 0))
    sspec = pl.BlockSpec((1, HG_HEADS, HG_DIM, HG_DIM), lambda b: (b, 0, 0, 0))
    return pl.pallas_call(
        functools.partial(_hgrn_sample_kernel, t=t),
        grid=(nb,),
        in_specs=[spec] * 5 + [_const_spec((1, HG_DIM)), sspec],
        out_specs=[spec, sspec],
        out_shape=[jax.ShapeDtypeStruct((nb, t, HG_W), BF16),
                   jax.ShapeDtypeStruct((nb, HG_HEADS, HG_DIM, HG_DIM), F32)],
        scratch_shapes=[pltpu.VMEM((HG_HEADS, HG_DIM, HG_DIM), F32)],
        compiler_params=_cparams(("arbitrary",)),
        name="hgrn_sample",
    )(qh, kh, vh, lf, gs, gn, s0)


def _select_topk(x, blk, n, always=None):
    nblk = x.shape[0]
    if always is None:
        sel = jnp.zeros_like(x)
    else:
        sel = jnp.where(always, 1.0, 0.0)
        x = jnp.where(always, -3e38, x)
    for _ in range(n):
        m = jnp.max(x, axis=0, keepdims=True)
        idx = jnp.min(jnp.where(x == m, blk, float(nblk)), axis=0, keepdims=True)
        pick = blk == idx
        sel = jnp.where(pick, 1.0, sel)
        x = jnp.where(pick, -3e38, x)
    return sel


def _softmax_cols(s, valid):
    m = jnp.max(jnp.where(valid, s, NEG), axis=0, keepdims=True)
    p = jnp.where(valid, jnp.exp2(s - m), 0.0)
    l = jnp.sum(p, axis=0, keepdims=True)
    return p * jnp.where(l > 0.0, 1.0 / l, 0.0)


def _split_dot(a_bf16, x):
    hi = x.astype(BF16)
    lo = (x - hi.astype(F32)).astype(BF16)
    return (jnp.dot(a_bf16, hi, preferred_element_type=F32)
            + jnp.dot(a_bf16, lo, preferred_element_type=F32))


def _split_dot_rhs(x, w_bf16):
    hi = x.astype(BF16)
    lo = (x - hi.astype(F32)).astype(BF16)
    return (jnp.dot(hi, w_bf16, preferred_element_type=F32)
            + jnp.dot(lo, w_bf16, preferred_element_type=F32))


def _nt(a, b):
    return lax.dot_general(a, b, (((1,), (1,)), ((), ())), preferred_element_type=F32)


def _online_update(s, v, m_ref, l_ref, acc_ref, v_transposed=False):
    m_old = m_ref[...]
    m_new = jnp.maximum(m_old, jnp.max(s, axis=1, keepdims=True))
    p = jnp.exp2(s - m_new)
    alpha = jnp.exp2(m_old - m_new)
    l_ref[...] = alpha * l_ref[...] + jnp.sum(p, axis=1, keepdims=True)
    pv = _nt(p.astype(BF16), v) if v_transposed else jnp.dot(p.astype(BF16), v, preferred_element_type=F32)
    acc_ref[...] = alpha * acc_ref[...] + pv
    m_ref[...] = m_new


def _score_matrix(n_rows, row_offset, n_blocks):
    c = CMP_BLOCK // CMP_STRIDE
    ratio = SEL_BLOCK // CMP_STRIDE
    n_ov = ratio + c - 1
    m = np.zeros((n_blocks, n_rows), np.float32)
    for j in range(n_blocks):
        for u in range(n_ov):
            start = CMP_STRIDE * (u - (c - 1))
            w_u = (min(start + CMP_BLOCK, SEL_BLOCK) - max(start, 0)) / CMP_STRIDE
            n = ratio * j + u - (c - 1)
            if 0 <= n and n + row_offset < n_rows:
                m[j, n + row_offset] = w_u
    return m


def _nsa_prompt_kernel(qt_ref, gt_ref, kc_ref, vct_ref, ks_ref, vst_ref, kw_ref, vwt_ref,
                       mt_ref, cc_ref, cs_ref, bw_ref, eb_ref, o_ref,
                       sc_ref, sw_ref, sel_ref, m_ref, acc_ref, oc_ref,
                       sa_ref, sb_ref, pa_ref, pb_ref, ala_ref, alb_ref, *, n_blocks, variants):
    qb = pl.program_id(0)
    nrow = kc_ref.shape[0]
    gq = GROUP * Q_BLOCK
    tiles_per_q = SEL_TILE // Q_BLOCK
    nband = WINDOW + Q_BLOCK
    max_tile = n_blocks // BLK_PER_TILE - 1
    frow = lax.broadcasted_iota(jnp.int32, (KV_DIM, Q_BLOCK), 0)

    def lanes4(x):
        return jnp.concatenate([x] * GROUP, axis=1)

    qx = []
    for kv in range(N_KV):
        keep = (frow >= HEAD_DIM) if kv else (frow < HEAD_DIM)
        qx.append(jnp.concatenate(
            [jnp.where(keep, qt_ref[j * KV_DIM:(j + 1) * KV_DIM, :], jnp.zeros((), BF16))
             for j in range(GROUP)], axis=1))

    def qk_stage(kt, s_ref, penalty=0.0):
        k = ks_ref[pl.ds(pl.multiple_of(kt * SEL_TILE, SEL_TILE), SEL_TILE), :]
        k_aug = jnp.concatenate([k, eb_ref[...]], axis=1)
        for kv in range(N_KV):
            srow = sel_ref[kv, pl.ds(pl.multiple_of(kt * BLK_PER_TILE, BLK_PER_TILE), BLK_PER_TILE), :]
            mrows = jnp.concatenate([lanes4(srow + penalty), jnp.zeros((KV_DIM - BLK_PER_TILE, gq), F32)], axis=0)
            q_aug = jnp.concatenate([qx[kv], mrows.astype(BF16)], axis=0)
            s_ref[kv] = jnp.dot(k_aug, q_aug, preferred_element_type=F32)

    def gate(kv, i):
        return jnp.concatenate(
            [gt_ref[3 * (kv * GROUP + j) + i:3 * (kv * GROUP + j) + i + 1, :] for j in range(GROUP)], axis=1)

    def compressed_and_select(n_r, n_b):
        for kv in range(N_KV):
            sc_ref[kv, 0:n_r, :] = jnp.dot(kc_ref[0:n_r, :], qx[kv], preferred_element_type=F32)
        r = lax.broadcasted_iota(jnp.int32, (n_r, Q_BLOCK), 0)
        qpos_c = qb * Q_BLOCK + lax.broadcasted_iota(jnp.int32, (n_r, Q_BLOCK), 1)
        end_pos = (r - CMP_PAD) * CMP_STRIDE + (CMP_BLOCK - 1)
        vis = (r >= CMP_PAD) & (r < CMP_PAD + n_blocks * (SEL_BLOCK // CMP_STRIDE) - 1) & (end_pos <= qpos_c)
        vis_add = lanes4(jnp.where(vis, 0.0, NEG))
        band = pl.ds(pl.multiple_of(qb * SUBLANES, SUBLANES), CMP_BAND)
        blk_i = lax.broadcasted_iota(jnp.int32, (n_b, Q_BLOCK), 0)
        cur = (qb * Q_BLOCK + lax.broadcasted_iota(jnp.int32, (n_b, Q_BLOCK), 1)) // SEL_BLOCK
        forced = (blk_i == 0) | (blk_i == cur) | (blk_i == cur - 1)
        scores = []
        for kv in range(N_KV):
            sc_ref[kv, band, :] = sc_ref[kv, band, :] + cc_ref[kv]
            s = sc_ref[kv, 0:n_r, :] + vis_add
            m = jnp.max(s, axis=0, keepdims=True)
            p = jnp.exp2(s - m)
            l = jnp.sum(p, axis=0, keepdims=True)
            pn = p * jnp.where(m > 0.5 * NEG, 1.0 / l, 0.0)
            oc_ref[kv] = gate(kv, 0) * jnp.dot(vct_ref[:, 0:n_r], pn.astype(BF16), preferred_element_type=F32)
            imp = pn[:, 0:Q_BLOCK]
            for j in range(1, GROUP):
                imp = imp + pn[:, j * Q_BLOCK:(j + 1) * Q_BLOCK]
            score = _split_dot(mt_ref[0:n_b, 0:n_r], imp)
            scores.append(jnp.where(blk_i <= cur, score, -FORCE_BONUS))
        kband = kw_ref[pl.ds(pl.multiple_of(qb * Q_BLOCK, Q_BLOCK), nband), :]
        for kv in range(N_KV):
            sw_ref[kv] = jnp.dot(kband, qx[kv], preferred_element_type=F32)
        blk2 = jnp.concatenate([blk_i.astype(F32)] * N_KV, axis=1)
        forced2 = jnp.concatenate([forced] * N_KV, axis=1)
        sel = _select_topk(jnp.concatenate(scores, axis=1), blk2, N_SEL - 3, always=forced2)
        for kv in range(N_KV):
            sel_ref[kv, 0:n_b, :] = (sel[:, kv * Q_BLOCK:(kv + 1) * Q_BLOCK] - 1.0) * (-NEG)

    sel_ref[...] = jnp.full_like(sel_ref, NEG)
    lo = 0
    for q_hi, n_r, n_b in variants:
        @pl.when((qb >= lo) & (qb <= q_hi))
        def _(n_r=n_r, n_b=n_b):
            compressed_and_select(n_r, n_b)
        lo = q_hi + 1

    m_ref[...] = jnp.full_like(m_ref, NEG)
    acc_ref[...] = jnp.zeros_like(acc_ref)

    def pv_stage(kt, p_ref, al_ref):
        for kv in range(N_KV):
            acc_ref[kv] = al_ref[kv] * acc_ref[kv] + jnp.dot(vst_ref[kv, kt], p_ref[kv],
                                                             preferred_element_type=F32)

    def sm_stage(kt, s_ref, p_ref, al_ref, near):
        d0 = qb - kt * tiles_per_q
        for kv in range(N_KV):
            s = s_ref[kv]
            if near:
                s = s + jnp.concatenate(
                    [cs_ref[kv, jnp.clip(d0 - i, -1, 2) + 1] for i in range(tiles_per_q)], axis=0)
            m_old = m_ref[kv]
            m_new = jnp.maximum(m_old, jnp.max(s, axis=0, keepdims=True))
            p_ref[kv] = jnp.exp2(s - m_new).astype(BF16)
            al_ref[kv] = jnp.exp2(m_old - m_new)
            m_ref[kv] = m_new

    n_far = jnp.maximum(qb - 1, 0) // tiles_per_q
    pb_ref[...] = jnp.zeros_like(pb_ref)
    alb_ref[...] = jnp.ones_like(alb_ref)
    qk_stage(0, sa_ref)

    def window(kv):
        vband = jnp.concatenate([vwt_ref[qb + i] for i in range(nband // Q_BLOCK)], axis=1)
        krow = lax.broadcasted_iota(jnp.int32, (nband, Q_BLOCK), 0)
        pos_add = lanes4(jnp.where(krow >= WINDOW - qb * Q_BLOCK, 0.0, NEG))
        s = sw_ref[kv] + bw_ref[kv] + pos_add
        m = jnp.max(s, axis=0, keepdims=True)
        p = jnp.exp2(s - m)
        o_w = (jnp.dot(vband, p.astype(BF16), preferred_element_type=F32)
               / jnp.sum(p, axis=0, keepdims=True))
        oc_ref[kv] = oc_ref[kv] + gate(kv, 2) * o_w

    window(0)

    def pair_body(u, c):
        t0 = 2 * u
        t1 = jnp.minimum(t0 + 1, max_tile)
        t2 = jnp.minimum(t0 + 2, max_tile)
        qk_stage(t1, sb_ref, jnp.where(t0 + 1 < n_far, 0.0, NEG))
        pv_stage(jnp.maximum(t0 - 1, 0), pb_ref, alb_ref)
        sm_stage(t0, sa_ref, pa_ref, ala_ref, False)
        qk_stage(t2, sa_ref)
        pv_stage(t0, pa_ref, ala_ref)
        sm_stage(t1, sb_ref, pb_ref, alb_ref, False)
        return c
    n_pairs = (n_far + 1) // 2
    lax.fori_loop(0, n_pairs, pair_body, 0)

    near_a = n_far
    near_b = jnp.minimum(n_far + 1, max_tile)
    qk_stage(near_a, sa_ref)
    pv_stage(jnp.clip(2 * n_pairs - 1, 0, max_tile), pb_ref, alb_ref)
    window(1)
    sm_stage(near_a, sa_ref, pa_ref, ala_ref, True)
    pv_stage(near_a, pa_ref, ala_ref)

    @pl.when(n_far + 1 <= qb // tiles_per_q)
    def _():
        qk_stage(near_b, sb_ref)
        sm_stage(near_b, sb_ref, pb_ref, alb_ref, True)
        pv_stage(near_b, pb_ref, alb_ref)

    o_kv = []
    for kv in range(N_KV):
        acc = acc_ref[kv]
        o_s = acc[0:HEAD_DIM, :] / acc[HEAD_DIM:HEAD_DIM + 1, :]
        o_kv.append(oc_ref[kv, kv * HEAD_DIM:(kv + 1) * HEAD_DIM, :] + gate(kv, 1) * o_s)
    o_t = jnp.concatenate(o_kv, axis=0)
    for j in range(GROUP):
        o_ref[:, j * LANES:(j + 1) * LANES] = o_t[:, j * Q_BLOCK:(j + 1) * Q_BLOCK].T.astype(BF16)


def _nsa_prompt(qat, gat, kc, vct, ksb, vst3, kwb, vwt3, mt, cc, cs, bw, eb):
    t = qat.shape[1]
    n_blocks = t // SEL_BLOCK
    nrow = kc.shape[0]
    gq = GROUP * Q_BLOCK
    n_q = t // Q_BLOCK
    variants = []
    for n_r in list(range(2 * LANES, nrow - LANES, 2 * LANES)) + [nrow]:
        q_hi = n_q - 1 if n_r == nrow else min((n_r - CMP_BAND) // SUBLANES, n_q - 1)
        n_b = min(-(-(2 * q_hi + 2) // (2 * SUBLANES)) * (2 * SUBLANES), n_blocks)
        variants.append((q_hi, n_r, n_b))
    return pl.pallas_call(
        functools.partial(_nsa_prompt_kernel, n_blocks=n_blocks, variants=tuple(variants)),
        grid=(t // Q_BLOCK,),
        in_specs=[pl.BlockSpec((Q_DIM, Q_BLOCK), lambda i: (0, i)),
                  pl.BlockSpec((LANES, Q_BLOCK), lambda i: (0, i)),
                  _const_spec(kc.shape), _const_spec(vct.shape),
                  _const_spec(ksb.shape), _const_spec(vst3.shape),
                  _const_spec(kwb.shape), _const_spec(vwt3.shape),
                  _const_spec(mt.shape), _const_spec(cc.shape),
                  _const_spec(cs.shape), _const_spec(bw.shape), _const_spec(eb.shape)],
        out_specs=pl.BlockSpec((Q_BLOCK, Q_DIM), lambda i: (i, 0)),
        out_shape=jax.ShapeDtypeStruct((t, Q_DIM), BF16),
        scratch_shapes=[pltpu.VMEM((N_KV, nrow, gq), F32),
                        pltpu.VMEM((N_KV, WINDOW + Q_BLOCK, gq), F32),
                        pltpu.VMEM((N_KV, n_blocks, Q_BLOCK), F32),
                        pltpu.VMEM((N_KV, 1, gq), F32),
                        pltpu.VMEM((N_KV, vst3.shape[2], gq), F32),
                        pltpu.VMEM((N_KV, KV_DIM, gq), F32),
                        pltpu.VMEM((N_KV, SEL_TILE, gq), F32), pltpu.VMEM((N_KV, SEL_TILE, gq), F32),
                        pltpu.VMEM((N_KV, SEL_TILE, gq), BF16), pltpu.VMEM((N_KV, SEL_TILE, gq), BF16),
                        pltpu.VMEM((N_KV, 1, gq), F32), pltpu.VMEM((N_KV, 1, gq), F32)],
        compiler_params=_cparams(("arbitrary",)),
        name="nsa_prompt",
    )(qat, gat, kc, vct, ksb, vst3, kwb, vwt3, mt, cc, cs, bw, eb)


S_ROWS = 128
S_CHUNK_PAGES = 128
S_CHUNK = S_CHUNK_PAGES * PAGE


def _nsa_sample_kernel(pt_ref, qx_ref, gm_ref, kc_ref, vct_ref, kpool_ref, vpool_ref,
                       knew_ref, vnew_ref, kwin_ref, vwin_ref, kwnew_ref, vwnew_ref,
                       mt_ref, gsum_ref, ccs_ref, css_ref, cns_ref, cws_ref, o_ref,
                       kbuf, vbuf, sem, mask_ref, m_ref, l_ref, acc_ref, oc_ref, ow_ref,
                       *, n_chunks, n_blocks):
    b = pl.program_id(0)
    c = pl.program_id(1)
    step = b * n_chunks + c
    total = pl.num_programs(0) * n_chunks
    slot = step % 2

    def copies(bb, cc, s, p):
        pg = cc * S_CHUNK_PAGES + p
        dst = pl.ds(pl.multiple_of(p * PAGE, PAGE), PAGE)
        return (pltpu.make_async_copy(kpool_ref.at[pt_ref[bb, pg]], kbuf.at[s, :, dst], sem.at[0, s]),
                pltpu.make_async_copy(vpool_ref.at[pt_ref[bb, pg]], vbuf.at[s, :, dst], sem.at[1, s]))

    def start_all(st, s):
        bb = st // n_chunks
        cc = st % n_chunks

        def body(p, carry):
            ck, cv = copies(bb, cc, s, p)
            ck.start(priority=0)
            cv.start(priority=1)
            return carry
        lax.fori_loop(0, S_CHUNK_PAGES, body, 0)

    @pl.when(step == 0)
    def _():
        start_all(step, slot)

    @pl.when(step + 1 < total)
    def _():
        start_all(step + 1, 1 - slot)

    qx = qx_ref[0]
    n_q = o_ref.shape[1]
    qq = qx[:n_q]

    @pl.when(c == 0)
    def _():
        nrow = kc_ref.shape[1]
        s = _nt(kc_ref[0], qx)
        r = lax.broadcasted_iota(jnp.int32, (nrow, S_ROWS), 0)
        band0 = nrow - CMP_BAND
        s = s + jnp.concatenate([jnp.zeros((band0, S_ROWS), F32), ccs_ref[...]], axis=0)
        pn = _softmax_cols(s, r < nrow - 1)
        oc_ref[...] = jnp.dot(vct_ref[0], pn.astype(BF16), preferred_element_type=F32).T[:n_q]
        imp = jnp.dot(pn, gsum_ref[...], preferred_element_type=F32, precision=HIGHEST)
        score = _split_dot(mt_ref[...], imp)
        blk_i = lax.broadcasted_iota(jnp.int32, (n_blocks, S_ROWS), 0)
        forced = (blk_i == 0) | (blk_i == n_blocks - 1)
        selt = _select_topk(score, blk_i.astype(F32), N_SEL - 3, always=forced)
        selt = lax.dot_general(selt.astype(BF16), gsum_ref[...].astype(BF16), (((1,), (1,)), ((), ())),
                               preferred_element_type=F32)
        selm1 = selt.T[:n_q] - 1.0
        e_r = lax.broadcasted_iota(jnp.int32, (BLK_PER_TILE, SEL_TILE), 0)
        e_c = lax.broadcasted_iota(jnp.int32, (BLK_PER_TILE, SEL_TILE), 1)
        expand = jnp.where(e_c // SEL_BLOCK == e_r, -NEG, 0.0)
        for kt in range(n_blocks // BLK_PER_TILE):
            mask_ref[:, kt * SEL_TILE:(kt + 1) * SEL_TILE] = jnp.dot(
                selm1[:, kt * BLK_PER_TILE:(kt + 1) * BLK_PER_TILE], expand, preferred_element_type=F32)

        sw = _nt(qq, kwin_ref[0].astype(BF16)) + cws_ref[:n_q, :]
        sn = _nt(qq, kwnew_ref[0].astype(BF16)) + cns_ref[:n_q, :]
        m = jnp.maximum(jnp.max(sw, axis=1, keepdims=True), jnp.max(sn, axis=1, keepdims=True))
        pw = jnp.exp2(sw - m)
        pn2 = jnp.exp2(sn - m)
        l = jnp.sum(pw, axis=1, keepdims=True) + jnp.sum(pn2, axis=1, keepdims=True)
        ow = (jnp.dot(pw.astype(BF16), vwin_ref[0].astype(BF16), preferred_element_type=F32)
              + jnp.dot(pn2.astype(BF16), vwnew_ref[0].astype(BF16), preferred_element_type=F32))
        ow_ref[...] = ow / l

        m_ref[...] = jnp.full_like(m_ref, NEG)
        l_ref[...] = jnp.zeros_like(l_ref)
        acc_ref[...] = jnp.zeros_like(acc_ref)

    pltpu.make_async_copy(kbuf.at[slot], kbuf.at[slot], sem.at[0, slot]).wait()
    pltpu.make_async_copy(vbuf.at[slot], vbuf.at[slot], sem.at[1, slot]).wait()

    col0 = pl.multiple_of(c * S_CHUNK, S_CHUNK)
    s = (jnp.dot(qq, kbuf[slot].astype(BF16), preferred_element_type=F32)
         + mask_ref[:, pl.ds(col0, S_CHUNK)])

    @pl.when(c < n_chunks - 1)
    def _():
        _online_update(s, vbuf[slot].astype(BF16), m_ref, l_ref, acc_ref, v_transposed=True)

    @pl.when(c == n_chunks - 1)
    def _():
        near = jnp.concatenate([jnp.zeros((n_q, S_CHUNK - LANES), F32), css_ref[:n_q, :]], axis=1)
        _online_update(s + near, vbuf[slot].astype(BF16), m_ref, l_ref, acc_ref, v_transposed=True)
        sn = _nt(qq, knew_ref[0].astype(BF16)) + cns_ref[:n_q, :]
        _online_update(sn, vnew_ref[0].astype(BF16), m_ref, l_ref, acc_ref)
        o_s = acc_ref[...] / l_ref[...]
        o_ref[0] = gm_ref[0, 0] * oc_ref[...] + gm_ref[0, 1] * o_s + gm_ref[0, 2] * ow_ref[...]


def _nsa_sample(page_table, qx, gm, kc, vct, kpool, vpool, knew, vnew, kwin, vwin, kwnew, vwnew,
                mt, gsum, ccs, css, cns, cws):
    nb = qx.shape[0]
    n_pages = page_table.shape[1]
    n_chunks = n_pages // S_CHUNK_PAGES
    past = n_pages * PAGE
    n_blocks = past // SEL_BLOCK
    n_q = gm.shape[2]

    def bspec(shape):
        nd = len(shape)
        return pl.BlockSpec((1,) + tuple(shape[1:]), lambda b, c, pt: (b,) + (0,) * (nd - 1))

    def cspec(shape):
        nd = len(shape)
        return pl.BlockSpec(tuple(shape), lambda b, c, pt: (0,) * nd, pipeline_mode=pl.Buffered(1))

    grid_spec = pltpu.PrefetchScalarGridSpec(
        num_scalar_prefetch=1,
        grid=(nb, n_chunks),
        in_specs=[bspec(qx.shape), bspec(gm.shape), bspec(kc.shape), bspec(vct.shape),
                  pl.BlockSpec(memory_space=pl.ANY), pl.BlockSpec(memory_space=pl.ANY),
                  bspec(knew.shape), bspec(vnew.shape), bspec(kwin.shape), bspec(vwin.shape),
                  bspec(kwnew.shape), bspec(vwnew.shape),
                  cspec(mt.shape), cspec(gsum.shape), cspec(ccs.shape), cspec(css.shape),
                  cspec(cns.shape), cspec(cws.shape)],
        out_specs=pl.BlockSpec((1, n_q, KV_DIM), lambda b, c, pt: (b, 0, 0)),
        scratch_shapes=[pltpu.VMEM((2, KV_DIM, S_CHUNK), F32),
                        pltpu.VMEM((2, KV_DIM, S_CHUNK), F32),
                        pltpu.SemaphoreType.DMA((2, 2)),
                        pltpu.VMEM((n_q, past), F32),
                        pltpu.VMEM((n_q, 1), F32), pltpu.VMEM((n_q, 1), F32),
                        pltpu.VMEM((n_q, KV_DIM), F32),
                        pltpu.VMEM((n_q, KV_DIM), F32), pltpu.VMEM((n_q, KV_DIM), F32)],
    )
    return pl.pallas_call(
        functools.partial(_nsa_sample_kernel, n_chunks=n_chunks, n_blocks=n_blocks),
        grid_spec=grid_spec,
        out_shape=jax.ShapeDtypeStruct((nb, n_q, KV_DIM), F32),
        compiler_params=_cparams(("arbitrary", "arbitrary")),
        name="nsa_sample",
    )(page_table, qx, gm, kc, vct, kpool, vpool, knew, vnew, kwin, vwin, kwnew, vwnew,
      mt, gsum, ccs, css, cns, cws)


def _ffn_kernel(x_ref, oa_ref, ob_ref, sa_ref, sb_ref, wpa_ref, wpb_ref, wo_ref, nf_ref,
                wg_ref, wu_ref, wd_ref, nl_ref, y_ref):
    pa = jnp.dot(oa_ref[...], wpa_ref[...], preferred_element_type=F32)
    pb = jnp.dot(ob_ref[...], wpb_ref[...], preferred_element_type=F32)
    merged = sa_ref[...] * pa + sb_ref[...] * pb
    x = x_ref[...] + jnp.dot(merged.astype(BF16), wo_ref[...], preferred_element_type=F32)
    hn = (x * lax.rsqrt(jnp.mean(x * x, axis=-1, keepdims=True) + EPS) * nf_ref[...]).astype(BF16)
    gate = jnp.dot(hn, wg_ref[...], preferred_element_type=F32)
    up = jnp.dot(hn, wu_ref[...], preferred_element_type=F32)
    ff = (jax.nn.silu(gate) * up).astype(BF16)
    x = x + jnp.dot(ff, wd_ref[...], preferred_element_type=F32)
    y_ref[...] = x * lax.rsqrt(jnp.mean(x * x, axis=-1, keepdims=True) + EPS) * nl_ref[...]


def _ffn(x2d, oa, ob, sa, sb, wpa, wpb, wo, nf, wg, wu, wd, nl, tm):
    rows = x2d.shape[0]

    def rspec(n):
        return pl.BlockSpec((tm, n), lambda i: (i, 0))

    return pl.pallas_call(
        _ffn_kernel,
        grid=(rows // tm,),
        in_specs=[rspec(D_MODEL), rspec(Q_DIM), rspec(HG_W), rspec(D_MODEL), rspec(D_MODEL),
                  _const_spec(wpa.shape), _const_spec(wpb.shape), _const_spec(wo.shape),
                  _const_spec(nf.shape), _const_spec(wg.shape), _const_spec(wu.shape),
                  _const_spec(wd.shape), _const_spec(nl.shape)],
        out_specs=rspec(D_MODEL),
        out_shape=jax.ShapeDtypeStruct((rows, D_MODEL), F32),
        compiler_params=_cparams(("arbitrary",)),
        name="ffn",
    )(x2d, oa, ob, sa, sb, wpa, wpb, wo, nf, wg, wu, wd, nl)


def _pack_w_in(w_in):
    sizes = (Q_DIM,) + (KV_DIM,) * 6 + (3 * N_HEADS,) + (HG_W,) * 4 + (D_MODEL,) * 2
    offs = np.concatenate([[0], np.cumsum(sizes)])
    q = w_in[:, offs[0]:offs[1]].reshape(D_MODEL, N_HEADS, HEAD_DIM)[:, _HEAD_PERM, :].reshape(D_MODEL, Q_DIM)
    g = jnp.pad(w_in[:, offs[7]:offs[8]], ((0, 0), (0, LANES - 3 * N_HEADS)))
    return jnp.concatenate([q, w_in[:, offs[1]:offs[7]], g, w_in[:, offs[8]:]], axis=1).astype(BF16)


def _strip(bvc, rel, lo=0, hi=None, masked=NEG):
    val = bvc[:, np.clip(rel, 0, 255)]
    ok = rel >= lo
    if hi is not None:
        ok = ok & (rel < hi)
    return jnp.where(jnp.asarray(ok)[None], val, masked)


def _toeplitz(bvc, a, n_rows, n_cols, lo=0, hi=None):
    n = n_rows + n_cols - 1
    u = _strip(bvc, a - (n_rows - 1) + np.arange(n), lo, hi)
    u = jnp.pad(u, ((0, 0), (0, 1)))
    circ = jnp.tile(u, (1, n_rows))[:, :n_rows * n].reshape(N_HEADS, n_rows, n)
    return circ[:, :, n_rows - 1:n_rows - 1 + n_cols]


def _bias_strips_prompt(bvc):
    gq = GROUP * Q_BLOCK

    def lanes(x):
        return x.reshape(N_KV, GROUP, x.shape[1], Q_BLOCK).transpose(0, 2, 1, 3).reshape(N_KV, x.shape[1], gq)

    cs = jnp.stack([lanes(_toeplitz(bvc, Q_BLOCK * d, Q_BLOCK, Q_BLOCK)) for d in (-1, 0, 1, 2)], axis=1)
    bw = lanes(jnp.concatenate(
        [_toeplitz(bvc, Q_BLOCK * d, Q_BLOCK, Q_BLOCK, 0, WINDOW) for d in range(WINDOW // Q_BLOCK, -1, -1)], axis=1))
    rr = np.arange(CMP_BAND)[:, None]
    rel_c = np.arange(Q_BLOCK)[None, :] - CMP_STRIDE * (rr - CMP_PAD) - (CMP_BLOCK - 1)
    cc = _strip(bvc, rel_c, masked=0.0).reshape(N_KV, GROUP, CMP_BAND, Q_BLOCK)
    cc = cc.transpose(0, 2, 1, 3).reshape(N_KV, CMP_BAND, gq)
    return cc, cs, bw


def _bias_strips_sample(bvc, past, t):
    def rows(a):
        a = a.reshape(N_HEADS * t, a.shape[-1])
        return jnp.pad(a, ((0, S_ROWS - N_HEADS * t), (0, 0)))
    tt = np.arange(t)[:, None]
    nrow = past // CMP_STRIDE
    n = (nrow - CMP_BAND + np.arange(CMP_BAND))[None, :]
    ccs = rows(_strip(bvc, past + tt - CMP_STRIDE * n - (CMP_BLOCK - 1), masked=0.0)).T
    i = np.arange(LANES)[None, :]
    css = rows(_strip(bvc, LANES + tt - i))
    cns = rows(_strip(bvc, np.where(i < t, tt - i, -1)))
    iw = np.arange(WINDOW)[None, :]
    cws = rows(_strip(bvc, WINDOW + tt - iw, 0, WINDOW))
    return ccs, css, cns, cws


def kernel(x_prompt, x_sample, cache_k_cmp, cache_v_cmp, cache_k_slc, cache_v_slc, state_k_win, state_v_win,
           state_hgrn, page_table, norm_mix, w_in, cmp_pe_k, cmp_w1_k, cmp_w2_k, cmp_pe_v, cmp_w1_v, cmp_w2_v,
           rel_bias, hg_lb_logits, hg_norm, w_proj_a, w_proj_b, w_out, norm_ffn, w_gate, w_up, w_down, norm_final):
    nbp, t_p, _ = x_prompt.shape
    nbs, t_s, _ = x_sample.shape
    assert nbp == 1 and norm_mix.shape[0] == 1
    n_pages = page_table.shape[1]
    past = n_pages * PAGE
    assert state_k_win.shape[2] == WINDOW and past % S_CHUNK == 0 and t_s <= SUBLANES

    lb = jnp.cumsum(jax.nn.softmax(hg_lb_logits.astype(F32), axis=0), axis=0)[0]
    lb3 = jnp.pad(jnp.stack([jnp.log(lb), jnp.log1p(-lb), 1.0 - lb]), ((0, SUBLANES - 3), (0, 0)))
    w_pack = _pack_w_in(w_in[0])
    g_mix = norm_mix[0][None, :]
    wpa = w_proj_a[0].reshape(N_HEADS, HEAD_DIM, D_MODEL)[_HEAD_PERM].reshape(Q_DIM, D_MODEL).astype(BF16)
    wpb = w_proj_b[0].astype(BF16)
    wo = w_out[0].astype(BF16)
    wg, wu, wd = w_gate[0].astype(BF16), w_up[0].astype(BF16), w_down[0].astype(BF16)
    nf, nl = norm_ffn[0][None, :], norm_final[None, :]
    gn = hg_norm[0][None, :]
    pe_k, w1_k, w2_k = _compress_weights(cmp_pe_k[0], cmp_w1_k[0], cmp_w2_k[0])
    pe_v, w1_v, w2_v = _compress_weights(cmp_pe_v[0], cmp_w1_v[0], cmp_w2_v[0])
    bvc = (rel_bias[_BUCKET] - rel_bias[N_BUCKETS - 1][None, :]).T * LOG2E
    cc, cs, bw = _bias_strips_prompt(bvc)
    ccs, css, cns, cws = _bias_strips_sample(bvc, past, t_s)

    xp2 = x_prompt.reshape(t_p, D_MODEL)
    xs2 = x_sample.reshape(nbs * t_s, D_MODEL)
    seg = lambda off, n: w_pack[:, off:off + n]
    w_t = jnp.concatenate([seg(_OFF_Q, Q_DIM), seg(_OFF_G, LANES), seg(_OFF_KV + 3 * KV_DIM, KV_DIM),
                           seg(_OFF_KV + 5 * KV_DIM, KV_DIM)], axis=1).T
    pp = _proj(xp2, g_mix, w_pack, w_t, lb3, PROJ_ROWS, True)
    ps = _proj(xs2, g_mix, w_pack, w_t, lb3, nbs * t_s, False)
    (kc_p, vc_p, ks_p, vs_p, kw_p, vw_p, ksb_p, kwb_p,
     qh_p, lf_p, kh_p, vh_p, gs_p, sa_p, sb_p, qat_p, gat_p, vst_p, vwt_p) = pp
    (qa_s, kc_s, vc_s, ks_s, vs_s, kw_s, vw_s, ga_s,
     qh_s, lf_s, kh_s, vh_s, gs_s, sa_s, sb_s) = ps

    ob_p, s_p = _hgrn_prompt(qh_p, kh_p, vh_p, lf_p, gs_p, gn)
    r3 = lambda a: a.reshape(nbs, t_s, a.shape[-1])
    ob_s, s_s = _hgrn_sample(r3(qh_s), r3(kh_s), r3(vh_s), r3(lf_s), r3(gs_s), gn, state_hgrn[0])

    ident = jnp.arange(t_p // PAGE, dtype=jnp.int32)[None, :]
    pool_rows = lambda a: a.reshape(-1, PAGE, KV_DIM)
    kcb_p, _ = _compress(pool_rows(kc_p), ident, pe_k, w1_k, w2_k, False)
    _, vct_p = _compress(pool_rows(vc_p), ident, pe_v, w1_v, w2_v, False)
    back = -(CMP_PAD + kcb_p.shape[1]) % LANES
    kcb_p = jnp.pad(kcb_p[0], ((CMP_PAD, back), (0, 0)))
    vct_p = jnp.pad(vct_p[0], ((0, 0), (CMP_PAD, back)))
    nrow_p = kcb_p.shape[0]
    mt_p = jnp.asarray(_score_matrix(nrow_p, CMP_PAD, t_p // SEL_BLOCK), BF16)
    kwb_pad = jnp.pad(kwb_p, ((WINDOW, 0), (0, 0)))
    vst3 = vst_p.reshape(KV_DIM, t_p // SEL_TILE, SEL_TILE).transpose(1, 0, 2)
    ones_rows = jnp.ones((t_p // SEL_TILE, 2 * SUBLANES, SEL_TILE), BF16)
    vst3 = jnp.stack([jnp.concatenate([vst3[:, kv * HEAD_DIM:(kv + 1) * HEAD_DIM, :], ones_rows], axis=1)
                      for kv in range(N_KV)])
    vwt3 = jnp.pad(vwt_p, ((0, 0), (WINDOW, 0))).reshape(KV_DIM, (t_p + WINDOW) // Q_BLOCK, Q_BLOCK)
    vwt3 = vwt3.transpose(1, 0, 2)
    eb = (np.arange(LANES)[None, :] == np.arange(SEL_TILE)[:, None] // SEL_BLOCK).astype(np.float32)
    oa_p = _nsa_prompt(qat_p, gat_p, kcb_p, vct_p, ksb_p, vst3, kwb_pad, vwt3, mt_p, cc, cs, bw,
                       jnp.asarray(eb, BF16))

    pool_t = lambda a: a.transpose(0, 2, 3, 1).reshape(-1, KV_DIM, PAGE)
    kcb_s, _ = _compress(pool_t(cache_k_cmp[0]), page_table, pe_k, w1_k, w2_k, True)
    _, vct_s = _compress(pool_t(cache_v_cmp[0]), page_table, pe_v, w1_v, w2_v, True)
    nq = N_HEADS * t_s
    qs4 = qa_s.reshape(nbs, t_s, GROUP, N_KV, HEAD_DIM).astype(F32)
    qx = jnp.einsum('btjkd,kq->bkjtqd', qs4, jnp.eye(N_KV, dtype=F32)).reshape(nbs, nq, KV_DIM)
    qx = jnp.pad(qx, ((0, 0), (0, S_ROWS - nq), (0, 0))).astype(BF16)
    g4 = ga_s[:, :3 * N_HEADS].reshape(nbs, t_s, N_KV, GROUP, 3)
    gm = jnp.transpose(g4, (0, 4, 2, 3, 1)).reshape(nbs, 3, nq, 1)
    gm = jnp.broadcast_to(gm, (nbs, 3, nq, KV_DIM))
    new_tile = lambda a: jnp.pad(a.reshape(nbs, t_s, KV_DIM), ((0, 0), (0, LANES - t_s), (0, 0)))
    mt_s = jnp.asarray(_score_matrix(past // CMP_STRIDE, 0, past // SEL_BLOCK), BF16)
    gsum = np.zeros((S_ROWS, S_ROWS), np.float32)
    for kv in range(N_KV):
        for j in range(GROUP):
            for t in range(t_s):
                gsum[(kv * GROUP + j) * t_s + t, kv * t_s + t] = 1.0
    o_kv = _nsa_sample(page_table, qx, gm, kcb_s, vct_s,
                       pool_t(cache_k_slc[0]), pool_t(cache_v_slc[0]),
                       new_tile(ks_s), new_tile(vs_s), state_k_win[0].reshape(nbs, WINDOW, KV_DIM),
                       state_v_win[0].reshape(nbs, WINDOW, KV_DIM), new_tile(kw_s), new_tile(vw_s),
                       mt_s, jnp.asarray(gsum), ccs, css, cns, cws)
    o5 = o_kv.reshape(nbs, N_KV, GROUP, t_s, N_KV, HEAD_DIM)
    oa_s = jnp.einsum('bkjtqd,kq->btjkd', o5, jnp.eye(N_KV, dtype=F32)).reshape(nbs * t_s, Q_DIM).astype(BF16)

    y_p = _ffn(xp2, oa_p, ob_p, sa_p, sb_p, wpa, wpb, wo, nf, wg, wu, wd, nl, FFN_ROWS)
    y_s = _ffn(xs2, oa_s, ob_s.reshape(nbs * t_s, HG_W), sa_s, sb_s, wpa, wpb, wo, nf, wg, wu, wd, nl, nbs * t_s)

    kv5 = lambda a, nb_, tt: a.reshape(1, nb_, tt, N_KV, HEAD_DIM)
    wl = min(WINDOW, t_p)
    win = lambda st, new: jnp.concatenate(
        [st[0], new.reshape(nbs, t_s, N_KV, HEAD_DIM)], axis=1)[:, -WINDOW:][None]
    return (y_p.reshape(1, t_p, D_MODEL), y_s.reshape(nbs, t_s, D_MODEL),
            kv5(kc_p, 1, t_p), kv5(vc_p, 1, t_p), kv5(ks_p, 1, t_p), kv5(vs_p, 1, t_p),
            kv5(kw_p[-wl:], 1, wl), kv5(vw_p[-wl:], 1, wl), s_p[None, None],
            kv5(kc_s, nbs, t_s), kv5(vc_s, nbs, t_s), kv5(ks_s, nbs, t_s), kv5(vs_s, nbs, t_s),
            win(state_k_win, kw_s), win(state_v_win, vw_s), s_s[None])
```
